```python
import math
import jax, jax.numpy as jnp
from jax import lax
import numpy as np

D_MODEL = 1024
BATCH = 8
SEQ = 4096
DEPTH = 1
DEC_BATCH = 128
DEC_SEQ = 8
PAST_LEN = 16384
PAGE_SIZE = 128

MIX_DIM = D_MODEL
SSD_D_INNER = MIX_DIM // 2
SSD_HEAD_DIM = 64
SSD_HEADS = SSD_D_INNER // SSD_HEAD_DIM
SSD_GROUPS = 2
SSD_HEADS_PER_GROUP = SSD_HEADS // SSD_GROUPS
D_STATE = 128
CONV_K = 4
CONV_DIM = SSD_D_INNER + 2 * SSD_GROUPS * D_STATE
CHUNK = 128
ATTN_DIM = MIX_DIM - SSD_D_INNER
ATTN_HEAD_DIM = 64
N_HEADS = ATTN_DIM // ATTN_HEAD_DIM
N_KV_HEADS = 2
Q_PER_KV = N_HEADS // N_KV_HEADS
KV_DIM = N_KV_HEADS * ATTN_HEAD_DIM
WINDOW = 128
N_BUCKETS = 32
MAX_EXACT = N_BUCKETS // 2
MAX_DISTANCE = 128
N_MEM = 256
CA_HEADS = 4
CA_HEAD_DIM = 128
CA_DIM = CA_HEADS * CA_HEAD_DIM
D_FF = -(-8 * D_MODEL // (3 * 256)) * 256
IN_DIM = SSD_D_INNER + CONV_DIM + SSD_HEADS + ATTN_DIM + 2 * KV_DIM
SPLITS = (SSD_D_INNER, SSD_D_INNER + CONV_DIM, SSD_D_INNER + CONV_DIM + SSD_HEADS,
          SSD_D_INNER + CONV_DIM + SSD_HEADS + ATTN_DIM,
          SSD_D_INNER + CONV_DIM + SSD_HEADS + ATTN_DIM + KV_DIM)
EPS = 1e-6

kernel_name = 'hymba_ssd_swa_sink_t5_xattn_step'


def rmsnorm(x, g):
    xf = x.astype(jnp.float32)
    y = xf * lax.rsqrt(jnp.mean(xf * xf, axis=-1, keepdims=True) + EPS)
    return (y * g.astype(jnp.float32)).astype(x.dtype)


def gated_group_rmsnorm(y, z, g):
    yf = y.astype(jnp.float32) * jax.nn.silu(z.astype(jnp.float32))
    yg = yf.reshape(yf.shape[:-1] + (SSD_GROUPS, SSD_D_INNER // SSD_GROUPS))
    yg = yg * lax.rsqrt(jnp.mean(yg * yg, axis=-1, keepdims=True) + EPS)
    return (yg.reshape(yf.shape) * g.astype(jnp.float32)).astype(z.dtype)


def ssd_scan(x, dt, A, B, C, h0):
    b, L = x.shape[:2]
    Q = CHUNK if L % CHUNK == 0 else L
    nc = L // Q
    G, Hg, P, N = SSD_GROUPS, SSD_HEADS_PER_GROUP, SSD_HEAD_DIM, D_STATE
    x = x.reshape(b, nc, Q, G, Hg, P)
    dt = dt.reshape(b, nc, Q, G, Hg)
    B = B.reshape(b, nc, Q, G, N)
    C = C.reshape(b, nc, Q, G, N)
    cs = jnp.cumsum(dt * A.reshape(G, Hg), axis=2)
    xdt = x * dt[..., None]
    causal = jnp.tril(jnp.ones((Q, Q), dtype=bool))[:, :, None, None]
    diff = cs[:, :, :, None] - cs[:, :, None, :]
    decay_qs = jnp.exp(jnp.where(causal, diff, -jnp.inf))
    cb = jnp.einsum('bcqgn,bcsgn->bcqsg', C, B)
    y_diag = jnp.einsum('bcqsgh,bcsghp->bcqghp', cb[..., None] * decay_qs, xdt)
    decay_end = jnp.exp(cs[:, :, -1:] - cs)
    st = jnp.einsum('bcsgn,bcsghp->bcghpn', B, xdt * decay_end[..., None])
    chunk_decay = jnp.exp(cs[:, :, -1])

    def step(h, inp):
        dec, s = inp
        return dec[..., None, None] * h + s, h

    h_final, h_starts = lax.scan(step, h0.reshape(b, G, Hg, P, N),
                                 (jnp.moveaxis(chunk_decay, 1, 0), jnp.moveaxis(st, 1, 0)))
    h_starts = jnp.moveaxis(h_starts, 0, 1)
    y_off = jnp.einsum('bcqgn,bcghpn->bcqghp', C, h_starts) * jnp.exp(cs)[..., None]
    y = (y_diag + y_off).reshape(b, L, SSD_HEADS, P)
    return y, h_final.reshape(b, SSD_HEADS, P, N)


def ssd_branch(z, xbc, dt_raw, conv_prev, ssm_prev, conv_w, conv_b, dt_bias, a_log, d_skip, g_norm):
    b, L, _ = xbc.shape
    xp = jnp.concatenate([conv_prev.astype(xbc.dtype), xbc], axis=1)
    conv = conv_b.astype(jnp.float32)
    for k in range(CONV_K):
        conv = conv + xp[:, k:k + L].astype(jnp.float32) * conv_w[k].astype(jnp.float32)
    conv = jax.nn.silu(conv)
    new_conv = xp[:, L:].astype(conv_prev.dtype)
    xs, Bm, Cm = jnp.split(conv, [SSD_D_INNER, SSD_D_INNER + SSD_GROUPS * D_STATE], axis=-1)
    xs = xs.reshape(b, L, SSD_HEADS, SSD_HEAD_DIM)
    Bm = Bm.reshape(b, L, SSD_GROUPS, D_STATE)
    Cm = Cm.reshape(b, L, SSD_GROUPS, D_STATE)
    dt = jax.nn.softplus(dt_raw.astype(jnp.float32) + dt_bias.astype(jnp.float32))
    A = -jnp.exp(a_log.astype(jnp.float32))
    y, new_ssm = ssd_scan(xs, dt, A, Bm, Cm, ssm_prev.astype(jnp.float32))
    y = y + d_skip.astype(jnp.float32)[:, None] * xs
    y = gated_group_rmsnorm(y.reshape(b, L, SSD_D_INNER), z, g_norm)
    return y, new_conv, new_ssm.astype(ssm_prev.dtype)


def t5_bucket(dist):
    n = jnp.maximum(dist, 0)
    large = MAX_EXACT + (jnp.log(jnp.maximum(n, 1).astype(jnp.float32) / MAX_EXACT)
                         / math.log(MAX_DISTANCE / MAX_EXACT) * (N_BUCKETS - MAX_EXACT)).astype(jnp.int32)
    large = jnp.minimum(large, N_BUCKETS - 1)
    return jnp.where(n < MAX_EXACT, n, large)


def rel_bias_logits(rel_bias, dist):
    bias = rel_bias.astype(jnp.float32)[t5_bucket(dist)]
    q_len, s_len = dist.shape
    return jnp.transpose(bias, (2, 0, 1)).reshape(N_KV_HEADS, Q_PER_KV, q_len, s_len)


def sink_attention(q, k, v, bias, mask, sinks):
    s = jnp.einsum('...qkgd,...skd->...kgqs', q, k).astype(jnp.float32) * (ATTN_HEAD_DIM ** -0.5) + bias
    s = jnp.where(mask, s, -jnp.inf)
    sk = sinks.astype(jnp.float32)[:, :, None, None]
    m = jnp.maximum(jnp.max(s, axis=-1, keepdims=True), sk)
    p = jnp.exp(s - m)
    p = p / (jnp.sum(p, axis=-1, keepdims=True) + jnp.exp(sk - m))
    return jnp.einsum('...kgqs,...skd->...qkgd', p.astype(v.dtype), v)


def swa_banded(q, k, v, sinks, rel_bias):
    b, L = q.shape[:2]
    nb = L // WINDOW
    qb = q.reshape(b, nb, WINDOW, N_KV_HEADS, Q_PER_KV, ATTN_HEAD_DIM)

    def band(t):
        tp = jnp.concatenate([jnp.zeros((b, WINDOW) + t.shape[2:], t.dtype), t], axis=1)
        tp = tp.reshape(b, nb + 1, WINDOW, N_KV_HEADS, ATTN_HEAD_DIM)
        return jnp.concatenate([tp[:, :-1], tp[:, 1:]], axis=2)

    kb, vb = band(k), band(v)
    j = jnp.arange(2 * WINDOW)
    dist = jnp.arange(WINDOW)[:, None] + WINDOW - j[None, :]
    inband = (dist >= 0) & (dist < WINDOW)
    key_pos = jnp.arange(nb)[:, None] * WINDOW - WINDOW + j[None, :]
    mask = inband[None] & (key_pos >= 0)[:, None, :]
    out = sink_attention(qb, kb, vb, rel_bias_logits(rel_bias, dist), mask[:, None, None], sinks)
    return out.reshape(b, L, ATTN_DIM)


def swa_with_buffer(q, k, v, k_buf, v_buf, sinks, rel_bias):
    b, T = q.shape[:2]
    lb = k_buf.shape[1]
    kk = jnp.concatenate([k_buf.astype(k.dtype), k], axis=1)
    vv = jnp.concatenate([v_buf.astype(v.dtype), v], axis=1)
    dist = jnp.arange(T)[:, None] + lb - jnp.arange(lb + T)[None, :]
    mask = (dist >= 0) & (dist < WINDOW)
    out = sink_attention(q, kk, vv, rel_bias_logits(rel_bias, dist), mask, sinks)
    return out.reshape(b, T, ATTN_DIM), kk[:, T:].astype(k_buf.dtype), vv[:, T:].astype(v_buf.dtype)


def memory_kv(mem, g_mem, w_ck, w_cv):
    b, m, _ = mem.shape
    mn = rmsnorm(mem, g_mem)
    return ((mn @ w_ck).reshape(b, m, CA_HEADS, CA_HEAD_DIM),
            (mn @ w_cv).reshape(b, m, CA_HEADS, CA_HEAD_DIM))


def cross_attention(hn, mem_k, mem_v, w_cq, w_co):
    b, L, _ = hn.shape
    q = (hn @ w_cq).reshape(b, L, CA_HEADS, CA_HEAD_DIM)
    s = jnp.einsum('blhd,bmhd->bhlm', q, mem_k.astype(q.dtype)).astype(jnp.float32) * (CA_HEAD_DIM ** -0.5)
    p = jax.nn.softmax(s, axis=-1).astype(q.dtype)
    o = jnp.einsum('bhlm,bmhd->blhd', p, mem_v.astype(q.dtype))
    return o.reshape(b, L, CA_DIM) @ w_co


def decoder_layer(h, mem_k, mem_v, conv_prev, ssm_prev, k_buf, v_buf, rel_bias,
                  g_mix, w_in, conv_w, conv_b, dt_bias, a_log, d_skip, g_ssd_norm, sinks, w_out,
                  g_cross, w_cq, w_co, g_ffn, w_gate, w_up, w_down):
    b, L, _ = h.shape
    xn = rmsnorm(h, g_mix)
    z, xbc, dt_raw, q, k, v = jnp.split(xn @ w_in, SPLITS, axis=-1)
    y_ssd, new_conv, new_ssm = ssd_branch(z, xbc, dt_raw, conv_prev, ssm_prev, conv_w, conv_b,
                                          dt_bias, a_log, d_skip, g_ssd_norm)
    q = q.reshape(b, L, N_KV_HEADS, Q_PER_KV, ATTN_HEAD_DIM)
    k = k.reshape(b, L, N_KV_HEADS, ATTN_HEAD_DIM)
    v = v.reshape(b, L, N_KV_HEADS, ATTN_HEAD_DIM)
    sk = sinks.reshape(N_KV_HEADS, Q_PER_KV)
    if k_buf is None:
        y_att = swa_banded(q, k, v, sk, rel_bias)
        keep = min(WINDOW, L)
        new_k, new_v = k[:, L - keep:], v[:, L - keep:]
    else:
        y_att, new_k, new_v = swa_with_buffer(q, k, v, k_buf, v_buf, sk, rel_bias)
    mix = jnp.concatenate([y_ssd.astype(h.dtype), y_att.astype(h.dtype)], axis=-1)
    h = h + mix @ w_out
    h = h + cross_attention(rmsnorm(h, g_cross), mem_k, mem_v, w_cq, w_co)
    hn = rmsnorm(h, g_ffn)
    h = h + (jax.nn.silu(hn @ w_gate) * (hn @ w_up)) @ w_down
    return h, new_conv, new_ssm, new_k, new_v


def setup_inputs(seed: int = 0) -> dict:
    key = jax.random.key(seed)
    ks = iter(jax.random.split(key, 40))
    f32 = jnp.float32

    def nrm(shape, scale):
        return jax.random.normal(next(ks), shape, f32) * scale

    def gain(shape):
        return 1.0 + nrm(shape, 0.02)

    swa_buf = min(WINDOW, PAST_LEN)
    u = jax.random.uniform(next(ks), (DEPTH, SSD_HEADS), f32)
    dt0 = jnp.exp(u * (math.log(0.1) - math.log(0.001)) + math.log(0.001))
    dt_bias = dt0 + jnp.log(-jnp.expm1(-dt0))
    a_log = jnp.log(jax.random.uniform(next(ks), (DEPTH, SSD_HEADS), f32, minval=1.0, maxval=16.0))
    return {
        'x_prompt': nrm((BATCH, SEQ, D_MODEL), 1.0),
        'x_sample': nrm((DEC_BATCH, DEC_SEQ, D_MODEL), 1.0),
        'mem_prompt': nrm((BATCH, N_MEM, D_MODEL), 1.0),
        'state_conv': nrm((DEPTH, DEC_BATCH, CONV_K - 1, CONV_DIM), 1.0),
        'state_ssm': nrm((DEPTH, DEC_BATCH, SSD_HEADS, SSD_HEAD_DIM, D_STATE), 0.1),
        'cache_swa_k': nrm((DEPTH, DEC_BATCH, swa_buf, N_KV_HEADS, ATTN_HEAD_DIM), 1.0),
        'cache_swa_v': nrm((DEPTH, DEC_BATCH, swa_buf, N_KV_HEADS, ATTN_HEAD_DIM), 1.0),
        'cache_mem_k': nrm((DEPTH, DEC_BATCH, N_MEM, CA_HEADS, CA_HEAD_DIM), 1.0),
        'cache_mem_v': nrm((DEPTH, DEC_BATCH, N_MEM, CA_HEADS, CA_HEAD_DIM), 1.0),
        'rel_bias': nrm((N_BUCKETS, N_HEADS), 0.5),
        'g_mix': gain((DEPTH, D_MODEL)),
        'w_in': nrm((DEPTH, D_MODEL, IN_DIM), D_MODEL ** -0.5),
        'conv_w': nrm((DEPTH, CONV_K, CONV_DIM), CONV_K ** -0.5),
        'conv_b': nrm((DEPTH, CONV_DIM), 0.02),
        'dt_bias': dt_bias,
        'a_log': a_log,
        'd_skip': 1.0 + nrm((DEPTH, SSD_HEADS), 0.1),
        'g_ssd_norm': gain((DEPTH, SSD_D_INNER)),
        'sinks': nrm((DEPTH, N_HEADS), 0.5),
        'w_out': nrm((DEPTH, MIX_DIM, D_MODEL), MIX_DIM ** -0.5),
        'g_cross': gain((DEPTH, D_MODEL)),
        'g_mem': gain((DEPTH, D_MODEL)),
        'w_cq': nrm((DEPTH, D_MODEL, CA_DIM), D_MODEL ** -0.5),
        'w_ck': nrm((DEPTH, D_MODEL, CA_DIM), D_MODEL ** -0.5),
        'w_cv': nrm((DEPTH, D_MODEL, CA_DIM), D_MODEL ** -0.5),
        'w_co': nrm((DEPTH, CA_DIM, D_MODEL), CA_DIM ** -0.5),
        'g_ffn': gain((DEPTH, D_MODEL)),
        'w_gate': nrm((DEPTH, D_MODEL, D_FF), D_MODEL ** -0.5),
        'w_up': nrm((DEPTH, D_MODEL, D_FF), D_MODEL ** -0.5),
        'w_down': nrm((DEPTH, D_FF, D_MODEL), D_FF ** -0.5),
        'g_final': gain((D_MODEL,)),
    }


def reference(x_prompt, x_sample, mem_prompt, state_conv, state_ssm, cache_swa_k, cache_swa_v,
              cache_mem_k, cache_mem_v, rel_bias, g_mix, w_in, conv_w, conv_b, dt_bias, a_log,
              d_skip, g_ssd_norm, sinks, w_out, g_cross, g_mem, w_cq, w_ck, w_cv, w_co, g_ffn,
              w_gate, w_up, w_down, g_final):
    bp = x_prompt.shape[0]
    hp, hs = x_prompt, x_sample
    p_conv, p_ssm, p_k, p_v, p_mk, p_mv = [], [], [], [], [], []
    s_conv, s_ssm, s_k, s_v = [], [], [], []
    for l in range(DEPTH):
        lw = (g_mix[l], w_in[l], conv_w[l], conv_b[l], dt_bias[l], a_log[l], d_skip[l], g_ssd_norm[l],
              sinks[l], w_out[l], g_cross[l], w_cq[l], w_co[l], g_ffn[l], w_gate[l], w_up[l], w_down[l])
        mk, mv = memory_kv(mem_prompt, g_mem[l], w_ck[l], w_cv[l])
        conv0 = jnp.zeros((bp, CONV_K - 1, CONV_DIM), x_prompt.dtype)
        ssm0 = jnp.zeros((bp, SSD_HEADS, SSD_HEAD_DIM, D_STATE), jnp.float32)
        hp, c_new, h_new, k_new, v_new = decoder_layer(hp, mk, mv, conv0, ssm0, None, None, rel_bias, *lw)
        p_conv.append(c_new); p_ssm.append(h_new); p_k.append(k_new); p_v.append(v_new)
        p_mk.append(mk); p_mv.append(mv)
        hs, c_new, h_new, k_new, v_new = decoder_layer(hs, cache_mem_k[l], cache_mem_v[l], state_conv[l],
                                                       state_ssm[l], cache_swa_k[l], cache_swa_v[l],
                                                       rel_bias, *lw)
        s_conv.append(c_new); s_ssm.append(h_new); s_k.append(k_new); s_v.append(v_new)
    y_prompt = rmsnorm(hp, g_final)
    y_sample = rmsnorm(hs, g_final)
    return (y_prompt, y_sample,
            jnp.stack(p_conv), jnp.stack(p_ssm), jnp.stack(p_k), jnp.stack(p_v),
            jnp.stack(p_mk), jnp.stack(p_mv),
            jnp.stack(s_conv), jnp.stack(s_ssm), jnp.stack(s_k), jnp.stack(s_v))
```

```python
import functools
import math

import numpy as np
import jax
import jax.numpy as jnp
from jax import lax
from jax.experimental import pallas as pl
from jax.experimental.pallas import tpu as pltpu

F32 = jnp.float32
BF16 = jnp.bfloat16

D_MODEL = 1024
SSD_D_INNER = 512
SSD_HEAD_DIM = 64
SSD_HEADS = 8
SSD_GROUPS = 2
GROUP_W = SSD_D_INNER // SSD_GROUPS
D_STATE = 128
CONV_K = 4
CONV_DIM = SSD_D_INNER + 2 * SSD_GROUPS * D_STATE
CHUNK = 128
ATTN_DIM = 512
ATTN_HEAD_DIM = 64
N_HEADS = 8
N_KV_HEADS = 2
Q_PER_KV = N_HEADS // N_KV_HEADS
KV_DIM = N_KV_HEADS * ATTN_HEAD_DIM
WINDOW = 128
N_BUCKETS = 32
MAX_EXACT = N_BUCKETS // 2
MAX_DISTANCE = 128
N_MEM = 256
CA_HEADS = 4
CA_HEAD_DIM = 128
CA_DIM = CA_HEADS * CA_HEAD_DIM
D_FF = 2816
EPS = 1e-6

LANES = 128
HALF = LANES // 2
DT_PAD = LANES
COL_Z = 0
COL_XBC = COL_Z + SSD_D_INNER
COL_Q = COL_XBC + CONV_DIM
COL_K = COL_Q + ATTN_DIM
COL_V = COL_K + KV_DIM
COL_DT = COL_V + KV_DIM
IN_COLS = COL_DT + DT_PAD
NEG = -1e30
SAMPLE_SEQS = 16
VMEM_LIMIT = 56 * 1024 * 1024
FF_SPLITS = ((0, 1024), (1024, 2048), (2048, D_FF))


def _rms(x, g):
    return x * lax.rsqrt(jnp.mean(x * x, axis=-1, keepdims=True) + EPS) * g


def _silu(x):
    return x * jax.nn.sigmoid(x)


def _softplus(x):
    return jnp.maximum(x, 0.0) + jnp.log1p(jnp.exp(-jnp.abs(x)))


def _dot(a, b):
    return jnp.dot(a, b, preferred_element_type=F32)


def _dot_nt(a, b):
    return lax.dot_general(a, b, (((1,), (1,)), ((), ())), preferred_element_type=F32)


def _split3(a):
    hi = a.astype(BF16)
    r = a - hi.astype(F32)
    mid = r.astype(BF16)
    lo = (r - mid.astype(F32)).astype(BF16)
    return hi, mid, lo


def _sel_left(t01, a):
    hi, mid, lo = _split3(a)
    return _dot(t01, hi) + _dot(t01, mid) + _dot(t01, lo)


def _sel_right(a, e01):
    hi, mid, lo = _split3(a)
    return _dot(hi, e01) + _dot(mid, e01) + _dot(lo, e01)


def _lane_lo():
    return lax.broadcasted_iota(jnp.int32, (1, LANES), 1) < HALF


def _t5_bucket_np(dist):
    n = np.maximum(dist, 0)
    ratio = np.log(np.maximum(n, 1).astype(np.float32) / np.float32(MAX_EXACT))
    large = MAX_EXACT + (ratio / np.float32(math.log(MAX_DISTANCE / MAX_EXACT))
                         * np.float32(N_BUCKETS - MAX_EXACT)).astype(np.int32)
    large = np.minimum(large, N_BUCKETS - 1)
    return np.where(n < MAX_EXACT, n, large).astype(np.int32)


def _tril_np(n):
    return np.tril(np.ones((n, n), np.float32))


def _expand_np():
    e = np.zeros((LANES, SSD_D_INNER), np.float32)
    for h in range(SSD_HEADS):
        e[h, h * SSD_HEAD_DIM:(h + 1) * SSD_HEAD_DIM] = 1.0
    return e


def _full_spec(shape):
    nd = len(shape)
    return pl.BlockSpec(shape, lambda *_: (0,) * nd)


def _smem_spec():
    return pl.BlockSpec(memory_space=pltpu.SMEM)


def _params(sem):
    return pltpu.CompilerParams(dimension_semantics=sem, vmem_limit_bytes=VMEM_LIMIT)


def _in_proj_body(x_ref, g_ref, w_ref, z_ref, xbc_ref, q_ref, k_ref, v_ref, dt_ref):
    xn = _rms(x_ref[...], g_ref[...]).astype(BF16)

    def seg(lo, hi):
        return _dot(xn, w_ref[:, lo:hi])

    z_ref[...] = seg(COL_Z, COL_XBC)
    xbc_ref[...] = seg(COL_XBC, COL_Q)
    q_ref[...] = (seg(COL_Q, COL_K) * (ATTN_HEAD_DIM ** -0.5)).astype(BF16)
    k_ref[...] = seg(COL_K, COL_V)
    v_ref[...] = seg(COL_V, COL_DT)
    dt_ref[...] = seg(COL_DT, IN_COLS)


def _in_proj(x2, g_mix, w_in_r, tm):
    t = x2.shape[0]
    row = lambda w: pl.BlockSpec((tm, w), lambda i: (i, 0))
    outs = [(SSD_D_INNER, F32), (CONV_DIM, F32), (ATTN_DIM, BF16), (KV_DIM, F32), (KV_DIM, F32), (DT_PAD, F32)]
    return pl.pallas_call(
        _in_proj_body,
        grid=(t // tm,),
        in_specs=[row(D_MODEL), _full_spec((1, D_MODEL)), _full_spec((D_MODEL, IN_COLS))],
        out_specs=[row(w) for w, _ in outs],
        out_shape=[jax.ShapeDtypeStruct((t, w), d) for w, d in outs],
        compiler_params=_params(("parallel",)),
        name="in_proj",
    )(x2, g_mix, w_in_r)


def _conv_taps(cw_ref, cb_ref, taps):
    acc = cb_ref[...] + taps[0] * cw_ref[0:1, :]
    for k in range(1, CONV_K):
        acc = acc + taps[k] * cw_ref[k:k + 1, :]
    return _silu(acc)


def _ssd_prepare(conv, dt_raw, dtb_ref, alog_ref, tcum, total_fn, extra_fn, expand):
    xs = conv[:, :SSD_D_INNER]
    bm = conv[:, SSD_D_INNER:SSD_D_INNER + SSD_GROUPS * D_STATE]
    cm = conv[:, SSD_D_INNER + SSD_GROUPS * D_STATE:]
    dt = _softplus(dt_raw + dtb_ref[...])
    a = dt * (-jnp.exp(alog_ref[...]))
    cs = _sel_left(tcum, a)
    total = total_fn(cs, a)
    pieces = [dt, jnp.exp(total - cs), jnp.exp(cs)] + extra_fn(total)
    rows = cs.shape[0]
    ex = _sel_right(jnp.concatenate(pieces, axis=0), expand)
    ex = [ex[i * rows:(i + 1) * rows] for i in range(len(pieces))]
    return xs, bm, cm, cs, cs.T, ex


def _ssd_diag(cs, cs_t, cb_g, xdt, mask, g):
    lo = _lane_lo()
    out = []
    for pr in range(2):
        h0 = g * 4 + 2 * pr
        xp = xdt[:, (h0 // 2) * LANES:(h0 // 2 + 1) * LANES]
        x_lo = jnp.where(lo, xp, 0.0).astype(BF16)
        x_hi = jnp.where(lo, 0.0, xp).astype(BF16)
        acc = None
        for h, xh in ((h0, x_lo), (h0 + 1, x_hi)):
            diff = cs[:, h:h + 1] - cs_t[h:h + 1, :]
            decay = jnp.exp(jnp.where(mask, diff, -jnp.inf))
            part = _dot((cb_g * decay).astype(BF16), xh)
            acc = part if acc is None else acc + part
        out.append(acc)
    return jnp.concatenate(out, axis=-1)


def _gated_norm(y, z, gn_ref):
    yf = y * _silu(z)
    parts = []
    for g in range(SSD_GROUPS):
        yg = yf[:, g * GROUP_W:(g + 1) * GROUP_W]
        parts.append(yg * lax.rsqrt(jnp.mean(yg * yg, axis=-1, keepdims=True) + EPS))
    return jnp.concatenate(parts, axis=-1) * gn_ref[...]


def _build_bias(bias_ref, bucket_of, relb_ref, n_tables):
    for i in range(n_tables):
        for h in range(N_HEADS):
            bias_ref[i, h] = jnp.full(bias_ref.shape[2:], NEG, F32)

    def body(t, carry):
        for i in range(n_tables):
            hit = bucket_of(i) == t
            for h in range(N_HEADS):
                bias_ref[i, h] = jnp.where(hit, relb_ref[t, h], bias_ref[i, h])
        return carry

    lax.fori_loop(0, N_BUCKETS, body, 0)


def _prompt_mixer_body(z_ref, xbc_ref, q_ref, k_ref, v_ref, dt_ref,
                       cw_ref, cb_ref, dtb_ref, alog_ref, dskip_ref, gn_ref,
                       tril_ref, expand_ref, bucket_ref, relb_ref, sink_ref,
                       mix_ref, conv_out, ssm_out, k_out, v_out,
                       xpad, state_t, kbuf, vbuf, bias):
    b = pl.program_id(0)
    c = pl.program_id(1)
    last = pl.num_programs(1) - 1
    q_rows = CHUNK

    @pl.when((b == 0) & (c == 0))
    def _():
        _build_bias(bias, lambda i: bucket_ref[i], relb_ref, 2)

    @pl.when(c == 0)
    def _():
        xpad[0:8, :] = jnp.zeros((8, CONV_DIM), F32)
        state_t[...] = jnp.zeros_like(state_t)
        kbuf[...] = jnp.zeros_like(kbuf)
        vbuf[...] = jnp.zeros_like(vbuf)

    xbc = xbc_ref[0]
    xpad[8:8 + q_rows, :] = xbc
    taps = [xpad[5 + k:5 + k + q_rows, :] for k in range(CONV_K - 1)] + [xbc]
    conv = _conv_taps(cw_ref, cb_ref, taps)
    xpad[5:8, :] = xbc[q_rows - 3:q_rows, :]

    row = lax.broadcasted_iota(jnp.int32, (q_rows, q_rows), 0)
    col = lax.broadcasted_iota(jnp.int32, (q_rows, q_rows), 1)
    causal = col <= row
    xs, bm, cm, cs, cs_t, (dt_x, dend_x, ecs_x) = _ssd_prepare(
        conv, dt_ref[0], dtb_ref, alog_ref, tril_ref[...],
        lambda cs_, a_: cs_[q_rows - 1:q_rows, :], lambda total: [], expand_ref[...])
    xdt = xs * dt_x
    xde = (xdt * dend_x).astype(BF16)
    bb = bm.astype(BF16)
    cb = cm.astype(BF16)
    y_parts = []
    for g in range(SSD_GROUPS):
        gs = slice(g * GROUP_W, (g + 1) * GROUP_W)
        ns = slice(g * D_STATE, (g + 1) * D_STATE)
        cb_g = _dot_nt(cb[:, ns], bb[:, ns])
        y_diag = _ssd_diag(cs, cs_t, cb_g, xdt, causal, g)
        st_old = state_t[:, gs]
        y_off = _dot(cb[:, ns], st_old.astype(BF16)) * ecs_x[:, gs]
        y_parts.append(y_diag + y_off)
        b_t = bm[:, ns].T.astype(BF16)
        state_t[:, gs] = ecs_x[q_rows - 1:q_rows, gs] * st_old + _dot(b_t, xde[:, gs])
    y = jnp.concatenate(y_parts, axis=-1) + dskip_ref[...] * xs
    mix_ref[0, :, 0:SSD_D_INNER] = _gated_norm(y, z_ref[0], gn_ref).astype(BF16)

    lo = _lane_lo()
    k_cur = k_ref[0]
    v_cur = v_ref[0]
    kbuf[0, q_rows:, :] = k_cur.astype(BF16)
    kbuf[1, q_rows:, :] = pltpu.roll(k_cur, HALF, 1).astype(BF16)
    vbuf[0, q_rows:, :] = v_cur.astype(BF16)
    vbuf[1, q_rows:, :] = pltpu.roll(v_cur, HALF, 1).astype(BF16)
    table = jnp.minimum(c, 1)
    for pair in range(N_HEADS // 2):
        j = (2 * pair) // Q_PER_KV
        q_pair = q_ref[0, :, pair * LANES:(pair + 1) * LANES]
        acc = None
        for par in range(2):
            h = 2 * pair + par
            qm = jnp.where(lo, q_pair, 0.0) if par == 0 else jnp.where(lo, 0.0, q_pair)
            variant = (j + par) % 2
            s = _dot_nt(qm.astype(BF16), kbuf[variant]) + bias[table, h]
            sink = sink_ref[h]
            m = jnp.maximum(jnp.max(s, axis=-1, keepdims=True), sink)
            p = jnp.exp(s - m)
            denom = jnp.sum(p, axis=-1, keepdims=True) + jnp.exp(sink - m)
            vv = vbuf[variant]
            vm = jnp.where(lo, vv, 0.0) if par == 0 else jnp.where(lo, 0.0, vv)
            part = _dot(p.astype(BF16), vm.astype(BF16)) / denom
            acc = part if acc is None else acc + part
        mix_ref[0, :, SSD_D_INNER + pair * LANES:SSD_D_INNER + (pair + 1) * LANES] = acc.astype(BF16)
    kbuf[:, 0:q_rows, :] = kbuf[:, q_rows:, :]
    vbuf[:, 0:q_rows, :] = vbuf[:, q_rows:, :]

    @pl.when(c == last)
    def _():
        conv_out[0] = xbc[q_rows - 3:q_rows, :]
        ssm_out[0] = state_t[...].T
        k_out[0] = k_cur
        v_out[0] = v_cur


def _prompt_mixer(z, xbc, q, k, v, dt, small, consts, batch, seq):
    cw, cb, dtb, alog, dskip_x, gn, relb, sinks = small
    tril, expand, buckets = consts
    nc = seq // CHUNK
    r3 = lambda a: a.reshape(batch, seq, a.shape[-1])
    tok = lambda w: pl.BlockSpec((1, CHUNK, w), lambda b, c: (b, c, 0))
    per_b = lambda s: pl.BlockSpec((1,) + s, lambda b, c: (b,) + (0,) * len(s))
    out_shape = [
        jax.ShapeDtypeStruct((batch, seq, D_MODEL), BF16),
        jax.ShapeDtypeStruct((batch, CONV_K - 1, CONV_DIM), F32),
        jax.ShapeDtypeStruct((batch, SSD_D_INNER, D_STATE), F32),
        jax.ShapeDtypeStruct((batch, WINDOW, KV_DIM), F32),
        jax.ShapeDtypeStruct((batch, WINDOW, KV_DIM), F32),
    ]
    return pl.pallas_call(
        _prompt_mixer_body,
        grid=(batch, nc),
        in_specs=[tok(SSD_D_INNER), tok(CONV_DIM), tok(ATTN_DIM), tok(KV_DIM), tok(KV_DIM), tok(DT_PAD),
                  _full_spec(cw.shape), _full_spec(cb.shape), _full_spec(dtb.shape), _full_spec(alog.shape),
                  _full_spec(dskip_x.shape), _full_spec(gn.shape),
                  _full_spec(tril.shape), _full_spec(expand.shape), _full_spec(buckets.shape),
                  _smem_spec(), _smem_spec()],
        out_specs=[tok(D_MODEL), per_b((CONV_K - 1, CONV_DIM)), per_b((SSD_D_INNER, D_STATE)),
                   per_b((WINDOW, KV_DIM)), per_b((WINDOW, KV_DIM))],
        out_shape=out_shape,
        scratch_shapes=[
            pltpu.VMEM((8 + CHUNK, CONV_DIM), F32),
            pltpu.VMEM((D_STATE, SSD_D_INNER), F32),
            pltpu.VMEM((2, 2 * WINDOW, KV_DIM), BF16),
            pltpu.VMEM((2, 2 * WINDOW, KV_DIM), BF16),
            pltpu.VMEM((2, N_HEADS, WINDOW, 2 * WINDOW), F32),
        ],
        compiler_params=_params(("arbitrary", "arbitrary")),
        name="prompt_mixer",
    )(r3(z), r3(xbc), r3(q), r3(k), r3(v), r3(dt), cw, cb, dtb, alog, dskip_x, gn,
      tril, expand, buckets, relb, sinks)


def _sample_mixer_body(z_ref, xbc_ref, q_ref, k_ref, v_ref, dt_ref,
                       sconv_ref, sssm_ref, ck_ref, cv_ref,
                       cw_ref, cb_ref, dtb_ref, alog_ref, dskip_ref, gn_ref,
                       tcum_ref, tseq_ref, expand_ref, bkt_c_ref, bkt_n_ref, relb_ref, sink_ref,
                       mix_ref, conv_out, ssm_out, k_out, v_out,
                       xpad, bias_c, bias_n, *, dec_seq):
    step = pl.program_id(0)
    n_seq = SAMPLE_SEQS
    rows = n_seq * dec_seq

    @pl.when(step == 0)
    def _():
        _build_bias(bias_c, lambda i: bkt_c_ref[...], relb_ref, 1)
        _build_bias(bias_n, lambda i: bkt_n_ref[...], relb_ref, 1)

    xbc = xbc_ref[...]
    xpad[:, 8:8 + dec_seq, :] = xbc.reshape(n_seq, dec_seq, CONV_DIM)
    xpad[:, 5:8, :] = sconv_ref[...]
    taps = [xpad[:, 5 + k:5 + k + dec_seq, :].reshape(rows, CONV_DIM) for k in range(CONV_K - 1)] + [xbc]
    conv = _conv_taps(cw_ref, cb_ref, taps)
    conv_out[...] = xpad[:, 5 + dec_seq:8 + dec_seq, :]

    row = lax.broadcasted_iota(jnp.int32, (rows, rows), 0)
    col = lax.broadcasted_iota(jnp.int32, (rows, rows), 1)
    tseq = tseq_ref[...]
    same_seq = tseq > 0
    causal = same_seq & (col <= row)
    xs, bm, cm, cs, cs_t, (dt_x, dend_x, ecs_x, seqdec_x) = _ssd_prepare(
        conv, dt_ref[...], dtb_ref, alog_ref, tcum_ref[...],
        lambda cs_, a_: _sel_left(tseq, a_), lambda total: [jnp.exp(total)], expand_ref[...])
    xdt = xs * dt_x
    xde = (xdt * dend_x).astype(BF16)
    bb = bm.astype(BF16)
    cb = cm.astype(BF16)
    seq_of_row = lax.broadcasted_iota(jnp.int32, (rows, 1), 0) // dec_seq
    seq_of_lane = lax.broadcasted_iota(jnp.int32, (1, rows), 1) // dec_seq
    b_t = [bm[:, g * D_STATE:(g + 1) * D_STATE].T for g in range(SSD_GROUPS)]
    y_off = [None] * SSD_GROUPS
    for i in range(n_seq):
        st_t = sssm_ref[i].T
        new_parts = []
        for g in range(SSD_GROUPS):
            gs = slice(g * GROUP_W, (g + 1) * GROUP_W)
            ns = slice(g * D_STATE, (g + 1) * D_STATE)
            c_i = jnp.where(seq_of_row == i, cm[:, ns], 0.0).astype(BF16)
            part = _dot(c_i, st_t[:, gs].astype(BF16))
            y_off[g] = part if y_off[g] is None else y_off[g] + part
            b_i = jnp.where(seq_of_lane == i, b_t[g], 0.0).astype(BF16)
            dec = seqdec_x[i * dec_seq:i * dec_seq + 1, gs]
            new_parts.append(dec * st_t[:, gs] + _dot(b_i, xde[:, gs]))
        ssm_out[i] = jnp.concatenate(new_parts, axis=-1).T
    y_parts = []
    for g in range(SSD_GROUPS):
        gs = slice(g * GROUP_W, (g + 1) * GROUP_W)
        ns = slice(g * D_STATE, (g + 1) * D_STATE)
        cb_g = _dot_nt(cb[:, ns], bb[:, ns])
        y_parts.append(_ssd_diag(cs, cs_t, cb_g, xdt, causal, g) + y_off[g] * ecs_x[:, gs])
    y = jnp.concatenate(y_parts, axis=-1) + dskip_ref[...] * xs
    mix_ref[:, 0:SSD_D_INNER] = _gated_norm(y, z_ref[...], gn_ref).astype(BF16)

    lo = _lane_lo()
    k_new = k_ref[...]
    v_new = v_ref[...]
    k_var = [k_new.astype(BF16), pltpu.roll(k_new, HALF, 1).astype(BF16)]
    v_new_r = pltpu.roll(v_new, HALF, 1)
    v_dup = [jnp.where(lo, v_new, v_new_r).astype(BF16), jnp.where(lo, v_new_r, v_new).astype(BF16)]
    qf = q_ref[...].astype(F32)
    q_masked = []
    s_new = []
    for h in range(N_HEADS):
        pair, par = h // 2, h % 2
        j = h // Q_PER_KV
        q_pair = qf[:, pair * LANES:(pair + 1) * LANES]
        qm = jnp.where(lo, q_pair, 0.0) if par == 0 else jnp.where(lo, 0.0, q_pair)
        q_masked.append(qm)
        s_new.append(_dot_nt(qm.astype(BF16), k_var[(j + par) % 2]) + bias_n[0, h])
    stack_rows = lax.broadcasted_iota(jnp.int32, (Q_PER_KV * dec_seq, 1), 0) // dec_seq
    att_rows = []
    for i in range(n_seq):
        rs = slice(i * dec_seq, (i + 1) * dec_seq)
        kc = ck_ref[i]
        vc = cv_ref[i]
        kc_r = pltpu.roll(kc, HALF, 1)
        vc_r = pltpu.roll(vc, HALF, 1)
        pieces = []
        for j in range(N_KV_HEADS):
            heads = range(j * Q_PER_KV, (j + 1) * Q_PER_KV)
            kdup = (jnp.where(lo, kc, kc_r) if j == 0 else jnp.where(lo, kc_r, kc)).astype(BF16)
            vdup = (jnp.where(lo, vc, vc_r) if j == 0 else jnp.where(lo, vc_r, vc)).astype(BF16)
            qs = jnp.concatenate([q_masked[h][rs] for h in heads], axis=0).astype(BF16)
            sc = _dot_nt(qs, kdup) + jnp.concatenate([bias_c[0, h] for h in heads], axis=0)
            sn = jnp.concatenate([s_new[h][rs] for h in heads], axis=0)
            sink = jnp.zeros((Q_PER_KV * dec_seq, 1), F32)
            for hh, h in enumerate(heads):
                sink = jnp.where(stack_rows == hh, sink_ref[h], sink)
            m = jnp.maximum(jnp.maximum(jnp.max(sc, axis=-1, keepdims=True),
                                        jnp.max(sn, axis=-1, keepdims=True)), sink)
            pc = jnp.exp(sc - m)
            pn = jnp.exp(sn - m)
            denom = (jnp.sum(pc, axis=-1, keepdims=True) + jnp.sum(pn, axis=-1, keepdims=True)
                     + jnp.exp(sink - m))
            o = (_dot(pc.astype(BF16), vdup) + _dot(pn.astype(BF16), v_dup[j])) / denom
            for pr in range(Q_PER_KV // 2):
                even = o[(2 * pr) * dec_seq:(2 * pr + 1) * dec_seq]
                odd = o[(2 * pr + 1) * dec_seq:(2 * pr + 2) * dec_seq]
                pieces.append(jnp.where(lo, even, odd))
        att_rows.append(jnp.concatenate(pieces, axis=-1))
        keep = WINDOW - dec_seq
        k_out[i] = jnp.concatenate([kc[dec_seq:, :], k_new[rs]], axis=0) if keep else k_new[rs]
        v_out[i] = jnp.concatenate([vc[dec_seq:, :], v_new[rs]], axis=0) if keep else v_new[rs]
    mix_ref[:, SSD_D_INNER:] = jnp.concatenate(att_rows, axis=0).astype(BF16)


def _sample_mixer(z, xbc, q, k, v, dt, sconv, sssm, ck, cv, small, consts, n_seq_total, dec_seq):
    cw, cb, dtb, alog, dskip_x, gn, relb, sinks = small
    tcum, tseq, expand, bkt_c, bkt_n = consts
    rows = SAMPLE_SEQS * dec_seq
    tok = lambda w: pl.BlockSpec((rows, w), lambda i: (i, 0))
    per_s = lambda s: pl.BlockSpec((SAMPLE_SEQS,) + s, lambda i: (i,) + (0,) * len(s))
    out_shape = [
        jax.ShapeDtypeStruct((n_seq_total * dec_seq, D_MODEL), BF16),
        jax.ShapeDtypeStruct((n_seq_total, CONV_K - 1, CONV_DIM), F32),
        jax.ShapeDtypeStruct((n_seq_total, SSD_D_INNER, D_STATE), F32),
        jax.ShapeDtypeStruct((n_seq_total, WINDOW, KV_DIM), F32),
        jax.ShapeDtypeStruct((n_seq_total, WINDOW, KV_DIM), F32),
    ]
    return pl.pallas_call(
        functools.partial(_sample_mixer_body, dec_seq=dec_seq),
        grid=(n_seq_total // SAMPLE_SEQS,),
        in_specs=[tok(SSD_D_INNER), tok(CONV_DIM), tok(ATTN_DIM), tok(KV_DIM), tok(KV_DIM), tok(DT_PAD),
                  per_s((CONV_K - 1, CONV_DIM)), per_s((SSD_D_INNER, D_STATE)),
                  per_s((WINDOW, KV_DIM)), per_s((WINDOW, KV_DIM)),
                  _full_spec(cw.shape), _full_spec(cb.shape), _full_spec(dtb.shape), _full_spec(alog.shape),
                  _full_spec(dskip_x.shape), _full_spec(gn.shape),
                  _full_spec(tcum.shape), _full_spec(tseq.shape), _full_spec(expand.shape),
                  _full_spec(bkt_c.shape), _full_spec(bkt_n.shape), _smem_spec(), _smem_spec()],
        out_specs=[tok(D_MODEL), per_s((CONV_K - 1, CONV_DIM)), per_s((SSD_D_INNER, D_STATE)),
                   per_s((WINDOW, KV_DIM)), per_s((WINDOW, KV_DIM))],
        out_shape=out_shape,
        scratch_shapes=[
            pltpu.VMEM((SAMPLE_SEQS, 8 + dec_seq, CONV_DIM), F32),
            pltpu.VMEM((1, N_HEADS, dec_seq, WINDOW), F32),
            pltpu.VMEM((1, N_HEADS, rows, rows), F32),
        ],
        compiler_params=_params(("arbitrary",)),
        name="sample_mixer",
    )(z, xbc, q, k, v, dt, sconv, sssm, ck, cv, cw, cb, dtb, alog, dskip_x, gn,
      tcum, tseq, expand, bkt_c, bkt_n, relb, sinks)


def _post1_body(x_ref, mix_ref, wout_ref, gc_ref, wcq_ref, h_ref, qc_ref):
    h = x_ref[...] + _dot(mix_ref[...], wout_ref[...])
    h_ref[...] = h
    qc_ref[...] = _dot(_rms(h, gc_ref[...]).astype(BF16), wcq_ref[...]).astype(BF16)


def _post1(x2, mix, w_out, g_cross, w_cq, tm):
    t = x2.shape[0]
    row = lambda w: pl.BlockSpec((tm, w), lambda i: (i, 0))
    return pl.pallas_call(
        _post1_body,
        grid=(t // tm,),
        in_specs=[row(D_MODEL), row(D_MODEL), _full_spec(w_out.shape), _full_spec(g_cross.shape),
                  _full_spec(w_cq.shape)],
        out_specs=[row(D_MODEL), row(CA_DIM)],
        out_shape=[jax.ShapeDtypeStruct((t, D_MODEL), F32), jax.ShapeDtypeStruct((t, CA_DIM), BF16)],
        compiler_params=_params(("parallel",)),
        name="out_proj",
    )(x2, mix, w_out, g_cross, w_cq)


def _mem_kv_body(mem_ref, g_ref, wk_ref, wv_ref, k_ref, v_ref):
    mn = _rms(mem_ref[...], g_ref[...]).astype(BF16)
    k_ref[...] = _dot(mn, wk_ref[...])
    v_ref[...] = _dot(mn, wv_ref[...])


def _mem_kv(mem2, g_mem, w_ck, w_cv, tm):
    t = mem2.shape[0]
    row = lambda w: pl.BlockSpec((tm, w), lambda i: (i, 0))
    return pl.pallas_call(
        _mem_kv_body,
        grid=(t // tm,),
        in_specs=[row(D_MODEL), _full_spec(g_mem.shape), _full_spec(w_ck.shape), _full_spec(w_cv.shape)],
        out_specs=[row(CA_DIM), row(CA_DIM)],
        out_shape=[jax.ShapeDtypeStruct((t, CA_DIM), F32)] * 2,
        compiler_params=_params(("parallel",)),
        name="mem_kv",
    )(mem2, g_mem, w_ck, w_cv)


def _cross_heads(q, k, v):
    out = []
    for h in range(CA_HEADS):
        hs = slice(h * CA_HEAD_DIM, (h + 1) * CA_HEAD_DIM)
        s = _dot_nt(q[:, hs], k[:, hs]) * (CA_HEAD_DIM ** -0.5)
        m = jnp.max(s, axis=-1, keepdims=True)
        p = jnp.exp(s - m)
        out.append(_dot(p.astype(BF16), v[:, hs]) / jnp.sum(p, axis=-1, keepdims=True))
    return jnp.concatenate(out, axis=-1)


def _cross_prompt_body(q_ref, k_ref, v_ref, o_ref):
    o_ref[0] = _cross_heads(q_ref[0], k_ref[0].astype(BF16), v_ref[0].astype(BF16)).astype(BF16)


def _cross_prompt(qc, mk, mv, batch, seq, tm):
    q3 = qc.reshape(batch, seq, CA_DIM)
    tok = pl.BlockSpec((1, tm, CA_DIM), lambda b, i: (b, i, 0))
    mem = pl.BlockSpec((1, N_MEM, CA_DIM), lambda b, i: (b, 0, 0))
    return pl.pallas_call(
        _cross_prompt_body,
        grid=(batch, seq // tm),
        in_specs=[tok, mem, mem],
        out_specs=tok,
        out_shape=jax.ShapeDtypeStruct((batch, seq, CA_DIM), BF16),
        compiler_params=_params(("parallel", "parallel")),
        name="cross_prompt",
    )(q3, mk, mv).reshape(batch * seq, CA_DIM)


def _cross_sample_body(q_ref, k_ref, v_ref, o_ref, *, n_seq, dec_seq):
    q = q_ref[...].astype(F32)
    rows = []
    for i in range(n_seq):
        qi = q[i * dec_seq:(i + 1) * dec_seq].astype(BF16)
        rows.append(_cross_heads(qi, k_ref[i].astype(BF16), v_ref[i].astype(BF16)))
    o_ref[...] = jnp.concatenate(rows, axis=0).astype(BF16)


def _cross_sample(qc, ck, cv, n_seq_total, dec_seq, n_seq):
    rows = n_seq * dec_seq
    tok = pl.BlockSpec((rows, CA_DIM), lambda i: (i, 0))
    mem = pl.BlockSpec((n_seq, N_MEM, CA_DIM), lambda i: (i, 0, 0))
    return pl.pallas_call(
        functools.partial(_cross_sample_body, n_seq=n_seq, dec_seq=dec_seq),
        grid=(n_seq_total // n_seq,),
        in_specs=[tok, mem, mem],
        out_specs=tok,
        out_shape=jax.ShapeDtypeStruct((n_seq_total * dec_seq, CA_DIM), BF16),
        compiler_params=_params(("parallel",)),
        name="cross_sample",
    )(qc, ck, cv)


def _post2_body(h_ref, o_ref, wco_ref, gf_ref, wg_ref, wu_ref, wd_ref, gfin_ref, y_ref):
    h = h_ref[...] + _dot(o_ref[...], wco_ref[...])
    hn = _rms(h, gf_ref[...]).astype(BF16)
    acc = h
    for lo, hi in FF_SPLITS:
        act = _silu(_dot(hn, wg_ref[:, lo:hi])) * _dot(hn, wu_ref[:, lo:hi])
        acc = acc + _dot(act.astype(BF16), wd_ref[lo:hi, :])
    y_ref[...] = _rms(acc, gfin_ref[...])


def _post2(h1, o, w_co, g_ffn, w_gate, w_up, w_down, g_final, tm):
    t = h1.shape[0]
    row = lambda w: pl.BlockSpec((tm, w), lambda i: (i, 0))
    return pl.pallas_call(
        _post2_body,
        grid=(t // tm,),
        in_specs=[row(D_MODEL), row(CA_DIM), _full_spec(w_co.shape), _full_spec(g_ffn.shape),
                  _full_spec(w_gate.shape), _full_spec(w_up.shape), _full_spec(w_down.shape),
                  _full_spec(g_final.shape)],
        out_specs=row(D_MODEL),
        out_shape=jax.ShapeDtypeStruct((t, D_MODEL), F32),
        compiler_params=_params(("parallel",)),
        name="ffn",
    )(h1, o, w_co, g_ffn, w_gate, w_up, w_down, g_final)


def _prompt_consts():
    qi = np.arange(WINDOW)[:, None]
    ji = np.arange(2 * WINDOW)[None, :]
    dist = qi + WINDOW - ji
    inband = (dist >= 0) & (dist < WINDOW)
    bucket = np.where(inband, _t5_bucket_np(dist), -1)
    first = np.where(ji >= WINDOW, bucket, -1)
    buckets = np.stack([first, bucket]).astype(np.int32)
    return (jnp.asarray(_tril_np(CHUNK), BF16), jnp.asarray(_expand_np(), BF16), jnp.asarray(buckets))


def _sample_consts(dec_seq, cache_len):
    rows = SAMPLE_SEQS * dec_seq
    r = np.arange(rows)
    same = (r[:, None] // dec_seq) == (r[None, :] // dec_seq)
    tcum = (same & (r[None, :] <= r[:, None])).astype(np.float32)
    tseq = same.astype(np.float32)
    t = np.arange(dec_seq)[:, None]
    j = np.arange(cache_len)[None, :]
    dist_c = t + cache_len - j
    bkt_c = np.where((dist_c >= 0) & (dist_c < WINDOW), _t5_bucket_np(dist_c), -1).astype(np.int32)
    dist_n = (r[:, None] % dec_seq) - (r[None, :] % dec_seq)
    ok = same & (dist_n >= 0) & (dist_n < WINDOW)
    bkt_n = np.where(ok, _t5_bucket_np(dist_n), -1).astype(np.int32)
    return (jnp.asarray(tcum, BF16), jnp.asarray(tseq, BF16), jnp.asarray(_expand_np(), BF16),
            jnp.asarray(bkt_c), jnp.asarray(bkt_n))


def _pick_tile(t, pref):
    tm = min(t, pref)
    while t % tm:
        tm //= 2
    return tm


def kernel(x_prompt, x_sample, mem_prompt, state_conv, state_ssm, cache_swa_k, cache_swa_v, cache_mem_k, cache_mem_v, rel_bias, g_mix, w_in, conv_w, conv_b, dt_bias, a_log, d_skip, g_ssd_norm, sinks, w_out, g_cross, g_mem, w_cq, w_ck, w_cv, w_co, g_ffn, w_gate, w_up, w_down, g_final):
    assert g_mix.shape[0] == 1, "single-layer trunk"
    batch, seq, _ = x_prompt.shape
    n_dec, dec_seq, _ = x_sample.shape
    cache_len = cache_swa_k.shape[2]
    assert seq % CHUNK == 0 and cache_len == WINDOW and n_dec % SAMPLE_SEQS == 0 and dec_seq == 8

    wi = w_in[0]
    s0, s1, s2, s3, s4 = (SSD_D_INNER, SSD_D_INNER + CONV_DIM, SSD_D_INNER + CONV_DIM + SSD_HEADS,
                          SSD_D_INNER + CONV_DIM + SSD_HEADS + ATTN_DIM,
                          SSD_D_INNER + CONV_DIM + SSD_HEADS + ATTN_DIM + KV_DIM)
    w_in_r = jnp.concatenate(
        [wi[:, :s0], wi[:, s0:s1], wi[:, s2:s3], wi[:, s3:s4], wi[:, s4:],
         jnp.pad(wi[:, s1:s2], ((0, 0), (0, DT_PAD - SSD_HEADS)))], axis=1).astype(BF16)
    row = lambda a: a.reshape(1, -1).astype(F32)
    pad_h = lambda a: jnp.pad(a.reshape(1, -1).astype(F32), ((0, 0), (0, DT_PAD - SSD_HEADS)))
    small = (conv_w[0].astype(F32), row(conv_b[0]), pad_h(dt_bias[0]), pad_h(a_log[0]),
             jnp.repeat(d_skip[0].astype(F32), SSD_HEAD_DIM).reshape(1, -1), row(g_ssd_norm[0]),
             rel_bias.astype(F32), sinks[0].astype(F32))
    bf = lambda w: w[0].astype(BF16)
    w_out_b, w_cq_b, w_ck_b, w_cv_b, w_co_b = bf(w_out), bf(w_cq), bf(w_ck), bf(w_cv), bf(w_co)
    w_gate_b, w_up_b, w_down_b = bf(w_gate), bf(w_up), bf(w_down)
    g_mix_r, g_cross_r, g_mem_r, g_ffn_r, g_fin_r = row(g_mix[0]), row(g_cross[0]), row(g_mem[0]), row(g_ffn[0]), row(g_final)

    tp = batch * seq
    xp2 = x_prompt.reshape(tp, D_MODEL)
    tm_p = _pick_tile(seq, 512)
    z, xbc, q, k, v, dt = _in_proj(xp2, g_mix_r, w_in_r, tm_p)
    mix, p_conv, p_ssm, p_k, p_v = _prompt_mixer(z, xbc, q, k, v, dt, small, _prompt_consts(), batch, seq)
    mem2 = mem_prompt.reshape(batch * N_MEM, D_MODEL)
    mk, mv = _mem_kv(mem2, g_mem_r, w_ck_b, w_cv_b, _pick_tile(batch * N_MEM, 512))
    h1, qc = _post1(xp2, mix.reshape(tp, D_MODEL), w_out_b, g_cross_r, w_cq_b, tm_p)
    o = _cross_prompt(qc, mk.reshape(batch, N_MEM, CA_DIM), mv.reshape(batch, N_MEM, CA_DIM), batch, seq, tm_p)
    y_prompt = _post2(h1, o, w_co_b, g_ffn_r, w_gate_b, w_up_b, w_down_b, g_fin_r, tm_p)

    ts = n_dec * dec_seq
    xs2 = x_sample.reshape(ts, D_MODEL)
    tm_s = _pick_tile(ts, 512)
    z, xbc, q, k, v, dt = _in_proj(xs2, g_mix_r, w_in_r, tm_s)
    mix_s, s_conv, s_ssm, s_k, s_v = _sample_mixer(
        z, xbc, q, k, v, dt, state_conv[0], state_ssm[0].reshape(n_dec, SSD_D_INNER, D_STATE),
        cache_swa_k[0].reshape(n_dec, cache_len, KV_DIM), cache_swa_v[0].reshape(n_dec, cache_len, KV_DIM),
        small, _sample_consts(dec_seq, cache_len), n_dec, dec_seq)
    h1s, qcs = _post1(xs2, mix_s, w_out_b, g_cross_r, w_cq_b, tm_s)
    os_ = _cross_sample(qcs, cache_mem_k[0].reshape(n_dec, N_MEM, CA_DIM),
                        cache_mem_v[0].reshape(n_dec, N_MEM, CA_DIM), n_dec, dec_seq, 8)
    y_sample = _post2(h1s, os_, w_co_b, g_ffn_r, w_gate_b, w_up_b, w_down_b, g_fin_r, tm_s)

    return (y_prompt.reshape(batch, seq, D_MODEL), y_sample.reshape(n_dec, dec_seq, D_MODEL),
            p_conv[None], p_ssm.reshape(1, batch, SSD_HEADS, SSD_HEAD_DIM, D_STATE),
            p_k.reshape(1, batch, WINDOW, N_KV_HEADS, ATTN_HEAD_DIM),
            p_v.reshape(1, batch, WINDOW, N_KV_HEADS, ATTN_HEAD_DIM),
            mk.reshape(1, batch, N_MEM, CA_HEADS, CA_HEAD_DIM), mv.reshape(1, batch, N_MEM, CA_HEADS, CA_HEAD_DIM),
            s_conv[None], s_ssm.reshape(1, n_dec, SSD_HEADS, SSD_HEAD_DIM, D_STATE),
            s_k.reshape(1, n_dec, cache_len, N_KV_HEADS, ATTN_HEAD_DIM),
            s_v.reshape(1, n_dec, cache_len, N_KV_HEADS, ATTN_HEAD_DIM))
```

```python
import functools
import math

import numpy as np
import jax
import jax.numpy as jnp
from jax import lax
from jax.experimental import pallas as pl
from jax.experimental.pallas import tpu as pltpu

F32 = jnp.float32
BF16 = jnp.bfloat16

D_MODEL = 1024
SSD_D_INNER = 512
SSD_HEAD_DIM = 64
SSD_HEADS = 8
SSD_GROUPS = 2
GROUP_W = SSD_D_INNER // SSD_GROUPS
D_STATE = 128
CONV_K = 4
CONV_DIM = SSD_D_INNER + 2 * SSD_GROUPS * D_STATE
CHUNK = 128
ATTN_DIM = 512
ATTN_HEAD_DIM = 64
N_HEADS = 8
N_KV_HEADS = 2
Q_PER_KV = N_HEADS // N_KV_HEADS
KV_DIM = N_KV_HEADS * ATTN_HEAD_DIM
WINDOW = 128
N_BUCKETS = 32
MAX_EXACT = N_BUCKETS // 2
MAX_DISTANCE = 128
N_MEM = 256
CA_HEADS = 4
CA_HEAD_DIM = 128
CA_DIM = CA_HEADS * CA_HEAD_DIM
D_FF = 2816
EPS = 1e-6

LANES = 128
HALF = LANES // 2
DT_PAD = LANES
COL_Z = 0
COL_XBC = COL_Z + SSD_D_INNER
COL_Q = COL_XBC + CONV_DIM
COL_K = COL_Q + ATTN_DIM
COL_V = COL_K + KV_DIM
COL_DT = COL_V + KV_DIM
IN_COLS = COL_DT + DT_PAD
NEG = -1e30
SAMPLE_SEQS = 16
VMEM_LIMIT = 56 * 1024 * 1024
FF_SPLITS = ((0, 1024), (1024, 2048), (2048, D_FF))


def _rms(x, g):
    return x * lax.rsqrt(jnp.mean(x * x, axis=-1, keepdims=True) + EPS) * g


def _silu(x):
    return x * jax.nn.sigmoid(x)


def _softplus(x):
    return jnp.maximum(x, 0.0) + jnp.log1p(jnp.exp(-jnp.abs(x)))


def _dot(a, b):
    return jnp.dot(a, b, preferred_element_type=F32)


def _dot_nt(a, b):
    return lax.dot_general(a, b, (((1,), (1,)), ((), ())), preferred_element_type=F32)


def _split3(a):
    hi = a.astype(BF16)
    r = a - hi.astype(F32)
    mid = r.astype(BF16)
    lo = (r - mid.astype(F32)).astype(BF16)
    return hi, mid, lo


def _sel_left(t01, a):
    hi, mid, lo = _split3(a)
    return _dot(t01, hi) + _dot(t01, mid) + _dot(t01, lo)


def _sel_right(a, e01):
    hi, mid, lo = _split3(a)
    return _dot(hi, e01) + _dot(mid, e01) + _dot(lo, e01)


def _lane_lo():
    return lax.broadcasted_iota(jnp.int32, (1, LANES), 1) < HALF


def _t5_bucket_np(dist):
    n = np.maximum(dist, 0)
    ratio = np.log(np.maximum(n, 1).astype(np.float32) / np.float32(MAX_EXACT))
    large = MAX_EXACT + (ratio / np.float32(math.log(MAX_DISTANCE / MAX_EXACT))
                         * np.float32(N_BUCKETS - MAX_EXACT)).astype(np.int32)
    large = np.minimum(large, N_BUCKETS - 1)
    return np.where(n < MAX_EXACT, n, large).astype(np.int32)


def _tril_np(n):
    return np.tril(np.ones((n, n), np.float32))


def _expand_np():
    e = np.zeros((LANES, SSD_D_INNER), np.float32)
    for h in range(SSD_HEADS):
        e[h, h * SSD_HEAD_DIM:(h + 1) * SSD_HEAD_DIM] = 1.0
    return e


def _full_spec(shape):
    nd = len(shape)
    return pl.BlockSpec(shape, lambda *_: (0,) * nd)


def _smem_spec():
    return pl.BlockSpec(memory_space=pltpu.SMEM)


def _params(sem):
    return pltpu.CompilerParams(dimension_semantics=sem, vmem_limit_bytes=VMEM_LIMIT)


def _in_proj_body(x_ref, g_ref, w_ref, z_ref, xbc_ref, q_ref, k_ref, v_ref, dt_ref):
    xn = _rms(x_ref[...], g_ref[...]).astype(BF16)

    def seg(lo, hi):
        return _dot(xn, w_ref[:, lo:hi])

    z_ref[...] = seg(COL_Z, COL_XBC)
    xbc_ref[...] = seg(COL_XBC, COL_Q)
    q_ref[...] = (seg(COL_Q, COL_K) * (ATTN_HEAD_DIM ** -0.5)).astype(BF16)
    k_ref[...] = seg(COL_K, COL_V)
    v_ref[...] = seg(COL_V, COL_DT)
    dt_ref[...] = seg(COL_DT, IN_COLS)


def _in_proj(x2, g_mix, w_in_r, tm):
    t = x2.shape[0]
    row = lambda w: pl.BlockSpec((tm, w), lambda i: (i, 0))
    outs = [(SSD_D_INNER, F32), (CONV_DIM, F32), (ATTN_DIM, BF16), (KV_DIM, F32), (KV_DIM, F32), (DT_PAD, F32)]
    return pl.pallas_call(
        _in_proj_body,
        grid=(t // tm,),
        in_specs=[row(D_MODEL), _full_spec((1, D_MODEL)), _full_spec((D_MODEL, IN_COLS))],
        out_specs=[row(w) for w, _ in outs],
        out_shape=[jax.ShapeDtypeStruct((t, w), d) for w, d in outs],
        compiler_params=_params(("parallel",)),
        name="in_proj",
    )(x2, g_mix, w_in_r)


def _conv_taps(cw_ref, cb_ref, taps):
    acc = cb_ref[...] + taps[0] * cw_ref[0:1, :]
    for k in range(1, CONV_K):
        acc = acc + taps[k] * cw_ref[k:k + 1, :]
    return _silu(acc)


def _ssd_prepare(conv, dt_raw, dtb_ref, alog_ref, tcum, total_fn, extra_fn, expand):
    xs = conv[:, :SSD_D_INNER]
    bm = conv[:, SSD_D_INNER:SSD_D_INNER + SSD_GROUPS * D_STATE]
    cm = conv[:, SSD_D_INNER + SSD_GROUPS * D_STATE:]
    dt = _softplus(dt_raw + dtb_ref[...])
    a = dt * (-jnp.exp(alog_ref[...]))
    cs = _sel_left(tcum, a)
    total = total_fn(cs, a)
    pieces = [dt, jnp.exp(total - cs), jnp.exp(cs)] + extra_fn(total)
    rows = cs.shape[0]
    ex = _sel_right(jnp.concatenate(pieces, axis=0), expand)
    ex = [ex[i * rows:(i + 1) * rows] for i in range(len(pieces))]
    return xs, bm, cm, cs, cs.T, ex


def _ssd_diag(cs, cs_t, cb_g, xdt, mask, g):
    lo = _lane_lo()
    out = []
    for pr in range(2):
        h0 = g * 4 + 2 * pr
        xp = xdt[:, (h0 // 2) * LANES:(h0 // 2 + 1) * LANES]
        x_lo = jnp.where(lo, xp, 0.0).astype(BF16)
        x_hi = jnp.where(lo, 0.0, xp).astype(BF16)
        acc = None
        for h, xh in ((h0, x_lo), (h0 + 1, x_hi)):
            diff = cs[:, h:h + 1] - cs_t[h:h + 1, :]
            decay = jnp.exp(jnp.where(mask, diff, -jnp.inf))
            part = _dot((cb_g * decay).astype(BF16), xh)
            acc = part if acc is None else acc + part
        out.append(acc)
    return jnp.concatenate(out, axis=-1)


def _gated_norm(y, z, gn_ref):
    yf = y * _silu(z)
    parts = []
    for g in range(SSD_GROUPS):
        yg = yf[:, g * GROUP_W:(g + 1) * GROUP_W]
        parts.append(yg * lax.rsqrt(jnp.mean(yg * yg, axis=-1, keepdims=True) + EPS))
    return jnp.concatenate(parts, axis=-1) * gn_ref[...]


def _build_bias(bias_ref, bucket_of, relb_ref, n_tables):
    for i in range(n_tables):
        for h in range(N_HEADS):
            bias_ref[i, h] = jnp.full(bias_ref.shape[2:], NEG, F32)

    def body(t, carry):
        for i in range(n_tables):
            hit = bucket_of(i) == t
            for h in range(N_HEADS):
                bias_ref[i, h] = jnp.where(hit, relb_ref[t, h], bias_ref[i, h])
        return carry

    lax.fori_loop(0, N_BUCKETS, body, 0)


def _prompt_mixer_body(z_ref, xbc_ref, q_ref, k_ref, v_ref, dt_ref,
                       cw_ref, cb_ref, dtb_ref, alog_ref, dskip_ref, gn_ref,
                       tril_ref, expand_ref, bucket_ref, relb_ref, sink_ref,
                       mix_ref, conv_out, ssm_out, k_out, v_out,
                       xpad, state_t, kbuf, vbuf, bias):
    b = pl.program_id(0)
    c = pl.program_id(1)
    last = pl.num_programs(1) - 1
    q_rows = CHUNK

    @pl.when((b == 0) & (c == 0))
    def _():
        _build_bias(bias, lambda i: bucket_ref[i], relb_ref, 2)

    @pl.when(c == 0)
    def _():
        xpad[0:8, :] = jnp.zeros((8, CONV_DIM), F32)
        state_t[...] = jnp.zeros_like(state_t)
        kbuf[...] = jnp.zeros_like(kbuf)
        vbuf[...] = jnp.zeros_like(vbuf)

    xbc = xbc_ref[0]
    xpad[8:8 + q_rows, :] = xbc
    taps = [xpad[5 + k:5 + k + q_rows, :] for k in range(CONV_K - 1)] + [xbc]
    conv = _conv_taps(cw_ref, cb_ref, taps)
    xpad[5:8, :] = xbc[q_rows - 3:q_rows, :]

    row = lax.broadcasted_iota(jnp.int32, (q_rows, q_rows), 0)
    col = lax.broadcasted_iota(jnp.int32, (q_rows, q_rows), 1)
    causal = col <= row
    xs, bm, cm, cs, cs_t, (dt_x, dend_x, ecs_x) = _ssd_prepare(
        conv, dt_ref[0], dtb_ref, alog_ref, tril_ref[...],
        lambda cs_, a_: cs_[q_rows - 1:q_rows, :], lambda total: [], expand_ref[...])
    xdt = xs * dt_x
    xde = (xdt * dend_x).astype(BF16)
    bb = bm.astype(BF16)
    cb = cm.astype(BF16)
    y_parts = []
    for g in range(SSD_GROUPS):
        gs = slice(g * GROUP_W, (g + 1) * GROUP_W)
        ns = slice(g * D_STATE, (g + 1) * D_STATE)
        cb_g = _dot_nt(cb[:, ns], bb[:, ns])
        y_diag = _ssd_diag(cs, cs_t, cb_g, xdt, causal, g)
        st_old = state_t[:, gs]
        y_off = _dot(cb[:, ns], st_old.astype(BF16)) * ecs_x[:, gs]
        y_parts.append(y_diag + y_off)
        b_t = bm[:, ns].T.astype(BF16)
        state_t[:, gs] = ecs_x[q_rows - 1:q_rows, gs] * st_old + _dot(b_t, xde[:, gs])
    y = jnp.concatenate(y_parts, axis=-1) + dskip_ref[...] * xs
    mix_ref[0, :, 0:SSD_D_INNER] = _gated_norm(y, z_ref[0], gn_ref).astype(BF16)

    lo = _lane_lo()
    k_cur = k_ref[0]
    v_cur = v_ref[0]
    kbuf[0, q_rows:, :] = k_cur.astype(BF16)
    kbuf[1, q_rows:, :] = pltpu.roll(k_cur, HALF, 1).astype(BF16)
    vbuf[0, q_rows:, :] = v_cur.astype(BF16)
    vbuf[1, q_rows:, :] = pltpu.roll(v_cur, HALF, 1).astype(BF16)
    table = jnp.minimum(c, 1)
    for pair in range(N_HEADS // 2):
        j = (2 * pair) // Q_PER_KV
        q_pair = q_ref[0, :, pair * LANES:(pair + 1) * LANES]
        acc = None
        for par in range(2):
            h = 2 * pair + par
            qm = jnp.where(lo, q_pair, 0.0) if par == 0 else jnp.where(lo, 0.0, q_pair)
            variant = (j + par) % 2
            s = _dot_nt(qm.astype(BF16), kbuf[variant]) + bias[table, h]
            sink = sink_ref[h]
            m = jnp.maximum(jnp.max(s, axis=-1, keepdims=True), sink)
            p = jnp.exp(s - m)
            denom = jnp.sum(p, axis=-1, keepdims=True) + jnp.exp(sink - m)
            vv = vbuf[variant]
            vm = jnp.where(lo, vv, 0.0) if par == 0 else jnp.where(lo, 0.0, vv)
            part = _dot(p.astype(BF16), vm.astype(BF16)) / denom
            acc = part if acc is None else acc + part
        mix_ref[0, :, SSD_D_INNER + pair * LANES:SSD_D_INNER + (pair + 1) * LANES] = acc.astype(BF16)
    kbuf[:, 0:q_rows, :] = kbuf[:, q_rows:, :]
    vbuf[:, 0:q_rows, :] = vbuf[:, q_rows:, :]

    @pl.when(c == last)
    def _():
        conv_out[0] = xbc[q_rows - 3:q_rows, :]
        ssm_out[0] = state_t[...].T
        k_out[0] = k_cur
        v_out[0] = v_cur


def _prompt_mixer(z, xbc, q, k, v, dt, small, consts, batch, seq):
    cw, cb, dtb, alog, dskip_x, gn, relb, sinks = small
    tril, expand, buckets = consts
    nc = seq // CHUNK
    r3 = lambda a: a.reshape(batch, seq, a.shape[-1])
    tok = lambda w: pl.BlockSpec((1, CHUNK, w), lambda b, c: (b, c, 0))
    per_b = lambda s: pl.BlockSpec((1,) + s, lambda b, c: (b,) + (0,) * len(s))
    out_shape = [
        jax.ShapeDtypeStruct((batch, seq, D_MODEL), BF16),
        jax.ShapeDtypeStruct((batch, CONV_K - 1, CONV_DIM), F32),
        jax.ShapeDtypeStruct((batch, SSD_D_INNER, D_STATE), F32),
        jax.ShapeDtypeStruct((batch, WINDOW, KV_DIM), F32),
        jax.ShapeDtypeStruct((batch, WINDOW, KV_DIM), F32),
    ]
    return pl.pallas_call(
        _prompt_mixer_body,
        grid=(batch, nc),
        in_specs=[tok(SSD_D_INNER), tok(CONV_DIM), tok(ATTN_DIM), tok(KV_DIM), tok(KV_DIM), tok(DT_PAD),
                  _full_spec(cw.shape), _full_spec(cb.shape), _full_spec(dtb.shape), _full_spec(alog.shape),
                  _full_spec(dskip_x.shape), _full_spec(gn.shape),
                  _full_spec(tril.shape), _full_spec(expand.shape), _full_spec(buckets.shape),
                  _smem_spec(), _smem_spec()],
        out_specs=[tok(D_MODEL), per_b((CONV_K - 1, CONV_DIM)), per_b((SSD_D_INNER, D_STATE)),
                   per_b((WINDOW, KV_DIM)), per_b((WINDOW, KV_DIM))],
        out_shape=out_shape,
        scratch_shapes=[
            pltpu.VMEM((8 + CHUNK, CONV_DIM), F32),
            pltpu.VMEM((D_STATE, SSD_D_INNER), F32),
            pltpu.VMEM((2, 2 * WINDOW, KV_DIM), BF16),
            pltpu.VMEM((2, 2 * WINDOW, KV_DIM), BF16),
            pltpu.VMEM((2, N_HEADS, WINDOW, 2 * WINDOW), F32),
        ],
        compiler_params=_params(("arbitrary", "arbitrary")),
        name="prompt_mixer",
    )(r3(z), r3(xbc), r3(q), r3(k), r3(v), r3(dt), cw, cb, dtb, alog, dskip_x, gn,
      tril, expand, buckets, relb, sinks)


def _sample_mixer_body(z_ref, xbc_ref, q_ref, k_ref, v_ref, dt_ref,
                       sconv_ref, sssm_ref, ck_ref, cv_ref,
                       cw_ref, cb_ref, dtb_ref, alog_ref, dskip_ref, gn_ref,
                       tcum_ref, tseq_ref, expand_ref, bkt_c_ref, bkt_n_ref, relb_ref, sink_ref,
                       mix_ref, conv_out, ssm_out, k_out, v_out,
                       xpad, bias_c, bias_n, *, dec_seq):
    step = pl.program_id(0)
    n_seq = SAMPLE_SEQS
    rows = n_seq * dec_seq

    @pl.when(step == 0)
    def _():
        _build_bias(bias_c, lambda i: bkt_c_ref[...], relb_ref, 1)
        _build_bias(bias_n, lambda i: bkt_n_ref[...], relb_ref, 1)

    xbc = xbc_ref[...]
    xpad[:, 8:8 + dec_seq, :] = xbc.reshape(n_seq, dec_seq, CONV_DIM)
    xpad[:, 5:8, :] = sconv_ref[...]
    taps = [xpad[:, 5 + k:5 + k + dec_seq, :].reshape(rows, CONV_DIM) for k in range(CONV_K - 1)] + [xbc]
    conv = _conv_taps(cw_ref, cb_ref, taps)
    conv_out[...] = xpad[:, 5 + dec_seq:8 + dec_seq, :]

    row = lax.broadcasted_iota(jnp.int32, (rows, rows), 0)
    col = lax.broadcasted_iota(jnp.int32, (rows, rows), 1)
    tseq = tseq_ref[...]
    same_seq = tseq > 0
    causal = same_seq & (col <= row)
    xs, bm, cm, cs, cs_t, (dt_x, dend_x, ecs_x, seqdec_x) = _ssd_prepare(
        conv, dt_ref[...], dtb_ref, alog_ref, tcum_ref[...],
        lambda cs_, a_: _sel_left(tseq, a_), lambda total: [jnp.exp(total)], expand_ref[...])
    xdt = xs * dt_x
    xde = (xdt * dend_x).astype(BF16)
    bb = bm.astype(BF16)
    cb = cm.astype(BF16)
    seq_of_row = lax.broadcasted_iota(jnp.int32, (rows, 1), 0) // dec_seq
    seq_of_lane = lax.broadcasted_iota(jnp.int32, (1, rows), 1) // dec_seq
    b_t = [bm[:, g * D_STATE:(g + 1) * D_STATE].T for g in range(SSD_GROUPS)]
    y_off = [None] * SSD_GROUPS
    for i in range(n_seq):
        st_t = sssm_ref[i].T
        new_parts = []
        for g in range(SSD_GROUPS):
            gs = slice(g * GROUP_W, (g + 1) * GROUP_W)
            ns = slice(g * D_STATE, (g + 1) * D_STATE)
            c_i = jnp.where(seq_of_row == i, cm[:, ns], 0.0).astype(BF16)
            part = _dot(c_i, st_t[:, gs].astype(BF16))
            y_off[g] = part if y_off[g] is None else y_off[g] + part
            b_i = jnp.where(seq_of_lane == i, b_t[g], 0.0).astype(BF16)
            dec = seqdec_x[i * dec_seq:i * dec_seq + 1, gs]
            new_parts.append(dec * st_t[:, gs] + _dot(b_i, xde[:, gs]))
        ssm_out[i] = jnp.concatenate(new_parts, axis=-1).T
    y_parts = []
    for g in range(SSD_GROUPS):
        gs = slice(g * GROUP_W, (g + 1) * GROUP_W)
        ns = slice(g * D_STATE, (g + 1) * D_STATE)
        cb_g = _dot_nt(cb[:, ns], bb[:, ns])
        y_parts.append(_ssd_diag(cs, cs_t, cb_g, xdt, causal, g) + y_off[g] * ecs_x[:, gs])
    y = jnp.concatenate(y_parts, axis=-1) + dskip_ref[...] * xs
    mix_ref[:, 0:SSD_D_INNER] = _gated_norm(y, z_ref[...], gn_ref).astype(BF16)

    lo = _lane_lo()
    k_new = k_ref[...]
    v_new = v_ref[...]
    k_var = [k_new.astype(BF16), pltpu.roll(k_new, HALF, 1).astype(BF16)]
    v_new_r = pltpu.roll(v_new, HALF, 1)
    v_dup = [jnp.where(lo, v_new, v_new_r).astype(BF16), jnp.where(lo, v_new_r, v_new).astype(BF16)]
    qf = q_ref[...].astype(F32)
    q_masked = []
    s_new = []
    for h in range(N_HEADS):
        pair, par = h // 2, h % 2
        j = h // Q_PER_KV
        q_pair = qf[:, pair * LANES:(pair + 1) * LANES]
        qm = jnp.where(lo, q_pair, 0.0) if par == 0 else jnp.where(lo, 0.0, q_pair)
        q_masked.append(qm)
        s_new.append(_dot_nt(qm.astype(BF16), k_var[(j + par) % 2]) + bias_n[0, h])
    stack_rows = lax.broadcasted_iota(jnp.int32, (Q_PER_KV * dec_seq, 1), 0) // dec_seq
    att_rows = []
    for i in range(n_seq):
        rs = slice(i * dec_seq, (i + 1) * dec_seq)
        kc = ck_ref[i]
        vc = cv_ref[i]
        kc_r = pltpu.roll(kc, HALF, 1)
        vc_r = pltpu.roll(vc, HALF, 1)
        pieces = []
        for j in range(N_KV_HEADS):
            heads = range(j * Q_PER_KV, (j + 1) * Q_PER_KV)
            kdup = (jnp.where(lo, kc, kc_r) if j == 0 else jnp.where(lo, kc_r, kc)).astype(BF16)
            vdup = (jnp.where(lo, vc, vc_r) if j == 0 else jnp.where(lo, vc_r, vc)).astype(BF16)
            qs = jnp.concatenate([q_masked[h][rs] for h in heads], axis=0).astype(BF16)
            sc = _dot_nt(qs, kdup) + jnp.concatenate([bias_c[0, h] for h in heads], axis=0)
            sn = jnp.concatenate([s_new[h][rs] for h in heads], axis=0)
            sink = jnp.zeros((Q_PER_KV * dec_seq, 1), F32)
            for hh, h in enumerate(heads):
                sink = jnp.where(stack_rows == hh, sink_ref[h], sink)
            m = jnp.maximum(jnp.maximum(jnp.max(sc, axis=-1, keepdims=True),
                                        jnp.max(sn, axis=-1, keepdims=True)), sink)
            pc = jnp.exp(sc - m)
            pn = jnp.exp(sn - m)
            denom = (jnp.sum(pc, axis=-1, keepdims=True) + jnp.sum(pn, axis=-1, keepdims=True)
                     + jnp.exp(sink - m))
            o = (_dot(pc.astype(BF16), vdup) + _dot(pn.astype(BF16), v_dup[j])) / denom
            for pr in range(Q_PER_KV // 2):
                even = o[(2 * pr) * dec_seq:(2 * pr + 1) * dec_seq]
                odd = o[(2 * pr + 1) * dec_seq:(2 * pr + 2) * dec_seq]
                pieces.append(jnp.where(lo, even, odd))
        att_rows.append(jnp.concatenate(pieces, axis=-1))
        keep = WINDOW - dec_seq
        k_out[i] = jnp.concatenate([kc[dec_seq:, :], k_new[rs]], axis=0) if keep else k_new[rs]
        v_out[i] = jnp.concatenate([vc[dec_seq:, :], v_new[rs]], axis=0) if keep else v_new[rs]
    mix_ref[:, SSD_D_INNER:] = jnp.concatenate(att_rows, axis=0).astype(BF16)


def _sample_mixer(z, xbc, q, k, v, dt, sconv, sssm, ck, cv, small, consts, n_seq_total, dec_seq):
    cw, cb, dtb, alog, dskip_x, gn, relb, sinks = small
    tcum, tseq, expand, bkt_c, bkt_n = consts
    rows = SAMPLE_SEQS * dec_seq
    tok = lambda w: pl.BlockSpec((rows, w), lambda i: (i, 0))
    per_s = lambda s: pl.BlockSpec((SAMPLE_SEQS,) + s, lambda i: (i,) + (0,) * len(s))
    out_shape = [
        jax.ShapeDtypeStruct((n_seq_total * dec_seq, D_MODEL), BF16),
        jax.ShapeDtypeStruct((n_seq_total, CONV_K - 1, CONV_DIM), F32),
        jax.ShapeDtypeStruct((n_seq_total, SSD_D_INNER, D_STATE), F32),
        jax.ShapeDtypeStruct((n_seq_total, WINDOW, KV_DIM), F32),
        jax.ShapeDtypeStruct((n_seq_total, WINDOW, KV_DIM), F32),
    ]
    return pl.pallas_call(
        functools.partial(_sample_mixer_body, dec_seq=dec_seq),
        grid=(n_seq_total // SAMPLE_SEQS,),
        in_specs=[tok(SSD_D_INNER), tok(CONV_DIM), tok(ATTN_DIM), tok(KV_DIM), tok(KV_DIM), tok(DT_PAD),
                  per_s((CONV_K - 1, CONV_DIM)), per_s((SSD_D_INNER, D_STATE)),
                  per_s((WINDOW, KV_DIM)), per_s((WINDOW, KV_DIM)),
                  _full_spec(cw.shape), _full_spec(cb.shape), _full_spec(dtb.shape), _full_spec(alog.shape),
                  _full_spec(dskip_x.shape), _full_spec(gn.shape),
                  _full_spec(tcum.shape), _full_spec(tseq.shape), _full_spec(expand.shape),
                  _full_spec(bkt_c.shape), _full_spec(bkt_n.shape), _smem_spec(), _smem_spec()],
        out_specs=[tok(D_MODEL), per_s((CONV_K - 1, CONV_DIM)), per_s((SSD_D_INNER, D_STATE)),
                   per_s((WINDOW, KV_DIM)), per_s((WINDOW, KV_DIM))],
        out_shape=out_shape,
        scratch_shapes=[
            pltpu.VMEM((SAMPLE_SEQS, 8 + dec_seq, CONV_DIM), F32),
            pltpu.VMEM((1, N_HEADS, dec_seq, WINDOW), F32),
            pltpu.VMEM((1, N_HEADS, rows, rows), F32),
        ],
        compiler_params=_params(("arbitrary",)),
        name="sample_mixer",
    )(z, xbc, q, k, v, dt, sconv, sssm, ck, cv, cw, cb, dtb, alog, dskip_x, gn,
      tcum, tseq, expand, bkt_c, bkt_n, relb, sinks)


def _post1_body(x_ref, mix_ref, wout_ref, gc_ref, wcq_ref, h_ref, qc_ref):
    h = x_ref[...] + _dot(mix_ref[...], wout_ref[...])
    h_ref[...] = h
    qc_ref[...] = _dot(_rms(h, gc_ref[...]).astype(BF16), wcq_ref[...]).astype(BF16)


def _post1(x2, mix, w_out, g_cross, w_cq, tm):
    t = x2.shape[0]
    row = lambda w: pl.BlockSpec((tm, w), lambda i: (i, 0))
    return pl.pallas_call(
        _post1_body,
        grid=(t // tm,),
        in_specs=[row(D_MODEL), row(D_MODEL), _full_spec(w_out.shape), _full_spec(g_cross.shape),
                  _full_spec(w_cq.shape)],
        out_specs=[row(D_MODEL), row(CA_DIM)],
        out_shape=[jax.ShapeDtypeStruct((t, D_MODEL), F32), jax.ShapeDtypeStruct((t, CA_DIM), BF16)],
        compiler_params=_params(("parallel",)),
        name="out_proj",
    )(x2, mix, w_out, g_cross, w_cq)


def _mem_kv_body(mem_ref, g_ref, wk_ref, wv_ref, k_ref, v_ref):
    mn = _rms(mem_ref[...], g_ref[...]).astype(BF16)
    k_ref[...] = _dot(mn, wk_ref[...])
    v_ref[...] = _dot(mn, wv_ref[...])


def _mem_kv(mem2, g_mem, w_ck, w_cv, tm):
    t = mem2.shape[0]
    row = lambda w: pl.BlockSpec((tm, w), lambda i: (i, 0))
    return pl.pallas_call(
        _mem_kv_body,
        grid=(t // tm,),
        in_specs=[row(D_MODEL), _full_spec(g_mem.shape), _full_spec(w_ck.shape), _full_spec(w_cv.shape)],
        out_specs=[row(CA_DIM), row(CA_DIM)],
        out_shape=[jax.ShapeDtypeStruct((t, CA_DIM), F32)] * 2,
        compiler_params=_params(("parallel",)),
        name="mem_kv",
    )(mem2, g_mem, w_ck, w_cv)


def _cross_heads(q, k, v):
    out = []
    for h in range(CA_HEADS):
        hs = slice(h * CA_HEAD_DIM, (h + 1) * CA_HEAD_DIM)
        s = _dot_nt(q[:, hs], k[:, hs]) * (CA_HEAD_DIM ** -0.5)
        m = jnp.max(s, axis=-1, keepdims=True)
        p = jnp.exp(s - m)
        out.append(_dot(p.astype(BF16), v[:, hs]) / jnp.sum(p, axis=-1, keepdims=True))
    return jnp.concatenate(out, axis=-1)


def _cross_prompt_body(q_ref, k_ref, v_ref, o_ref):
    o_ref[0] = _cross_heads(q_ref[0], k_ref[0].astype(BF16), v_ref[0].astype(BF16)).astype(BF16)


def _cross_prompt(qc, mk, mv, batch, seq, tm):
    q3 = qc.reshape(batch, seq, CA_DIM)
    tok = pl.BlockSpec((1, tm, CA_DIM), lambda b, i: (b, i, 0))
    mem = pl.BlockSpec((1, N_MEM, CA_DIM), lambda b, i: (b, 0, 0))
    return pl.pallas_call(
        _cross_prompt_body,
        grid=(batch, seq // tm),
        in_specs=[tok, mem, mem],
        out_specs=tok,
        out_shape=jax.ShapeDtypeStruct((batch, seq, CA_DIM), BF16),
        compiler_params=_params(("parallel", "parallel")),
        name="cross_prompt",
    )(q3, mk, mv).reshape(batch * seq, CA_DIM)


def _cross_sample_body(q_ref, k_ref, v_ref, o_ref, *, n_seq, dec_seq):
    q = q_ref[...].astype(F32)
    n_keys = N_MEM * CA_HEADS
    col_head = lax.broadcasted_iota(jnp.int32, (1, n_keys), 1) & (CA_HEADS - 1)
    row_head = lax.broadcasted_iota(jnp.int32, (CA_HEADS * dec_seq, 1), 0) // dec_seq
    own = col_head == row_head
    rows = []
    for i in range(n_seq):
        qi = q[i * dec_seq:(i + 1) * dec_seq]
        qs = jnp.concatenate([qi[:, h * CA_HEAD_DIM:(h + 1) * CA_HEAD_DIM] for h in range(CA_HEADS)], axis=0)
        s = _dot_nt(qs.astype(BF16), k_ref[i].astype(BF16)) * (CA_HEAD_DIM ** -0.5)
        s = jnp.where(own, s, NEG)
        m = jnp.max(s, axis=-1, keepdims=True)
        p = jnp.exp(s - m)
        o = _dot(p.astype(BF16), v_ref[i].astype(BF16)) / jnp.sum(p, axis=-1, keepdims=True)
        rows.append(jnp.concatenate([o[h * dec_seq:(h + 1) * dec_seq] for h in range(CA_HEADS)], axis=-1))
    o_ref[...] = jnp.concatenate(rows, axis=0).astype(BF16)


def _cross_sample(qc, ck, cv, n_seq_total, dec_seq, n_seq):
    rows = n_seq * dec_seq
    tok = pl.BlockSpec((rows, CA_DIM), lambda i: (i, 0))
    mem = pl.BlockSpec((n_seq, N_MEM * CA_HEADS, CA_HEAD_DIM), lambda i: (i, 0, 0))
    return pl.pallas_call(
        functools.partial(_cross_sample_body, n_seq=n_seq, dec_seq=dec_seq),
        grid=(n_seq_total // n_seq,),
        in_specs=[tok, mem, mem],
        out_specs=tok,
        out_shape=jax.ShapeDtypeStruct((n_seq_total * dec_seq, CA_DIM), BF16),
        compiler_params=_params(("parallel",)),
        name="cross_sample",
    )(qc, ck, cv)


def _post2_body(h_ref, o_ref, wco_ref, gf_ref, wg_ref, wu_ref, wd_ref, gfin_ref, y_ref):
    h = h_ref[...] + _dot(o_ref[...], wco_ref[...])
    hn = _rms(h, gf_ref[...]).astype(BF16)
    acc = h
    for lo, hi in FF_SPLITS:
        act = _silu(_dot(hn, wg_ref[:, lo:hi])) * _dot(hn, wu_ref[:, lo:hi])
        acc = acc + _dot(act.astype(BF16), wd_ref[lo:hi, :])
    y_ref[...] = _rms(acc, gfin_ref[...])


def _post2(h1, o, w_co, g_ffn, w_gate, w_up, w_down, g_final, tm):
    t = h1.shape[0]
    row = lambda w: pl.BlockSpec((tm, w), lambda i: (i, 0))
    return pl.pallas_call(
        _post2_body,
        grid=(t // tm,),
        in_specs=[row(D_MODEL), row(CA_DIM), _full_spec(w_co.shape), _full_spec(g_ffn.shape),
                  _full_spec(w_gate.shape), _full_spec(w_up.shape), _full_spec(w_down.shape),
                  _full_spec(g_final.shape)],
        out_specs=row(D_MODEL),
        out_shape=jax.ShapeDtypeStruct((t, D_MODEL), F32),
        compiler_params=_params(("parallel",)),
        name="ffn",
    )(h1, o, w_co, g_ffn, w_gate, w_up, w_down, g_final)


def _prompt_consts():
    qi = np.arange(WINDOW)[:, None]
    ji = np.arange(2 * WINDOW)[None, :]
    dist = qi + WINDOW - ji
    inband = (dist >= 0) & (dist < WINDOW)
    bucket = np.where(inband, _t5_bucket_np(dist), -1)
    first = np.where(ji >= WINDOW, bucket, -1)
    buckets = np.stack([first, bucket]).astype(np.int32)
    return (jnp.asarray(_tril_np(CHUNK), BF16), jnp.asarray(_expand_np(), BF16), jnp.asarray(buckets))


def _sample_consts(dec_seq, cache_len):
    rows = SAMPLE_SEQS * dec_seq
    r = np.arange(rows)
    same = (r[:, None] // dec_seq) == (r[None, :] // dec_seq)
    tcum = (same & (r[None, :] <= r[:, None])).astype(np.float32)
    tseq = same.astype(np.float32)
    t = np.arange(dec_seq)[:, None]
    j = np.arange(cache_len)[None, :]
    dist_c = t + cache_len - j
    bkt_c = np.where((dist_c >= 0) & (dist_c < WINDOW), _t5_bucket_np(dist_c), -1).astype(np.int32)
    dist_n = (r[:, None] % dec_seq) - (r[None, :] % dec_seq)
    ok = same & (dist_n >= 0) & (dist_n < WINDOW)
    bkt_n = np.where(ok, _t5_bucket_np(dist_n), -1).astype(np.int32)
    return (jnp.asarray(tcum, BF16), jnp.asarray(tseq, BF16), jnp.asarray(_expand_np(), BF16),
            jnp.asarray(bkt_c), jnp.asarray(bkt_n))


def _pick_tile(t, pref):
    tm = min(t, pref)
    while t % tm:
        tm //= 2
    return tm


def kernel(x_prompt, x_sample, mem_prompt, state_conv, state_ssm, cache_swa_k, cache_swa_v, cache_mem_k, cache_mem_v, rel_bias, g_mix, w_in, conv_w, conv_b, dt_bias, a_log, d_skip, g_ssd_norm, sinks, w_out, g_cross, g_mem, w_cq, w_ck, w_cv, w_co, g_ffn, w_gate, w_up, w_down, g_final):
    assert g_mix.shape[0] == 1, "single-layer trunk"
    batch, seq, _ = x_prompt.shape
    n_dec, dec_seq, _ = x_sample.shape
    cache_len = cache_swa_k.shape[2]
    assert seq % CHUNK == 0 and cache_len == WINDOW and n_dec % SAMPLE_SEQS == 0 and dec_seq == 8

    wi = w_in[0]
    s0, s1, s2, s3, s4 = (SSD_D_INNER, SSD_D_INNER + CONV_DIM, SSD_D_INNER + CONV_DIM + SSD_HEADS,
                          SSD_D_INNER + CONV_DIM + SSD_HEADS + ATTN_DIM,
                          SSD_D_INNER + CONV_DIM + SSD_HEADS + ATTN_DIM + KV_DIM)
    w_in_r = jnp.concatenate(
        [wi[:, :s0], wi[:, s0:s1], wi[:, s2:s3], wi[:, s3:s4], wi[:, s4:],
         jnp.pad(wi[:, s1:s2], ((0, 0), (0, DT_PAD - SSD_HEADS)))], axis=1).astype(BF16)
    row = lambda a: a.reshape(1, -1).astype(F32)
    pad_h = lambda a: jnp.pad(a.reshape(1, -1).astype(F32), ((0, 0), (0, DT_PAD - SSD_HEADS)))
    small = (conv_w[0].astype(F32), row(conv_b[0]), pad_h(dt_bias[0]), pad_h(a_log[0]),
             jnp.repeat(d_skip[0].astype(F32), SSD_HEAD_DIM).reshape(1, -1), row(g_ssd_norm[0]),
             rel_bias.astype(F32), sinks[0].astype(F32))
    bf = lambda w: w[0].astype(BF16)
    w_out_b, w_cq_b, w_ck_b, w_cv_b, w_co_b = bf(w_out), bf(w_cq), bf(w_ck), bf(w_cv), bf(w_co)
    w_gate_b, w_up_b, w_down_b = bf(w_gate), bf(w_up), bf(w_down)
    g_mix_r, g_cross_r, g_mem_r, g_ffn_r, g_fin_r = row(g_mix[0]), row(g_cross[0]), row(g_mem[0]), row(g_ffn[0]), row(g_final)

    tp = batch * seq
    xp2 = x_prompt.reshape(tp, D_MODEL)
    tm_p = _pick_tile(seq, 512)
    z, xbc, q, k, v, dt = _in_proj(xp2, g_mix_r, w_in_r, tm_p)
    mix, p_conv, p_ssm, p_k, p_v = _prompt_mixer(z, xbc, q, k, v, dt, small, _prompt_consts(), batch, seq)
    mem2 = mem_prompt.reshape(batch * N_MEM, D_MODEL)
    mk, mv = _mem_kv(mem2, g_mem_r, w_ck_b, w_cv_b, _pick_tile(batch * N_MEM, 512))
    h1, qc = _post1(xp2, mix.reshape(tp, D_MODEL), w_out_b, g_cross_r, w_cq_b, tm_p)
    o = _cross_prompt(qc, mk.reshape(batch, N_MEM, CA_DIM), mv.reshape(batch, N_MEM, CA_DIM), batch, seq, tm_p)
    y_prompt = _post2(h1, o, w_co_b, g_ffn_r, w_gate_b, w_up_b, w_down_b, g_fin_r, tm_p)

    ts = n_dec * dec_seq
    xs2 = x_sample.reshape(ts, D_MODEL)
    tm_s = _pick_tile(ts, 512)
    z, xbc, q, k, v, dt = _in_proj(xs2, g_mix_r, w_in_r, tm_s)
    mix_s, s_conv, s_ssm, s_k, s_v = _sample_mixer(
        z, xbc, q, k, v, dt, state_conv[0], state_ssm[0].reshape(n_dec, SSD_D_INNER, D_STATE),
        cache_swa_k[0].reshape(n_dec, cache_len, KV_DIM), cache_swa_v[0].reshape(n_dec, cache_len, KV_DIM),
        small, _sample_consts(dec_seq, cache_len), n_dec, dec_seq)
    h1s, qcs = _post1(xs2, mix_s, w_out_b, g_cross_r, w_cq_b, tm_s)
    os_ = _cross_sample(qcs, cache_mem_k[0].reshape(n_dec, N_MEM * CA_HEADS, CA_HEAD_DIM),
                        cache_mem_v[0].reshape(n_dec, N_MEM * CA_HEADS, CA_HEAD_DIM), n_dec, dec_seq, 8)
    y_sample = _post2(h1s, os_, w_co_b, g_ffn_r, w_gate_b, w_up_b, w_down_b, g_fin_r, tm_s)

    return (y_prompt.reshape(batch, seq, D_MODEL), y_sample.reshape(n_dec, dec_seq, D_MODEL),
            p_conv[None], p_ssm.reshape(1, batch, SSD_HEADS, SSD_HEAD_DIM, D_STATE),
            p_k.reshape(1, batch, WINDOW, N_KV_HEADS, ATTN_HEAD_DIM),
            p_v.reshape(1, batch, WINDOW, N_KV_HEADS, ATTN_HEAD_DIM),
            mk.reshape(1, batch, N_MEM, CA_HEADS, CA_HEAD_DIM), mv.reshape(1, batch, N_MEM, CA_HEADS, CA_HEAD_DIM),
            s_conv[None], s_ssm.reshape(1, n_dec, SSD_HEADS, SSD_HEAD_DIM, D_STATE),
            s_k.reshape(1, n_dec, cache_len, N_KV_HEADS, ATTN_HEAD_DIM),
            s_v.reshape(1, n_dec, cache_len, N_KV_HEADS, ATTN_HEAD_DIM))
```

```python
import functools
import math

import numpy as np
import jax
import jax.numpy as jnp
from jax import lax
from jax.experimental import pallas as pl
from jax.experimental.pallas import tpu as pltpu

F32 = jnp.float32
BF16 = jnp.bfloat16

D_MODEL = 1024
SSD_D_INNER = 512
SSD_HEAD_DIM = 64
SSD_HEADS = 8
SSD_GROUPS = 2
GROUP_W = SSD_D_INNER // SSD_GROUPS
D_STATE = 128
CONV_K = 4
CONV_DIM = SSD_D_INNER + 2 * SSD_GROUPS * D_STATE
CHUNK = 128
ATTN_DIM = 512
ATTN_HEAD_DIM = 64
N_HEADS = 8
N_KV_HEADS = 2
Q_PER_KV = N_HEADS // N_KV_HEADS
KV_DIM = N_KV_HEADS * ATTN_HEAD_DIM
WINDOW = 128
N_BUCKETS = 32
MAX_EXACT = N_BUCKETS // 2
MAX_DISTANCE = 128
N_MEM = 256
CA_HEADS = 4
CA_HEAD_DIM = 128
CA_DIM = CA_HEADS * CA_HEAD_DIM
D_FF = 2816
EPS = 1e-6

LANES = 128
HALF = LANES // 2
DT_PAD = LANES
COL_Z = 0
COL_XBC = COL_Z + SSD_D_INNER
COL_Q = COL_XBC + CONV_DIM
COL_K = COL_Q + ATTN_DIM
COL_V = COL_K + KV_DIM
COL_DT = COL_V + KV_DIM
IN_COLS = COL_DT + DT_PAD
NEG = -1e30
SAMPLE_SEQS = 16
VMEM_LIMIT = 56 * 1024 * 1024
FF_SPLITS = ((0, 1024), (1024, 2048), (2048, D_FF))


def _rms(x, g):
    return x * lax.rsqrt(jnp.mean(x * x, axis=-1, keepdims=True) + EPS) * g


def _silu(x):
    return x * jax.nn.sigmoid(x)


def _softplus(x):
    return jnp.maximum(x, 0.0) + jnp.log1p(jnp.exp(-jnp.abs(x)))


def _dot(a, b):
    return jnp.dot(a, b, preferred_element_type=F32)


def _dot_nt(a, b):
    return lax.dot_general(a, b, (((1,), (1,)), ((), ())), preferred_element_type=F32)


def _split3(a):
    hi = a.astype(BF16)
    r = a - hi.astype(F32)
    mid = r.astype(BF16)
    lo = (r - mid.astype(F32)).astype(BF16)
    return hi, mid, lo


def _sel_left(t01, a):
    hi, mid, lo = _split3(a)
    return _dot(t01, hi) + _dot(t01, mid) + _dot(t01, lo)


def _sel_right(a, e01):
    hi, mid, lo = _split3(a)
    return _dot(hi, e01) + _dot(mid, e01) + _dot(lo, e01)


def _lane_lo():
    return lax.broadcasted_iota(jnp.int32, (1, LANES), 1) < HALF


def _t5_bucket_np(dist):
    n = np.maximum(dist, 0)
    ratio = np.log(np.maximum(n, 1).astype(np.float32) / np.float32(MAX_EXACT))
    large = MAX_EXACT + (ratio / np.float32(math.log(MAX_DISTANCE / MAX_EXACT))
                         * np.float32(N_BUCKETS - MAX_EXACT)).astype(np.int32)
    large = np.minimum(large, N_BUCKETS - 1)
    return np.where(n < MAX_EXACT, n, large).astype(np.int32)


def _tril_np(n):
    return np.tril(np.ones((n, n), np.float32))


def _expand_np():
    e = np.zeros((LANES, SSD_D_INNER), np.float32)
    for h in range(SSD_HEADS):
        e[h, h * SSD_HEAD_DIM:(h + 1) * SSD_HEAD_DIM] = 1.0
    return e


def _full_spec(shape):
    nd = len(shape)
    return pl.BlockSpec(shape, lambda *_: (0,) * nd, pipeline_mode=pl.Buffered(1))


def _smem_spec():
    return pl.BlockSpec(memory_space=pltpu.SMEM)


def _params(sem):
    return pltpu.CompilerParams(dimension_semantics=sem, vmem_limit_bytes=VMEM_LIMIT)


def _in_proj_body(x_ref, g_ref, w_ref, z_ref, xbc_ref, q_ref, k_ref, v_ref, dt_ref):
    xn = _rms(x_ref[...], g_ref[...]).astype(BF16)

    def seg(lo, hi):
        return _dot(xn, w_ref[:, lo:hi])

    z_ref[...] = seg(COL_Z, COL_XBC)
    xbc_ref[...] = seg(COL_XBC, COL_Q)
    q_ref[...] = (seg(COL_Q, COL_K) * (ATTN_HEAD_DIM ** -0.5)).astype(BF16)
    k_ref[...] = seg(COL_K, COL_V)
    v_ref[...] = seg(COL_V, COL_DT)
    dt_ref[...] = seg(COL_DT, IN_COLS)


def _in_proj(x2, g_mix, w_in_r, tm):
    t = x2.shape[0]
    row = lambda w: pl.BlockSpec((tm, w), lambda i: (i, 0))
    outs = [(SSD_D_INNER, F32), (CONV_DIM, F32), (ATTN_DIM, BF16), (KV_DIM, F32), (KV_DIM, F32), (DT_PAD, F32)]
    return pl.pallas_call(
        _in_proj_body,
        grid=(t // tm,),
        in_specs=[row(D_MODEL), _full_spec((1, D_MODEL)), _full_spec((D_MODEL, IN_COLS))],
        out_specs=[row(w) for w, _ in outs],
        out_shape=[jax.ShapeDtypeStruct((t, w), d) for w, d in outs],
        compiler_params=_params(("parallel",)),
        name="in_proj",
    )(x2, g_mix, w_in_r)


def _conv_taps(cw_ref, cb_ref, taps):
    acc = cb_ref[...] + taps[0] * cw_ref[0:1, :]
    for k in range(1, CONV_K):
        acc = acc + taps[k] * cw_ref[k:k + 1, :]
    return _silu(acc)


def _ssd_prepare(conv, dt_raw, dtb_ref, alog_ref, tcum, total_fn, extra_fn, expand):
    xs = conv[:, :SSD_D_INNER]
    bm = conv[:, SSD_D_INNER:SSD_D_INNER + SSD_GROUPS * D_STATE]
    cm = conv[:, SSD_D_INNER + SSD_GROUPS * D_STATE:]
    dt = _softplus(dt_raw + dtb_ref[...])
    a = dt * (-jnp.exp(alog_ref[...]))
    cs = _sel_left(tcum, a)
    total = total_fn(cs, a)
    pieces = [dt, jnp.exp(total - cs), jnp.exp(cs)] + extra_fn(total)
    rows = cs.shape[0]
    ex = _sel_right(jnp.concatenate(pieces, axis=0), expand)
    ex = [ex[i * rows:(i + 1) * rows] for i in range(len(pieces))]
    return xs, bm, cm, cs, cs.T, ex


def _ssd_diag(cs, cs_t, cb_g, xdt, mask, g):
    lo = _lane_lo()
    out = []
    for pr in range(2):
        h0 = g * 4 + 2 * pr
        xp = xdt[:, (h0 // 2) * LANES:(h0 // 2 + 1) * LANES]
        x_lo = jnp.where(lo, xp, 0.0).astype(BF16)
        x_hi = jnp.where(lo, 0.0, xp).astype(BF16)
        acc = None
        for h, xh in ((h0, x_lo), (h0 + 1, x_hi)):
            diff = cs[:, h:h + 1] - cs_t[h:h + 1, :]
            decay = jnp.exp(jnp.where(mask, diff, -jnp.inf))
            part = _dot((cb_g * decay).astype(BF16), xh)
            acc = part if acc is None else acc + part
        out.append(acc)
    return jnp.concatenate(out, axis=-1)


def _gated_norm(y, z, gn_ref):
    yf = y * _silu(z)
    parts = []
    for g in range(SSD_GROUPS):
        yg = yf[:, g * GROUP_W:(g + 1) * GROUP_W]
        parts.append(yg * lax.rsqrt(jnp.mean(yg * yg, axis=-1, keepdims=True) + EPS))
    return jnp.concatenate(parts, axis=-1) * gn_ref[...]


def _build_bias(bias_ref, bucket_of, relb_ref, n_tables):
    for i in range(n_tables):
        for h in range(N_HEADS):
            bias_ref[i, h] = jnp.full(bias_ref.shape[2:], NEG, F32)

    def body(t, carry):
        for i in range(n_tables):
            hit = bucket_of(i) == t
            for h in range(N_HEADS):
                bias_ref[i, h] = jnp.where(hit, relb_ref[t, h], bias_ref[i, h])
        return carry

    lax.fori_loop(0, N_BUCKETS, body, 0)


def _prompt_mixer_body(z_ref, xbc_ref, q_ref, k_ref, v_ref, dt_ref,
                       cw_ref, cb_ref, dtb_ref, alog_ref, dskip_ref, gn_ref,
                       tril_ref, expand_ref, bucket_ref, relb_ref, sink_ref,
                       mix_ref, conv_out, ssm_out, k_out, v_out,
                       xpad, state_t, kbuf, vbuf, bias):
    b = pl.program_id(0)
    c = pl.program_id(1)
    last = pl.num_programs(1) - 1
    q_rows = CHUNK

    @pl.when((b == 0) & (c == 0))
    def _():
        _build_bias(bias, lambda i: bucket_ref[i], relb_ref, 2)

    @pl.when(c == 0)
    def _():
        xpad[0:8, :] = jnp.zeros((8, CONV_DIM), F32)
        state_t[...] = jnp.zeros_like(state_t)
        kbuf[...] = jnp.zeros_like(kbuf)
        vbuf[...] = jnp.zeros_like(vbuf)

    xbc = xbc_ref[0]
    xpad[8:8 + q_rows, :] = xbc
    taps = [xpad[5 + k:5 + k + q_rows, :] for k in range(CONV_K - 1)] + [xbc]
    conv = _conv_taps(cw_ref, cb_ref, taps)
    xpad[5:8, :] = xbc[q_rows - 3:q_rows, :]

    row = lax.broadcasted_iota(jnp.int32, (q_rows, q_rows), 0)
    col = lax.broadcasted_iota(jnp.int32, (q_rows, q_rows), 1)
    causal = col <= row
    xs, bm, cm, cs, cs_t, (dt_x, dend_x, ecs_x) = _ssd_prepare(
        conv, dt_ref[0], dtb_ref, alog_ref, tril_ref[...],
        lambda cs_, a_: cs_[q_rows - 1:q_rows, :], lambda total: [], expand_ref[...])
    xdt = xs * dt_x
    xde = (xdt * dend_x).astype(BF16)
    bb = bm.astype(BF16)
    cb = cm.astype(BF16)
    y_parts = []
    for g in range(SSD_GROUPS):
        gs = slice(g * GROUP_W, (g + 1) * GROUP_W)
        ns = slice(g * D_STATE, (g + 1) * D_STATE)
        cb_g = _dot_nt(cb[:, ns], bb[:, ns])
        y_diag = _ssd_diag(cs, cs_t, cb_g, xdt, causal, g)
        st_old = state_t[:, gs]
        y_off = _dot(cb[:, ns], st_old.astype(BF16)) * ecs_x[:, gs]
        y_parts.append(y_diag + y_off)
        b_t = bm[:, ns].T.astype(BF16)
        state_t[:, gs] = ecs_x[q_rows - 1:q_rows, gs] * st_old + _dot(b_t, xde[:, gs])
    y = jnp.concatenate(y_parts, axis=-1) + dskip_ref[...] * xs
    mix_ref[0, :, 0:SSD_D_INNER] = _gated_norm(y, z_ref[0], gn_ref).astype(BF16)

    lo = _lane_lo()
    k_cur = k_ref[0]
    v_cur = v_ref[0]
    kbuf[0, q_rows:, :] = k_cur.astype(BF16)
    kbuf[1, q_rows:, :] = pltpu.roll(k_cur, HALF, 1).astype(BF16)
    vbuf[0, q_rows:, :] = v_cur.astype(BF16)
    vbuf[1, q_rows:, :] = pltpu.roll(v_cur, HALF, 1).astype(BF16)
    table = jnp.minimum(c, 1)
    for pair in range(N_HEADS // 2):
        j = (2 * pair) // Q_PER_KV
        q_pair = q_ref[0, :, pair * LANES:(pair + 1) * LANES]
        acc = None
        for par in range(2):
            h = 2 * pair + par
            qm = jnp.where(lo, q_pair, 0.0) if par == 0 else jnp.where(lo, 0.0, q_pair)
            variant = (j + par) % 2
            s = _dot_nt(qm.astype(BF16), kbuf[variant]) + bias[table, h]
            sink = sink_ref[h]
            m = jnp.maximum(jnp.max(s, axis=-1, keepdims=True), sink)
            p = jnp.exp(s - m)
            denom = jnp.sum(p, axis=-1, keepdims=True) + jnp.exp(sink - m)
            vv = vbuf[variant]
            vm = jnp.where(lo, vv, 0.0) if par == 0 else jnp.where(lo, 0.0, vv)
            part = _dot(p.astype(BF16), vm.astype(BF16)) / denom
            acc = part if acc is None else acc + part
        mix_ref[0, :, SSD_D_INNER + pair * LANES:SSD_D_INNER + (pair + 1) * LANES] = acc.astype(BF16)
    kbuf[:, 0:q_rows, :] = kbuf[:, q_rows:, :]
    vbuf[:, 0:q_rows, :] = vbuf[:, q_rows:, :]

    @pl.when(c == last)
    def _():
        conv_out[0] = xbc[q_rows - 3:q_rows, :]
        ssm_out[0] = state_t[...].T
        k_out[0] = k_cur
        v_out[0] = v_cur


def _prompt_mixer(z, xbc, q, k, v, dt, small, consts, batch, seq):
    cw, cb, dtb, alog, dskip_x, gn, relb, sinks = small
    tril, expand, buckets = consts
    nc = seq // CHUNK
    r3 = lambda a: a.reshape(batch, seq, a.shape[-1])
    tok = lambda w: pl.BlockSpec((1, CHUNK, w), lambda b, c: (b, c, 0))
    per_b = lambda s: pl.BlockSpec((1,) + s, lambda b, c: (b,) + (0,) * len(s))
    out_shape = [
        jax.ShapeDtypeStruct((batch, seq, D_MODEL), BF16),
        jax.ShapeDtypeStruct((batch, CONV_K - 1, CONV_DIM), F32),
        jax.ShapeDtypeStruct((batch, SSD_D_INNER, D_STATE), F32),
        jax.ShapeDtypeStruct((batch, WINDOW, KV_DIM), F32),
        jax.ShapeDtypeStruct((batch, WINDOW, KV_DIM), F32),
    ]
    return pl.pallas_call(
        _prompt_mixer_body,
        grid=(batch, nc),
        in_specs=[tok(SSD_D_INNER), tok(CONV_DIM), tok(ATTN_DIM), tok(KV_DIM), tok(KV_DIM), tok(DT_PAD),
                  _full_spec(cw.shape), _full_spec(cb.shape), _full_spec(dtb.shape), _full_spec(alog.shape),
                  _full_spec(dskip_x.shape), _full_spec(gn.shape),
                  _full_spec(tril.shape), _full_spec(expand.shape), _full_spec(buckets.shape),
                  _smem_spec(), _smem_spec()],
        out_specs=[tok(D_MODEL), per_b((CONV_K - 1, CONV_DIM)), per_b((SSD_D_INNER, D_STATE)),
                   per_b((WINDOW, KV_DIM)), per_b((WINDOW, KV_DIM))],
        out_shape=out_shape,
        scratch_shapes=[
            pltpu.VMEM((8 + CHUNK, CONV_DIM), F32),
            pltpu.VMEM((D_STATE, SSD_D_INNER), F32),
            pltpu.VMEM((2, 2 * WINDOW, KV_DIM), BF16),
            pltpu.VMEM((2, 2 * WINDOW, KV_DIM), BF16),
            pltpu.VMEM((2, N_HEADS, WINDOW, 2 * WINDOW), F32),
        ],
        compiler_params=_params(("arbitrary", "arbitrary")),
        name="prompt_mixer",
    )(r3(z), r3(xbc), r3(q), r3(k), r3(v), r3(dt), cw, cb, dtb, alog, dskip_x, gn,
      tril, expand, buckets, relb, sinks)


def _prompt_front_body(x_ref, gmix_ref, win_ref, wout_ref, gc_ref, wcq_ref,
                       cw_ref, cb_ref, dtb_ref, alog_ref, dskip_ref, gn_ref,
                       tril_ref, expand_ref, bucket_ref, relb_ref, sink_ref,
                       h_ref, qc_ref, conv_out, ssm_out, k_out, v_out,
                       xpad, state_t, kbuf, vbuf, bias, mix, *, n_chunks):
    b = pl.program_id(0)
    c = pl.program_id(1)
    last = pl.num_programs(1) - 1
    tq = n_chunks * CHUNK

    @pl.when((b == 0) & (c == 0))
    def _():
        _build_bias(bias, lambda i: bucket_ref[i], relb_ref, 2)

    @pl.when(c == 0)
    def _():
        xpad[0:8, :] = jnp.zeros((8, CONV_DIM), F32)
        state_t[...] = jnp.zeros_like(state_t)
        kbuf[:, 0:CHUNK, :] = jnp.zeros((2, CHUNK, KV_DIM), BF16)
        vbuf[:, 0:CHUNK, :] = jnp.zeros((2, CHUNK, KV_DIM), BF16)

    x = x_ref[0]
    xn = _rms(x, gmix_ref[...]).astype(BF16)

    def seg(lo, hi):
        return _dot(xn, win_ref[:, lo:hi])

    z = seg(COL_Z, COL_XBC)
    xbc = seg(COL_XBC, COL_Q)
    q = (seg(COL_Q, COL_K) * (ATTN_HEAD_DIM ** -0.5)).astype(BF16)
    k_new = seg(COL_K, COL_V)
    v_new = seg(COL_V, COL_DT)
    dt_raw = seg(COL_DT, IN_COLS)

    xpad[8:8 + tq, :] = xbc
    taps = [xpad[5 + k:5 + k + tq, :] for k in range(CONV_K - 1)] + [xbc]
    conv = _conv_taps(cw_ref, cb_ref, taps)
    xpad[5:8, :] = xbc[tq - 3:tq, :]

    kbuf[0, CHUNK:, :] = k_new.astype(BF16)
    kbuf[1, CHUNK:, :] = pltpu.roll(k_new, HALF, 1).astype(BF16)
    vbuf[0, CHUNK:, :] = v_new.astype(BF16)
    vbuf[1, CHUNK:, :] = pltpu.roll(v_new, HALF, 1).astype(BF16)

    row = lax.broadcasted_iota(jnp.int32, (CHUNK, CHUNK), 0)
    col = lax.broadcasted_iota(jnp.int32, (CHUNK, CHUNK), 1)
    causal = col <= row
    lo = _lane_lo()
    for ci in range(n_chunks):
        rs = slice(ci * CHUNK, (ci + 1) * CHUNK)
        xs, bm, cm, cs, cs_t, (dt_x, dend_x, ecs_x) = _ssd_prepare(
            conv[rs], dt_raw[rs], dtb_ref, alog_ref, tril_ref[...],
            lambda cs_, a_: cs_[CHUNK - 1:CHUNK, :], lambda total: [], expand_ref[...])
        xdt = xs * dt_x
        xde = (xdt * dend_x).astype(BF16)
        bb = bm.astype(BF16)
        cb = cm.astype(BF16)
        y_parts = []
        for g in range(SSD_GROUPS):
            gs = slice(g * GROUP_W, (g + 1) * GROUP_W)
            ns = slice(g * D_STATE, (g + 1) * D_STATE)
            cb_g = _dot_nt(cb[:, ns], bb[:, ns])
            y_diag = _ssd_diag(cs, cs_t, cb_g, xdt, causal, g)
            st_old = state_t[:, gs]
            y_off = _dot(cb[:, ns], st_old.astype(BF16)) * ecs_x[:, gs]
            y_parts.append(y_diag + y_off)
            b_t = bm[:, ns].T.astype(BF16)
            state_t[:, gs] = ecs_x[CHUNK - 1:CHUNK, gs] * st_old + _dot(b_t, xde[:, gs])
        y = jnp.concatenate(y_parts, axis=-1) + dskip_ref[...] * xs
        mix[rs, 0:SSD_D_INNER] = _gated_norm(y, z[rs], gn_ref).astype(BF16)

        keys = slice(ci * CHUNK, (ci + 2) * CHUNK)
        table = jnp.minimum(c, 1) if ci == 0 else 1
        for pair in range(N_HEADS // 2):
            j = (2 * pair) // Q_PER_KV
            q_pair = q[rs, pair * LANES:(pair + 1) * LANES]
            acc = None
            for par in range(2):
                h = 2 * pair + par
                qm = jnp.where(lo, q_pair, 0.0) if par == 0 else jnp.where(lo, 0.0, q_pair)
                variant = (j + par) % 2
                s = _dot_nt(qm.astype(BF16), kbuf[variant, keys, :]) + bias[table, h]
                sink = sink_ref[h]
                m = jnp.maximum(jnp.max(s, axis=-1, keepdims=True), sink)
                p = jnp.exp(s - m)
                denom = jnp.sum(p, axis=-1, keepdims=True) + jnp.exp(sink - m)
                vv = vbuf[variant, keys, :]
                vm = jnp.where(lo, vv, 0.0) if par == 0 else jnp.where(lo, 0.0, vv)
                part = _dot(p.astype(BF16), vm.astype(BF16)) / denom
                acc = part if acc is None else acc + part
            mix[rs, SSD_D_INNER + pair * LANES:SSD_D_INNER + (pair + 1) * LANES] = acc.astype(BF16)
    kbuf[:, 0:CHUNK, :] = kbuf[:, tq:tq + CHUNK, :]
    vbuf[:, 0:CHUNK, :] = vbuf[:, tq:tq + CHUNK, :]

    h1 = x + _dot(mix[...], wout_ref[...])
    h_ref[0] = h1
    qc_ref[0] = _dot(_rms(h1, gc_ref[...]).astype(BF16), wcq_ref[...]).astype(BF16)

    @pl.when(c == last)
    def _():
        conv_out[0] = xbc[tq - 3:tq, :]
        ssm_out[0] = state_t[...].T
        k_out[0] = k_new[tq - WINDOW:tq, :]
        v_out[0] = v_new[tq - WINDOW:tq, :]


def _prompt_front(x3, g_mix, w_in_r, w_out, g_cross, w_cq, small, consts, n_chunks):
    cw, cb, dtb, alog, dskip_x, gn, relb, sinks = small
    tril, expand, buckets = consts
    batch, seq, _ = x3.shape
    tq = n_chunks * CHUNK
    tok = lambda w: pl.BlockSpec((1, tq, w), lambda b, c: (b, c, 0))
    per_b = lambda s: pl.BlockSpec((1,) + s, lambda b, c: (b,) + (0,) * len(s))
    out_shape = [
        jax.ShapeDtypeStruct((batch, seq, D_MODEL), F32),
        jax.ShapeDtypeStruct((batch, seq, CA_DIM), BF16),
        jax.ShapeDtypeStruct((batch, CONV_K - 1, CONV_DIM), F32),
        jax.ShapeDtypeStruct((batch, SSD_D_INNER, D_STATE), F32),
        jax.ShapeDtypeStruct((batch, WINDOW, KV_DIM), F32),
        jax.ShapeDtypeStruct((batch, WINDOW, KV_DIM), F32),
    ]
    full = [g_mix, w_in_r, w_out, g_cross, w_cq, cw, cb, dtb, alog, dskip_x, gn, tril, expand, buckets]
    return pl.pallas_call(
        functools.partial(_prompt_front_body, n_chunks=n_chunks),
        grid=(batch, seq // tq),
        in_specs=[tok(D_MODEL)] + [_full_spec(a.shape) for a in full] + [_smem_spec(), _smem_spec()],
        out_specs=[tok(D_MODEL), tok(CA_DIM), per_b((CONV_K - 1, CONV_DIM)), per_b((SSD_D_INNER, D_STATE)),
                   per_b((WINDOW, KV_DIM)), per_b((WINDOW, KV_DIM))],
        out_shape=out_shape,
        scratch_shapes=[
            pltpu.VMEM((8 + tq, CONV_DIM), F32),
            pltpu.VMEM((D_STATE, SSD_D_INNER), F32),
            pltpu.VMEM((2, CHUNK + tq, KV_DIM), BF16),
            pltpu.VMEM((2, CHUNK + tq, KV_DIM), BF16),
            pltpu.VMEM((2, N_HEADS, WINDOW, 2 * WINDOW), F32),
            pltpu.VMEM((tq, D_MODEL), BF16),
        ],
        compiler_params=_params(("arbitrary", "arbitrary")),
        name="prompt_front",
    )(x3, *full, relb, sinks)


def _sample_mixer_body(z_ref, xbc_ref, q_ref, k_ref, v_ref, dt_ref,
                       sconv_ref, sssm_ref, ck_ref, cv_ref,
                       cw_ref, cb_ref, dtb_ref, alog_ref, dskip_ref, gn_ref,
                       tcum_ref, tseq_ref, expand_ref, bkt_c_ref, bkt_n_ref, relb_ref, sink_ref,
                       mix_ref, conv_out, ssm_out, k_out, v_out,
                       xpad, bias_c, bias_n, *, dec_seq):
    step = pl.program_id(0)
    n_seq = SAMPLE_SEQS
    rows = n_seq * dec_seq

    @pl.when(step == 0)
    def _():
        _build_bias(bias_c, lambda i: bkt_c_ref[...], relb_ref, 1)
        _build_bias(bias_n, lambda i: bkt_n_ref[...], relb_ref, 1)

    xbc = xbc_ref[...]
    xpad[:, 8:8 + dec_seq, :] = xbc.reshape(n_seq, dec_seq, CONV_DIM)
    xpad[:, 5:8, :] = sconv_ref[...]
    taps = [xpad[:, 5 + k:5 + k + dec_seq, :].reshape(rows, CONV_DIM) for k in range(CONV_K - 1)] + [xbc]
    conv = _conv_taps(cw_ref, cb_ref, taps)
    conv_out[...] = xpad[:, 5 + dec_seq:8 + dec_seq, :]

    row = lax.broadcasted_iota(jnp.int32, (rows, rows), 0)
    col = lax.broadcasted_iota(jnp.int32, (rows, rows), 1)
    tseq = tseq_ref[...]
    same_seq = tseq > 0
    causal = same_seq & (col <= row)
    xs, bm, cm, cs, cs_t, (dt_x, dend_x, ecs_x, seqdec_x) = _ssd_prepare(
        conv, dt_ref[...], dtb_ref, alog_ref, tcum_ref[...],
        lambda cs_, a_: _sel_left(tseq, a_), lambda total: [jnp.exp(total)], expand_ref[...])
    xdt = xs * dt_x
    xde = (xdt * dend_x).astype(BF16)
    bb = bm.astype(BF16)
    cb = cm.astype(BF16)
    seq_of_row = lax.broadcasted_iota(jnp.int32, (rows, 1), 0) // dec_seq
    seq_of_lane = lax.broadcasted_iota(jnp.int32, (1, rows), 1) // dec_seq
    b_t = [bm[:, g * D_STATE:(g + 1) * D_STATE].T for g in range(SSD_GROUPS)]
    y_off = [None] * SSD_GROUPS
    for i in range(n_seq):
        st_t = sssm_ref[i].T
        new_parts = []
        for g in range(SSD_GROUPS):
            gs = slice(g * GROUP_W, (g + 1) * GROUP_W)
            ns = slice(g * D_STATE, (g + 1) * D_STATE)
            c_i = jnp.where(seq_of_row == i, cm[:, ns], 0.0).astype(BF16)
            part = _dot(c_i, st_t[:, gs].astype(BF16))
            y_off[g] = part if y_off[g] is None else y_off[g] + part
            b_i = jnp.where(seq_of_lane == i, b_t[g], 0.0).astype(BF16)
            dec = seqdec_x[i * dec_seq:i * dec_seq + 1, gs]
            new_parts.append(dec * st_t[:, gs] + _dot(b_i, xde[:, gs]))
        ssm_out[i] = jnp.concatenate(new_parts, axis=-1).T
    y_parts = []
    for g in range(SSD_GROUPS):
        gs = slice(g * GROUP_W, (g + 1) * GROUP_W)
        ns = slice(g * D_STATE, (g + 1) * D_STATE)
        cb_g = _dot_nt(cb[:, ns], bb[:, ns])
        y_parts.append(_ssd_diag(cs, cs_t, cb_g, xdt, causal, g) + y_off[g] * ecs_x[:, gs])
    y = jnp.concatenate(y_parts, axis=-1) + dskip_ref[...] * xs
    mix_ref[:, 0:SSD_D_INNER] = _gated_norm(y, z_ref[...], gn_ref).astype(BF16)

    lo = _lane_lo()
    k_new = k_ref[...]
    v_new = v_ref[...]
    k_var = [k_new.astype(BF16), pltpu.roll(k_new, HALF, 1).astype(BF16)]
    v_new_r = pltpu.roll(v_new, HALF, 1)
    v_dup = [jnp.where(lo, v_new, v_new_r).astype(BF16), jnp.where(lo, v_new_r, v_new).astype(BF16)]
    qf = q_ref[...].astype(F32)
    q_masked = []
    s_new = []
    for h in range(N_HEADS):
        pair, par = h // 2, h % 2
        j = h // Q_PER_KV
        q_pair = qf[:, pair * LANES:(pair + 1) * LANES]
        qm = jnp.where(lo, q_pair, 0.0) if par == 0 else jnp.where(lo, 0.0, q_pair)
        q_masked.append(qm)
        s_new.append(_dot_nt(qm.astype(BF16), k_var[(j + par) % 2]) + bias_n[0, h])
    stack_rows = lax.broadcasted_iota(jnp.int32, (Q_PER_KV * dec_seq, 1), 0) // dec_seq
    att_rows = []
    for i in range(n_seq):
        rs = slice(i * dec_seq, (i + 1) * dec_seq)
        kc = ck_ref[i]
        vc = cv_ref[i]
        kc_r = pltpu.roll(kc, HALF, 1)
        vc_r = pltpu.roll(vc, HALF, 1)
        pieces = []
        for j in range(N_KV_HEADS):
            heads = range(j * Q_PER_KV, (j + 1) * Q_PER_KV)
            kdup = (jnp.where(lo, kc, kc_r) if j == 0 else jnp.where(lo, kc_r, kc)).astype(BF16)
            vdup = (jnp.where(lo, vc, vc_r) if j == 0 else jnp.where(lo, vc_r, vc)).astype(BF16)
            qs = jnp.concatenate([q_masked[h][rs] for h in heads], axis=0).astype(BF16)
            sc = _dot_nt(qs, kdup) + jnp.concatenate([bias_c[0, h] for h in heads], axis=0)
            sn = jnp.concatenate([s_new[h][rs] for h in heads], axis=0)
            sink = jnp.zeros((Q_PER_KV * dec_seq, 1), F32)
            for hh, h in enumerate(heads):
                sink = jnp.where(stack_rows == hh, sink_ref[h], sink)
            m = jnp.maximum(jnp.maximum(jnp.max(sc, axis=-1, keepdims=True),
                                        jnp.max(sn, axis=-1, keepdims=True)), sink)
            pc = jnp.exp(sc - m)
            pn = jnp.exp(sn - m)
            denom = (jnp.sum(pc, axis=-1, keepdims=True) + jnp.sum(pn, axis=-1, keepdims=True)
                     + jnp.exp(sink - m))
            o = (_dot(pc.astype(BF16), vdup) + _dot(pn.astype(BF16), v_dup[j])) / denom
            for pr in range(Q_PER_KV // 2):
                even = o[(2 * pr) * dec_seq:(2 * pr + 1) * dec_seq]
                odd = o[(2 * pr + 1) * dec_seq:(2 * pr + 2) * dec_seq]
                pieces.append(jnp.where(lo, even, odd))
        att_rows.append(jnp.concatenate(pieces, axis=-1))
        keep = WINDOW - dec_seq
        k_out[i] = jnp.concatenate([kc[dec_seq:, :], k_new[rs]], axis=0) if keep else k_new[rs]
        v_out[i] = jnp.concatenate([vc[dec_seq:, :], v_new[rs]], axis=0) if keep else v_new[rs]
    mix_ref[:, SSD_D_INNER:] = jnp.concatenate(att_rows, axis=0).astype(BF16)


def _sample_mixer(z, xbc, q, k, v, dt, sconv, sssm, ck, cv, small, consts, n_seq_total, dec_seq):
    cw, cb, dtb, alog, dskip_x, gn, relb, sinks = small
    tcum, tseq, expand, bkt_c, bkt_n = consts
    rows = SAMPLE_SEQS * dec_seq
    tok = lambda w: pl.BlockSpec((rows, w), lambda i: (i, 0))
    per_s = lambda s: pl.BlockSpec((SAMPLE_SEQS,) + s, lambda i: (i,) + (0,) * len(s))
    out_shape = [
        jax.ShapeDtypeStruct((n_seq_total * dec_seq, D_MODEL), BF16),
        jax.ShapeDtypeStruct((n_seq_total, CONV_K - 1, CONV_DIM), F32),
        jax.ShapeDtypeStruct((n_seq_total, SSD_D_INNER, D_STATE), F32),
        jax.ShapeDtypeStruct((n_seq_total, WINDOW, KV_DIM), F32),
        jax.ShapeDtypeStruct((n_seq_total, WINDOW, KV_DIM), F32),
    ]
    return pl.pallas_call(
        functools.partial(_sample_mixer_body, dec_seq=dec_seq),
        grid=(n_seq_total // SAMPLE_SEQS,),
        in_specs=[tok(SSD_D_INNER), tok(CONV_DIM), tok(ATTN_DIM), tok(KV_DIM), tok(KV_DIM), tok(DT_PAD),
                  per_s((CONV_K - 1, CONV_DIM)), per_s((SSD_D_INNER, D_STATE)),
                  per_s((WINDOW, KV_DIM)), per_s((WINDOW, KV_DIM)),
                  _full_spec(cw.shape), _full_spec(cb.shape), _full_spec(dtb.shape), _full_spec(alog.shape),
                  _full_spec(dskip_x.shape), _full_spec(gn.shape),
                  _full_spec(tcum.shape), _full_spec(tseq.shape), _full_spec(expand.shape),
                  _full_spec(bkt_c.shape), _full_spec(bkt_n.shape), _smem_spec(), _smem_spec()],
        out_specs=[tok(D_MODEL), per_s((CONV_K - 1, CONV_DIM)), per_s((SSD_D_INNER, D_STATE)),
                   per_s((WINDOW, KV_DIM)), per_s((WINDOW, KV_DIM))],
        out_shape=out_shape,
        scratch_shapes=[
            pltpu.VMEM((SAMPLE_SEQS, 8 + dec_seq, CONV_DIM), F32),
            pltpu.VMEM((1, N_HEADS, dec_seq, WINDOW), F32),
            pltpu.VMEM((1, N_HEADS, rows, rows), F32),
        ],
        compiler_params=_params(("arbitrary",)),
        name="sample_mixer",
    )(z, xbc, q, k, v, dt, sconv, sssm, ck, cv, cw, cb, dtb, alog, dskip_x, gn,
      tcum, tseq, expand, bkt_c, bkt_n, relb, sinks)


def _post1_body(x_ref, mix_ref, wout_ref, gc_ref, wcq_ref, h_ref, qc_ref):
    h = x_ref[...] + _dot(mix_ref[...], wout_ref[...])
    h_ref[...] = h
    qc_ref[...] = _dot(_rms(h, gc_ref[...]).astype(BF16), wcq_ref[...]).astype(BF16)


def _post1(x2, mix, w_out, g_cross, w_cq, tm):
    t = x2.shape[0]
    row = lambda w: pl.BlockSpec((tm, w), lambda i: (i, 0))
    return pl.pallas_call(
        _post1_body,
        grid=(t // tm,),
        in_specs=[row(D_MODEL), row(D_MODEL), _full_spec(w_out.shape), _full_spec(g_cross.shape),
                  _full_spec(w_cq.shape)],
        out_specs=[row(D_MODEL), row(CA_DIM)],
        out_shape=[jax.ShapeDtypeStruct((t, D_MODEL), F32), jax.ShapeDtypeStruct((t, CA_DIM), BF16)],
        compiler_params=_params(("parallel",)),
        name="out_proj",
    )(x2, mix, w_out, g_cross, w_cq)


def _mem_kv_body(mem_ref, g_ref, wk_ref, wv_ref, k_ref, v_ref):
    mn = _rms(mem_ref[...], g_ref[...]).astype(BF16)
    k_ref[...] = _dot(mn, wk_ref[...])
    v_ref[...] = _dot(mn, wv_ref[...])


def _mem_kv(mem2, g_mem, w_ck, w_cv, tm):
    t = mem2.shape[0]
    row = lambda w: pl.BlockSpec((tm, w), lambda i: (i, 0))
    return pl.pallas_call(
        _mem_kv_body,
        grid=(t // tm,),
        in_specs=[row(D_MODEL), _full_spec(g_mem.shape), _full_spec(w_ck.shape), _full_spec(w_cv.shape)],
        out_specs=[row(CA_DIM), row(CA_DIM)],
        out_shape=[jax.ShapeDtypeStruct((t, CA_DIM), F32)] * 2,
        compiler_params=_params(("parallel",)),
        name="mem_kv",
    )(mem2, g_mem, w_ck, w_cv)


def _cross_heads(q, k, v):
    out = []
    for h in range(CA_HEADS):
        hs = slice(h * CA_HEAD_DIM, (h + 1) * CA_HEAD_DIM)
        s = _dot_nt(q[:, hs], k[:, hs]) * (CA_HEAD_DIM ** -0.5)
        m = jnp.max(s, axis=-1, keepdims=True)
        p = jnp.exp(s - m)
        out.append(_dot(p.astype(BF16), v[:, hs]) / jnp.sum(p, axis=-1, keepdims=True))
    return jnp.concatenate(out, axis=-1)


def _cross_prompt_body(q_ref, k_ref, v_ref, o_ref):
    o_ref[0] = _cross_heads(q_ref[0], k_ref[0].astype(BF16), v_ref[0].astype(BF16)).astype(BF16)


def _cross_prompt(qc, mk, mv, batch, seq, tm):
    q3 = qc.reshape(batch, seq, CA_DIM)
    tok = pl.BlockSpec((1, tm, CA_DIM), lambda b, i: (b, i, 0))
    mem = pl.BlockSpec((1, N_MEM, CA_DIM), lambda b, i: (b, 0, 0))
    return pl.pallas_call(
        _cross_prompt_body,
        grid=(batch, seq // tm),
        in_specs=[tok, mem, mem],
        out_specs=tok,
        out_shape=jax.ShapeDtypeStruct((batch, seq, CA_DIM), BF16),
        compiler_params=_params(("parallel", "parallel")),
        name="cross_prompt",
    )(q3, mk, mv).reshape(batch * seq, CA_DIM)


def _cross_sample_body(q_ref, k_ref, v_ref, o_ref, *, n_seq, dec_seq):
    q = q_ref[...].astype(F32)
    n_keys = N_MEM * CA_HEADS
    col_head = lax.broadcasted_iota(jnp.int32, (1, n_keys), 1) & (CA_HEADS - 1)
    row_head = lax.broadcasted_iota(jnp.int32, (CA_HEADS * dec_seq, 1), 0) // dec_seq
    own = col_head == row_head
    rows = []
    for i in range(n_seq):
        qi = q[i * dec_seq:(i + 1) * dec_seq]
        qs = jnp.concatenate([qi[:, h * CA_HEAD_DIM:(h + 1) * CA_HEAD_DIM] for h in range(CA_HEADS)], axis=0)
        s = _dot_nt(qs.astype(BF16), k_ref[i].astype(BF16)) * (CA_HEAD_DIM ** -0.5)
        s = jnp.where(own, s, NEG)
        m = jnp.max(s, axis=-1, keepdims=True)
        p = jnp.exp(s - m)
        o = _dot(p.astype(BF16), v_ref[i].astype(BF16)) / jnp.sum(p, axis=-1, keepdims=True)
        rows.append(jnp.concatenate([o[h * dec_seq:(h + 1) * dec_seq] for h in range(CA_HEADS)], axis=-1))
    o_ref[...] = jnp.concatenate(rows, axis=0).astype(BF16)


def _cross_sample(qc, ck, cv, n_seq_total, dec_seq, n_seq):
    rows = n_seq * dec_seq
    tok = pl.BlockSpec((rows, CA_DIM), lambda i: (i, 0))
    mem = pl.BlockSpec((n_seq, N_MEM * CA_HEADS, CA_HEAD_DIM), lambda i: (i, 0, 0))
    return pl.pallas_call(
        functools.partial(_cross_sample_body, n_seq=n_seq, dec_seq=dec_seq),
        grid=(n_seq_total // n_seq,),
        in_specs=[tok, mem, mem],
        out_specs=tok,
        out_shape=jax.ShapeDtypeStruct((n_seq_total * dec_seq, CA_DIM), BF16),
        compiler_params=_params(("parallel",)),
        name="cross_sample",
    )(qc, ck, cv)


def _post2_body(h_ref, o_ref, wco_ref, gf_ref, wg_ref, wu_ref, wd_ref, gfin_ref, y_ref):
    h = h_ref[...] + _dot(o_ref[...], wco_ref[...])
    hn = _rms(h, gf_ref[...]).astype(BF16)
    acc = h
    for lo, hi in FF_SPLITS:
        act = _silu(_dot(hn, wg_ref[:, lo:hi])) * _dot(hn, wu_ref[:, lo:hi])
        acc = acc + _dot(act.astype(BF16), wd_ref[lo:hi, :])
    y_ref[...] = _rms(acc, gfin_ref[...])


def _post2(h1, o, w_co, g_ffn, w_gate, w_up, w_down, g_final, tm):
    t = h1.shape[0]
    row = lambda w: pl.BlockSpec((tm, w), lambda i: (i, 0))
    return pl.pallas_call(
        _post2_body,
        grid=(t // tm,),
        in_specs=[row(D_MODEL), row(CA_DIM), _full_spec(w_co.shape), _full_spec(g_ffn.shape),
                  _full_spec(w_gate.shape), _full_spec(w_up.shape), _full_spec(w_down.shape),
                  _full_spec(g_final.shape)],
        out_specs=row(D_MODEL),
        out_shape=jax.ShapeDtypeStruct((t, D_MODEL), F32),
        compiler_params=_params(("parallel",)),
        name="ffn",
    )(h1, o, w_co, g_ffn, w_gate, w_up, w_down, g_final)


def _prompt_consts():
    qi = np.arange(WINDOW)[:, None]
    ji = np.arange(2 * WINDOW)[None, :]
    dist = qi + WINDOW - ji
    inband = (dist >= 0) & (dist < WINDOW)
    bucket = np.where(inband, _t5_bucket_np(dist), -1)
    first = np.where(ji >= WINDOW, bucket, -1)
    buckets = np.stack([first, bucket]).astype(np.int32)
    return (jnp.asarray(_tril_np(CHUNK), BF16), jnp.asarray(_expand_np(), BF16), jnp.asarray(buckets))


def _sample_consts(dec_seq, cache_len):
    rows = SAMPLE_SEQS * dec_seq
    r = np.arange(rows)
    same = (r[:, None] // dec_seq) == (r[None, :] // dec_seq)
    tcum = (same & (r[None, :] <= r[:, None])).astype(np.float32)
    tseq = same.astype(np.float32)
    t = np.arange(dec_seq)[:, None]
    j = np.arange(cache_len)[None, :]
    dist_c = t + cache_len - j
    bkt_c = np.where((dist_c >= 0) & (dist_c < WINDOW), _t5_bucket_np(dist_c), -1).astype(np.int32)
    dist_n = (r[:, None] % dec_seq) - (r[None, :] % dec_seq)
    ok = same & (dist_n >= 0) & (dist_n < WINDOW)
    bkt_n = np.where(ok, _t5_bucket_np(dist_n), -1).astype(np.int32)
    return (jnp.asarray(tcum, BF16), jnp.asarray(tseq, BF16), jnp.asarray(_expand_np(), BF16),
            jnp.asarray(bkt_c), jnp.asarray(bkt_n))


def _pick_tile(t, pref):
    tm = min(t, pref)
    while t % tm:
        tm //= 2
    return tm


def kernel(x_prompt, x_sample, mem_prompt, state_conv, state_ssm, cache_swa_k, cache_swa_v, cache_mem_k, cache_mem_v, rel_bias, g_mix, w_in, conv_w, conv_b, dt_bias, a_log, d_skip, g_ssd_norm, sinks, w_out, g_cross, g_mem, w_cq, w_ck, w_cv, w_co, g_ffn, w_gate, w_up, w_down, g_final):
    assert g_mix.shape[0] == 1, "single-layer trunk"
    batch, seq, _ = x_prompt.shape
    n_dec, dec_seq, _ = x_sample.shape
    cache_len = cache_swa_k.shape[2]
    assert seq % CHUNK == 0 and cache_len == WINDOW and n_dec % SAMPLE_SEQS == 0 and dec_seq == 8

    wi = w_in[0]
    s0, s1, s2, s3, s4 = (SSD_D_INNER, SSD_D_INNER + CONV_DIM, SSD_D_INNER + CONV_DIM + SSD_HEADS,
                          SSD_D_INNER + CONV_DIM + SSD_HEADS + ATTN_DIM,
                          SSD_D_INNER + CONV_DIM + SSD_HEADS + ATTN_DIM + KV_DIM)
    w_in_r = jnp.concatenate(
        [wi[:, :s0], wi[:, s0:s1], wi[:, s2:s3], wi[:, s3:s4], wi[:, s4:],
         jnp.pad(wi[:, s1:s2], ((0, 0), (0, DT_PAD - SSD_HEADS)))], axis=1).astype(BF16)
    row = lambda a: a.reshape(1, -1).astype(F32)
    pad_h = lambda a: jnp.pad(a.reshape(1, -1).astype(F32), ((0, 0), (0, DT_PAD - SSD_HEADS)))
    small = (conv_w[0].astype(F32), row(conv_b[0]), pad_h(dt_bias[0]), pad_h(a_log[0]),
             jnp.repeat(d_skip[0].astype(F32), SSD_HEAD_DIM).reshape(1, -1), row(g_ssd_norm[0]),
             rel_bias.astype(F32), sinks[0].astype(F32))
    bf = lambda w: w[0].astype(BF16)
    w_out_b, w_cq_b, w_ck_b, w_cv_b, w_co_b = bf(w_out), bf(w_cq), bf(w_ck), bf(w_cv), bf(w_co)
    w_gate_b, w_up_b, w_down_b = bf(w_gate), bf(w_up), bf(w_down)
    g_mix_r, g_cross_r, g_mem_r, g_ffn_r, g_fin_r = row(g_mix[0]), row(g_cross[0]), row(g_mem[0]), row(g_ffn[0]), row(g_final)

    tp = batch * seq
    tm_p = _pick_tile(seq, 512)
    n_chunks = 2 if seq % (2 * CHUNK) == 0 else 1
    h1, qc, p_conv, p_ssm, p_k, p_v = _prompt_front(x_prompt, g_mix_r, w_in_r, w_out_b, g_cross_r, w_cq_b,
                                                   small, _prompt_consts(), n_chunks)
    h1 = h1.reshape(tp, D_MODEL)
    qc = qc.reshape(tp, CA_DIM)
    mem2 = mem_prompt.reshape(batch * N_MEM, D_MODEL)
    mk, mv = _mem_kv(mem2, g_mem_r, w_ck_b, w_cv_b, _pick_tile(batch * N_MEM, 512))
    o = _cross_prompt(qc, mk.reshape(batch, N_MEM, CA_DIM), mv.reshape(batch, N_MEM, CA_DIM), batch, seq, tm_p)
    y_prompt = _post2(h1, o, w_co_b, g_ffn_r, w_gate_b, w_up_b, w_down_b, g_fin_r, tm_p)

    ts = n_dec * dec_seq
    xs2 = x_sample.reshape(ts, D_MODEL)
    tm_s = _pick_tile(ts, 512)
    z, xbc, q, k, v, dt = _in_proj(xs2, g_mix_r, w_in_r, tm_s)
    mix_s, s_conv, s_ssm, s_k, s_v = _sample_mixer(
        z, xbc, q, k, v, dt, state_conv[0], state_ssm[0].reshape(n_dec, SSD_D_INNER, D_STATE),
        cache_swa_k[0].reshape(n_dec, cache_len, KV_DIM), cache_swa_v[0].reshape(n_dec, cache_len, KV_DIM),
        small, _sample_consts(dec_seq, cache_len), n_dec, dec_seq)
    h1s, qcs = _post1(xs2, mix_s, w_out_b, g_cross_r, w_cq_b, tm_s)
    os_ = _cross_sample(qcs, cache_mem_k[0].reshape(n_dec, N_MEM * CA_HEADS, CA_HEAD_DIM),
                        cache_mem_v[0].reshape(n_dec, N_MEM * CA_HEADS, CA_HEAD_DIM), n_dec, dec_seq, 8)
    y_sample = _post2(h1s, os_, w_co_b, g_ffn_r, w_gate_b, w_up_b, w_down_b, g_fin_r, tm_s)

    return (y_prompt.reshape(batch, seq, D_MODEL), y_sample.reshape(n_dec, dec_seq, D_MODEL),
            p_conv[None], p_ssm.reshape(1, batch, SSD_HEADS, SSD_HEAD_DIM, D_STATE),
            p_k.reshape(1, batch, WINDOW, N_KV_HEADS, ATTN_HEAD_DIM),
            p_v.reshape(1, batch, WINDOW, N_KV_HEADS, ATTN_HEAD_DIM),
            mk.reshape(1, batch, N_MEM, CA_HEADS, CA_HEAD_DIM), mv.reshape(1, batch, N_MEM, CA_HEADS, CA_HEAD_DIM),
            s_conv[None], s_ssm.reshape(1, n_dec, SSD_HEADS, SSD_HEAD_DIM, D_STATE),
            s_k.reshape(1, n_dec, cache_len, N_KV_HEADS, ATTN_HEAD_DIM),
            s_v.reshape(1, n_dec, cache_len, N_KV_HEADS, ATTN_HEAD_DIM))
```

```python
import functools
import math

import numpy as np
import jax
import jax.numpy as jnp
from jax import lax
from jax.experimental import pallas as pl
from jax.experimental.pallas import tpu as pltpu

F32 = jnp.float32
BF16 = jnp.bfloat16

D_MODEL = 1024
SSD_D_INNER = 512
SSD_HEAD_DIM = 64
SSD_HEADS = 8
SSD_GROUPS = 2
GROUP_W = SSD_D_INNER // SSD_GROUPS
D_STATE = 128
CONV_K = 4
CONV_DIM = SSD_D_INNER + 2 * SSD_GROUPS * D_STATE
CHUNK = 128
ATTN_DIM = 512
ATTN_HEAD_DIM = 64
N_HEADS = 8
N_KV_HEADS = 2
Q_PER_KV = N_HEADS // N_KV_HEADS
KV_DIM = N_KV_HEADS * ATTN_HEAD_DIM
WINDOW = 128
N_BUCKETS = 32
MAX_EXACT = N_BUCKETS // 2
MAX_DISTANCE = 128
N_MEM = 256
CA_HEADS = 4
CA_HEAD_DIM = 128
CA_DIM = CA_HEADS * CA_HEAD_DIM
D_FF = 2816
EPS = 1e-6

LANES = 128
HALF = LANES // 2
DT_PAD = LANES
COL_Z = 0
COL_XBC = COL_Z + SSD_D_INNER
COL_Q = COL_XBC + CONV_DIM
COL_K = COL_Q + ATTN_DIM
COL_V = COL_K + KV_DIM
COL_DT = COL_V + KV_DIM
IN_COLS = COL_DT + DT_PAD
NEG = -1e30
SAMPLE_SEQS = 16
VMEM_LIMIT = 56 * 1024 * 1024
FF_SPLITS = ((0, 1024), (1024, 2048), (2048, D_FF))
FF_PIECES = tuple((lo, min(lo + 512, D_FF)) for lo in range(0, D_FF, 512))
PROJ_PIECES = ((COL_Z, COL_XBC), (COL_XBC, COL_XBC + 512), (COL_XBC + 512, COL_Q), (COL_Q, COL_K), (COL_K, IN_COLS))


def _rms(x, g):
    return x * lax.rsqrt(jnp.mean(x * x, axis=-1, keepdims=True) + EPS) * g


def _silu(x):
    return x * jax.nn.sigmoid(x)


def _softplus(x):
    return jnp.maximum(x, 0.0) + jnp.log1p(jnp.exp(-jnp.abs(x)))


def _dot(a, b):
    return jnp.dot(a, b, preferred_element_type=F32)


def _dot_nt(a, b):
    return lax.dot_general(a, b, (((1,), (1,)), ((), ())), preferred_element_type=F32)


def _split3(a):
    hi = a.astype(BF16)
    r = a - hi.astype(F32)
    mid = r.astype(BF16)
    lo = (r - mid.astype(F32)).astype(BF16)
    return hi, mid, lo


def _sel_left(t01, a):
    hi, mid, lo = _split3(a)
    return _dot(t01, hi) + _dot(t01, mid) + _dot(t01, lo)


def _sel_right(a, e01):
    hi, mid, lo = _split3(a)
    return _dot(hi, e01) + _dot(mid, e01) + _dot(lo, e01)


def _lane_lo():
    return lax.broadcasted_iota(jnp.int32, (1, LANES), 1) < HALF


def _t5_bucket_np(dist):
    n = np.maximum(dist, 0)
    ratio = np.log(np.maximum(n, 1).astype(np.float32) / np.float32(MAX_EXACT))
    large = MAX_EXACT + (ratio / np.float32(math.log(MAX_DISTANCE / MAX_EXACT))
                         * np.float32(N_BUCKETS - MAX_EXACT)).astype(np.int32)
    large = np.minimum(large, N_BUCKETS - 1)
    return np.where(n < MAX_EXACT, n, large).astype(np.int32)


def _tril_np(n):
    return np.tril(np.ones((n, n), np.float32))


def _expand_np():
    e = np.zeros((LANES, SSD_D_INNER), np.float32)
    for h in range(SSD_HEADS):
        e[h, h * SSD_HEAD_DIM:(h + 1) * SSD_HEAD_DIM] = 1.0
    return e


def _full_spec(shape):
    nd = len(shape)
    return pl.BlockSpec(shape, lambda *_: (0,) * nd, pipeline_mode=pl.Buffered(1))


def _smem_spec():
    return pl.BlockSpec(memory_space=pltpu.SMEM)


def _params(sem):
    return pltpu.CompilerParams(dimension_semantics=sem, vmem_limit_bytes=VMEM_LIMIT)


def _in_proj_body(x_ref, g_ref, w_ref, z_ref, xbc_ref, q_ref, k_ref, v_ref, dt_ref):
    xn = _rms(x_ref[...], g_ref[...]).astype(BF16)

    def seg(lo, hi):
        return _dot(xn, w_ref[:, lo:hi])

    z_ref[...] = seg(COL_Z, COL_XBC)
    xbc_ref[...] = seg(COL_XBC, COL_Q)
    q_ref[...] = (seg(COL_Q, COL_K) * (ATTN_HEAD_DIM ** -0.5)).astype(BF16)
    k_ref[...] = seg(COL_K, COL_V)
    v_ref[...] = seg(COL_V, COL_DT)
    dt_ref[...] = seg(COL_DT, IN_COLS)


def _in_proj(x2, g_mix, w_in_r, tm):
    t = x2.shape[0]
    row = lambda w: pl.BlockSpec((tm, w), lambda i: (i, 0))
    outs = [(SSD_D_INNER, F32), (CONV_DIM, F32), (ATTN_DIM, BF16), (KV_DIM, F32), (KV_DIM, F32), (DT_PAD, F32)]
    return pl.pallas_call(
        _in_proj_body,
        grid=(t // tm,),
        in_specs=[row(D_MODEL), _full_spec((1, D_MODEL)), _full_spec((D_MODEL, IN_COLS))],
        out_specs=[row(w) for w, _ in outs],
        out_shape=[jax.ShapeDtypeStruct((t, w), d) for w, d in outs],
        compiler_params=_params(("parallel",)),
        name="in_proj",
    )(x2, g_mix, w_in_r)


def _conv_taps(cw_ref, cb_ref, taps):
    acc = cb_ref[...] + taps[0] * cw_ref[0:1, :]
    for k in range(1, CONV_K):
        acc = acc + taps[k] * cw_ref[k:k + 1, :]
    return _silu(acc)


def _ssd_prepare(conv, dt_raw, dtb_ref, alog_ref, tcum, total_fn, extra_fn, expand):
    xs = conv[:, :SSD_D_INNER]
    bm = conv[:, SSD_D_INNER:SSD_D_INNER + SSD_GROUPS * D_STATE]
    cm = conv[:, SSD_D_INNER + SSD_GROUPS * D_STATE:]
    dt = _softplus(dt_raw + dtb_ref[...])
    a = dt * (-jnp.exp(alog_ref[...]))
    cs = _sel_left(tcum, a)
    total = total_fn(cs, a)
    pieces = [dt, jnp.exp(total - cs), jnp.exp(cs)] + extra_fn(total)
    rows = cs.shape[0]
    ex = _sel_right(jnp.concatenate(pieces, axis=0), expand)
    ex = [ex[i * rows:(i + 1) * rows] for i in range(len(pieces))]
    return xs, bm, cm, cs, cs.T, ex


def _ssd_diag(cs, cs_t, cb_g, xdt, mask, g):
    lo = _lane_lo()
    out = []
    for pr in range(2):
        h0 = g * 4 + 2 * pr
        xp = xdt[:, (h0 // 2) * LANES:(h0 // 2 + 1) * LANES]
        x_lo = jnp.where(lo, xp, 0.0).astype(BF16)
        x_hi = jnp.where(lo, 0.0, xp).astype(BF16)
        acc = None
        for h, xh in ((h0, x_lo), (h0 + 1, x_hi)):
            diff = cs[:, h:h + 1] - cs_t[h:h + 1, :]
            decay = jnp.exp(jnp.where(mask, diff, -jnp.inf))
            part = _dot((cb_g * decay).astype(BF16), xh)
            acc = part if acc is None else acc + part
        out.append(acc)
    return jnp.concatenate(out, axis=-1)


def _gated_norm(y, z, gn_ref):
    yf = y * _silu(z)
    parts = []
    for g in range(SSD_GROUPS):
        yg = yf[:, g * GROUP_W:(g + 1) * GROUP_W]
        parts.append(yg * lax.rsqrt(jnp.mean(yg * yg, axis=-1, keepdims=True) + EPS))
    return jnp.concatenate(parts, axis=-1) * gn_ref[...]


def _build_bias(bias_ref, bucket_of, relb_ref, n_tables):
    for i in range(n_tables):
        for h in range(N_HEADS):
            bias_ref[i, h] = jnp.full(bias_ref.shape[2:], NEG, F32)

    def body(t, carry):
        for i in range(n_tables):
            hit = bucket_of(i) == t
            for h in range(N_HEADS):
                bias_ref[i, h] = jnp.where(hit, relb_ref[t, h], bias_ref[i, h])
        return carry

    lax.fori_loop(0, N_BUCKETS, body, 0)


def _prompt_mixer_body(z_ref, xbc_ref, q_ref, k_ref, v_ref, dt_ref,
                       cw_ref, cb_ref, dtb_ref, alog_ref, dskip_ref, gn_ref,
                       tril_ref, expand_ref, bucket_ref, relb_ref, sink_ref,
                       mix_ref, conv_out, ssm_out, k_out, v_out,
                       xpad, state_t, kbuf, vbuf, bias):
    b = pl.program_id(0)
    c = pl.program_id(1)
    last = pl.num_programs(1) - 1
    q_rows = CHUNK

    @pl.when((b == 0) & (c == 0))
    def _():
        _build_bias(bias, lambda i: bucket_ref[i], relb_ref, 2)

    @pl.when(c == 0)
    def _():
        xpad[0:8, :] = jnp.zeros((8, CONV_DIM), F32)
        state_t[...] = jnp.zeros_like(state_t)
        kbuf[...] = jnp.zeros_like(kbuf)
        vbuf[...] = jnp.zeros_like(vbuf)

    xbc = xbc_ref[0]
    xpad[8:8 + q_rows, :] = xbc
    taps = [xpad[5 + k:5 + k + q_rows, :] for k in range(CONV_K - 1)] + [xbc]
    conv = _conv_taps(cw_ref, cb_ref, taps)
    xpad[5:8, :] = xbc[q_rows - 3:q_rows, :]

    row = lax.broadcasted_iota(jnp.int32, (q_rows, q_rows), 0)
    col = lax.broadcasted_iota(jnp.int32, (q_rows, q_rows), 1)
    causal = col <= row
    xs, bm, cm, cs, cs_t, (dt_x, dend_x, ecs_x) = _ssd_prepare(
        conv, dt_ref[0], dtb_ref, alog_ref, tril_ref[...],
        lambda cs_, a_: cs_[q_rows - 1:q_rows, :], lambda total: [], expand_ref[...])
    xdt = xs * dt_x
    xde = (xdt * dend_x).astype(BF16)
    bb = bm.astype(BF16)
    cb = cm.astype(BF16)
    y_parts = []
    for g in range(SSD_GROUPS):
        gs = slice(g * GROUP_W, (g + 1) * GROUP_W)
        ns = slice(g * D_STATE, (g + 1) * D_STATE)
        cb_g = _dot_nt(cb[:, ns], bb[:, ns])
        y_diag = _ssd_diag(cs, cs_t, cb_g, xdt, causal, g)
        st_old = state_t[:, gs]
        y_off = _dot(cb[:, ns], st_old.astype(BF16)) * ecs_x[:, gs]
        y_parts.append(y_diag + y_off)
        b_t = bm[:, ns].T.astype(BF16)
        state_t[:, gs] = ecs_x[q_rows - 1:q_rows, gs] * st_old + _dot(b_t, xde[:, gs])
    y = jnp.concatenate(y_parts, axis=-1) + dskip_ref[...] * xs
    mix_ref[0, :, 0:SSD_D_INNER] = _gated_norm(y, z_ref[0], gn_ref).astype(BF16)

    lo = _lane_lo()
    k_cur = k_ref[0]
    v_cur = v_ref[0]
    kbuf[0, q_rows:, :] = k_cur.astype(BF16)
    kbuf[1, q_rows:, :] = pltpu.roll(k_cur, HALF, 1).astype(BF16)
    vbuf[0, q_rows:, :] = v_cur.astype(BF16)
    vbuf[1, q_rows:, :] = pltpu.roll(v_cur, HALF, 1).astype(BF16)
    table = jnp.minimum(c, 1)
    for pair in range(N_HEADS // 2):
        j = (2 * pair) // Q_PER_KV
        q_pair = q_ref[0, :, pair * LANES:(pair + 1) * LANES]
        acc = None
        for par in range(2):
            h = 2 * pair + par
            qm = jnp.where(lo, q_pair, 0.0) if par == 0 else jnp.where(lo, 0.0, q_pair)
            variant = (j + par) % 2
            s = _dot_nt(qm.astype(BF16), kbuf[variant]) + bias[table, h]
            sink = sink_ref[h]
            m = jnp.maximum(jnp.max(s, axis=-1, keepdims=True), sink)
            p = jnp.exp(s - m)
            denom = jnp.sum(p, axis=-1, keepdims=True) + jnp.exp(sink - m)
            vv = vbuf[variant]
            vm = jnp.where(lo, vv, 0.0) if par == 0 else jnp.where(lo, 0.0, vv)
            part = _dot(p.astype(BF16), vm.astype(BF16)) / denom
            acc = part if acc is None else acc + part
        mix_ref[0, :, SSD_D_INNER + pair * LANES:SSD_D_INNER + (pair + 1) * LANES] = acc.astype(BF16)
    kbuf[:, 0:q_rows, :] = kbuf[:, q_rows:, :]
    vbuf[:, 0:q_rows, :] = vbuf[:, q_rows:, :]

    @pl.when(c == last)
    def _():
        conv_out[0] = xbc[q_rows - 3:q_rows, :]
        ssm_out[0] = state_t[...].T
        k_out[0] = k_cur
        v_out[0] = v_cur


def _prompt_mixer(z, xbc, q, k, v, dt, small, consts, batch, seq):
    cw, cb, dtb, alog, dskip_x, gn, relb, sinks = small
    tril, expand, buckets = consts
    nc = seq // CHUNK
    r3 = lambda a: a.reshape(batch, seq, a.shape[-1])
    tok = lambda w: pl.BlockSpec((1, CHUNK, w), lambda b, c: (b, c, 0))
    per_b = lambda s: pl.BlockSpec((1,) + s, lambda b, c: (b,) + (0,) * len(s))
    out_shape = [
        jax.ShapeDtypeStruct((batch, seq, D_MODEL), BF16),
        jax.ShapeDtypeStruct((batch, CONV_K - 1, CONV_DIM), F32),
        jax.ShapeDtypeStruct((batch, SSD_D_INNER, D_STATE), F32),
        jax.ShapeDtypeStruct((batch, WINDOW, KV_DIM), F32),
        jax.ShapeDtypeStruct((batch, WINDOW, KV_DIM), F32),
    ]
    return pl.pallas_call(
        _prompt_mixer_body,
        grid=(batch, nc),
        in_specs=[tok(SSD_D_INNER), tok(CONV_DIM), tok(ATTN_DIM), tok(KV_DIM), tok(KV_DIM), tok(DT_PAD),
                  _full_spec(cw.shape), _full_spec(cb.shape), _full_spec(dtb.shape), _full_spec(alog.shape),
                  _full_spec(dskip_x.shape), _full_spec(gn.shape),
                  _full_spec(tril.shape), _full_spec(expand.shape), _full_spec(buckets.shape),
                  _smem_spec(), _smem_spec()],
        out_specs=[tok(D_MODEL), per_b((CONV_K - 1, CONV_DIM)), per_b((SSD_D_INNER, D_STATE)),
                   per_b((WINDOW, KV_DIM)), per_b((WINDOW, KV_DIM))],
        out_shape=out_shape,
        scratch_shapes=[
            pltpu.VMEM((8 + CHUNK, CONV_DIM), F32),
            pltpu.VMEM((D_STATE, SSD_D_INNER), F32),
            pltpu.VMEM((2, 2 * WINDOW, KV_DIM), BF16),
            pltpu.VMEM((2, 2 * WINDOW, KV_DIM), BF16),
            pltpu.VMEM((2, N_HEADS, WINDOW, 2 * WINDOW), F32),
        ],
        compiler_params=_params(("arbitrary", "arbitrary")),
        name="prompt_mixer",
    )(r3(z), r3(xbc), r3(q), r3(k), r3(v), r3(dt), cw, cb, dtb, alog, dskip_x, gn,
      tril, expand, buckets, relb, sinks)


def _prompt_layer_body(x_ref, mk_ref, mv_ref, gmix_ref, win_ref, wout_ref, gc_ref, wcq_ref,
                       wco_ref, gf_ref, wg_ref, wu_ref, wd_ref, gfin_ref,
                       cw_ref, cb_ref, dtb_ref, alog_ref, dskip_ref, gn_ref,
                       tril_ref, expand_ref, bucket_ref, relb_ref, sink_ref,
                       y_ref, conv_out, ssm_out, k_out, v_out,
                       xpad, state_t, kbuf, vbuf, bias, mix, hbuf, qcbuf, *, n_chunks, tiles_per_seq):
    step = pl.program_id(0)
    n_tiles = pl.num_programs(0) - 1
    c = lax.rem(jnp.minimum(step, n_tiles - 1), tiles_per_seq)
    last = tiles_per_seq - 1
    tq = n_chunks * CHUNK

    @pl.when(step == 0)
    def _():
        _build_bias(bias, lambda i: bucket_ref[i], relb_ref, 2)
        hbuf[...] = jnp.zeros_like(hbuf)
        qcbuf[...] = jnp.zeros_like(qcbuf)

    @pl.when(c == 0)
    def _():
        xpad[0:8, :] = jnp.zeros((8, CONV_DIM), F32)
        state_t[...] = jnp.zeros_like(state_t)
        kbuf[:, 0:CHUNK, :] = jnp.zeros((2, CHUNK, KV_DIM), BF16)
        vbuf[:, 0:CHUNK, :] = jnp.zeros((2, CHUNK, KV_DIM), BF16)

    o = _cross_heads(qcbuf[...], mk_ref[0], mv_ref[0]).astype(BF16)
    h2 = hbuf[...] + _dot(o, wco_ref[...])
    hn = _rms(h2, gf_ref[...]).astype(BF16)
    acc = h2
    for f_lo, f_hi in FF_SPLITS:
        act = _silu(_dot(hn, wg_ref[:, f_lo:f_hi])) * _dot(hn, wu_ref[:, f_lo:f_hi])
        acc = acc + _dot(act.astype(BF16), wd_ref[f_lo:f_hi, :])
    y_ref[0] = _rms(acc, gfin_ref[...])

    x = x_ref[0]
    xn = _rms(x, gmix_ref[...]).astype(BF16)

    def seg(lo, hi):
        return _dot(xn, win_ref[:, lo:hi])

    z = seg(COL_Z, COL_XBC)
    xbc = seg(COL_XBC, COL_Q)
    q = (seg(COL_Q, COL_K) * (ATTN_HEAD_DIM ** -0.5)).astype(BF16)
    k_new = seg(COL_K, COL_V)
    v_new = seg(COL_V, COL_DT)
    dt_raw = seg(COL_DT, IN_COLS)

    xpad[8:8 + tq, :] = xbc
    taps = [xpad[5 + k:5 + k + tq, :] for k in range(CONV_K - 1)] + [xbc]
    conv = _conv_taps(cw_ref, cb_ref, taps)
    xpad[5:8, :] = xbc[tq - 3:tq, :]

    kbuf[0, CHUNK:, :] = k_new.astype(BF16)
    kbuf[1, CHUNK:, :] = pltpu.roll(k_new, HALF, 1).astype(BF16)
    vbuf[0, CHUNK:, :] = v_new.astype(BF16)
    vbuf[1, CHUNK:, :] = pltpu.roll(v_new, HALF, 1).astype(BF16)

    row = lax.broadcasted_iota(jnp.int32, (CHUNK, CHUNK), 0)
    col = lax.broadcasted_iota(jnp.int32, (CHUNK, CHUNK), 1)
    causal = col <= row
    lo = _lane_lo()
    for ci in range(n_chunks):
        rs = slice(ci * CHUNK, (ci + 1) * CHUNK)
        xs, bm, cm, cs, cs_t, (dt_x, dend_x, ecs_x) = _ssd_prepare(
            conv[rs], dt_raw[rs], dtb_ref, alog_ref, tril_ref[...],
            lambda cs_, a_: cs_[CHUNK - 1:CHUNK, :], lambda total: [], expand_ref[...])
        xdt = xs * dt_x
        xde = (xdt * dend_x).astype(BF16)
        bb = bm.astype(BF16)
        cb = cm.astype(BF16)
        y_parts = []
        for g in range(SSD_GROUPS):
            gs = slice(g * GROUP_W, (g + 1) * GROUP_W)
            ns = slice(g * D_STATE, (g + 1) * D_STATE)
            cb_g = _dot_nt(cb[:, ns], bb[:, ns])
            y_diag = _ssd_diag(cs, cs_t, cb_g, xdt, causal, g)
            st_old = state_t[:, gs]
            y_off = _dot(cb[:, ns], st_old.astype(BF16)) * ecs_x[:, gs]
            y_parts.append(y_diag + y_off)
            b_t = bm[:, ns].T.astype(BF16)
            state_t[:, gs] = ecs_x[CHUNK - 1:CHUNK, gs] * st_old + _dot(b_t, xde[:, gs])
        y = jnp.concatenate(y_parts, axis=-1) + dskip_ref[...] * xs
        mix[rs, 0:SSD_D_INNER] = _gated_norm(y, z[rs], gn_ref).astype(BF16)

        keys = slice(ci * CHUNK, (ci + 2) * CHUNK)
        table = jnp.minimum(c, 1) if ci == 0 else 1
        for pair in range(N_HEADS // 2):
            j = (2 * pair) // Q_PER_KV
            q_pair = q[rs, pair * LANES:(pair + 1) * LANES]
            acc = None
            for par in range(2):
                h = 2 * pair + par
                qm = jnp.where(lo, q_pair, 0.0) if par == 0 else jnp.where(lo, 0.0, q_pair)
                variant = (j + par) % 2
                s = _dot_nt(qm.astype(BF16), kbuf[variant, keys, :]) + bias[table, h]
                sink = sink_ref[h]
                m = jnp.maximum(jnp.max(s, axis=-1, keepdims=True), sink)
                p = jnp.exp(s - m)
                denom = jnp.sum(p, axis=-1, keepdims=True) + jnp.exp(sink - m)
                vv = vbuf[variant, keys, :]
                vm = jnp.where(lo, vv, 0.0) if par == 0 else jnp.where(lo, 0.0, vv)
                part = _dot(p.astype(BF16), vm.astype(BF16)) / denom
                acc = part if acc is None else acc + part
            mix[rs, SSD_D_INNER + pair * LANES:SSD_D_INNER + (pair + 1) * LANES] = acc.astype(BF16)
    kbuf[:, 0:CHUNK, :] = kbuf[:, tq:tq + CHUNK, :]
    vbuf[:, 0:CHUNK, :] = vbuf[:, tq:tq + CHUNK, :]

    h1 = x + _dot(mix[...], wout_ref[...])
    hbuf[...] = h1
    qcbuf[...] = _dot(_rms(h1, gc_ref[...]).astype(BF16), wcq_ref[...]).astype(BF16)

    @pl.when((c == last) & (step < n_tiles))
    def _():
        conv_out[0] = xbc[tq - 3:tq, :]
        ssm_out[0] = state_t[...].T
        k_out[0] = k_new[tq - WINDOW:tq, :]
        v_out[0] = v_new[tq - WINDOW:tq, :]


def _alternate(first, second):
    streams = [[0, first], [0, second]]
    while streams:
        entry = min(streams, key=lambda e: e[0])
        try:
            entry[0] += next(entry[1])
        except StopIteration:
            streams.remove(entry)


def _layer_step_body(x_ref, mk_ref, mv_ref, gmix_ref, win_ref, wout_ref, gc_ref, wcq_ref,
                     wco_ref, gf_ref, wg_ref, wu_ref, wd_ref, gfin_ref,
                     cw_ref, cb_ref, dtb_ref, alog_ref, dskip_ref, gn_ref,
                     tril_ref, expand_ref, bucket_ref, relb_ref, sink_ref,
                     y_ref, conv_out, ssm_out, k_out, v_out,
                     xpad, state_t, kbuf, vbuf, bias, mix, hbuf, proj, *, n_chunks, tiles_per_seq):
    step = pl.program_id(0)
    n_tiles = pl.num_programs(0) - 1
    c = lax.rem(jnp.minimum(step, n_tiles - 1), tiles_per_seq)
    last = tiles_per_seq - 1
    tq = n_chunks * CHUNK

    @pl.when(step == 0)
    def _():
        _build_bias(bias, lambda i: bucket_ref[i], relb_ref, 2)
        hbuf[...] = jnp.zeros_like(hbuf)
        mix[...] = jnp.zeros_like(mix)

    @pl.when(c == 0)
    def _():
        xpad[0:8, :] = jnp.zeros((8, CONV_DIM), F32)
        state_t[...] = jnp.zeros_like(state_t)
        kbuf[:, 0:CHUNK, :] = jnp.zeros((2, CHUNK, KV_DIM), BF16)
        vbuf[:, 0:CHUNK, :] = jnp.zeros((2, CHUNK, KV_DIM), BF16)

    def back():
        h1 = hbuf[...] + _dot(mix[...], wout_ref[...])
        yield 4
        qc = _dot(_rms(h1, gc_ref[...]).astype(BF16), wcq_ref[...]).astype(BF16)
        yield 2
        heads = []
        for h in range(CA_HEADS):
            hs = slice(h * CA_HEAD_DIM, (h + 1) * CA_HEAD_DIM)
            sc = _dot_nt(qc[:, hs], mk_ref[0, :, hs]) * (CA_HEAD_DIM ** -0.5)
            m = jnp.max(sc, axis=-1, keepdims=True)
            p = jnp.exp(sc - m)
            heads.append(_dot(p.astype(BF16), mv_ref[0, :, hs]) / jnp.sum(p, axis=-1, keepdims=True))
            yield 3
        o = jnp.concatenate(heads, axis=-1).astype(BF16)
        h2 = h1 + _dot(o, wco_ref[...])
        hn = _rms(h2, gf_ref[...]).astype(BF16)
        yield 4
        acc = h2
        for f_lo, f_hi in FF_PIECES:
            gate = _dot(hn, wg_ref[:, f_lo:f_hi])
            yield 5
            act = (_silu(gate) * _dot(hn, wu_ref[:, f_lo:f_hi])).astype(BF16)
            yield 6
            acc = acc + _dot(act, wd_ref[f_lo:f_hi, :])
            yield 5
        y_ref[0] = _rms(acc, gfin_ref[...])

    def front():
        x = x_ref[0]
        xn = _rms(x, gmix_ref[...]).astype(BF16)
        for p_lo, p_hi in PROJ_PIECES:
            proj[:, p_lo:p_hi] = _dot(xn, win_ref[:, p_lo:p_hi])
            yield 4

        xpad[8:8 + tq, :] = proj[:, COL_XBC:COL_Q]
        conv_halves = []
        half_w = CONV_DIM // 2
        for hf in range(2):
            cs_ = slice(hf * half_w, (hf + 1) * half_w)
            acc = cb_ref[:, cs_] + xpad[5:5 + tq, cs_] * cw_ref[0:1, cs_]
            for kk in range(1, CONV_K):
                acc = acc + xpad[5 + kk:5 + kk + tq, cs_] * cw_ref[kk:kk + 1, cs_]
            conv_halves.append(_silu(acc))
            yield 6
        xpad[5:8, :] = xpad[5 + tq:8 + tq, :]
        xs_all = conv_halves[0]
        bc_all = conv_halves[1]

        k_new = proj[:, COL_K:COL_V]
        v_new = proj[:, COL_V:COL_DT]
        kbuf[0, CHUNK:, :] = k_new.astype(BF16)
        kbuf[1, CHUNK:, :] = pltpu.roll(k_new, HALF, 1).astype(BF16)
        vbuf[0, CHUNK:, :] = v_new.astype(BF16)
        vbuf[1, CHUNK:, :] = pltpu.roll(v_new, HALF, 1).astype(BF16)
        q = (proj[:, COL_Q:COL_K] * (ATTN_HEAD_DIM ** -0.5)).astype(BF16)
        yield 2

        row = lax.broadcasted_iota(jnp.int32, (CHUNK, CHUNK), 0)
        col = lax.broadcasted_iota(jnp.int32, (CHUNK, CHUNK), 1)
        causal = col <= row
        lo = _lane_lo()
        for ci in range(n_chunks):
            rs = slice(ci * CHUNK, (ci + 1) * CHUNK)
            xs = xs_all[rs]
            bm = bc_all[rs, 0:SSD_GROUPS * D_STATE]
            cm = bc_all[rs, SSD_GROUPS * D_STATE:]
            dt = _softplus(proj[rs, COL_DT:IN_COLS] + dtb_ref[...])
            a = dt * (-jnp.exp(alog_ref[...]))
            cs = _sel_left(tril_ref[...], a)
            cs_t = cs.T
            total = cs[CHUNK - 1:CHUNK, :]
            ex = _sel_right(jnp.concatenate([dt, jnp.exp(total - cs), jnp.exp(cs)], axis=0), expand_ref[...])
            dt_x, dend_x, ecs_x = ex[0:CHUNK], ex[CHUNK:2 * CHUNK], ex[2 * CHUNK:3 * CHUNK]
            xdt = xs * dt_x
            xde = (xdt * dend_x).astype(BF16)
            bb = bm.astype(BF16)
            cb = cm.astype(BF16)
            yield 7
            y_parts = []
            for g in range(SSD_GROUPS):
                gs = slice(g * GROUP_W, (g + 1) * GROUP_W)
                ns = slice(g * D_STATE, (g + 1) * D_STATE)
                cb_g = _dot_nt(cb[:, ns], bb[:, ns])
                y_diag = _ssd_diag(cs, cs_t, cb_g, xdt, causal, g)
                st_old = state_t[:, gs]
                y_off = _dot(cb[:, ns], st_old.astype(BF16)) * ecs_x[:, gs]
                y_parts.append(y_diag + y_off)
                b_t = bm[:, ns].T.astype(BF16)
                state_t[:, gs] = ecs_x[CHUNK - 1:CHUNK, gs] * st_old + _dot(b_t, xde[:, gs])
                yield 9
            y = jnp.concatenate(y_parts, axis=-1) + dskip_ref[...] * xs
            mix[rs, 0:SSD_D_INNER] = _gated_norm(y, proj[rs, COL_Z:COL_XBC], gn_ref).astype(BF16)
            yield 3

            keys = slice(ci * CHUNK, (ci + 2) * CHUNK)
            table = jnp.minimum(c, 1) if ci == 0 else 1
            for pair in range(N_HEADS // 2):
                j = (2 * pair) // Q_PER_KV
                q_pair = q[rs, pair * LANES:(pair + 1) * LANES]
                acc = None
                for par in range(2):
                    h = 2 * pair + par
                    qm = jnp.where(lo, q_pair, 0.0) if par == 0 else jnp.where(lo, 0.0, q_pair)
                    variant = (j + par) % 2
                    sc = _dot_nt(qm.astype(BF16), kbuf[variant, keys, :]) + bias[table, h]
                    sink = sink_ref[h]
                    m = jnp.maximum(jnp.max(sc, axis=-1, keepdims=True), sink)
                    p = jnp.exp(sc - m)
                    denom = jnp.sum(p, axis=-1, keepdims=True) + jnp.exp(sink - m)
                    vv = vbuf[variant, keys, :]
                    vm = jnp.where(lo, vv, 0.0) if par == 0 else jnp.where(lo, 0.0, vv)
                    part = _dot(p.astype(BF16), vm.astype(BF16)) / denom
                    acc = part if acc is None else acc + part
                mix[rs, SSD_D_INNER + pair * LANES:SSD_D_INNER + (pair + 1) * LANES] = acc.astype(BF16)
                yield 5
        kbuf[:, 0:CHUNK, :] = kbuf[:, tq:tq + CHUNK, :]
        vbuf[:, 0:CHUNK, :] = vbuf[:, tq:tq + CHUNK, :]

        hbuf[...] = x

    _alternate(back(), front())

    @pl.when((c == last) & (step < n_tiles))
    def _():
        conv_out[0] = xpad[5:8, :]
        ssm_out[0] = state_t[...].T
        k_out[0] = proj[tq - WINDOW:tq, COL_K:COL_V]
        v_out[0] = proj[tq - WINDOW:tq, COL_V:COL_DT]


def _prompt_layer(x3, mk_b, mv_b, dense, small, consts, n_chunks):
    cw, cb, dtb, alog, dskip_x, gn, relb, sinks = small
    tril, expand, buckets = consts
    batch, seq, _ = x3.shape
    tq = n_chunks * CHUNK
    tps = seq // tq
    n_tiles = batch * tps
    front = lambda s: jnp.minimum(s, n_tiles - 1)
    back = lambda s: jnp.maximum(s - 1, 0)
    x_spec = pl.BlockSpec((1, tq, D_MODEL), lambda s: (front(s) // tps, front(s) % tps, 0))
    y_spec = pl.BlockSpec((1, tq, D_MODEL), lambda s: (back(s) // tps, back(s) % tps, 0))
    mem_spec = pl.BlockSpec((1, N_MEM, CA_DIM), lambda s: (back(s) // tps, 0, 0))
    per_b = lambda shp: pl.BlockSpec((1,) + shp, lambda s: (front(s) // tps,) + (0,) * len(shp))
    out_shape = [
        jax.ShapeDtypeStruct((batch, seq, D_MODEL), F32),
        jax.ShapeDtypeStruct((batch, CONV_K - 1, CONV_DIM), F32),
        jax.ShapeDtypeStruct((batch, SSD_D_INNER, D_STATE), F32),
        jax.ShapeDtypeStruct((batch, WINDOW, KV_DIM), F32),
        jax.ShapeDtypeStruct((batch, WINDOW, KV_DIM), F32),
    ]
    full = list(dense) + [cw, cb, dtb, alog, dskip_x, gn, tril, expand, buckets]
    return pl.pallas_call(
        functools.partial(_layer_step_body, n_chunks=n_chunks, tiles_per_seq=tps),
        grid=(n_tiles + 1,),
        in_specs=[x_spec, mem_spec, mem_spec] + [_full_spec(a.shape) for a in full] + [_smem_spec(), _smem_spec()],
        out_specs=[y_spec, per_b((CONV_K - 1, CONV_DIM)), per_b((SSD_D_INNER, D_STATE)),
                   per_b((WINDOW, KV_DIM)), per_b((WINDOW, KV_DIM))],
        out_shape=out_shape,
        scratch_shapes=[
            pltpu.VMEM((8 + tq, CONV_DIM), F32),
            pltpu.VMEM((D_STATE, SSD_D_INNER), F32),
            pltpu.VMEM((2, CHUNK + tq, KV_DIM), BF16),
            pltpu.VMEM((2, CHUNK + tq, KV_DIM), BF16),
            pltpu.VMEM((2, N_HEADS, WINDOW, 2 * WINDOW), F32),
            pltpu.VMEM((tq, D_MODEL), BF16),
            pltpu.VMEM((tq, D_MODEL), F32),
            pltpu.VMEM((tq, IN_COLS), F32),
        ],
        compiler_params=_params(("arbitrary",)),
        name="prompt_layer",
    )(x3, mk_b, mv_b, *full, relb, sinks)


def _sample_mixer_body(z_ref, xbc_ref, q_ref, k_ref, v_ref, dt_ref,
                       sconv_ref, sssm_ref, ck_ref, cv_ref,
                       cw_ref, cb_ref, dtb_ref, alog_ref, dskip_ref, gn_ref,
                       tcum_ref, tseq_ref, expand_ref, bkt_c_ref, bkt_n_ref, relb_ref, sink_ref,
                       mix_ref, conv_out, ssm_out, k_out, v_out,
                       xpad, bias_c, bias_n, *, dec_seq):
    step = pl.program_id(0)
    n_seq = SAMPLE_SEQS
    rows = n_seq * dec_seq

    @pl.when(step == 0)
    def _():
        _build_bias(bias_c, lambda i: bkt_c_ref[...], relb_ref, 1)
        _build_bias(bias_n, lambda i: bkt_n_ref[...], relb_ref, 1)

    xbc = xbc_ref[...]
    xpad[:, 8:8 + dec_seq, :] = xbc.reshape(n_seq, dec_seq, CONV_DIM)
    xpad[:, 5:8, :] = sconv_ref[...]
    taps = [xpad[:, 5 + k:5 + k + dec_seq, :].reshape(rows, CONV_DIM) for k in range(CONV_K - 1)] + [xbc]
    conv = _conv_taps(cw_ref, cb_ref, taps)
    conv_out[...] = xpad[:, 5 + dec_seq:8 + dec_seq, :]

    row = lax.broadcasted_iota(jnp.int32, (rows, rows), 0)
    col = lax.broadcasted_iota(jnp.int32, (rows, rows), 1)
    tseq = tseq_ref[...]
    same_seq = tseq > 0
    causal = same_seq & (col <= row)
    xs, bm, cm, cs, cs_t, (dt_x, dend_x, ecs_x, seqdec_x) = _ssd_prepare(
        conv, dt_ref[...], dtb_ref, alog_ref, tcum_ref[...],
        lambda cs_, a_: _sel_left(tseq, a_), lambda total: [jnp.exp(total)], expand_ref[...])
    xdt = xs * dt_x
    xde = (xdt * dend_x).astype(BF16)
    bb = bm.astype(BF16)
    cb = cm.astype(BF16)
    seq_of_row = lax.broadcasted_iota(jnp.int32, (rows, 1), 0) // dec_seq
    seq_of_lane = lax.broadcasted_iota(jnp.int32, (1, rows), 1) // dec_seq
    b_t = [bm[:, g * D_STATE:(g + 1) * D_STATE].T for g in range(SSD_GROUPS)]
    y_off = [None] * SSD_GROUPS
    for i in range(n_seq):
        st_t = sssm_ref[i].T
        new_parts = []
        for g in range(SSD_GROUPS):
            gs = slice(g * GROUP_W, (g + 1) * GROUP_W)
            ns = slice(g * D_STATE, (g + 1) * D_STATE)
            c_i = jnp.where(seq_of_row == i, cm[:, ns], 0.0).astype(BF16)
            part = _dot(c_i, st_t[:, gs].astype(BF16))
            y_off[g] = part if y_off[g] is None else y_off[g] + part
            b_i = jnp.where(seq_of_lane == i, b_t[g], 0.0).astype(BF16)
            dec = seqdec_x[i * dec_seq:i * dec_seq + 1, gs]
            new_parts.append(dec * st_t[:, gs] + _dot(b_i, xde[:, gs]))
        ssm_out[i] = jnp.concatenate(new_parts, axis=-1).T
    y_parts = []
    for g in range(SSD_GROUPS):
        gs = slice(g * GROUP_W, (g + 1) * GROUP_W)
        ns = slice(g * D_STATE, (g + 1) * D_STATE)
        cb_g = _dot_nt(cb[:, ns], bb[:, ns])
        y_parts.append(_ssd_diag(cs, cs_t, cb_g, xdt, causal, g) + y_off[g] * ecs_x[:, gs])
    y = jnp.concatenate(y_parts, axis=-1) + dskip_ref[...] * xs
    mix_ref[:, 0:SSD_D_INNER] = _gated_norm(y, z_ref[...], gn_ref).astype(BF16)

    lo = _lane_lo()
    k_new = k_ref[...]
    v_new = v_ref[...]
    k_var = [k_new.astype(BF16), pltpu.roll(k_new, HALF, 1).astype(BF16)]
    v_new_r = pltpu.roll(v_new, HALF, 1)
    v_dup = [jnp.where(lo, v_new, v_new_r).astype(BF16), jnp.where(lo, v_new_r, v_new).astype(BF16)]
    qf = q_ref[...].astype(F32)
    q_masked = []
    s_new = []
    for h in range(N_HEADS):
        pair, par = h // 2, h % 2
        j = h // Q_PER_KV
        q_pair = qf[:, pair * LANES:(pair + 1) * LANES]
        qm = jnp.where(lo, q_pair, 0.0) if par == 0 else jnp.where(lo, 0.0, q_pair)
        q_masked.append(qm)
        s_new.append(_dot_nt(qm.astype(BF16), k_var[(j + par) % 2]) + bias_n[0, h])
    stack_rows = lax.broadcasted_iota(jnp.int32, (Q_PER_KV * dec_seq, 1), 0) // dec_seq
    att_rows = []
    for i in range(n_seq):
        rs = slice(i * dec_seq, (i + 1) * dec_seq)
        kc = ck_ref[i]
        vc = cv_ref[i]
        kc_r = pltpu.roll(kc, HALF, 1)
        vc_r = pltpu.roll(vc, HALF, 1)
        pieces = []
        for j in range(N_KV_HEADS):
            heads = range(j * Q_PER_KV, (j + 1) * Q_PER_KV)
            kdup = (jnp.where(lo, kc, kc_r) if j == 0 else jnp.where(lo, kc_r, kc)).astype(BF16)
            vdup = (jnp.where(lo, vc, vc_r) if j == 0 else jnp.where(lo, vc_r, vc)).astype(BF16)
            qs = jnp.concatenate([q_masked[h][rs] for h in heads], axis=0).astype(BF16)
            sc = _dot_nt(qs, kdup) + jnp.concatenate([bias_c[0, h] for h in heads], axis=0)
            sn = jnp.concatenate([s_new[h][rs] for h in heads], axis=0)
            sink = jnp.zeros((Q_PER_KV * dec_seq, 1), F32)
            for hh, h in enumerate(heads):
                sink = jnp.where(stack_rows == hh, sink_ref[h], sink)
            m = jnp.maximum(jnp.maximum(jnp.max(sc, axis=-1, keepdims=True),
                                        jnp.max(sn, axis=-1, keepdims=True)), sink)
            pc = jnp.exp(sc - m)
            pn = jnp.exp(sn - m)
            denom = (jnp.sum(pc, axis=-1, keepdims=True) + jnp.sum(pn, axis=-1, keepdims=True)
                     + jnp.exp(sink - m))
            o = (_dot(pc.astype(BF16), vdup) + _dot(pn.astype(BF16), v_dup[j])) / denom
            for pr in range(Q_PER_KV // 2):
                even = o[(2 * pr) * dec_seq:(2 * pr + 1) * dec_seq]
                odd = o[(2 * pr + 1) * dec_seq:(2 * pr + 2) * dec_seq]
                pieces.append(jnp.where(lo, even, odd))
        att_rows.append(jnp.concatenate(pieces, axis=-1))
        keep = WINDOW - dec_seq
        k_out[i] = jnp.concatenate([kc[dec_seq:, :], k_new[rs]], axis=0) if keep else k_new[rs]
        v_out[i] = jnp.concatenate([vc[dec_seq:, :], v_new[rs]], axis=0) if keep else v_new[rs]
    mix_ref[:, SSD_D_INNER:] = jnp.concatenate(att_rows, axis=0).astype(BF16)


def _sample_mixer(z, xbc, q, k, v, dt, sconv, sssm, ck, cv, small, consts, n_seq_total, dec_seq):
    cw, cb, dtb, alog, dskip_x, gn, relb, sinks = small
    tcum, tseq, expand, bkt_c, bkt_n = consts
    rows = SAMPLE_SEQS * dec_seq
    tok = lambda w: pl.BlockSpec((rows, w), lambda i: (i, 0))
    per_s = lambda s: pl.BlockSpec((SAMPLE_SEQS,) + s, lambda i: (i,) + (0,) * len(s))
    out_shape = [
        jax.ShapeDtypeStruct((n_seq_total * dec_seq, D_MODEL), BF16),
        jax.ShapeDtypeStruct((n_seq_total, CONV_K - 1, CONV_DIM), F32),
        jax.ShapeDtypeStruct((n_seq_total, SSD_D_INNER, D_STATE), F32),
        jax.ShapeDtypeStruct((n_seq_total, WINDOW, KV_DIM), F32),
        jax.ShapeDtypeStruct((n_seq_total, WINDOW, KV_DIM), F32),
    ]
    return pl.pallas_call(
        functools.partial(_sample_mixer_body, dec_seq=dec_seq),
        grid=(n_seq_total // SAMPLE_SEQS,),
        in_specs=[tok(SSD_D_INNER), tok(CONV_DIM), tok(ATTN_DIM), tok(KV_DIM), tok(KV_DIM), tok(DT_PAD),
                  per_s((CONV_K - 1, CONV_DIM)), per_s((SSD_D_INNER, D_STATE)),
                  per_s((WINDOW, KV_DIM)), per_s((WINDOW, KV_DIM)),
                  _full_spec(cw.shape), _full_spec(cb.shape), _full_spec(dtb.shape), _full_spec(alog.shape),
                  _full_spec(dskip_x.shape), _full_spec(gn.shape),
                  _full_spec(tcum.shape), _full_spec(tseq.shape), _full_spec(expand.shape),
                  _full_spec(bkt_c.shape), _full_spec(bkt_n.shape), _smem_spec(), _smem_spec()],
        out_specs=[tok(D_MODEL), per_s((CONV_K - 1, CONV_DIM)), per_s((SSD_D_INNER, D_STATE)),
                   per_s((WINDOW, KV_DIM)), per_s((WINDOW, KV_DIM))],
        out_shape=out_shape,
        scratch_shapes=[
            pltpu.VMEM((SAMPLE_SEQS, 8 + dec_seq, CONV_DIM), F32),
            pltpu.VMEM((1, N_HEADS, dec_seq, WINDOW), F32),
            pltpu.VMEM((1, N_HEADS, rows, rows), F32),
        ],
        compiler_params=_params(("arbitrary",)),
        name="sample_mixer",
    )(z, xbc, q, k, v, dt, sconv, sssm, ck, cv, cw, cb, dtb, alog, dskip_x, gn,
      tcum, tseq, expand, bkt_c, bkt_n, relb, sinks)


def _post1_body(x_ref, mix_ref, wout_ref, gc_ref, wcq_ref, h_ref, qc_ref):
    h = x_ref[...] + _dot(mix_ref[...], wout_ref[...])
    h_ref[...] = h
    qc_ref[...] = _dot(_rms(h, gc_ref[...]).astype(BF16), wcq_ref[...]).astype(BF16)


def _post1(x2, mix, w_out, g_cross, w_cq, tm):
    t = x2.shape[0]
    row = lambda w: pl.BlockSpec((tm, w), lambda i: (i, 0))
    return pl.pallas_call(
        _post1_body,
        grid=(t // tm,),
        in_specs=[row(D_MODEL), row(D_MODEL), _full_spec(w_out.shape), _full_spec(g_cross.shape),
                  _full_spec(w_cq.shape)],
        out_specs=[row(D_MODEL), row(CA_DIM)],
        out_shape=[jax.ShapeDtypeStruct((t, D_MODEL), F32), jax.ShapeDtypeStruct((t, CA_DIM), BF16)],
        compiler_params=_params(("parallel",)),
        name="out_proj",
    )(x2, mix, w_out, g_cross, w_cq)


def _mem_kv_body(mem_ref, g_ref, wk_ref, wv_ref, k_ref, v_ref, kb_ref, vb_ref):
    mn = _rms(mem_ref[...], g_ref[...]).astype(BF16)
    k = _dot(mn, wk_ref[...])
    v = _dot(mn, wv_ref[...])
    k_ref[...] = k
    v_ref[...] = v
    kb_ref[...] = k.astype(BF16)
    vb_ref[...] = v.astype(BF16)


def _mem_kv(mem2, g_mem, w_ck, w_cv, tm):
    t = mem2.shape[0]
    row = lambda w: pl.BlockSpec((tm, w), lambda i: (i, 0))
    return pl.pallas_call(
        _mem_kv_body,
        grid=(t // tm,),
        in_specs=[row(D_MODEL), _full_spec(g_mem.shape), _full_spec(w_ck.shape), _full_spec(w_cv.shape)],
        out_specs=[row(CA_DIM)] * 4,
        out_shape=[jax.ShapeDtypeStruct((t, CA_DIM), F32)] * 2 + [jax.ShapeDtypeStruct((t, CA_DIM), BF16)] * 2,
        compiler_params=_params(("parallel",)),
        name="mem_kv",
    )(mem2, g_mem, w_ck, w_cv)


def _cross_heads(q, k, v):
    out = []
    for h in range(CA_HEADS):
        hs = slice(h * CA_HEAD_DIM, (h + 1) * CA_HEAD_DIM)
        s = _dot_nt(q[:, hs], k[:, hs]) * (CA_HEAD_DIM ** -0.5)
        m = jnp.max(s, axis=-1, keepdims=True)
        p = jnp.exp(s - m)
        out.append(_dot(p.astype(BF16), v[:, hs]) / jnp.sum(p, axis=-1, keepdims=True))
    return jnp.concatenate(out, axis=-1)


def _cross_prompt_body(q_ref, k_ref, v_ref, o_ref):
    o_ref[0] = _cross_heads(q_ref[0], k_ref[0].astype(BF16), v_ref[0].astype(BF16)).astype(BF16)


def _cross_prompt(qc, mk, mv, batch, seq, tm):
    q3 = qc.reshape(batch, seq, CA_DIM)
    tok = pl.BlockSpec((1, tm, CA_DIM), lambda b, i: (b, i, 0))
    mem = pl.BlockSpec((1, N_MEM, CA_DIM), lambda b, i: (b, 0, 0))
    return pl.pallas_call(
        _cross_prompt_body,
        grid=(batch, seq // tm),
        in_specs=[tok, mem, mem],
        out_specs=tok,
        out_shape=jax.ShapeDtypeStruct((batch, seq, CA_DIM), BF16),
        compiler_params=_params(("parallel", "parallel")),
        name="cross_prompt",
    )(q3, mk, mv).reshape(batch * seq, CA_DIM)


def _cross_sample_body(q_ref, k_ref, v_ref, o_ref, *, n_seq, dec_seq):
    q = q_ref[...].astype(F32)
    n_keys = N_MEM * CA_HEADS
    col_head = lax.broadcasted_iota(jnp.int32, (1, n_keys), 1) & (CA_HEADS - 1)
    row_head = lax.broadcasted_iota(jnp.int32, (CA_HEADS * dec_seq, 1), 0) // dec_seq
    own = col_head == row_head
    rows = []
    for i in range(n_seq):
        qi = q[i * dec_seq:(i + 1) * dec_seq]
        qs = jnp.concatenate([qi[:, h * CA_HEAD_DIM:(h + 1) * CA_HEAD_DIM] for h in range(CA_HEADS)], axis=0)
        s = _dot_nt(qs.astype(BF16), k_ref[i].astype(BF16)) * (CA_HEAD_DIM ** -0.5)
        s = jnp.where(own, s, NEG)
        m = jnp.max(s, axis=-1, keepdims=True)
        p = jnp.exp(s - m)
        o = _dot(p.astype(BF16), v_ref[i].astype(BF16)) / jnp.sum(p, axis=-1, keepdims=True)
        rows.append(jnp.concatenate([o[h * dec_seq:(h + 1) * dec_seq] for h in range(CA_HEADS)], axis=-1))
    o_ref[...] = jnp.concatenate(rows, axis=0).astype(BF16)


def _cross_sample(qc, ck, cv, n_seq_total, dec_seq, n_seq):
    rows = n_seq * dec_seq
    tok = pl.BlockSpec((rows, CA_DIM), lambda i: (i, 0))
    mem = pl.BlockSpec((n_seq, N_MEM * CA_HEADS, CA_HEAD_DIM), lambda i: (i, 0, 0))
    return pl.pallas_call(
        functools.partial(_cross_sample_body, n_seq=n_seq, dec_seq=dec_seq),
        grid=(n_seq_total // n_seq,),
        in_specs=[tok, mem, mem],
        out_specs=tok,
        out_shape=jax.ShapeDtypeStruct((n_seq_total * dec_seq, CA_DIM), BF16),
        compiler_params=_params(("parallel",)),
        name="cross_sample",
    )(qc, ck, cv)


def _post2_body(h_ref, o_ref, wco_ref, gf_ref, wg_ref, wu_ref, wd_ref, gfin_ref, y_ref):
    h = h_ref[...] + _dot(o_ref[...], wco_ref[...])
    hn = _rms(h, gf_ref[...]).astype(BF16)
    acc = h
    for lo, hi in FF_SPLITS:
        act = _silu(_dot(hn, wg_ref[:, lo:hi])) * _dot(hn, wu_ref[:, lo:hi])
        acc = acc + _dot(act.astype(BF16), wd_ref[lo:hi, :])
    y_ref[...] = _rms(acc, gfin_ref[...])


def _post2(h1, o, w_co, g_ffn, w_gate, w_up, w_down, g_final, tm):
    t = h1.shape[0]
    row = lambda w: pl.BlockSpec((tm, w), lambda i: (i, 0))
    return pl.pallas_call(
        _post2_body,
        grid=(t // tm,),
        in_specs=[row(D_MODEL), row(CA_DIM), _full_spec(w_co.shape), _full_spec(g_ffn.shape),
                  _full_spec(w_gate.shape), _full_spec(w_up.shape), _full_spec(w_down.shape),
                  _full_spec(g_final.shape)],
        out_specs=row(D_MODEL),
        out_shape=jax.ShapeDtypeStruct((t, D_MODEL), F32),
        compiler_params=_params(("parallel",)),
        name="ffn",
    )(h1, o, w_co, g_ffn, w_gate, w_up, w_down, g_final)


def _prompt_consts():
    qi = np.arange(WINDOW)[:, None]
    ji = np.arange(2 * WINDOW)[None, :]
    dist = qi + WINDOW - ji
    inband = (dist >= 0) & (dist < WINDOW)
    bucket = np.where(inband, _t5_bucket_np(dist), -1)
    first = np.where(ji >= WINDOW, bucket, -1)
    buckets = np.stack([first, bucket]).astype(np.int32)
    return (jnp.asarray(_tril_np(CHUNK), BF16), jnp.asarray(_expand_np(), BF16), jnp.asarray(buckets))


def _sample_consts(dec_seq, cache_len):
    rows = SAMPLE_SEQS * dec_seq
    r = np.arange(rows)
    same = (r[:, None] // dec_seq) == (r[None, :] // dec_seq)
    tcum = (same & (r[None, :] <= r[:, None])).astype(np.float32)
    tseq = same.astype(np.float32)
    t = np.arange(dec_seq)[:, None]
    j = np.arange(cache_len)[None, :]
    dist_c = t + cache_len - j
    bkt_c = np.where((dist_c >= 0) & (dist_c < WINDOW), _t5_bucket_np(dist_c), -1).astype(np.int32)
    dist_n = (r[:, None] % dec_seq) - (r[None, :] % dec_seq)
    ok = same & (dist_n >= 0) & (dist_n < WINDOW)
    bkt_n = np.where(ok, _t5_bucket_np(dist_n), -1).astype(np.int32)
    return (jnp.asarray(tcum, BF16), jnp.asarray(tseq, BF16), jnp.asarray(_expand_np(), BF16),
            jnp.asarray(bkt_c), jnp.asarray(bkt_n))


def _pick_tile(t, pref):
    tm = min(t, pref)
    while t % tm:
        tm //= 2
    return tm


def kernel(x_prompt, x_sample, mem_prompt, state_conv, state_ssm, cache_swa_k, cache_swa_v, cache_mem_k, cache_mem_v, rel_bias, g_mix, w_in, conv_w, conv_b, dt_bias, a_log, d_skip, g_ssd_norm, sinks, w_out, g_cross, g_mem, w_cq, w_ck, w_cv, w_co, g_ffn, w_gate, w_up, w_down, g_final):
    assert g_mix.shape[0] == 1, "single-layer trunk"
    batch, seq, _ = x_prompt.shape
    n_dec, dec_seq, _ = x_sample.shape
    cache_len = cache_swa_k.shape[2]
    assert seq % CHUNK == 0 and cache_len == WINDOW and n_dec % SAMPLE_SEQS == 0 and dec_seq == 8

    wi = w_in[0]
    s0, s1, s2, s3, s4 = (SSD_D_INNER, SSD_D_INNER + CONV_DIM, SSD_D_INNER + CONV_DIM + SSD_HEADS,
                          SSD_D_INNER + CONV_DIM + SSD_HEADS + ATTN_DIM,
                          SSD_D_INNER + CONV_DIM + SSD_HEADS + ATTN_DIM + KV_DIM)
    w_in_r = jnp.concatenate(
        [wi[:, :s0], wi[:, s0:s1], wi[:, s2:s3], wi[:, s3:s4], wi[:, s4:],
         jnp.pad(wi[:, s1:s2], ((0, 0), (0, DT_PAD - SSD_HEADS)))], axis=1).astype(BF16)
    row = lambda a: a.reshape(1, -1).astype(F32)
    pad_h = lambda a: jnp.pad(a.reshape(1, -1).astype(F32), ((0, 0), (0, DT_PAD - SSD_HEADS)))
    small = (conv_w[0].astype(F32), row(conv_b[0]), pad_h(dt_bias[0]), pad_h(a_log[0]),
             jnp.repeat(d_skip[0].astype(F32), SSD_HEAD_DIM).reshape(1, -1), row(g_ssd_norm[0]),
             rel_bias.astype(F32), sinks[0].astype(F32))
    bf = lambda w: w[0].astype(BF16)
    w_out_b, w_cq_b, w_ck_b, w_cv_b, w_co_b = bf(w_out), bf(w_cq), bf(w_ck), bf(w_cv), bf(w_co)
    w_gate_b, w_up_b, w_down_b = bf(w_gate), bf(w_up), bf(w_down)
    g_mix_r, g_cross_r, g_mem_r, g_ffn_r, g_fin_r = row(g_mix[0]), row(g_cross[0]), row(g_mem[0]), row(g_ffn[0]), row(g_final)

    tp = batch * seq
    n_chunks = 2 if seq % (2 * CHUNK) == 0 else 1
    mem2 = mem_prompt.reshape(batch * N_MEM, D_MODEL)
    mk, mv, mk_b, mv_b = _mem_kv(mem2, g_mem_r, w_ck_b, w_cv_b, _pick_tile(batch * N_MEM, 512))
    dense = (g_mix_r, w_in_r, w_out_b, g_cross_r, w_cq_b, w_co_b, g_ffn_r, w_gate_b, w_up_b, w_down_b, g_fin_r)
    y_prompt, p_conv, p_ssm, p_k, p_v = _prompt_layer(
        x_prompt, mk_b.reshape(batch, N_MEM, CA_DIM), mv_b.reshape(batch, N_MEM, CA_DIM),
        dense, small, _prompt_consts(), n_chunks)

    ts = n_dec * dec_seq
    xs2 = x_sample.reshape(ts, D_MODEL)
    tm_s = _pick_tile(ts, 512)
    z, xbc, q, k, v, dt = _in_proj(xs2, g_mix_r, w_in_r, tm_s)
    mix_s, s_conv, s_ssm, s_k, s_v = _sample_mixer(
        z, xbc, q, k, v, dt, state_conv[0], state_ssm[0].reshape(n_dec, SSD_D_INNER, D_STATE),
        cache_swa_k[0].reshape(n_dec, cache_len, KV_DIM), cache_swa_v[0].reshape(n_dec, cache_len, KV_DIM),
        small, _sample_consts(dec_seq, cache_len), n_dec, dec_seq)
    h1s, qcs = _post1(xs2, mix_s, w_out_b, g_cross_r, w_cq_b, tm_s)
    os_ = _cross_sample(qcs, cache_mem_k[0].reshape(n_dec, N_MEM * CA_HEADS, CA_HEAD_DIM),
                        cache_mem_v[0].reshape(n_dec, N_MEM * CA_HEADS, CA_HEAD_DIM), n_dec, dec_seq, 8)
    y_sample = _post2(h1s, os_, w_co_b, g_ffn_r, w_gate_b, w_up_b, w_down_b, g_fin_r, tm_s)

    return (y_prompt.reshape(batch, seq, D_MODEL), y_sample.reshape(n_dec, dec_seq, D_MODEL),
            p_conv[None], p_ssm.reshape(1, batch, SSD_HEADS, SSD_HEAD_DIM, D_STATE),
            p_k.reshape(1, batch, WINDOW, N_KV_HEADS, ATTN_HEAD_DIM),
            p_v.reshape(1, batch, WINDOW, N_KV_HEADS, ATTN_HEAD_DIM),
            mk.reshape(1, batch, N_MEM, CA_HEADS, CA_HEAD_DIM), mv.reshape(1, batch, N_MEM, CA_HEADS, CA_HEAD_DIM),
            s_conv[None], s_ssm.reshape(1, n_dec, SSD_HEADS, SSD_HEAD_DIM, D_STATE),
            s_k.reshape(1, n_dec, cache_len, N_KV_HEADS, ATTN_HEAD_DIM),
            s_v.reshape(1, n_dec, cache_len, N_KV_HEADS, ATTN_HEAD_DIM))
```

```python
import functools
import math

import numpy as np
import jax
import jax.numpy as jnp
from jax import lax
from jax.experimental import pallas as pl
from jax.experimental.pallas import tpu as pltpu

F32 = jnp.float32
BF16 = jnp.bfloat16

D_MODEL = 1024
SSD_D_INNER = 512
SSD_HEAD_DIM = 64
SSD_HEADS = 8
SSD_GROUPS = 2
GROUP_W = SSD_D_INNER // SSD_GROUPS
D_STATE = 128
CONV_K = 4
CONV_DIM = SSD_D_INNER + 2 * SSD_GROUPS * D_STATE
CHUNK = 128
ATTN_DIM = 512
ATTN_HEAD_DIM = 64
N_HEADS = 8
N_KV_HEADS = 2
Q_PER_KV = N_HEADS // N_KV_HEADS
KV_DIM = N_KV_HEADS * ATTN_HEAD_DIM
WINDOW = 128
N_BUCKETS = 32
MAX_EXACT = N_BUCKETS // 2
MAX_DISTANCE = 128
N_MEM = 256
CA_HEADS = 4
CA_HEAD_DIM = 128
CA_DIM = CA_HEADS * CA_HEAD_DIM
D_FF = 2816
EPS = 1e-6

LANES = 128
HALF = LANES // 2
DT_PAD = LANES
COL_Z = 0
COL_XBC = COL_Z + SSD_D_INNER
COL_Q = COL_XBC + CONV_DIM
COL_K = COL_Q + ATTN_DIM
COL_V = COL_K + KV_DIM
COL_DT = COL_V + KV_DIM
IN_COLS = COL_DT + DT_PAD
NEG = -1e30
SAMPLE_SEQS = 16
VMEM_LIMIT = 56 * 1024 * 1024
FF_SPLITS = ((0, 1024), (1024, 2048), (2048, D_FF))
SECOND_STREAM_LEAD = 0.5
FF_PIECES = tuple((lo, min(lo + 512, D_FF)) for lo in range(0, D_FF, 512))
PROJ_PIECES = ((COL_Z, COL_XBC), (COL_XBC, COL_XBC + 512), (COL_XBC + 512, COL_Q), (COL_Q, COL_K), (COL_K, IN_COLS))


def _rms(x, g):
    return x * lax.rsqrt(jnp.mean(x * x, axis=-1, keepdims=True) + EPS) * g


def _silu(x):
    return x * jax.nn.sigmoid(x)


def _softplus(x):
    return jnp.maximum(x, 0.0) + jnp.log1p(jnp.exp(-jnp.abs(x)))


def _dot(a, b):
    return jnp.dot(a, b, preferred_element_type=F32)


def _dot_nt(a, b):
    return lax.dot_general(a, b, (((1,), (1,)), ((), ())), preferred_element_type=F32)


def _split3(a):
    hi = a.astype(BF16)
    r = a - hi.astype(F32)
    mid = r.astype(BF16)
    lo = (r - mid.astype(F32)).astype(BF16)
    return hi, mid, lo


def _sel_left(t01, a):
    hi, mid, lo = _split3(a)
    return _dot(t01, hi) + _dot(t01, mid) + _dot(t01, lo)


def _sel_right(a, e01):
    hi, mid, lo = _split3(a)
    return _dot(hi, e01) + _dot(mid, e01) + _dot(lo, e01)


def _lane_lo():
    return lax.broadcasted_iota(jnp.int32, (1, LANES), 1) < HALF


def _t5_bucket_np(dist):
    n = np.maximum(dist, 0)
    ratio = np.log(np.maximum(n, 1).astype(np.float32) / np.float32(MAX_EXACT))
    large = MAX_EXACT + (ratio / np.float32(math.log(MAX_DISTANCE / MAX_EXACT))
                         * np.float32(N_BUCKETS - MAX_EXACT)).astype(np.int32)
    large = np.minimum(large, N_BUCKETS - 1)
    return np.where(n < MAX_EXACT, n, large).astype(np.int32)


def _tril_np(n):
    return np.tril(np.ones((n, n), np.float32))


def _expand_np():
    e = np.zeros((LANES, SSD_D_INNER), np.float32)
    for h in range(SSD_HEADS):
        e[h, h * SSD_HEAD_DIM:(h + 1) * SSD_HEAD_DIM] = 1.0
    return e


def _full_spec(shape):
    nd = len(shape)
    return pl.BlockSpec(shape, lambda *_: (0,) * nd, pipeline_mode=pl.Buffered(1))


def _smem_spec():
    return pl.BlockSpec(memory_space=pltpu.SMEM)


def _params(sem):
    return pltpu.CompilerParams(dimension_semantics=sem, vmem_limit_bytes=VMEM_LIMIT)


def _in_proj_body(x_ref, g_ref, w_ref, z_ref, xbc_ref, q_ref, k_ref, v_ref, dt_ref):
    xn = _rms(x_ref[...], g_ref[...]).astype(BF16)

    def seg(lo, hi):
        return _dot(xn, w_ref[:, lo:hi])

    z_ref[...] = seg(COL_Z, COL_XBC)
    xbc_ref[...] = seg(COL_XBC, COL_Q)
    q_ref[...] = (seg(COL_Q, COL_K) * (ATTN_HEAD_DIM ** -0.5)).astype(BF16)
    k_ref[...] = seg(COL_K, COL_V)
    v_ref[...] = seg(COL_V, COL_DT)
    dt_ref[...] = seg(COL_DT, IN_COLS)


def _in_proj(x2, g_mix, w_in_r, tm):
    t = x2.shape[0]
    row = lambda w: pl.BlockSpec((tm, w), lambda i: (i, 0))
    outs = [(SSD_D_INNER, F32), (CONV_DIM, F32), (ATTN_DIM, BF16), (KV_DIM, F32), (KV_DIM, F32), (DT_PAD, F32)]
    return pl.pallas_call(
        _in_proj_body,
        grid=(t // tm,),
        in_specs=[row(D_MODEL), _full_spec((1, D_MODEL)), _full_spec((D_MODEL, IN_COLS))],
        out_specs=[row(w) for w, _ in outs],
        out_shape=[jax.ShapeDtypeStruct((t, w), d) for w, d in outs],
        compiler_params=_params(("parallel",)),
        name="in_proj",
    )(x2, g_mix, w_in_r)


def _conv_taps(cw_ref, cb_ref, taps):
    acc = cb_ref[...] + taps[0] * cw_ref[0:1, :]
    for k in range(1, CONV_K):
        acc = acc + taps[k] * cw_ref[k:k + 1, :]
    return _silu(acc)


def _ssd_prepare(conv, dt_raw, dtb_ref, alog_ref, tcum, total_fn, extra_fn, expand):
    xs = conv[:, :SSD_D_INNER]
    bm = conv[:, SSD_D_INNER:SSD_D_INNER + SSD_GROUPS * D_STATE]
    cm = conv[:, SSD_D_INNER + SSD_GROUPS * D_STATE:]
    dt = _softplus(dt_raw + dtb_ref[...])
    a = dt * (-jnp.exp(alog_ref[...]))
    cs = _sel_left(tcum, a)
    total = total_fn(cs, a)
    pieces = [dt, jnp.exp(total - cs), jnp.exp(cs)] + extra_fn(total)
    rows = cs.shape[0]
    ex = _sel_right(jnp.concatenate(pieces, axis=0), expand)
    ex = [ex[i * rows:(i + 1) * rows] for i in range(len(pieces))]
    return xs, bm, cm, cs, cs.T, ex


def _ssd_diag(cs, cs_t, cb_g, xdt, mask, g):
    lo = _lane_lo()
    out = []
    for pr in range(2):
        h0 = g * 4 + 2 * pr
        xp = xdt[:, (h0 // 2) * LANES:(h0 // 2 + 1) * LANES]
        x_lo = jnp.where(lo, xp, 0.0).astype(BF16)
        x_hi = jnp.where(lo, 0.0, xp).astype(BF16)
        acc = None
        for h, xh in ((h0, x_lo), (h0 + 1, x_hi)):
            diff = cs[:, h:h + 1] - cs_t[h:h + 1, :]
            decay = jnp.exp(jnp.where(mask, diff, -jnp.inf))
            part = _dot((cb_g * decay).astype(BF16), xh)
            acc = part if acc is None else acc + part
        out.append(acc)
    return jnp.concatenate(out, axis=-1)


def _gated_norm(y, z, gn_ref):
    yf = y * _silu(z)
    parts = []
    for g in range(SSD_GROUPS):
        yg = yf[:, g * GROUP_W:(g + 1) * GROUP_W]
        parts.append(yg * lax.rsqrt(jnp.mean(yg * yg, axis=-1, keepdims=True) + EPS))
    return jnp.concatenate(parts, axis=-1) * gn_ref[...]


def _build_bias(bias_ref, bucket_of, relb_ref, n_tables):
    for i in range(n_tables):
        for h in range(N_HEADS):
            bias_ref[i, h] = jnp.full(bias_ref.shape[2:], NEG, F32)

    def body(t, carry):
        for i in range(n_tables):
            hit = bucket_of(i) == t
            for h in range(N_HEADS):
                bias_ref[i, h] = jnp.where(hit, relb_ref[t, h], bias_ref[i, h])
        return carry

    lax.fori_loop(0, N_BUCKETS, body, 0)


def _prompt_mixer_body(z_ref, xbc_ref, q_ref, k_ref, v_ref, dt_ref,
                       cw_ref, cb_ref, dtb_ref, alog_ref, dskip_ref, gn_ref,
                       tril_ref, expand_ref, bucket_ref, relb_ref, sink_ref,
                       mix_ref, conv_out, ssm_out, k_out, v_out,
                       xpad, state_t, kbuf, vbuf, bias):
    b = pl.program_id(0)
    c = pl.program_id(1)
    last = pl.num_programs(1) - 1
    q_rows = CHUNK

    @pl.when((b == 0) & (c == 0))
    def _():
        _build_bias(bias, lambda i: bucket_ref[i], relb_ref, 2)

    @pl.when(c == 0)
    def _():
        xpad[0:8, :] = jnp.zeros((8, CONV_DIM), F32)
        state_t[...] = jnp.zeros_like(state_t)
        kbuf[...] = jnp.zeros_like(kbuf)
        vbuf[...] = jnp.zeros_like(vbuf)

    xbc = xbc_ref[0]
    xpad[8:8 + q_rows, :] = xbc
    taps = [xpad[5 + k:5 + k + q_rows, :] for k in range(CONV_K - 1)] + [xbc]
    conv = _conv_taps(cw_ref, cb_ref, taps)
    xpad[5:8, :] = xbc[q_rows - 3:q_rows, :]

    row = lax.broadcasted_iota(jnp.int32, (q_rows, q_rows), 0)
    col = lax.broadcasted_iota(jnp.int32, (q_rows, q_rows), 1)
    causal = col <= row
    xs, bm, cm, cs, cs_t, (dt_x, dend_x, ecs_x) = _ssd_prepare(
        conv, dt_ref[0], dtb_ref, alog_ref, tril_ref[...],
        lambda cs_, a_: cs_[q_rows - 1:q_rows, :], lambda total: [], expand_ref[...])
    xdt = xs * dt_x
    xde = (xdt * dend_x).astype(BF16)
    bb = bm.astype(BF16)
    cb = cm.astype(BF16)
    y_parts = []
    for g in range(SSD_GROUPS):
        gs = slice(g * GROUP_W, (g + 1) * GROUP_W)
        ns = slice(g * D_STATE, (g + 1) * D_STATE)
        cb_g = _dot_nt(cb[:, ns], bb[:, ns])
        y_diag = _ssd_diag(cs, cs_t, cb_g, xdt, causal, g)
        st_old = state_t[:, gs]
        y_off = _dot(cb[:, ns], st_old.astype(BF16)) * ecs_x[:, gs]
        y_parts.append(y_diag + y_off)
        b_t = bm[:, ns].T.astype(BF16)
        state_t[:, gs] = ecs_x[q_rows - 1:q_rows, gs] * st_old + _dot(b_t, xde[:, gs])
    y = jnp.concatenate(y_parts, axis=-1) + dskip_ref[...] * xs
    mix_ref[0, :, 0:SSD_D_INNER] = _gated_norm(y, z_ref[0], gn_ref).astype(BF16)

    lo = _lane_lo()
    k_cur = k_ref[0]
    v_cur = v_ref[0]
    kbuf[0, q_rows:, :] = k_cur.astype(BF16)
    kbuf[1, q_rows:, :] = pltpu.roll(k_cur, HALF, 1).astype(BF16)
    vbuf[0, q_rows:, :] = v_cur.astype(BF16)
    vbuf[1, q_rows:, :] = pltpu.roll(v_cur, HALF, 1).astype(BF16)
    table = jnp.minimum(c, 1)
    for pair in range(N_HEADS // 2):
        j = (2 * pair) // Q_PER_KV
        q_pair = q_ref[0, :, pair * LANES:(pair + 1) * LANES]
        acc = None
        for par in range(2):
            h = 2 * pair + par
            qm = jnp.where(lo, q_pair, 0.0) if par == 0 else jnp.where(lo, 0.0, q_pair)
            variant = (j + par) % 2
            s = _dot_nt(qm.astype(BF16), kbuf[variant]) + bias[table, h]
            sink = sink_ref[h]
            m = jnp.maximum(jnp.max(s, axis=-1, keepdims=True), sink)
            p = jnp.exp(s - m)
            denom = jnp.sum(p, axis=-1, keepdims=True) + jnp.exp(sink - m)
            vv = vbuf[variant]
            vm = jnp.where(lo, vv, 0.0) if par == 0 else jnp.where(lo, 0.0, vv)
            part = _dot(p.astype(BF16), vm.astype(BF16)) / denom
            acc = part if acc is None else acc + part
        mix_ref[0, :, SSD_D_INNER + pair * LANES:SSD_D_INNER + (pair + 1) * LANES] = acc.astype(BF16)
    kbuf[:, 0:q_rows, :] = kbuf[:, q_rows:, :]
    vbuf[:, 0:q_rows, :] = vbuf[:, q_rows:, :]

    @pl.when(c == last)
    def _():
        conv_out[0] = xbc[q_rows - 3:q_rows, :]
        ssm_out[0] = state_t[...].T
        k_out[0] = k_cur
        v_out[0] = v_cur


def _prompt_mixer(z, xbc, q, k, v, dt, small, consts, batch, seq):
    cw, cb, dtb, alog, dskip_x, gn, relb, sinks = small
    tril, expand, buckets = consts
    nc = seq // CHUNK
    r3 = lambda a: a.reshape(batch, seq, a.shape[-1])
    tok = lambda w: pl.BlockSpec((1, CHUNK, w), lambda b, c: (b, c, 0))
    per_b = lambda s: pl.BlockSpec((1,) + s, lambda b, c: (b,) + (0,) * len(s))
    out_shape = [
        jax.ShapeDtypeStruct((batch, seq, D_MODEL), BF16),
        jax.ShapeDtypeStruct((batch, CONV_K - 1, CONV_DIM), F32),
        jax.ShapeDtypeStruct((batch, SSD_D_INNER, D_STATE), F32),
        jax.ShapeDtypeStruct((batch, WINDOW, KV_DIM), F32),
        jax.ShapeDtypeStruct((batch, WINDOW, KV_DIM), F32),
    ]
    return pl.pallas_call(
        _prompt_mixer_body,
        grid=(batch, nc),
        in_specs=[tok(SSD_D_INNER), tok(CONV_DIM), tok(ATTN_DIM), tok(KV_DIM), tok(KV_DIM), tok(DT_PAD),
                  _full_spec(cw.shape), _full_spec(cb.shape), _full_spec(dtb.shape), _full_spec(alog.shape),
                  _full_spec(dskip_x.shape), _full_spec(gn.shape),
                  _full_spec(tril.shape), _full_spec(expand.shape), _full_spec(buckets.shape),
                  _smem_spec(), _smem_spec()],
        out_specs=[tok(D_MODEL), per_b((CONV_K - 1, CONV_DIM)), per_b((SSD_D_INNER, D_STATE)),
                   per_b((WINDOW, KV_DIM)), per_b((WINDOW, KV_DIM))],
        out_shape=out_shape,
        scratch_shapes=[
            pltpu.VMEM((8 + CHUNK, CONV_DIM), F32),
            pltpu.VMEM((D_STATE, SSD_D_INNER), F32),
            pltpu.VMEM((2, 2 * WINDOW, KV_DIM), BF16),
            pltpu.VMEM((2, 2 * WINDOW, KV_DIM), BF16),
            pltpu.VMEM((2, N_HEADS, WINDOW, 2 * WINDOW), F32),
        ],
        compiler_params=_params(("arbitrary", "arbitrary")),
        name="prompt_mixer",
    )(r3(z), r3(xbc), r3(q), r3(k), r3(v), r3(dt), cw, cb, dtb, alog, dskip_x, gn,
      tril, expand, buckets, relb, sinks)


def _prompt_layer_body(x_ref, mk_ref, mv_ref, gmix_ref, win_ref, wout_ref, gc_ref, wcq_ref,
                       wco_ref, gf_ref, wg_ref, wu_ref, wd_ref, gfin_ref,
                       cw_ref, cb_ref, dtb_ref, alog_ref, dskip_ref, gn_ref,
                       tril_ref, expand_ref, bucket_ref, relb_ref, sink_ref,
                       y_ref, conv_out, ssm_out, k_out, v_out,
                       xpad, state_t, kbuf, vbuf, bias, mix, hbuf, qcbuf, *, n_chunks, tiles_per_seq):
    step = pl.program_id(0)
    n_tiles = pl.num_programs(0) - 1
    c = lax.rem(jnp.minimum(step, n_tiles - 1), tiles_per_seq)
    last = tiles_per_seq - 1
    tq = n_chunks * CHUNK

    @pl.when(step == 0)
    def _():
        _build_bias(bias, lambda i: bucket_ref[i], relb_ref, 2)
        hbuf[...] = jnp.zeros_like(hbuf)
        qcbuf[...] = jnp.zeros_like(qcbuf)

    @pl.when(c == 0)
    def _():
        xpad[0:8, :] = jnp.zeros((8, CONV_DIM), F32)
        state_t[...] = jnp.zeros_like(state_t)
        kbuf[:, 0:CHUNK, :] = jnp.zeros((2, CHUNK, KV_DIM), BF16)
        vbuf[:, 0:CHUNK, :] = jnp.zeros((2, CHUNK, KV_DIM), BF16)

    o = _cross_heads(qcbuf[...], mk_ref[0], mv_ref[0]).astype(BF16)
    h2 = hbuf[...] + _dot(o, wco_ref[...])
    hn = _rms(h2, gf_ref[...]).astype(BF16)
    acc = h2
    for f_lo, f_hi in FF_SPLITS:
        act = _silu(_dot(hn, wg_ref[:, f_lo:f_hi])) * _dot(hn, wu_ref[:, f_lo:f_hi])
        acc = acc + _dot(act.astype(BF16), wd_ref[f_lo:f_hi, :])
    y_ref[0] = _rms(acc, gfin_ref[...])

    x = x_ref[0]
    xn = _rms(x, gmix_ref[...]).astype(BF16)

    def seg(lo, hi):
        return _dot(xn, win_ref[:, lo:hi])

    z = seg(COL_Z, COL_XBC)
    xbc = seg(COL_XBC, COL_Q)
    q = (seg(COL_Q, COL_K) * (ATTN_HEAD_DIM ** -0.5)).astype(BF16)
    k_new = seg(COL_K, COL_V)
    v_new = seg(COL_V, COL_DT)
    dt_raw = seg(COL_DT, IN_COLS)

    xpad[8:8 + tq, :] = xbc
    taps = [xpad[5 + k:5 + k + tq, :] for k in range(CONV_K - 1)] + [xbc]
    conv = _conv_taps(cw_ref, cb_ref, taps)
    xpad[5:8, :] = xbc[tq - 3:tq, :]

    kbuf[0, CHUNK:, :] = k_new.astype(BF16)
    kbuf[1, CHUNK:, :] = pltpu.roll(k_new, HALF, 1).astype(BF16)
    vbuf[0, CHUNK:, :] = v_new.astype(BF16)
    vbuf[1, CHUNK:, :] = pltpu.roll(v_new, HALF, 1).astype(BF16)

    row = lax.broadcasted_iota(jnp.int32, (CHUNK, CHUNK), 0)
    col = lax.broadcasted_iota(jnp.int32, (CHUNK, CHUNK), 1)
    causal = col <= row
    lo = _lane_lo()
    for ci in range(n_chunks):
        rs = slice(ci * CHUNK, (ci + 1) * CHUNK)
        xs, bm, cm, cs, cs_t, (dt_x, dend_x, ecs_x) = _ssd_prepare(
            conv[rs], dt_raw[rs], dtb_ref, alog_ref, tril_ref[...],
            lambda cs_, a_: cs_[CHUNK - 1:CHUNK, :], lambda total: [], expand_ref[...])
        xdt = xs * dt_x
        xde = (xdt * dend_x).astype(BF16)
        bb = bm.astype(BF16)
        cb = cm.astype(BF16)
        y_parts = []
        for g in range(SSD_GROUPS):
            gs = slice(g * GROUP_W, (g + 1) * GROUP_W)
            ns = slice(g * D_STATE, (g + 1) * D_STATE)
            cb_g = _dot_nt(cb[:, ns], bb[:, ns])
            y_diag = _ssd_diag(cs, cs_t, cb_g, xdt, causal, g)
            st_old = state_t[:, gs]
            y_off = _dot(cb[:, ns], st_old.astype(BF16)) * ecs_x[:, gs]
            y_parts.append(y_diag + y_off)
            b_t = bm[:, ns].T.astype(BF16)
            state_t[:, gs] = ecs_x[CHUNK - 1:CHUNK, gs] * st_old + _dot(b_t, xde[:, gs])
        y = jnp.concatenate(y_parts, axis=-1) + dskip_ref[...] * xs
        mix[rs, 0:SSD_D_INNER] = _gated_norm(y, z[rs], gn_ref).astype(BF16)

        keys = slice(ci * CHUNK, (ci + 2) * CHUNK)
        table = jnp.minimum(c, 1) if ci == 0 else 1
        for pair in range(N_HEADS // 2):
            j = (2 * pair) // Q_PER_KV
            q_pair = q[rs, pair * LANES:(pair + 1) * LANES]
            acc = None
            for par in range(2):
                h = 2 * pair + par
                qm = jnp.where(lo, q_pair, 0.0) if par == 0 else jnp.where(lo, 0.0, q_pair)
                variant = (j + par) % 2
                s = _dot_nt(qm.astype(BF16), kbuf[variant, keys, :]) + bias[table, h]
                sink = sink_ref[h]
                m = jnp.maximum(jnp.max(s, axis=-1, keepdims=True), sink)
                p = jnp.exp(s - m)
                denom = jnp.sum(p, axis=-1, keepdims=True) + jnp.exp(sink - m)
                vv = vbuf[variant, keys, :]
                vm = jnp.where(lo, vv, 0.0) if par == 0 else jnp.where(lo, 0.0, vv)
                part = _dot(p.astype(BF16), vm.astype(BF16)) / denom
                acc = part if acc is None else acc + part
            mix[rs, SSD_D_INNER + pair * LANES:SSD_D_INNER + (pair + 1) * LANES] = acc.astype(BF16)
    kbuf[:, 0:CHUNK, :] = kbuf[:, tq:tq + CHUNK, :]
    vbuf[:, 0:CHUNK, :] = vbuf[:, tq:tq + CHUNK, :]

    h1 = x + _dot(mix[...], wout_ref[...])
    hbuf[...] = h1
    qcbuf[...] = _dot(_rms(h1, gc_ref[...]).astype(BF16), wcq_ref[...]).astype(BF16)

    @pl.when((c == last) & (step < n_tiles))
    def _():
        conv_out[0] = xbc[tq - 3:tq, :]
        ssm_out[0] = state_t[...].T
        k_out[0] = k_new[tq - WINDOW:tq, :]
        v_out[0] = v_new[tq - WINDOW:tq, :]


def _alternate(first, second):
    streams = [[0.0, 1.0, first], [0.0, SECOND_STREAM_LEAD, second]]
    while streams:
        entry = min(streams, key=lambda e: e[0])
        try:
            entry[0] += entry[1] * next(entry[2])
        except StopIteration:
            streams.remove(entry)


def _layer_step_body(x_ref, mk_ref, mv_ref, gmix_ref, win_ref, wout_ref, gc_ref, wcq_ref,
                     wco_ref, gf_ref, wg_ref, wu_ref, wd_ref, gfin_ref,
                     cw_ref, cb_ref, dtb_ref, alog_ref, dskip_ref, gn_ref,
                     tril_ref, expand_ref, bucket_ref, relb_ref, sink_ref,
                     y_ref, conv_out, ssm_out, k_out, v_out,
                     xpad, state_t, kbuf, vbuf, bias, mix, hbuf, proj, *, n_chunks, tiles_per_seq):
    step = pl.program_id(0)
    n_tiles = pl.num_programs(0) - 1
    c = lax.rem(jnp.minimum(step, n_tiles - 1), tiles_per_seq)
    last = tiles_per_seq - 1
    tq = n_chunks * CHUNK

    @pl.when(step == 0)
    def _():
        _build_bias(bias, lambda i: bucket_ref[i], relb_ref, 2)
        hbuf[...] = jnp.zeros_like(hbuf)
        mix[...] = jnp.zeros_like(mix)

    @pl.when(c == 0)
    def _():
        xpad[0:8, :] = jnp.zeros((8, CONV_DIM), F32)
        state_t[...] = jnp.zeros_like(state_t)
        kbuf[:, 0:CHUNK, :] = jnp.zeros((2, CHUNK, KV_DIM), BF16)
        vbuf[:, 0:CHUNK, :] = jnp.zeros((2, CHUNK, KV_DIM), BF16)

    def back():
        h1 = hbuf[...] + _dot(mix[...], wout_ref[...])
        yield 4
        qc = _dot(_rms(h1, gc_ref[...]).astype(BF16), wcq_ref[...]).astype(BF16)
        yield 2
        heads = []
        for h in range(CA_HEADS):
            hs = slice(h * CA_HEAD_DIM, (h + 1) * CA_HEAD_DIM)
            sc = _dot_nt(qc[:, hs], mk_ref[0, :, hs]) * (CA_HEAD_DIM ** -0.5)
            m = jnp.max(sc, axis=-1, keepdims=True)
            p = jnp.exp(sc - m)
            heads.append(_dot(p.astype(BF16), mv_ref[0, :, hs]) / jnp.sum(p, axis=-1, keepdims=True))
            yield 3
        o = jnp.concatenate(heads, axis=-1).astype(BF16)
        h2 = h1 + _dot(o, wco_ref[...])
        hn = _rms(h2, gf_ref[...]).astype(BF16)
        yield 4
        acc = h2
        for f_lo, f_hi in FF_PIECES:
            gate = _dot(hn, wg_ref[:, f_lo:f_hi])
            yield 5
            act = (_silu(gate) * _dot(hn, wu_ref[:, f_lo:f_hi])).astype(BF16)
            yield 6
            acc = acc + _dot(act, wd_ref[f_lo:f_hi, :])
            yield 5
        y_ref[0] = _rms(acc, gfin_ref[...])

    def front():
        x = x_ref[0]
        xn = _rms(x, gmix_ref[...]).astype(BF16)
        for p_lo, p_hi in PROJ_PIECES:
            proj[:, p_lo:p_hi] = _dot(xn, win_ref[:, p_lo:p_hi])
            yield 4

        xpad[8:8 + tq, :] = proj[:, COL_XBC:COL_Q]
        conv_halves = []
        half_w = CONV_DIM // 2
        for hf in range(2):
            cs_ = slice(hf * half_w, (hf + 1) * half_w)
            acc = cb_ref[:, cs_] + xpad[5:5 + tq, cs_] * cw_ref[0:1, cs_]
            for kk in range(1, CONV_K):
                acc = acc + xpad[5 + kk:5 + kk + tq, cs_] * cw_ref[kk:kk + 1, cs_]
            conv_halves.append(_silu(acc))
            yield 6
        xpad[5:8, :] = xpad[5 + tq:8 + tq, :]
        xs_all = conv_halves[0]
        bc_all = conv_halves[1]

        k_new = proj[:, COL_K:COL_V]
        v_new = proj[:, COL_V:COL_DT]
        kbuf[0, CHUNK:, :] = k_new.astype(BF16)
        kbuf[1, CHUNK:, :] = pltpu.roll(k_new, HALF, 1).astype(BF16)
        vbuf[0, CHUNK:, :] = v_new.astype(BF16)
        vbuf[1, CHUNK:, :] = pltpu.roll(v_new, HALF, 1).astype(BF16)
        q = (proj[:, COL_Q:COL_K] * (ATTN_HEAD_DIM ** -0.5)).astype(BF16)
        yield 2

        row = lax.broadcasted_iota(jnp.int32, (CHUNK, CHUNK), 0)
        col = lax.broadcasted_iota(jnp.int32, (CHUNK, CHUNK), 1)
        causal = col <= row
        lo = _lane_lo()
        for ci in range(n_chunks):
            rs = slice(ci * CHUNK, (ci + 1) * CHUNK)
            xs = xs_all[rs]
            bm = bc_all[rs, 0:SSD_GROUPS * D_STATE]
            cm = bc_all[rs, SSD_GROUPS * D_STATE:]
            dt = _softplus(proj[rs, COL_DT:IN_COLS] + dtb_ref[...])
            a = dt * (-jnp.exp(alog_ref[...]))
            cs = _sel_left(tril_ref[...], a)
            cs_t = cs.T
            dt_t = dt.T
            total = cs[CHUNK - 1:CHUNK, :]
            ecs = jnp.exp(cs)
            w_end = dt * jnp.exp(total - cs)
            bb = bm.astype(BF16)
            cb = cm.astype(BF16)
            yield 5
            y_parts = []
            for g in range(SSD_GROUPS):
                gs = slice(g * GROUP_W, (g + 1) * GROUP_W)
                ns = slice(g * D_STATE, (g + 1) * D_STATE)
                cb_g = _dot_nt(cb[:, ns], bb[:, ns])
                y_diag, xw, ecs_g = [], [], []
                for pr in range(2):
                    h0 = g * 4 + 2 * pr
                    xp = xs[:, (h0 // 2) * LANES:(h0 // 2 + 1) * LANES]
                    ecs_g.append(jnp.where(lo, ecs[:, h0:h0 + 1], ecs[:, h0 + 1:h0 + 2]))
                    xw.append((xp * jnp.where(lo, w_end[:, h0:h0 + 1], w_end[:, h0 + 1:h0 + 2])).astype(BF16))
                    x2 = jnp.concatenate([jnp.where(lo, xp, 0.0), jnp.where(lo, 0.0, xp)], axis=0).astype(BF16)
                    gmat = []
                    for h in (h0, h0 + 1):
                        diff = cs[:, h:h + 1] - cs_t[h:h + 1, :]
                        decay = jnp.exp(jnp.where(causal, diff, -jnp.inf))
                        gmat.append((cb_g * decay * dt_t[h:h + 1, :]).astype(BF16))
                    y_diag.append(_dot(jnp.concatenate(gmat, axis=1), x2))
                ecs_x = jnp.concatenate(ecs_g, axis=-1)
                st_old = state_t[:, gs]
                y_off = _dot(cb[:, ns], st_old.astype(BF16)) * ecs_x
                y_parts.append(jnp.concatenate(y_diag, axis=-1) + y_off)
                b_t = bm[:, ns].T.astype(BF16)
                state_t[:, gs] = ecs_x[CHUNK - 1:CHUNK, :] * st_old + _dot(b_t, jnp.concatenate(xw, axis=-1))
                yield 9
            y = jnp.concatenate(y_parts, axis=-1) + dskip_ref[...] * xs
            mix[rs, 0:SSD_D_INNER] = _gated_norm(y, proj[rs, COL_Z:COL_XBC], gn_ref).astype(BF16)
            yield 3

            keys = slice(ci * CHUNK, (ci + 2) * CHUNK)
            table = jnp.minimum(c, 1) if ci == 0 else 1
            probs, denoms = {}, {}
            for variant in range(2):
                hv = [h for h in range(N_HEADS) if (h // Q_PER_KV + h % 2) % 2 == variant]
                qms = []
                for h in hv:
                    q_pair = q[rs, (h // 2) * LANES:(h // 2 + 1) * LANES]
                    qms.append(jnp.where(lo, q_pair, 0.0) if h % 2 == 0 else jnp.where(lo, 0.0, q_pair))
                sc_all = _dot_nt(jnp.concatenate(qms, axis=0).astype(BF16), kbuf[variant, keys, :])
                for i, h in enumerate(hv):
                    sc = sc_all[i * CHUNK:(i + 1) * CHUNK] + bias[table, h]
                    sink = sink_ref[h]
                    m = jnp.maximum(jnp.max(sc, axis=-1, keepdims=True), sink)
                    p = jnp.exp(sc - m)
                    denoms[h] = jnp.sum(p, axis=-1, keepdims=True) + jnp.exp(sink - m)
                    probs[h] = p.astype(BF16)
                yield 10
            parts = {}
            for variant in range(2):
                vv = vbuf[variant, keys, :]
                for par in range(2):
                    hv = [h for h in range(N_HEADS) if (h // Q_PER_KV + h % 2) % 2 == variant and h % 2 == par]
                    vm = jnp.where(lo, vv, 0.0) if par == 0 else jnp.where(lo, 0.0, vv)
                    o_all = _dot(jnp.concatenate([probs[h] for h in hv], axis=0), vm.astype(BF16))
                    for i, h in enumerate(hv):
                        parts[h] = o_all[i * CHUNK:(i + 1) * CHUNK] / denoms[h]
            for pair in range(N_HEADS // 2):
                mix[rs, SSD_D_INNER + pair * LANES:SSD_D_INNER + (pair + 1) * LANES] = (
                    parts[2 * pair] + parts[2 * pair + 1]).astype(BF16)
            yield 4
        kbuf[:, 0:CHUNK, :] = kbuf[:, tq:tq + CHUNK, :]
        vbuf[:, 0:CHUNK, :] = vbuf[:, tq:tq + CHUNK, :]

        hbuf[...] = x

    _alternate(back(), front())

    @pl.when((c == last) & (step < n_tiles))
    def _():
        conv_out[0] = xpad[5:8, :]
        ssm_out[0] = state_t[...].T
        k_out[0] = proj[tq - WINDOW:tq, COL_K:COL_V]
        v_out[0] = proj[tq - WINDOW:tq, COL_V:COL_DT]


def _prompt_layer(x3, mk_b, mv_b, dense, small, consts, n_chunks):
    cw, cb, dtb, alog, dskip_x, gn, relb, sinks = small
    tril, expand, buckets = consts
    batch, seq, _ = x3.shape
    tq = n_chunks * CHUNK
    tps = seq // tq
    n_tiles = batch * tps
    front = lambda s: jnp.minimum(s, n_tiles - 1)
    back = lambda s: jnp.maximum(s - 1, 0)
    x_spec = pl.BlockSpec((1, tq, D_MODEL), lambda s: (front(s) // tps, front(s) % tps, 0))
    y_spec = pl.BlockSpec((1, tq, D_MODEL), lambda s: (back(s) // tps, back(s) % tps, 0))
    mem_spec = pl.BlockSpec((1, N_MEM, CA_DIM), lambda s: (back(s) // tps, 0, 0))
    per_b = lambda shp: pl.BlockSpec((1,) + shp, lambda s: (front(s) // tps,) + (0,) * len(shp))
    out_shape = [
        jax.ShapeDtypeStruct((batch, seq, D_MODEL), F32),
        jax.ShapeDtypeStruct((batch, CONV_K - 1, CONV_DIM), F32),
        jax.ShapeDtypeStruct((batch, SSD_D_INNER, D_STATE), F32),
        jax.ShapeDtypeStruct((batch, WINDOW, KV_DIM), F32),
        jax.ShapeDtypeStruct((batch, WINDOW, KV_DIM), F32),
    ]
    full = list(dense) + [cw, cb, dtb, alog, dskip_x, gn, tril, expand, buckets]
    return pl.pallas_call(
        functools.partial(_layer_step_body, n_chunks=n_chunks, tiles_per_seq=tps),
        grid=(n_tiles + 1,),
        in_specs=[x_spec, mem_spec, mem_spec] + [_full_spec(a.shape) for a in full] + [_smem_spec(), _smem_spec()],
        out_specs=[y_spec, per_b((CONV_K - 1, CONV_DIM)), per_b((SSD_D_INNER, D_STATE)),
                   per_b((WINDOW, KV_DIM)), per_b((WINDOW, KV_DIM))],
        out_shape=out_shape,
        scratch_shapes=[
            pltpu.VMEM((8 + tq, CONV_DIM), F32),
            pltpu.VMEM((D_STATE, SSD_D_INNER), F32),
            pltpu.VMEM((2, CHUNK + tq, KV_DIM), BF16),
            pltpu.VMEM((2, CHUNK + tq, KV_DIM), BF16),
            pltpu.VMEM((2, N_HEADS, WINDOW, 2 * WINDOW), F32),
            pltpu.VMEM((tq, D_MODEL), BF16),
            pltpu.VMEM((tq, D_MODEL), F32),
            pltpu.VMEM((tq, IN_COLS), F32),
        ],
        compiler_params=_params(("arbitrary",)),
        name="prompt_layer",
    )(x3, mk_b, mv_b, *full, relb, sinks)


def _sample_mixer_body(z_ref, xbc_ref, q_ref, k_ref, v_ref, dt_ref,
                       sconv_ref, sssm_ref, ck_ref, cv_ref,
                       cw_ref, cb_ref, dtb_ref, alog_ref, dskip_ref, gn_ref,
                       tcum_ref, tseq_ref, expand_ref, bkt_c_ref, bkt_n_ref, relb_ref, sink_ref,
                       mix_ref, conv_out, ssm_out, k_out, v_out,
                       xpad, bias_c, bias_n, *, dec_seq):
    step = pl.program_id(0)
    n_seq = SAMPLE_SEQS
    rows = n_seq * dec_seq

    @pl.when(step == 0)
    def _():
        _build_bias(bias_c, lambda i: bkt_c_ref[...], relb_ref, 1)
        _build_bias(bias_n, lambda i: bkt_n_ref[...], relb_ref, 1)

    xbc = xbc_ref[...]
    xpad[:, 8:8 + dec_seq, :] = xbc.reshape(n_seq, dec_seq, CONV_DIM)
    xpad[:, 5:8, :] = sconv_ref[...]
    taps = [xpad[:, 5 + k:5 + k + dec_seq, :].reshape(rows, CONV_DIM) for k in range(CONV_K - 1)] + [xbc]
    conv = _conv_taps(cw_ref, cb_ref, taps)
    conv_out[...] = xpad[:, 5 + dec_seq:8 + dec_seq, :]

    row = lax.broadcasted_iota(jnp.int32, (rows, rows), 0)
    col = lax.broadcasted_iota(jnp.int32, (rows, rows), 1)
    tseq = tseq_ref[...]
    same_seq = tseq > 0
    causal = same_seq & (col <= row)
    xs, bm, cm, cs, cs_t, (dt_x, dend_x, ecs_x, seqdec_x) = _ssd_prepare(
        conv, dt_ref[...], dtb_ref, alog_ref, tcum_ref[...],
        lambda cs_, a_: _sel_left(tseq, a_), lambda total: [jnp.exp(total)], expand_ref[...])
    xdt = xs * dt_x
    xde = (xdt * dend_x).astype(BF16)
    bb = bm.astype(BF16)
    cb = cm.astype(BF16)
    seq_of_row = lax.broadcasted_iota(jnp.int32, (rows, 1), 0) // dec_seq
    seq_of_lane = lax.broadcasted_iota(jnp.int32, (1, rows), 1) // dec_seq
    b_t = [bm[:, g * D_STATE:(g + 1) * D_STATE].T for g in range(SSD_GROUPS)]
    y_off = [None] * SSD_GROUPS
    for i in range(n_seq):
        st_t = sssm_ref[i].T
        new_parts = []
        for g in range(SSD_GROUPS):
            gs = slice(g * GROUP_W, (g + 1) * GROUP_W)
            ns = slice(g * D_STATE, (g + 1) * D_STATE)
            c_i = jnp.where(seq_of_row == i, cm[:, ns], 0.0).astype(BF16)
            part = _dot(c_i, st_t[:, gs].astype(BF16))
            y_off[g] = part if y_off[g] is None else y_off[g] + part
            b_i = jnp.where(seq_of_lane == i, b_t[g], 0.0).astype(BF16)
            dec = seqdec_x[i * dec_seq:i * dec_seq + 1, gs]
            new_parts.append(dec * st_t[:, gs] + _dot(b_i, xde[:, gs]))
        ssm_out[i] = jnp.concatenate(new_parts, axis=-1).T
    y_parts = []
    for g in range(SSD_GROUPS):
        gs = slice(g * GROUP_W, (g + 1) * GROUP_W)
        ns = slice(g * D_STATE, (g + 1) * D_STATE)
        cb_g = _dot_nt(cb[:, ns], bb[:, ns])
        y_parts.append(_ssd_diag(cs, cs_t, cb_g, xdt, causal, g) + y_off[g] * ecs_x[:, gs])
    y = jnp.concatenate(y_parts, axis=-1) + dskip_ref[...] * xs
    mix_ref[:, 0:SSD_D_INNER] = _gated_norm(y, z_ref[...], gn_ref).astype(BF16)

    lo = _lane_lo()
    k_new = k_ref[...]
    v_new = v_ref[...]
    k_var = [k_new.astype(BF16), pltpu.roll(k_new, HALF, 1).astype(BF16)]
    v_new_r = pltpu.roll(v_new, HALF, 1)
    v_dup = [jnp.where(lo, v_new, v_new_r).astype(BF16), jnp.where(lo, v_new_r, v_new).astype(BF16)]
    qf = q_ref[...].astype(F32)
    q_masked = []
    s_new = []
    for h in range(N_HEADS):
        pair, par = h // 2, h % 2
        j = h // Q_PER_KV
        q_pair = qf[:, pair * LANES:(pair + 1) * LANES]
        qm = jnp.where(lo, q_pair, 0.0) if par == 0 else jnp.where(lo, 0.0, q_pair)
        q_masked.append(qm)
        s_new.append(_dot_nt(qm.astype(BF16), k_var[(j + par) % 2]) + bias_n[0, h])
    stack_rows = lax.broadcasted_iota(jnp.int32, (Q_PER_KV * dec_seq, 1), 0) // dec_seq
    att_rows = []
    for i in range(n_seq):
        rs = slice(i * dec_seq, (i + 1) * dec_seq)
        kc = ck_ref[i]
        vc = cv_ref[i]
        kc_r = pltpu.roll(kc, HALF, 1)
        vc_r = pltpu.roll(vc, HALF, 1)
        pieces = []
        for j in range(N_KV_HEADS):
            heads = range(j * Q_PER_KV, (j + 1) * Q_PER_KV)
            kdup = (jnp.where(lo, kc, kc_r) if j == 0 else jnp.where(lo, kc_r, kc)).astype(BF16)
            vdup = (jnp.where(lo, vc, vc_r) if j == 0 else jnp.where(lo, vc_r, vc)).astype(BF16)
            qs = jnp.concatenate([q_masked[h][rs] for h in heads], axis=0).astype(BF16)
            sc = _dot_nt(qs, kdup) + jnp.concatenate([bias_c[0, h] for h in heads], axis=0)
            sn = jnp.concatenate([s_new[h][rs] for h in heads], axis=0)
            sink = jnp.zeros((Q_PER_KV * dec_seq, 1), F32)
            for hh, h in enumerate(heads):
                sink = jnp.where(stack_rows == hh, sink_ref[h], sink)
            m = jnp.maximum(jnp.maximum(jnp.max(sc, axis=-1, keepdims=True),
                                        jnp.max(sn, axis=-1, keepdims=True)), sink)
            pc = jnp.exp(sc - m)
            pn = jnp.exp(sn - m)
            denom = (jnp.sum(pc, axis=-1, keepdims=True) + jnp.sum(pn, axis=-1, keepdims=True)
                     + jnp.exp(sink - m))
            o = (_dot(pc.astype(BF16), vdup) + _dot(pn.astype(BF16), v_dup[j])) / denom
            for pr in range(Q_PER_KV // 2):
                even = o[(2 * pr) * dec_seq:(2 * pr + 1) * dec_seq]
                odd = o[(2 * pr + 1) * dec_seq:(2 * pr + 2) * dec_seq]
                pieces.append(jnp.where(lo, even, odd))
        att_rows.append(jnp.concatenate(pieces, axis=-1))
        keep = WINDOW - dec_seq
        k_out[i] = jnp.concatenate([kc[dec_seq:, :], k_new[rs]], axis=0) if keep else k_new[rs]
        v_out[i] = jnp.concatenate([vc[dec_seq:, :], v_new[rs]], axis=0) if keep else v_new[rs]
    mix_ref[:, SSD_D_INNER:] = jnp.concatenate(att_rows, axis=0).astype(BF16)


def _sample_mixer(z, xbc, q, k, v, dt, sconv, sssm, ck, cv, small, consts, n_seq_total, dec_seq):
    cw, cb, dtb, alog, dskip_x, gn, relb, sinks = small
    tcum, tseq, expand, bkt_c, bkt_n = consts
    rows = SAMPLE_SEQS * dec_seq
    tok = lambda w: pl.BlockSpec((rows, w), lambda i: (i, 0))
    per_s = lambda s: pl.BlockSpec((SAMPLE_SEQS,) + s, lambda i: (i,) + (0,) * len(s))
    out_shape = [
        jax.ShapeDtypeStruct((n_seq_total * dec_seq, D_MODEL), BF16),
        jax.ShapeDtypeStruct((n_seq_total, CONV_K - 1, CONV_DIM), F32),
        jax.ShapeDtypeStruct((n_seq_total, SSD_D_INNER, D_STATE), F32),
        jax.ShapeDtypeStruct((n_seq_total, WINDOW, KV_DIM), F32),
        jax.ShapeDtypeStruct((n_seq_total, WINDOW, KV_DIM), F32),
    ]
    return pl.pallas_call(
        functools.partial(_sample_mixer_body, dec_seq=dec_seq),
        grid=(n_seq_total // SAMPLE_SEQS,),
        in_specs=[tok(SSD_D_INNER), tok(CONV_DIM), tok(ATTN_DIM), tok(KV_DIM), tok(KV_DIM), tok(DT_PAD),
                  per_s((CONV_K - 1, CONV_DIM)), per_s((SSD_D_INNER, D_STATE)),
                  per_s((WINDOW, KV_DIM)), per_s((WINDOW, KV_DIM)),
                  _full_spec(cw.shape), _full_spec(cb.shape), _full_spec(dtb.shape), _full_spec(alog.shape),
                  _full_spec(dskip_x.shape), _full_spec(gn.shape),
                  _full_spec(tcum.shape), _full_spec(tseq.shape), _full_spec(expand.shape),
                  _full_spec(bkt_c.shape), _full_spec(bkt_n.shape), _smem_spec(), _smem_spec()],
        out_specs=[tok(D_MODEL), per_s((CONV_K - 1, CONV_DIM)), per_s((SSD_D_INNER, D_STATE)),
                   per_s((WINDOW, KV_DIM)), per_s((WINDOW, KV_DIM))],
        out_shape=out_shape,
        scratch_shapes=[
            pltpu.VMEM((SAMPLE_SEQS, 8 + dec_seq, CONV_DIM), F32),
            pltpu.VMEM((1, N_HEADS, dec_seq, WINDOW), F32),
            pltpu.VMEM((1, N_HEADS, rows, rows), F32),
        ],
        compiler_params=_params(("arbitrary",)),
        name="sample_mixer",
    )(z, xbc, q, k, v, dt, sconv, sssm, ck, cv, cw, cb, dtb, alog, dskip_x, gn,
      tcum, tseq, expand, bkt_c, bkt_n, relb, sinks)


def _post1_body(x_ref, mix_ref, wout_ref, gc_ref, wcq_ref, h_ref, qc_ref):
    h = x_ref[...] + _dot(mix_ref[...], wout_ref[...])
    h_ref[...] = h
    qc_ref[...] = _dot(_rms(h, gc_ref[...]).astype(BF16), wcq_ref[...]).astype(BF16)


def _post1(x2, mix, w_out, g_cross, w_cq, tm):
    t = x2.shape[0]
    row = lambda w: pl.BlockSpec((tm, w), lambda i: (i, 0))
    return pl.pallas_call(
        _post1_body,
        grid=(t // tm,),
        in_specs=[row(D_MODEL), row(D_MODEL), _full_spec(w_out.shape), _full_spec(g_cross.shape),
                  _full_spec(w_cq.shape)],
        out_specs=[row(D_MODEL), row(CA_DIM)],
        out_shape=[jax.ShapeDtypeStruct((t, D_MODEL), F32), jax.ShapeDtypeStruct((t, CA_DIM), BF16)],
        compiler_params=_params(("parallel",)),
        name="out_proj",
    )(x2, mix, w_out, g_cross, w_cq)


def _mem_kv_body(mem_ref, g_ref, wk_ref, wv_ref, k_ref, v_ref, kb_ref, vb_ref):
    mn = _rms(mem_ref[...], g_ref[...]).astype(BF16)
    k = _dot(mn, wk_ref[...])
    v = _dot(mn, wv_ref[...])
    k_ref[...] = k
    v_ref[...] = v
    kb_ref[...] = k.astype(BF16)
    vb_ref[...] = v.astype(BF16)


def _mem_kv(mem2, g_mem, w_ck, w_cv, tm):
    t = mem2.shape[0]
    row = lambda w: pl.BlockSpec((tm, w), lambda i: (i, 0))
    return pl.pallas_call(
        _mem_kv_body,
        grid=(t // tm,),
        in_specs=[row(D_MODEL), _full_spec(g_mem.shape), _full_spec(w_ck.shape), _full_spec(w_cv.shape)],
        out_specs=[row(CA_DIM)] * 4,
        out_shape=[jax.ShapeDtypeStruct((t, CA_DIM), F32)] * 2 + [jax.ShapeDtypeStruct((t, CA_DIM), BF16)] * 2,
        compiler_params=_params(("parallel",)),
        name="mem_kv",
    )(mem2, g_mem, w_ck, w_cv)


def _cross_heads(q, k, v):
    out = []
    for h in range(CA_HEADS):
        hs = slice(h * CA_HEAD_DIM, (h + 1) * CA_HEAD_DIM)
        s = _dot_nt(q[:, hs], k[:, hs]) * (CA_HEAD_DIM ** -0.5)
        m = jnp.max(s, axis=-1, keepdims=True)
        p = jnp.exp(s - m)
        out.append(_dot(p.astype(BF16), v[:, hs]) / jnp.sum(p, axis=-1, keepdims=True))
    return jnp.concatenate(out, axis=-1)


def _cross_prompt_body(q_ref, k_ref, v_ref, o_ref):
    o_ref[0] = _cross_heads(q_ref[0], k_ref[0].astype(BF16), v_ref[0].astype(BF16)).astype(BF16)


def _cross_prompt(qc, mk, mv, batch, seq, tm):
    q3 = qc.reshape(batch, seq, CA_DIM)
    tok = pl.BlockSpec((1, tm, CA_DIM), lambda b, i: (b, i, 0))
    mem = pl.BlockSpec((1, N_MEM, CA_DIM), lambda b, i: (b, 0, 0))
    return pl.pallas_call(
        _cross_prompt_body,
        grid=(batch, seq // tm),
        in_specs=[tok, mem, mem],
        out_specs=tok,
        out_shape=jax.ShapeDtypeStruct((batch, seq, CA_DIM), BF16),
        compiler_params=_params(("parallel", "parallel")),
        name="cross_prompt",
    )(q3, mk, mv).reshape(batch * seq, CA_DIM)


def _cross_sample_body(q_ref, k_ref, v_ref, o_ref, *, n_seq, dec_seq):
    q = q_ref[...].astype(F32)
    n_keys = N_MEM * CA_HEADS
    col_head = lax.broadcasted_iota(jnp.int32, (1, n_keys), 1) & (CA_HEADS - 1)
    row_head = lax.broadcasted_iota(jnp.int32, (CA_HEADS * dec_seq, 1), 0) // dec_seq
    own = col_head == row_head
    rows = []
    for i in range(n_seq):
        qi = q[i * dec_seq:(i + 1) * dec_seq]
        qs = jnp.concatenate([qi[:, h * CA_HEAD_DIM:(h + 1) * CA_HEAD_DIM] for h in range(CA_HEADS)], axis=0)
        s = _dot_nt(qs.astype(BF16), k_ref[i].astype(BF16)) * (CA_HEAD_DIM ** -0.5)
        s = jnp.where(own, s, NEG)
        m = jnp.max(s, axis=-1, keepdims=True)
        p = jnp.exp(s - m)
        o = _dot(p.astype(BF16), v_ref[i].astype(BF16)) / jnp.sum(p, axis=-1, keepdims=True)
        rows.append(jnp.concatenate([o[h * dec_seq:(h + 1) * dec_seq] for h in range(CA_HEADS)], axis=-1))
    o_ref[...] = jnp.concatenate(rows, axis=0).astype(BF16)


def _cross_sample(qc, ck, cv, n_seq_total, dec_seq, n_seq):
    rows = n_seq * dec_seq
    tok = pl.BlockSpec((rows, CA_DIM), lambda i: (i, 0))
    mem = pl.BlockSpec((n_seq, N_MEM * CA_HEADS, CA_HEAD_DIM), lambda i: (i, 0, 0))
    return pl.pallas_call(
        functools.partial(_cross_sample_body, n_seq=n_seq, dec_seq=dec_seq),
        grid=(n_seq_total // n_seq,),
        in_specs=[tok, mem, mem],
        out_specs=tok,
        out_shape=jax.ShapeDtypeStruct((n_seq_total * dec_seq, CA_DIM), BF16),
        compiler_params=_params(("parallel",)),
        name="cross_sample",
    )(qc, ck, cv)


def _post2_body(h_ref, o_ref, wco_ref, gf_ref, wg_ref, wu_ref, wd_ref, gfin_ref, y_ref):
    h = h_ref[...] + _dot(o_ref[...], wco_ref[...])
    hn = _rms(h, gf_ref[...]).astype(BF16)
    acc = h
    for lo, hi in FF_SPLITS:
        act = _silu(_dot(hn, wg_ref[:, lo:hi])) * _dot(hn, wu_ref[:, lo:hi])
        acc = acc + _dot(act.astype(BF16), wd_ref[lo:hi, :])
    y_ref[...] = _rms(acc, gfin_ref[...])


def _post2(h1, o, w_co, g_ffn, w_gate, w_up, w_down, g_final, tm):
    t = h1.shape[0]
    row = lambda w: pl.BlockSpec((tm, w), lambda i: (i, 0))
    return pl.pallas_call(
        _post2_body,
        grid=(t // tm,),
        in_specs=[row(D_MODEL), row(CA_DIM), _full_spec(w_co.shape), _full_spec(g_ffn.shape),
                  _full_spec(w_gate.shape), _full_spec(w_up.shape), _full_spec(w_down.shape),
                  _full_spec(g_final.shape)],
        out_specs=row(D_MODEL),
        out_shape=jax.ShapeDtypeStruct((t, D_MODEL), F32),
        compiler_params=_params(("parallel",)),
        name="ffn",
    )(h1, o, w_co, g_ffn, w_gate, w_up, w_down, g_final)


def _prompt_consts():
    qi = np.arange(WINDOW)[:, None]
    ji = np.arange(2 * WINDOW)[None, :]
    dist = qi + WINDOW - ji
    inband = (dist >= 0) & (dist < WINDOW)
    bucket = np.where(inband, _t5_bucket_np(dist), -1)
    first = np.where(ji >= WINDOW, bucket, -1)
    buckets = np.stack([first, bucket]).astype(np.int32)
    return (jnp.asarray(_tril_np(CHUNK), BF16), jnp.asarray(_expand_np(), BF16), jnp.asarray(buckets))


def _sample_consts(dec_seq, cache_len):
    rows = SAMPLE_SEQS * dec_seq
    r = np.arange(rows)
    same = (r[:, None] // dec_seq) == (r[None, :] // dec_seq)
    tcum = (same & (r[None, :] <= r[:, None])).astype(np.float32)
    tseq = same.astype(np.float32)
    t = np.arange(dec_seq)[:, None]
    j = np.arange(cache_len)[None, :]
    dist_c = t + cache_len - j
    bkt_c = np.where((dist_c >= 0) & (dist_c < WINDOW), _t5_bucket_np(dist_c), -1).astype(np.int32)
    dist_n = (r[:, None] % dec_seq) - (r[None, :] % dec_seq)
    ok = same & (dist_n >= 0) & (dist_n < WINDOW)
    bkt_n = np.where(ok, _t5_bucket_np(dist_n), -1).astype(np.int32)
    return (jnp.asarray(tcum, BF16), jnp.asarray(tseq, BF16), jnp.asarray(_expand_np(), BF16),
            jnp.asarray(bkt_c), jnp.asarray(bkt_n))


def _pick_tile(t, pref):
    tm = min(t, pref)
    while t % tm:
        tm //= 2
    return tm


def kernel(x_prompt, x_sample, mem_prompt, state_conv, state_ssm, cache_swa_k, cache_swa_v, cache_mem_k, cache_mem_v, rel_bias, g_mix, w_in, conv_w, conv_b, dt_bias, a_log, d_skip, g_ssd_norm, sinks, w_out, g_cross, g_mem, w_cq, w_ck, w_cv, w_co, g_ffn, w_gate, w_up, w_down, g_final):
    assert g_mix.shape[0] == 1, "single-layer trunk"
    batch, seq, _ = x_prompt.shape
    n_dec, dec_seq, _ = x_sample.shape
    cache_len = cache_swa_k.shape[2]
    assert seq % CHUNK == 0 and cache_len == WINDOW and n_dec % SAMPLE_SEQS == 0 and dec_seq == 8

    wi = w_in[0]
    s0, s1, s2, s3, s4 = (SSD_D_INNER, SSD_D_INNER + CONV_DIM, SSD_D_INNER + CONV_DIM + SSD_HEADS,
                          SSD_D_INNER + CONV_DIM + SSD_HEADS + ATTN_DIM,
                          SSD_D_INNER + CONV_DIM + SSD_HEADS + ATTN_DIM + KV_DIM)
    w_in_r = jnp.concatenate(
        [wi[:, :s0], wi[:, s0:s1], wi[:, s2:s3], wi[:, s3:s4], wi[:, s4:],
         jnp.pad(wi[:, s1:s2], ((0, 0), (0, DT_PAD - SSD_HEADS)))], axis=1).astype(BF16)
    row = lambda a: a.reshape(1, -1).astype(F32)
    pad_h = lambda a: jnp.pad(a.reshape(1, -1).astype(F32), ((0, 0), (0, DT_PAD - SSD_HEADS)))
    small = (conv_w[0].astype(F32), row(conv_b[0]), pad_h(dt_bias[0]), pad_h(a_log[0]),
             jnp.repeat(d_skip[0].astype(F32), SSD_HEAD_DIM).reshape(1, -1), row(g_ssd_norm[0]),
             rel_bias.astype(F32), sinks[0].astype(F32))
    bf = lambda w: w[0].astype(BF16)
    w_out_b, w_cq_b, w_ck_b, w_cv_b, w_co_b = bf(w_out), bf(w_cq), bf(w_ck), bf(w_cv), bf(w_co)
    w_gate_b, w_up_b, w_down_b = bf(w_gate), bf(w_up), bf(w_down)
    g_mix_r, g_cross_r, g_mem_r, g_ffn_r, g_fin_r = row(g_mix[0]), row(g_cross[0]), row(g_mem[0]), row(g_ffn[0]), row(g_final)

    tp = batch * seq
    n_chunks = 2 if seq % (2 * CHUNK) == 0 else 1
    mem2 = mem_prompt.reshape(batch * N_MEM, D_MODEL)
    mk, mv, mk_b, mv_b = _mem_kv(mem2, g_mem_r, w_ck_b, w_cv_b, _pick_tile(batch * N_MEM, 512))
    dense = (g_mix_r, w_in_r, w_out_b, g_cross_r, w_cq_b, w_co_b, g_ffn_r, w_gate_b, w_up_b, w_down_b, g_fin_r)
    y_prompt, p_conv, p_ssm, p_k, p_v = _prompt_layer(
        x_prompt, mk_b.reshape(batch, N_MEM, CA_DIM), mv_b.reshape(batch, N_MEM, CA_DIM),
        dense, small, _prompt_consts(), n_chunks)

    ts = n_dec * dec_seq
    xs2 = x_sample.reshape(ts, D_MODEL)
    tm_s = _pick_tile(ts, 512)
    z, xbc, q, k, v, dt = _in_proj(xs2, g_mix_r, w_in_r, tm_s)
    mix_s, s_conv, s_ssm, s_k, s_v = _sample_mixer(
        z, xbc, q, k, v, dt, state_conv[0], state_ssm[0].reshape(n_dec, SSD_D_INNER, D_STATE),
        cache_swa_k[0].reshape(n_dec, cache_len, KV_DIM), cache_swa_v[0].reshape(n_dec, cache_len, KV_DIM),
        small, _sample_consts(dec_seq, cache_len), n_dec, dec_seq)
    h1s, qcs = _post1(xs2, mix_s, w_out_b, g_cross_r, w_cq_b, tm_s)
    os_ = _cross_sample(qcs, cache_mem_k[0].reshape(n_dec, N_MEM * CA_HEADS, CA_HEAD_DIM),
                        cache_mem_v[0].reshape(n_dec, N_MEM * CA_HEADS, CA_HEAD_DIM), n_dec, dec_seq, 8)
    y_sample = _post2(h1s, os_, w_co_b, g_ffn_r, w_gate_b, w_up_b, w_down_b, g_fin_r, tm_s)

    return (y_prompt.reshape(batch, seq, D_MODEL), y_sample.reshape(n_dec, dec_seq, D_MODEL),
            p_conv[None], p_ssm.reshape(1, batch, SSD_HEADS, SSD_HEAD_DIM, D_STATE),
            p_k.reshape(1, batch, WINDOW, N_KV_HEADS, ATTN_HEAD_DIM),
            p_v.reshape(1, batch, WINDOW, N_KV_HEADS, ATTN_HEAD_DIM),
            mk.reshape(1, batch, N_MEM, CA_HEADS, CA_HEAD_DIM), mv.reshape(1, batch, N_MEM, CA_HEADS, CA_HEAD_DIM),
            s_conv[None], s_ssm.reshape(1, n_dec, SSD_HEADS, SSD_HEAD_DIM, D_STATE),
            s_k.reshape(1, n_dec, cache_len, N_KV_HEADS, ATTN_HEAD_DIM),
            s_v.reshape(1, n_dec, cache_len, N_KV_HEADS, ATTN_HEAD_DIM))
```

```python
import functools
import math

import numpy as np
import jax
import jax.numpy as jnp
from jax import lax
from jax.experimental import pallas as pl
from jax.experimental.pallas import tpu as pltpu

F32 = jnp.float32
BF16 = jnp.bfloat16

D_MODEL = 1024
SSD_D_INNER = 512
SSD_HEAD_DIM = 64
SSD_HEADS = 8
SSD_GROUPS = 2
GROUP_W = SSD_D_INNER // SSD_GROUPS
D_STATE = 128
CONV_K = 4
CONV_DIM = SSD_D_INNER + 2 * SSD_GROUPS * D_STATE
CHUNK = 128
ATTN_DIM = 512
ATTN_HEAD_DIM = 64
N_HEADS = 8
N_KV_HEADS = 2
Q_PER_KV = N_HEADS // N_KV_HEADS
KV_DIM = N_KV_HEADS * ATTN_HEAD_DIM
WINDOW = 128
N_BUCKETS = 32
MAX_EXACT = N_BUCKETS // 2
MAX_DISTANCE = 128
N_MEM = 256
CA_HEADS = 4
CA_HEAD_DIM = 128
CA_DIM = CA_HEADS * CA_HEAD_DIM
D_FF = 2816
EPS = 1e-6

LANES = 128
HALF = LANES // 2
DT_PAD = LANES
COL_Z = 0
COL_XBC = COL_Z + SSD_D_INNER
COL_Q = COL_XBC + CONV_DIM
COL_K = COL_Q + ATTN_DIM
COL_V = COL_K + KV_DIM
COL_DT = COL_V + KV_DIM
IN_COLS = COL_DT + DT_PAD
NEG = -1e30
SAMPLE_SEQS = 16
VMEM_LIMIT = 56 * 1024 * 1024
FF_SPLITS = ((0, 1024), (1024, 2048), (2048, D_FF))
SECOND_STREAM_LEAD = 0.5
FF_PIECES = tuple((lo, min(lo + 512, D_FF)) for lo in range(0, D_FF, 512))
PROJ_PIECES = ((COL_Z, COL_XBC), (COL_XBC, COL_XBC + 512), (COL_XBC + 512, COL_Q), (COL_Q, COL_K), (COL_K, IN_COLS))


def _rms(x, g):
    return x * lax.rsqrt(jnp.mean(x * x, axis=-1, keepdims=True) + EPS) * g


def _silu(x):
    return x * jax.nn.sigmoid(x)


def _softplus(x):
    return jnp.maximum(x, 0.0) + jnp.log1p(jnp.exp(-jnp.abs(x)))


def _dot(a, b):
    return jnp.dot(a, b, preferred_element_type=F32)


def _dot_nt(a, b):
    return lax.dot_general(a, b, (((1,), (1,)), ((), ())), preferred_element_type=F32)


def _split3(a):
    hi = a.astype(BF16)
    r = a - hi.astype(F32)
    mid = r.astype(BF16)
    lo = (r - mid.astype(F32)).astype(BF16)
    return hi, mid, lo


def _sel_left(t01, a):
    hi, mid, lo = _split3(a)
    return _dot(t01, hi) + _dot(t01, mid) + _dot(t01, lo)


def _sel_right(a, e01):
    hi, mid, lo = _split3(a)
    return _dot(hi, e01) + _dot(mid, e01) + _dot(lo, e01)


def _lane_lo():
    return lax.broadcasted_iota(jnp.int32, (1, LANES), 1) < HALF


def _t5_bucket_np(dist):
    n = np.maximum(dist, 0)
    ratio = np.log(np.maximum(n, 1).astype(np.float32) / np.float32(MAX_EXACT))
    large = MAX_EXACT + (ratio / np.float32(math.log(MAX_DISTANCE / MAX_EXACT))
                         * np.float32(N_BUCKETS - MAX_EXACT)).astype(np.int32)
    large = np.minimum(large, N_BUCKETS - 1)
    return np.where(n < MAX_EXACT, n, large).astype(np.int32)


def _tril_np(n):
    return np.tril(np.ones((n, n), np.float32))


def _expand_np():
    e = np.zeros((LANES, SSD_D_INNER), np.float32)
    for h in range(SSD_HEADS):
        e[h, h * SSD_HEAD_DIM:(h + 1) * SSD_HEAD_DIM] = 1.0
    return e


def _full_spec(shape):
    nd = len(shape)
    return pl.BlockSpec(shape, lambda *_: (0,) * nd, pipeline_mode=pl.Buffered(1))


def _smem_spec():
    return pl.BlockSpec(memory_space=pltpu.SMEM)


def _params(sem):
    return pltpu.CompilerParams(dimension_semantics=sem, vmem_limit_bytes=VMEM_LIMIT)


def _in_proj_body(x_ref, g_ref, w_ref, z_ref, xbc_ref, q_ref, k_ref, v_ref, dt_ref):
    xn = _rms(x_ref[...], g_ref[...]).astype(BF16)

    def seg(lo, hi):
        return _dot(xn, w_ref[:, lo:hi])

    z_ref[...] = seg(COL_Z, COL_XBC)
    xbc_ref[...] = seg(COL_XBC, COL_Q)
    q_ref[...] = (seg(COL_Q, COL_K) * (ATTN_HEAD_DIM ** -0.5)).astype(BF16)
    k_ref[...] = seg(COL_K, COL_V)
    v_ref[...] = seg(COL_V, COL_DT)
    dt_ref[...] = seg(COL_DT, IN_COLS)


def _in_proj(x2, g_mix, w_in_r, tm):
    t = x2.shape[0]
    row = lambda w: pl.BlockSpec((tm, w), lambda i: (i, 0))
    outs = [(SSD_D_INNER, F32), (CONV_DIM, F32), (ATTN_DIM, BF16), (KV_DIM, F32), (KV_DIM, F32), (DT_PAD, F32)]
    return pl.pallas_call(
        _in_proj_body,
        grid=(t // tm,),
        in_specs=[row(D_MODEL), _full_spec((1, D_MODEL)), _full_spec((D_MODEL, IN_COLS))],
        out_specs=[row(w) for w, _ in outs],
        out_shape=[jax.ShapeDtypeStruct((t, w), d) for w, d in outs],
        compiler_params=_params(("parallel",)),
        name="in_proj",
    )(x2, g_mix, w_in_r)


def _conv_taps(cw_ref, cb_ref, taps):
    acc = cb_ref[...] + taps[0] * cw_ref[0:1, :]
    for k in range(1, CONV_K):
        acc = acc + taps[k] * cw_ref[k:k + 1, :]
    return _silu(acc)


def _ssd_prepare(conv, dt_raw, dtb_ref, alog_ref, tcum, total_fn, extra_fn, expand):
    xs = conv[:, :SSD_D_INNER]
    bm = conv[:, SSD_D_INNER:SSD_D_INNER + SSD_GROUPS * D_STATE]
    cm = conv[:, SSD_D_INNER + SSD_GROUPS * D_STATE:]
    dt = _softplus(dt_raw + dtb_ref[...])
    a = dt * (-jnp.exp(alog_ref[...]))
    cs = _sel_left(tcum, a)
    total = total_fn(cs, a)
    pieces = [dt, jnp.exp(total - cs), jnp.exp(cs)] + extra_fn(total)
    rows = cs.shape[0]
    ex = _sel_right(jnp.concatenate(pieces, axis=0), expand)
    ex = [ex[i * rows:(i + 1) * rows] for i in range(len(pieces))]
    return xs, bm, cm, cs, cs.T, ex


def _ssd_diag(cs, cs_t, cb_g, xdt, mask, g):
    lo = _lane_lo()
    out = []
    for pr in range(2):
        h0 = g * 4 + 2 * pr
        xp = xdt[:, (h0 // 2) * LANES:(h0 // 2 + 1) * LANES]
        x_lo = jnp.where(lo, xp, 0.0).astype(BF16)
        x_hi = jnp.where(lo, 0.0, xp).astype(BF16)
        acc = None
        for h, xh in ((h0, x_lo), (h0 + 1, x_hi)):
            diff = cs[:, h:h + 1] - cs_t[h:h + 1, :]
            decay = jnp.exp(jnp.where(mask, diff, -jnp.inf))
            part = _dot((cb_g * decay).astype(BF16), xh)
            acc = part if acc is None else acc + part
        out.append(acc)
    return jnp.concatenate(out, axis=-1)


def _gated_norm(y, z, gn_ref):
    yf = y * _silu(z)
    parts = []
    for g in range(SSD_GROUPS):
        yg = yf[:, g * GROUP_W:(g + 1) * GROUP_W]
        parts.append(yg * lax.rsqrt(jnp.mean(yg * yg, axis=-1, keepdims=True) + EPS))
    return jnp.concatenate(parts, axis=-1) * gn_ref[...]


def _build_bias(bias_ref, bucket_of, relb_ref, n_tables):
    for i in range(n_tables):
        for h in range(N_HEADS):
            bias_ref[i, h] = jnp.full(bias_ref.shape[2:], NEG, F32)

    def body(t, carry):
        for i in range(n_tables):
            hit = bucket_of(i) == t
            for h in range(N_HEADS):
                bias_ref[i, h] = jnp.where(hit, relb_ref[t, h], bias_ref[i, h])
        return carry

    lax.fori_loop(0, N_BUCKETS, body, 0)


def _prompt_mixer_body(z_ref, xbc_ref, q_ref, k_ref, v_ref, dt_ref,
                       cw_ref, cb_ref, dtb_ref, alog_ref, dskip_ref, gn_ref,
                       tril_ref, expand_ref, bucket_ref, relb_ref, sink_ref,
                       mix_ref, conv_out, ssm_out, k_out, v_out,
                       xpad, state_t, kbuf, vbuf, bias):
    b = pl.program_id(0)
    c = pl.program_id(1)
    last = pl.num_programs(1) - 1
    q_rows = CHUNK

    @pl.when((b == 0) & (c == 0))
    def _():
        _build_bias(bias, lambda i: bucket_ref[i], relb_ref, 2)

    @pl.when(c == 0)
    def _():
        xpad[0:8, :] = jnp.zeros((8, CONV_DIM), F32)
        state_t[...] = jnp.zeros_like(state_t)
        kbuf[...] = jnp.zeros_like(kbuf)
        vbuf[...] = jnp.zeros_like(vbuf)

    xbc = xbc_ref[0]
    xpad[8:8 + q_rows, :] = xbc
    taps = [xpad[5 + k:5 + k + q_rows, :] for k in range(CONV_K - 1)] + [xbc]
    conv = _conv_taps(cw_ref, cb_ref, taps)
    xpad[5:8, :] = xbc[q_rows - 3:q_rows, :]

    row = lax.broadcasted_iota(jnp.int32, (q_rows, q_rows), 0)
    col = lax.broadcasted_iota(jnp.int32, (q_rows, q_rows), 1)
    causal = col <= row
    xs, bm, cm, cs, cs_t, (dt_x, dend_x, ecs_x) = _ssd_prepare(
        conv, dt_ref[0], dtb_ref, alog_ref, tril_ref[...],
        lambda cs_, a_: cs_[q_rows - 1:q_rows, :], lambda total: [], expand_ref[...])
    xdt = xs * dt_x
    xde = (xdt * dend_x).astype(BF16)
    bb = bm.astype(BF16)
    cb = cm.astype(BF16)
    y_parts = []
    for g in range(SSD_GROUPS):
        gs = slice(g * GROUP_W, (g + 1) * GROUP_W)
        ns = slice(g * D_STATE, (g + 1) * D_STATE)
        cb_g = _dot_nt(cb[:, ns], bb[:, ns])
        y_diag = _ssd_diag(cs, cs_t, cb_g, xdt, causal, g)
        st_old = state_t[:, gs]
        y_off = _dot(cb[:, ns], st_old.astype(BF16)) * ecs_x[:, gs]
        y_parts.append(y_diag + y_off)
        b_t = bm[:, ns].T.astype(BF16)
        state_t[:, gs] = ecs_x[q_rows - 1:q_rows, gs] * st_old + _dot(b_t, xde[:, gs])
    y = jnp.concatenate(y_parts, axis=-1) + dskip_ref[...] * xs
    mix_ref[0, :, 0:SSD_D_INNER] = _gated_norm(y, z_ref[0], gn_ref).astype(BF16)

    lo = _lane_lo()
    k_cur = k_ref[0]
    v_cur = v_ref[0]
    kbuf[0, q_rows:, :] = k_cur.astype(BF16)
    kbuf[1, q_rows:, :] = pltpu.roll(k_cur, HALF, 1).astype(BF16)
    vbuf[0, q_rows:, :] = v_cur.astype(BF16)
    vbuf[1, q_rows:, :] = pltpu.roll(v_cur, HALF, 1).astype(BF16)
    table = jnp.minimum(c, 1)
    for pair in range(N_HEADS // 2):
        j = (2 * pair) // Q_PER_KV
        q_pair = q_ref[0, :, pair * LANES:(pair + 1) * LANES]
        acc = None
        for par in range(2):
            h = 2 * pair + par
            qm = jnp.where(lo, q_pair, 0.0) if par == 0 else jnp.where(lo, 0.0, q_pair)
            variant = (j + par) % 2
            s = _dot_nt(qm.astype(BF16), kbuf[variant]) + bias[table, h]
            sink = sink_ref[h]
            m = jnp.maximum(jnp.max(s, axis=-1, keepdims=True), sink)
            p = jnp.exp(s - m)
            denom = jnp.sum(p, axis=-1, keepdims=True) + jnp.exp(sink - m)
            vv = vbuf[variant]
            vm = jnp.where(lo, vv, 0.0) if par == 0 else jnp.where(lo, 0.0, vv)
            part = _dot(p.astype(BF16), vm.astype(BF16)) / denom
            acc = part if acc is None else acc + part
        mix_ref[0, :, SSD_D_INNER + pair * LANES:SSD_D_INNER + (pair + 1) * LANES] = acc.astype(BF16)
    kbuf[:, 0:q_rows, :] = kbuf[:, q_rows:, :]
    vbuf[:, 0:q_rows, :] = vbuf[:, q_rows:, :]

    @pl.when(c == last)
    def _():
        conv_out[0] = xbc[q_rows - 3:q_rows, :]
        ssm_out[0] = state_t[...].T
        k_out[0] = k_cur
        v_out[0] = v_cur


def _prompt_mixer(z, xbc, q, k, v, dt, small, consts, batch, seq):
    cw, cb, dtb, alog, dskip_x, gn, relb, sinks = small
    tril, expand, buckets = consts
    nc = seq // CHUNK
    r3 = lambda a: a.reshape(batch, seq, a.shape[-1])
    tok = lambda w: pl.BlockSpec((1, CHUNK, w), lambda b, c: (b, c, 0))
    per_b = lambda s: pl.BlockSpec((1,) + s, lambda b, c: (b,) + (0,) * len(s))
    out_shape = [
        jax.ShapeDtypeStruct((batch, seq, D_MODEL), BF16),
        jax.ShapeDtypeStruct((batch, CONV_K - 1, CONV_DIM), F32),
        jax.ShapeDtypeStruct((batch, SSD_D_INNER, D_STATE), F32),
        jax.ShapeDtypeStruct((batch, WINDOW, KV_DIM), F32),
        jax.ShapeDtypeStruct((batch, WINDOW, KV_DIM), F32),
    ]
    return pl.pallas_call(
        _prompt_mixer_body,
        grid=(batch, nc),
        in_specs=[tok(SSD_D_INNER), tok(CONV_DIM), tok(ATTN_DIM), tok(KV_DIM), tok(KV_DIM), tok(DT_PAD),
                  _full_spec(cw.shape), _full_spec(cb.shape), _full_spec(dtb.shape), _full_spec(alog.shape),
                  _full_spec(dskip_x.shape), _full_spec(gn.shape),
                  _full_spec(tril.shape), _full_spec(expand.shape), _full_spec(buckets.shape),
                  _smem_spec(), _smem_spec()],
        out_specs=[tok(D_MODEL), per_b((CONV_K - 1, CONV_DIM)), per_b((SSD_D_INNER, D_STATE)),
                   per_b((WINDOW, KV_DIM)), per_b((WINDOW, KV_DIM))],
        out_shape=out_shape,
        scratch_shapes=[
            pltpu.VMEM((8 + CHUNK, CONV_DIM), F32),
            pltpu.VMEM((D_STATE, SSD_D_INNER), F32),
            pltpu.VMEM((2, 2 * WINDOW, KV_DIM), BF16),
            pltpu.VMEM((2, 2 * WINDOW, KV_DIM), BF16),
            pltpu.VMEM((2, N_HEADS, WINDOW, 2 * WINDOW), F32),
        ],
        compiler_params=_params(("arbitrary", "arbitrary")),
        name="prompt_mixer",
    )(r3(z), r3(xbc), r3(q), r3(k), r3(v), r3(dt), cw, cb, dtb, alog, dskip_x, gn,
      tril, expand, buckets, relb, sinks)


def _prompt_layer_body(x_ref, mk_ref, mv_ref, gmix_ref, win_ref, wout_ref, gc_ref, wcq_ref,
                       wco_ref, gf_ref, wg_ref, wu_ref, wd_ref, gfin_ref,
                       cw_ref, cb_ref, dtb_ref, alog_ref, dskip_ref, gn_ref,
                       tril_ref, expand_ref, bucket_ref, relb_ref, sink_ref,
                       y_ref, conv_out, ssm_out, k_out, v_out,
                       xpad, state_t, kbuf, vbuf, bias, mix, hbuf, qcbuf, *, n_chunks, tiles_per_seq):
    step = pl.program_id(0)
    n_tiles = pl.num_programs(0) - 1
    c = lax.rem(jnp.minimum(step, n_tiles - 1), tiles_per_seq)
    last = tiles_per_seq - 1
    tq = n_chunks * CHUNK

    @pl.when(step == 0)
    def _():
        _build_bias(bias, lambda i: bucket_ref[i], relb_ref, 2)
        hbuf[...] = jnp.zeros_like(hbuf)
        qcbuf[...] = jnp.zeros_like(qcbuf)

    @pl.when(c == 0)
    def _():
        xpad[0:8, :] = jnp.zeros((8, CONV_DIM), F32)
        state_t[...] = jnp.zeros_like(state_t)
        kbuf[:, 0:CHUNK, :] = jnp.zeros((2, CHUNK, KV_DIM), BF16)
        vbuf[:, 0:CHUNK, :] = jnp.zeros((2, CHUNK, KV_DIM), BF16)

    o = _cross_heads(qcbuf[...], mk_ref[0], mv_ref[0]).astype(BF16)
    h2 = hbuf[...] + _dot(o, wco_ref[...])
    hn = _rms(h2, gf_ref[...]).astype(BF16)
    acc = h2
    for f_lo, f_hi in FF_SPLITS:
        act = _silu(_dot(hn, wg_ref[:, f_lo:f_hi])) * _dot(hn, wu_ref[:, f_lo:f_hi])
        acc = acc + _dot(act.astype(BF16), wd_ref[f_lo:f_hi, :])
    y_ref[0] = _rms(acc, gfin_ref[...])

    x = x_ref[0]
    xn = _rms(x, gmix_ref[...]).astype(BF16)

    def seg(lo, hi):
        return _dot(xn, win_ref[:, lo:hi])

    z = seg(COL_Z, COL_XBC)
    xbc = seg(COL_XBC, COL_Q)
    q = (seg(COL_Q, COL_K) * (ATTN_HEAD_DIM ** -0.5)).astype(BF16)
    k_new = seg(COL_K, COL_V)
    v_new = seg(COL_V, COL_DT)
    dt_raw = seg(COL_DT, IN_COLS)

    xpad[8:8 + tq, :] = xbc
    taps = [xpad[5 + k:5 + k + tq, :] for k in range(CONV_K - 1)] + [xbc]
    conv = _conv_taps(cw_ref, cb_ref, taps)
    xpad[5:8, :] = xbc[tq - 3:tq, :]

    kbuf[0, CHUNK:, :] = k_new.astype(BF16)
    kbuf[1, CHUNK:, :] = pltpu.roll(k_new, HALF, 1).astype(BF16)
    vbuf[0, CHUNK:, :] = v_new.astype(BF16)
    vbuf[1, CHUNK:, :] = pltpu.roll(v_new, HALF, 1).astype(BF16)

    row = lax.broadcasted_iota(jnp.int32, (CHUNK, CHUNK), 0)
    col = lax.broadcasted_iota(jnp.int32, (CHUNK, CHUNK), 1)
    causal = col <= row
    lo = _lane_lo()
    for ci in range(n_chunks):
        rs = slice(ci * CHUNK, (ci + 1) * CHUNK)
        xs, bm, cm, cs, cs_t, (dt_x, dend_x, ecs_x) = _ssd_prepare(
            conv[rs], dt_raw[rs], dtb_ref, alog_ref, tril_ref[...],
            lambda cs_, a_: cs_[CHUNK - 1:CHUNK, :], lambda total: [], expand_ref[...])
        xdt = xs * dt_x
        xde = (xdt * dend_x).astype(BF16)
        bb = bm.astype(BF16)
        cb = cm.astype(BF16)
        y_parts = []
        for g in range(SSD_GROUPS):
            gs = slice(g * GROUP_W, (g + 1) * GROUP_W)
            ns = slice(g * D_STATE, (g + 1) * D_STATE)
            cb_g = _dot_nt(cb[:, ns], bb[:, ns])
            y_diag = _ssd_diag(cs, cs_t, cb_g, xdt, causal, g)
            st_old = state_t[:, gs]
            y_off = _dot(cb[:, ns], st_old.astype(BF16)) * ecs_x[:, gs]
            y_parts.append(y_diag + y_off)
            b_t = bm[:, ns].T.astype(BF16)
            state_t[:, gs] = ecs_x[CHUNK - 1:CHUNK, gs] * st_old + _dot(b_t, xde[:, gs])
        y = jnp.concatenate(y_parts, axis=-1) + dskip_ref[...] * xs
        mix[rs, 0:SSD_D_INNER] = _gated_norm(y, z[rs], gn_ref).astype(BF16)

        keys = slice(ci * CHUNK, (ci + 2) * CHUNK)
        table = jnp.minimum(c, 1) if ci == 0 else 1
        for pair in range(N_HEADS // 2):
            j = (2 * pair) // Q_PER_KV
            q_pair = q[rs, pair * LANES:(pair + 1) * LANES]
            acc = None
            for par in range(2):
                h = 2 * pair + par
                qm = jnp.where(lo, q_pair, 0.0) if par == 0 else jnp.where(lo, 0.0, q_pair)
                variant = (j + par) % 2
                s = _dot_nt(qm.astype(BF16), kbuf[variant, keys, :]) + bias[table, h]
                sink = sink_ref[h]
                m = jnp.maximum(jnp.max(s, axis=-1, keepdims=True), sink)
                p = jnp.exp(s - m)
                denom = jnp.sum(p, axis=-1, keepdims=True) + jnp.exp(sink - m)
                vv = vbuf[variant, keys, :]
                vm = jnp.where(lo, vv, 0.0) if par == 0 else jnp.where(lo, 0.0, vv)
                part = _dot(p.astype(BF16), vm.astype(BF16)) / denom
                acc = part if acc is None else acc + part
            mix[rs, SSD_D_INNER + pair * LANES:SSD_D_INNER + (pair + 1) * LANES] = acc.astype(BF16)
    kbuf[:, 0:CHUNK, :] = kbuf[:, tq:tq + CHUNK, :]
    vbuf[:, 0:CHUNK, :] = vbuf[:, tq:tq + CHUNK, :]

    h1 = x + _dot(mix[...], wout_ref[...])
    hbuf[...] = h1
    qcbuf[...] = _dot(_rms(h1, gc_ref[...]).astype(BF16), wcq_ref[...]).astype(BF16)

    @pl.when((c == last) & (step < n_tiles))
    def _():
        conv_out[0] = xbc[tq - 3:tq, :]
        ssm_out[0] = state_t[...].T
        k_out[0] = k_new[tq - WINDOW:tq, :]
        v_out[0] = v_new[tq - WINDOW:tq, :]


def _alternate(first, second):
    streams = [[0.0, 1.0, first], [0.0, SECOND_STREAM_LEAD, second]]
    while streams:
        entry = min(streams, key=lambda e: e[0])
        try:
            entry[0] += entry[1] * next(entry[2])
        except StopIteration:
            streams.remove(entry)


def _layer_step_body(x_ref, mk_ref, mv_ref, gmix_ref, win_ref, wout_ref, gc_ref, wcq_ref,
                     wco_ref, gf_ref, wg_ref, wu_ref, wd_ref, gfin_ref,
                     cw_ref, cb_ref, dtb_ref, alog_ref, dskip_ref, gn_ref,
                     tril_ref, expand_ref, bucket_ref, relb_ref, sink_ref,
                     y_ref, conv_out, ssm_out, k_out, v_out,
                     xpad, state_t, kbuf, vbuf, bias, mix, hbuf, proj, *, n_chunks, tiles_per_seq):
    step = pl.program_id(0)
    n_tiles = pl.num_programs(0) - 1
    c = lax.rem(jnp.minimum(step, n_tiles - 1), tiles_per_seq)
    last = tiles_per_seq - 1
    tq = n_chunks * CHUNK

    @pl.when(step == 0)
    def _():
        _build_bias(bias, lambda i: bucket_ref[i], relb_ref, 2)
        hbuf[...] = jnp.zeros_like(hbuf)
        mix[...] = jnp.zeros_like(mix)

    @pl.when(c == 0)
    def _():
        xpad[0:8, :] = jnp.zeros((8, CONV_DIM), F32)
        state_t[...] = jnp.zeros_like(state_t)
        kbuf[:, 0:CHUNK, :] = jnp.zeros((2, CHUNK, KV_DIM), BF16)
        vbuf[:, 0:CHUNK, :] = jnp.zeros((2, CHUNK, KV_DIM), BF16)

    def back():
        h1 = hbuf[...] + _dot(mix[...], wout_ref[...])
        yield 4
        qc = _dot(_rms(h1, gc_ref[...]).astype(BF16), wcq_ref[...]).astype(BF16)
        yield 2
        heads = []
        for h in range(CA_HEADS):
            hs = slice(h * CA_HEAD_DIM, (h + 1) * CA_HEAD_DIM)
            sc = _dot_nt(qc[:, hs], mk_ref[0, :, hs]) * (CA_HEAD_DIM ** -0.5)
            m = jnp.max(sc, axis=-1, keepdims=True)
            p = jnp.exp(sc - m)
            heads.append(_dot(p.astype(BF16), mv_ref[0, :, hs]) / jnp.sum(p, axis=-1, keepdims=True))
            yield 3
        o = jnp.concatenate(heads, axis=-1).astype(BF16)
        h2 = h1 + _dot(o, wco_ref[...])
        hn = _rms(h2, gf_ref[...]).astype(BF16)
        yield 4
        acc = h2
        for f_lo, f_hi in FF_PIECES:
            gate = _dot(hn, wg_ref[:, f_lo:f_hi])
            yield 5
            act = (_silu(gate) * _dot(hn, wu_ref[:, f_lo:f_hi])).astype(BF16)
            yield 6
            acc = acc + _dot(act, wd_ref[f_lo:f_hi, :])
            yield 5
        y_ref[0] = _rms(acc, gfin_ref[...])

    def front():
        x = x_ref[0]
        xn = _rms(x, gmix_ref[...]).astype(BF16)
        for p_lo, p_hi in PROJ_PIECES:
            proj[:, p_lo:p_hi] = _dot(xn, win_ref[:, p_lo:p_hi])
            yield 4

        xpad[8:8 + tq, :] = proj[:, COL_XBC:COL_Q]
        conv_halves = []
        half_w = CONV_DIM // 2
        for hf in range(2):
            cs_ = slice(hf * half_w, (hf + 1) * half_w)
            acc = cb_ref[:, cs_] + xpad[5:5 + tq, cs_] * cw_ref[0:1, cs_]
            for kk in range(1, CONV_K):
                acc = acc + xpad[5 + kk:5 + kk + tq, cs_] * cw_ref[kk:kk + 1, cs_]
            conv_halves.append(_silu(acc))
            yield 6
        xpad[5:8, :] = xpad[5 + tq:8 + tq, :]
        xs_all = conv_halves[0]
        bc_all = conv_halves[1]

        k_new = proj[:, COL_K:COL_V]
        v_new = proj[:, COL_V:COL_DT]
        kbuf[0, CHUNK:, :] = k_new.astype(BF16)
        kbuf[1, CHUNK:, :] = pltpu.roll(k_new, HALF, 1).astype(BF16)
        vbuf[0, CHUNK:, :] = v_new.astype(BF16)
        vbuf[1, CHUNK:, :] = pltpu.roll(v_new, HALF, 1).astype(BF16)
        q = (proj[:, COL_Q:COL_K] * (ATTN_HEAD_DIM ** -0.5)).astype(BF16)
        yield 2

        row = lax.broadcasted_iota(jnp.int32, (CHUNK, CHUNK), 0)
        col = lax.broadcasted_iota(jnp.int32, (CHUNK, CHUNK), 1)
        causal = col <= row
        lo = _lane_lo()
        for ci in range(n_chunks):
            rs = slice(ci * CHUNK, (ci + 1) * CHUNK)
            xs = xs_all[rs]
            bm = bc_all[rs, 0:SSD_GROUPS * D_STATE]
            cm = bc_all[rs, SSD_GROUPS * D_STATE:]
            dt = _softplus(proj[rs, COL_DT:IN_COLS] + dtb_ref[...])
            a = dt * (-jnp.exp(alog_ref[...]))
            cs = _sel_left(tril_ref[...], a)
            cs_t = cs.T
            dt_t = dt.T
            total = cs[CHUNK - 1:CHUNK, :]
            ecs = jnp.exp(cs)
            w_end = dt * jnp.exp(total - cs)
            bb = bm.astype(BF16)
            cb = cm.astype(BF16)
            yield 5
            y_parts = []
            for g in range(SSD_GROUPS):
                gs = slice(g * GROUP_W, (g + 1) * GROUP_W)
                ns = slice(g * D_STATE, (g + 1) * D_STATE)
                cb_g = _dot_nt(cb[:, ns], bb[:, ns])
                y_diag, xw, ecs_g = [], [], []
                for pr in range(2):
                    h0 = g * 4 + 2 * pr
                    xp = xs[:, (h0 // 2) * LANES:(h0 // 2 + 1) * LANES]
                    ecs_g.append(jnp.where(lo, ecs[:, h0:h0 + 1], ecs[:, h0 + 1:h0 + 2]))
                    xw.append((xp * jnp.where(lo, w_end[:, h0:h0 + 1], w_end[:, h0 + 1:h0 + 2])).astype(BF16))
                    x2 = jnp.concatenate([jnp.where(lo, xp, 0.0), jnp.where(lo, 0.0, xp)], axis=0).astype(BF16)
                    gmat = []
                    for h in (h0, h0 + 1):
                        diff = cs[:, h:h + 1] - cs_t[h:h + 1, :]
                        decay = jnp.exp(jnp.where(causal, diff, -jnp.inf))
                        gmat.append((cb_g * decay * dt_t[h:h + 1, :]).astype(BF16))
                    y_diag.append(_dot(jnp.concatenate(gmat, axis=1), x2))
                ecs_x = jnp.concatenate(ecs_g, axis=-1)
                st_old = state_t[:, gs]
                y_off = _dot(cb[:, ns], st_old.astype(BF16)) * ecs_x
                y_parts.append(jnp.concatenate(y_diag, axis=-1) + y_off)
                b_t = bm[:, ns].T.astype(BF16)
                state_t[:, gs] = ecs_x[CHUNK - 1:CHUNK, :] * st_old + _dot(b_t, jnp.concatenate(xw, axis=-1))
                yield 9
            y = jnp.concatenate(y_parts, axis=-1) + dskip_ref[...] * xs
            mix[rs, 0:SSD_D_INNER] = _gated_norm(y, proj[rs, COL_Z:COL_XBC], gn_ref).astype(BF16)
            yield 3

            keys = slice(ci * CHUNK, (ci + 2) * CHUNK)
            table = jnp.minimum(c, 1) if ci == 0 else 1
            probs, denoms = {}, {}
            for variant in range(2):
                hv = [h for h in range(N_HEADS) if (h // Q_PER_KV + h % 2) % 2 == variant]
                qms = []
                for h in hv:
                    q_pair = q[rs, (h // 2) * LANES:(h // 2 + 1) * LANES]
                    qms.append(jnp.where(lo, q_pair, 0.0) if h % 2 == 0 else jnp.where(lo, 0.0, q_pair))
                sc_all = _dot_nt(jnp.concatenate(qms, axis=0).astype(BF16), kbuf[variant, keys, :])
                for i, h in enumerate(hv):
                    sc = sc_all[i * CHUNK:(i + 1) * CHUNK] + bias[table, h]
                    sink = sink_ref[h]
                    m = jnp.maximum(jnp.max(sc, axis=-1, keepdims=True), sink)
                    p = jnp.exp(sc - m)
                    denoms[h] = jnp.sum(p, axis=-1, keepdims=True) + jnp.exp(sink - m)
                    probs[h] = p.astype(BF16)
                yield 10
            parts = {}
            for variant in range(2):
                vv = vbuf[variant, keys, :]
                for par in range(2):
                    hv = [h for h in range(N_HEADS) if (h // Q_PER_KV + h % 2) % 2 == variant and h % 2 == par]
                    vm = jnp.where(lo, vv, 0.0) if par == 0 else jnp.where(lo, 0.0, vv)
                    o_all = _dot(jnp.concatenate([probs[h] for h in hv], axis=0), vm.astype(BF16))
                    for i, h in enumerate(hv):
                        parts[h] = o_all[i * CHUNK:(i + 1) * CHUNK] / denoms[h]
            for pair in range(N_HEADS // 2):
                mix[rs, SSD_D_INNER + pair * LANES:SSD_D_INNER + (pair + 1) * LANES] = (
                    parts[2 * pair] + parts[2 * pair + 1]).astype(BF16)
            yield 4
        kbuf[:, 0:CHUNK, :] = kbuf[:, tq:tq + CHUNK, :]
        vbuf[:, 0:CHUNK, :] = vbuf[:, tq:tq + CHUNK, :]

        hbuf[...] = x

    _alternate(back(), front())

    @pl.when((c == last) & (step < n_tiles))
    def _():
        conv_out[0] = xpad[5:8, :]
        ssm_out[0] = state_t[...].T
        k_out[0] = proj[tq - WINDOW:tq, COL_K:COL_V]
        v_out[0] = proj[tq - WINDOW:tq, COL_V:COL_DT]


def _prompt_layer(x3, mk_b, mv_b, dense, small, consts, n_chunks):
    cw, cb, dtb, alog, dskip_x, gn, relb, sinks = small
    tril, expand, buckets = consts
    batch, seq, _ = x3.shape
    tq = n_chunks * CHUNK
    tps = seq // tq
    n_tiles = batch * tps
    front = lambda s: jnp.minimum(s, n_tiles - 1)
    back = lambda s: jnp.maximum(s - 1, 0)
    x_spec = pl.BlockSpec((1, tq, D_MODEL), lambda s: (front(s) // tps, front(s) % tps, 0))
    y_spec = pl.BlockSpec((1, tq, D_MODEL), lambda s: (back(s) // tps, back(s) % tps, 0))
    mem_spec = pl.BlockSpec((1, N_MEM, CA_DIM), lambda s: (back(s) // tps, 0, 0))
    per_b = lambda shp: pl.BlockSpec((1,) + shp, lambda s: (front(s) // tps,) + (0,) * len(shp))
    out_shape = [
        jax.ShapeDtypeStruct((batch, seq, D_MODEL), F32),
        jax.ShapeDtypeStruct((batch, CONV_K - 1, CONV_DIM), F32),
        jax.ShapeDtypeStruct((batch, SSD_D_INNER, D_STATE), F32),
        jax.ShapeDtypeStruct((batch, WINDOW, KV_DIM), F32),
        jax.ShapeDtypeStruct((batch, WINDOW, KV_DIM), F32),
    ]
    full = list(dense) + [cw, cb, dtb, alog, dskip_x, gn, tril, expand, buckets]
    return pl.pallas_call(
        functools.partial(_layer_step_body, n_chunks=n_chunks, tiles_per_seq=tps),
        grid=(n_tiles + 1,),
        in_specs=[x_spec, mem_spec, mem_spec] + [_full_spec(a.shape) for a in full] + [_smem_spec(), _smem_spec()],
        out_specs=[y_spec, per_b((CONV_K - 1, CONV_DIM)), per_b((SSD_D_INNER, D_STATE)),
                   per_b((WINDOW, KV_DIM)), per_b((WINDOW, KV_DIM))],
        out_shape=out_shape,
        scratch_shapes=[
            pltpu.VMEM((8 + tq, CONV_DIM), F32),
            pltpu.VMEM((D_STATE, SSD_D_INNER), F32),
            pltpu.VMEM((2, CHUNK + tq, KV_DIM), BF16),
            pltpu.VMEM((2, CHUNK + tq, KV_DIM), BF16),
            pltpu.VMEM((2, N_HEADS, WINDOW, 2 * WINDOW), F32),
            pltpu.VMEM((tq, D_MODEL), BF16),
            pltpu.VMEM((tq, D_MODEL), F32),
            pltpu.VMEM((tq, IN_COLS), F32),
        ],
        compiler_params=_params(("arbitrary",)),
        name="prompt_layer",
    )(x3, mk_b, mv_b, *full, relb, sinks)


def _sample_mixer_body(z_ref, xbc_ref, q_ref, k_ref, v_ref, dt_ref,
                       sconv_ref, sssm_ref, ck_ref, cv_ref,
                       cw_ref, cb_ref, dtb_ref, alog_ref, dskip_ref, gn_ref,
                       tcum_ref, tseq_ref, expand_ref, bkt_c_ref, bkt_n_ref, relb_ref, sink_ref,
                       mix_ref, conv_out, ssm_out, k_out, v_out,
                       xpad, bias_c, bias_n, *, dec_seq):
    step = pl.program_id(0)
    n_seq = SAMPLE_SEQS
    rows = n_seq * dec_seq

    @pl.when(step == 0)
    def _():
        _build_bias(bias_c, lambda i: bkt_c_ref[...], relb_ref, 1)
        _build_bias(bias_n, lambda i: bkt_n_ref[...], relb_ref, 1)

    xbc = xbc_ref[...]
    xpad[:, 8:8 + dec_seq, :] = xbc.reshape(n_seq, dec_seq, CONV_DIM)
    for r in range(CONV_K - 1):
        xpad[:, 5 + r, :] = sconv_ref[r]
    taps = [xpad[:, 5 + k:5 + k + dec_seq, :].reshape(rows, CONV_DIM) for k in range(CONV_K - 1)] + [xbc]
    conv = _conv_taps(cw_ref, cb_ref, taps)
    for r in range(CONV_K - 1):
        conv_out[r] = xpad[:, 5 + dec_seq + r, :]

    row = lax.broadcasted_iota(jnp.int32, (rows, rows), 0)
    col = lax.broadcasted_iota(jnp.int32, (rows, rows), 1)
    tseq = tseq_ref[...]
    same_seq = tseq > 0
    causal = same_seq & (col <= row)
    xs, bm, cm, cs, cs_t, (dt_x, dend_x, ecs_x, seqdec_x) = _ssd_prepare(
        conv, dt_ref[...], dtb_ref, alog_ref, tcum_ref[...],
        lambda cs_, a_: _sel_left(tseq, a_), lambda total: [jnp.exp(total)], expand_ref[...])
    xdt = xs * dt_x
    xde = (xdt * dend_x).astype(BF16)
    bb = bm.astype(BF16)
    cb = cm.astype(BF16)
    seq_of_row = lax.broadcasted_iota(jnp.int32, (rows, 1), 0) // dec_seq
    seq_of_lane = lax.broadcasted_iota(jnp.int32, (1, rows), 1) // dec_seq
    b_t = [bm[:, g * D_STATE:(g + 1) * D_STATE].T for g in range(SSD_GROUPS)]
    y_off = [None] * SSD_GROUPS
    for i in range(n_seq):
        st_t = sssm_ref[i].T
        new_parts = []
        for g in range(SSD_GROUPS):
            gs = slice(g * GROUP_W, (g + 1) * GROUP_W)
            ns = slice(g * D_STATE, (g + 1) * D_STATE)
            c_i = jnp.where(seq_of_row == i, cm[:, ns], 0.0).astype(BF16)
            part = _dot(c_i, st_t[:, gs].astype(BF16))
            y_off[g] = part if y_off[g] is None else y_off[g] + part
            b_i = jnp.where(seq_of_lane == i, b_t[g], 0.0).astype(BF16)
            dec = seqdec_x[i * dec_seq:i * dec_seq + 1, gs]
            new_parts.append(dec * st_t[:, gs] + _dot(b_i, xde[:, gs]))
        ssm_out[i] = jnp.concatenate(new_parts, axis=-1).T
    y_parts = []
    for g in range(SSD_GROUPS):
        gs = slice(g * GROUP_W, (g + 1) * GROUP_W)
        ns = slice(g * D_STATE, (g + 1) * D_STATE)
        cb_g = _dot_nt(cb[:, ns], bb[:, ns])
        y_parts.append(_ssd_diag(cs, cs_t, cb_g, xdt, causal, g) + y_off[g] * ecs_x[:, gs])
    y = jnp.concatenate(y_parts, axis=-1) + dskip_ref[...] * xs
    mix_ref[:, 0:SSD_D_INNER] = _gated_norm(y, z_ref[...], gn_ref).astype(BF16)

    lo = _lane_lo()
    k_new = k_ref[...]
    v_new = v_ref[...]
    k_var = [k_new.astype(BF16), pltpu.roll(k_new, HALF, 1).astype(BF16)]
    v_new_r = pltpu.roll(v_new, HALF, 1)
    v_dup = [jnp.where(lo, v_new, v_new_r).astype(BF16), jnp.where(lo, v_new_r, v_new).astype(BF16)]
    qf = q_ref[...].astype(F32)
    q_masked = []
    s_new = []
    for h in range(N_HEADS):
        pair, par = h // 2, h % 2
        j = h // Q_PER_KV
        q_pair = qf[:, pair * LANES:(pair + 1) * LANES]
        qm = jnp.where(lo, q_pair, 0.0) if par == 0 else jnp.where(lo, 0.0, q_pair)
        q_masked.append(qm)
        s_new.append(_dot_nt(qm.astype(BF16), k_var[(j + par) % 2]) + bias_n[0, h])
    stack_rows = lax.broadcasted_iota(jnp.int32, (Q_PER_KV * dec_seq, 1), 0) // dec_seq
    k_new_t = k_new.T
    v_new_t = v_new.T
    keep = WINDOW - dec_seq
    old_lane = lax.broadcasted_iota(jnp.int32, (1, WINDOW), 1) < keep
    att_rows = []
    for i in range(n_seq):
        rs = slice(i * dec_seq, (i + 1) * dec_seq)
        kc_t = ck_ref[i]
        vc_t = cv_ref[i]
        pieces = []
        for j in range(N_KV_HEADS):
            heads = range(j * Q_PER_KV, (j + 1) * Q_PER_KV)
            cj = slice(j * ATTN_HEAD_DIM, (j + 1) * ATTN_HEAD_DIM)
            kdup_t = jnp.concatenate([kc_t[cj], kc_t[cj]], axis=0).astype(BF16)
            vdup_t = jnp.concatenate([vc_t[cj], vc_t[cj]], axis=0).astype(BF16)
            qs = jnp.concatenate([q_masked[h][rs] for h in heads], axis=0).astype(BF16)
            sc = _dot(qs, kdup_t) + jnp.concatenate([bias_c[0, h] for h in heads], axis=0)
            sn = jnp.concatenate([s_new[h][rs] for h in heads], axis=0)
            sink = jnp.zeros((Q_PER_KV * dec_seq, 1), F32)
            for hh, h in enumerate(heads):
                sink = jnp.where(stack_rows == hh, sink_ref[h], sink)
            m = jnp.maximum(jnp.maximum(jnp.max(sc, axis=-1, keepdims=True),
                                        jnp.max(sn, axis=-1, keepdims=True)), sink)
            pc = jnp.exp(sc - m)
            pn = jnp.exp(sn - m)
            denom = (jnp.sum(pc, axis=-1, keepdims=True) + jnp.sum(pn, axis=-1, keepdims=True)
                     + jnp.exp(sink - m))
            o = (_dot_nt(pc.astype(BF16), vdup_t) + _dot(pn.astype(BF16), v_dup[j])) / denom
            for pr in range(Q_PER_KV // 2):
                even = o[(2 * pr) * dec_seq:(2 * pr + 1) * dec_seq]
                odd = o[(2 * pr + 1) * dec_seq:(2 * pr + 2) * dec_seq]
                pieces.append(jnp.where(lo, even, odd))
        att_rows.append(jnp.concatenate(pieces, axis=-1))
        new_shift = (keep - i * dec_seq) % WINDOW
        k_out[i] = jnp.where(old_lane, pltpu.roll(kc_t, keep, 1), pltpu.roll(k_new_t, new_shift, 1))
        v_out[i] = jnp.where(old_lane, pltpu.roll(vc_t, keep, 1), pltpu.roll(v_new_t, new_shift, 1))
    mix_ref[:, SSD_D_INNER:] = jnp.concatenate(att_rows, axis=0).astype(BF16)


def _sample_mixer(z, xbc, q, k, v, dt, sconv, sssm, ck, cv, small, consts, n_seq_total, dec_seq):
    cw, cb, dtb, alog, dskip_x, gn, relb, sinks = small
    tcum, tseq, expand, bkt_c, bkt_n = consts
    rows = SAMPLE_SEQS * dec_seq
    tok = lambda w: pl.BlockSpec((rows, w), lambda i: (i, 0))
    per_s = lambda s: pl.BlockSpec((SAMPLE_SEQS,) + s, lambda i: (i,) + (0,) * len(s))
    out_shape = [
        jax.ShapeDtypeStruct((n_seq_total * dec_seq, D_MODEL), BF16),
        jax.ShapeDtypeStruct((CONV_K - 1, n_seq_total, CONV_DIM), F32),
        jax.ShapeDtypeStruct((n_seq_total, SSD_D_INNER, D_STATE), F32),
        jax.ShapeDtypeStruct((n_seq_total, WINDOW, KV_DIM), F32),
        jax.ShapeDtypeStruct((n_seq_total, WINDOW, KV_DIM), F32),
    ]
    conv_spec = pl.BlockSpec((CONV_K - 1, SAMPLE_SEQS, CONV_DIM), lambda i: (0, i, 0))
    return pl.pallas_call(
        functools.partial(_sample_mixer_body, dec_seq=dec_seq),
        grid=(n_seq_total // SAMPLE_SEQS,),
        in_specs=[tok(SSD_D_INNER), tok(CONV_DIM), tok(ATTN_DIM), tok(KV_DIM), tok(KV_DIM), tok(DT_PAD),
                  conv_spec, per_s((SSD_D_INNER, D_STATE)),
                  per_s((WINDOW, KV_DIM)), per_s((WINDOW, KV_DIM)),
                  _full_spec(cw.shape), _full_spec(cb.shape), _full_spec(dtb.shape), _full_spec(alog.shape),
                  _full_spec(dskip_x.shape), _full_spec(gn.shape),
                  _full_spec(tcum.shape), _full_spec(tseq.shape), _full_spec(expand.shape),
                  _full_spec(bkt_c.shape), _full_spec(bkt_n.shape), _smem_spec(), _smem_spec()],
        out_specs=[tok(D_MODEL), conv_spec, per_s((SSD_D_INNER, D_STATE)),
                   per_s((WINDOW, KV_DIM)), per_s((WINDOW, KV_DIM))],
        out_shape=out_shape,
        scratch_shapes=[
            pltpu.VMEM((SAMPLE_SEQS, 8 + dec_seq, CONV_DIM), F32),
            pltpu.VMEM((1, N_HEADS, dec_seq, WINDOW), F32),
            pltpu.VMEM((1, N_HEADS, rows, rows), F32),
        ],
        compiler_params=_params(("arbitrary",)),
        name="sample_mixer",
    )(z, xbc, q, k, v, dt, sconv, sssm, ck, cv, cw, cb, dtb, alog, dskip_x, gn,
      tcum, tseq, expand, bkt_c, bkt_n, relb, sinks)


def _post1_body(x_ref, mix_ref, wout_ref, gc_ref, wcq_ref, h_ref, qc_ref):
    h = x_ref[...] + _dot(mix_ref[...], wout_ref[...])
    h_ref[...] = h
    qc_ref[...] = _dot(_rms(h, gc_ref[...]).astype(BF16), wcq_ref[...]).astype(BF16)


def _post1(x2, mix, w_out, g_cross, w_cq, tm):
    t = x2.shape[0]
    row = lambda w: pl.BlockSpec((tm, w), lambda i: (i, 0))
    return pl.pallas_call(
        _post1_body,
        grid=(t // tm,),
        in_specs=[row(D_MODEL), row(D_MODEL), _full_spec(w_out.shape), _full_spec(g_cross.shape),
                  _full_spec(w_cq.shape)],
        out_specs=[row(D_MODEL), row(CA_DIM)],
        out_shape=[jax.ShapeDtypeStruct((t, D_MODEL), F32), jax.ShapeDtypeStruct((t, CA_DIM), BF16)],
        compiler_params=_params(("parallel",)),
        name="out_proj",
    )(x2, mix, w_out, g_cross, w_cq)


def _mem_kv_body(mem_ref, g_ref, wk_ref, wv_ref, k_ref, v_ref, kb_ref, vb_ref):
    mn = _rms(mem_ref[...], g_ref[...]).astype(BF16)
    k = _dot(mn, wk_ref[...])
    v = _dot(mn, wv_ref[...])
    rows = k.shape[0]
    for h in range(CA_HEADS):
        hs = slice(h * CA_HEAD_DIM, (h + 1) * CA_HEAD_DIM)
        k_ref[pl.ds(h, rows, stride=CA_HEADS), :] = k[:, hs]
        v_ref[pl.ds(h, rows, stride=CA_HEADS), :] = v[:, hs]
    kb_ref[...] = k.astype(BF16)
    vb_ref[...] = v.astype(BF16)


def _mem_kv(mem2, g_mem, w_ck, w_cv, tm):
    t = mem2.shape[0]
    row = lambda w: pl.BlockSpec((tm, w), lambda i: (i, 0))
    return pl.pallas_call(
        _mem_kv_body,
        grid=(t // tm,),
        in_specs=[row(D_MODEL), _full_spec(g_mem.shape), _full_spec(w_ck.shape), _full_spec(w_cv.shape)],
        out_specs=[pl.BlockSpec((tm * CA_HEADS, CA_HEAD_DIM), lambda i: (i, 0))] * 2 + [row(CA_DIM)] * 2,
        out_shape=([jax.ShapeDtypeStruct((t * CA_HEADS, CA_HEAD_DIM), F32)] * 2
                   + [jax.ShapeDtypeStruct((t, CA_DIM), BF16)] * 2),
        compiler_params=_params(("parallel",)),
        name="mem_kv",
    )(mem2, g_mem, w_ck, w_cv)


def _cross_heads(q, k, v):
    out = []
    for h in range(CA_HEADS):
        hs = slice(h * CA_HEAD_DIM, (h + 1) * CA_HEAD_DIM)
        s = _dot_nt(q[:, hs], k[:, hs]) * (CA_HEAD_DIM ** -0.5)
        m = jnp.max(s, axis=-1, keepdims=True)
        p = jnp.exp(s - m)
        out.append(_dot(p.astype(BF16), v[:, hs]) / jnp.sum(p, axis=-1, keepdims=True))
    return jnp.concatenate(out, axis=-1)


def _cross_prompt_body(q_ref, k_ref, v_ref, o_ref):
    o_ref[0] = _cross_heads(q_ref[0], k_ref[0].astype(BF16), v_ref[0].astype(BF16)).astype(BF16)


def _cross_prompt(qc, mk, mv, batch, seq, tm):
    q3 = qc.reshape(batch, seq, CA_DIM)
    tok = pl.BlockSpec((1, tm, CA_DIM), lambda b, i: (b, i, 0))
    mem = pl.BlockSpec((1, N_MEM, CA_DIM), lambda b, i: (b, 0, 0))
    return pl.pallas_call(
        _cross_prompt_body,
        grid=(batch, seq // tm),
        in_specs=[tok, mem, mem],
        out_specs=tok,
        out_shape=jax.ShapeDtypeStruct((batch, seq, CA_DIM), BF16),
        compiler_params=_params(("parallel", "parallel")),
        name="cross_prompt",
    )(q3, mk, mv).reshape(batch * seq, CA_DIM)


def _cross_sample_body(q_ref, k_ref, v_ref, o_ref, *, n_seq, dec_seq):
    q = q_ref[...].astype(F32)
    n_keys = N_MEM * CA_HEADS
    col_head = lax.broadcasted_iota(jnp.int32, (1, n_keys), 1) & (CA_HEADS - 1)
    row_head = lax.broadcasted_iota(jnp.int32, (CA_HEADS * dec_seq, 1), 0) // dec_seq
    own = col_head == row_head
    rows = []
    for i in range(n_seq):
        qi = q[i * dec_seq:(i + 1) * dec_seq]
        qs = jnp.concatenate([qi[:, h * CA_HEAD_DIM:(h + 1) * CA_HEAD_DIM] for h in range(CA_HEADS)], axis=0)
        s = _dot_nt(qs.astype(BF16), k_ref[i].astype(BF16)) * (CA_HEAD_DIM ** -0.5)
        s = jnp.where(own, s, NEG)
        m = jnp.max(s, axis=-1, keepdims=True)
        p = jnp.exp(s - m)
        o = _dot(p.astype(BF16), v_ref[i].astype(BF16)) / jnp.sum(p, axis=-1, keepdims=True)
        rows.append(jnp.concatenate([o[h * dec_seq:(h + 1) * dec_seq] for h in range(CA_HEADS)], axis=-1))
    o_ref[...] = jnp.concatenate(rows, axis=0).astype(BF16)


def _cross_sample(qc, ck, cv, n_seq_total, dec_seq, n_seq):
    rows = n_seq * dec_seq
    tok = pl.BlockSpec((rows, CA_DIM), lambda i: (i, 0))
    mem = pl.BlockSpec((n_seq, N_MEM * CA_HEADS, CA_HEAD_DIM), lambda i: (i, 0, 0))
    return pl.pallas_call(
        functools.partial(_cross_sample_body, n_seq=n_seq, dec_seq=dec_seq),
        grid=(n_seq_total // n_seq,),
        in_specs=[tok, mem, mem],
        out_specs=tok,
        out_shape=jax.ShapeDtypeStruct((n_seq_total * dec_seq, CA_DIM), BF16),
        compiler_params=_params(("parallel",)),
        name="cross_sample",
    )(qc, ck, cv)


def _post2_body(h_ref, o_ref, wco_ref, gf_ref, wg_ref, wu_ref, wd_ref, gfin_ref, y_ref):
    h = h_ref[...] + _dot(o_ref[...], wco_ref[...])
    hn = _rms(h, gf_ref[...]).astype(BF16)
    acc = h
    for lo, hi in FF_SPLITS:
        act = _silu(_dot(hn, wg_ref[:, lo:hi])) * _dot(hn, wu_ref[:, lo:hi])
        acc = acc + _dot(act.astype(BF16), wd_ref[lo:hi, :])
    y_ref[...] = _rms(acc, gfin_ref[...])


def _post2(h1, o, w_co, g_ffn, w_gate, w_up, w_down, g_final, tm):
    t = h1.shape[0]
    row = lambda w: pl.BlockSpec((tm, w), lambda i: (i, 0))
    return pl.pallas_call(
        _post2_body,
        grid=(t // tm,),
        in_specs=[row(D_MODEL), row(CA_DIM), _full_spec(w_co.shape), _full_spec(g_ffn.shape),
                  _full_spec(w_gate.shape), _full_spec(w_up.shape), _full_spec(w_down.shape),
                  _full_spec(g_final.shape)],
        out_specs=row(D_MODEL),
        out_shape=jax.ShapeDtypeStruct((t, D_MODEL), F32),
        compiler_params=_params(("parallel",)),
        name="ffn",
    )(h1, o, w_co, g_ffn, w_gate, w_up, w_down, g_final)


def _prompt_consts():
    qi = np.arange(WINDOW)[:, None]
    ji = np.arange(2 * WINDOW)[None, :]
    dist = qi + WINDOW - ji
    inband = (dist >= 0) & (dist < WINDOW)
    bucket = np.where(inband, _t5_bucket_np(dist), -1)
    first = np.where(ji >= WINDOW, bucket, -1)
    buckets = np.stack([first, bucket]).astype(np.int32)
    return (jnp.asarray(_tril_np(CHUNK), BF16), jnp.asarray(_expand_np(), BF16), jnp.asarray(buckets))


def _sample_consts(dec_seq, cache_len):
    rows = SAMPLE_SEQS * dec_seq
    r = np.arange(rows)
    same = (r[:, None] // dec_seq) == (r[None, :] // dec_seq)
    tcum = (same & (r[None, :] <= r[:, None])).astype(np.float32)
    tseq = same.astype(np.float32)
    t = np.arange(dec_seq)[:, None]
    j = np.arange(cache_len)[None, :]
    dist_c = t + cache_len - j
    bkt_c = np.where((dist_c >= 0) & (dist_c < WINDOW), _t5_bucket_np(dist_c), -1).astype(np.int32)
    dist_n = (r[:, None] % dec_seq) - (r[None, :] % dec_seq)
    ok = same & (dist_n >= 0) & (dist_n < WINDOW)
    bkt_n = np.where(ok, _t5_bucket_np(dist_n), -1).astype(np.int32)
    return (jnp.asarray(tcum, BF16), jnp.asarray(tseq, BF16), jnp.asarray(_expand_np(), BF16),
            jnp.asarray(bkt_c), jnp.asarray(bkt_n))


def _pick_tile(t, pref):
    tm = min(t, pref)
    while t % tm:
        tm //= 2
    return tm


def kernel(x_prompt, x_sample, mem_prompt, state_conv, state_ssm, cache_swa_k, cache_swa_v, cache_mem_k, cache_mem_v, rel_bias, g_mix, w_in, conv_w, conv_b, dt_bias, a_log, d_skip, g_ssd_norm, sinks, w_out, g_cross, g_mem, w_cq, w_ck, w_cv, w_co, g_ffn, w_gate, w_up, w_down, g_final):
    assert g_mix.shape[0] == 1, "single-layer trunk"
    batch, seq, _ = x_prompt.shape
    n_dec, dec_seq, _ = x_sample.shape
    cache_len = cache_swa_k.shape[2]
    assert seq % CHUNK == 0 and cache_len == WINDOW and n_dec % SAMPLE_SEQS == 0 and dec_seq == 8

    wi = w_in[0]
    s0, s1, s2, s3, s4 = (SSD_D_INNER, SSD_D_INNER + CONV_DIM, SSD_D_INNER + CONV_DIM + SSD_HEADS,
                          SSD_D_INNER + CONV_DIM + SSD_HEADS + ATTN_DIM,
                          SSD_D_INNER + CONV_DIM + SSD_HEADS + ATTN_DIM + KV_DIM)
    w_in_r = jnp.concatenate(
        [wi[:, :s0], wi[:, s0:s1], wi[:, s2:s3], wi[:, s3:s4], wi[:, s4:],
         jnp.pad(wi[:, s1:s2], ((0, 0), (0, DT_PAD - SSD_HEADS)))], axis=1).astype(BF16)
    row = lambda a: a.reshape(1, -1).astype(F32)
    pad_h = lambda a: jnp.pad(a.reshape(1, -1).astype(F32), ((0, 0), (0, DT_PAD - SSD_HEADS)))
    small = (conv_w[0].astype(F32), row(conv_b[0]), pad_h(dt_bias[0]), pad_h(a_log[0]),
             jnp.repeat(d_skip[0].astype(F32), SSD_HEAD_DIM).reshape(1, -1), row(g_ssd_norm[0]),
             rel_bias.astype(F32), sinks[0].astype(F32))
    bf = lambda w: w[0].astype(BF16)
    w_out_b, w_cq_b, w_ck_b, w_cv_b, w_co_b = bf(w_out), bf(w_cq), bf(w_ck), bf(w_cv), bf(w_co)
    w_gate_b, w_up_b, w_down_b = bf(w_gate), bf(w_up), bf(w_down)
    g_mix_r, g_cross_r, g_mem_r, g_ffn_r, g_fin_r = row(g_mix[0]), row(g_cross[0]), row(g_mem[0]), row(g_ffn[0]), row(g_final)

    tp = batch * seq
    n_chunks = 2 if seq % (2 * CHUNK) == 0 else 1
    mem2 = mem_prompt.reshape(batch * N_MEM, D_MODEL)
    mk, mv, mk_b, mv_b = _mem_kv(mem2, g_mem_r, w_ck_b, w_cv_b, _pick_tile(batch * N_MEM, 512))
    dense = (g_mix_r, w_in_r, w_out_b, g_cross_r, w_cq_b, w_co_b, g_ffn_r, w_gate_b, w_up_b, w_down_b, g_fin_r)
    y_prompt, p_conv, p_ssm, p_k, p_v = _prompt_layer(
        x_prompt, mk_b.reshape(batch, N_MEM, CA_DIM), mv_b.reshape(batch, N_MEM, CA_DIM),
        dense, small, _prompt_consts(), n_chunks)

    def channel_major(cache):
        return jnp.transpose(cache, (0, 2, 3, 1)).reshape(n_dec, KV_DIM, cache_len)

    def position_major(cache_t):
        return jnp.transpose(cache_t.reshape(n_dec, N_KV_HEADS, ATTN_HEAD_DIM, cache_len), (0, 3, 1, 2))

    ts = n_dec * dec_seq
    xs2 = x_sample.reshape(ts, D_MODEL)
    tm_s = _pick_tile(ts, 512)
    z, xbc, q, k, v, dt = _in_proj(xs2, g_mix_r, w_in_r, tm_s)
    mix_s, s_conv, s_ssm, s_k, s_v = _sample_mixer(
        z, xbc, q, k, v, dt, jnp.transpose(state_conv[0], (1, 0, 2)),
        state_ssm[0].reshape(n_dec, SSD_D_INNER, D_STATE),
        channel_major(cache_swa_k[0]), channel_major(cache_swa_v[0]),
        small, _sample_consts(dec_seq, cache_len), n_dec, dec_seq)
    h1s, qcs = _post1(xs2, mix_s, w_out_b, g_cross_r, w_cq_b, tm_s)
    os_ = _cross_sample(qcs, cache_mem_k[0].reshape(n_dec, N_MEM * CA_HEADS, CA_HEAD_DIM),
                        cache_mem_v[0].reshape(n_dec, N_MEM * CA_HEADS, CA_HEAD_DIM), n_dec, dec_seq, 8)
    y_sample = _post2(h1s, os_, w_co_b, g_ffn_r, w_gate_b, w_up_b, w_down_b, g_fin_r, tm_s)

    return (y_prompt.reshape(batch, seq, D_MODEL), y_sample.reshape(n_dec, dec_seq, D_MODEL),
            p_conv[None], p_ssm.reshape(1, batch, SSD_HEADS, SSD_HEAD_DIM, D_STATE),
            p_k.reshape(1, batch, WINDOW, N_KV_HEADS, ATTN_HEAD_DIM),
            p_v.reshape(1, batch, WINDOW, N_KV_HEADS, ATTN_HEAD_DIM),
            mk.reshape(1, batch, N_MEM, CA_HEADS, CA_HEAD_DIM), mv.reshape(1, batch, N_MEM, CA_HEADS, CA_HEAD_DIM),
            jnp.transpose(s_conv, (1, 0, 2))[None], s_ssm.reshape(1, n_dec, SSD_HEADS, SSD_HEAD_DIM, D_STATE),
            position_major(s_k)[None], position_major(s_v)[None])
```

```python
import functools
import math

import numpy as np
import jax
import jax.numpy as jnp
from jax import lax
from jax.experimental import pallas as pl
from jax.experimental.pallas import tpu as pltpu

F32 = jnp.float32
BF16 = jnp.bfloat16

D_MODEL = 1024
SSD_D_INNER = 512
SSD_HEAD_DIM = 64
SSD_HEADS = 8
SSD_GROUPS = 2
GROUP_W = SSD_D_INNER // SSD_GROUPS
D_STATE = 128
CONV_K = 4
CONV_DIM = SSD_D_INNER + 2 * SSD_GROUPS * D_STATE
CHUNK = 128
ATTN_DIM = 512
ATTN_HEAD_DIM = 64
N_HEADS = 8
N_KV_HEADS = 2
Q_PER_KV = N_HEADS // N_KV_HEADS
KV_DIM = N_KV_HEADS * ATTN_HEAD_DIM
WINDOW = 128
N_BUCKETS = 32
MAX_EXACT = N_BUCKETS // 2
MAX_DISTANCE = 128
N_MEM = 256
CA_HEADS = 4
CA_HEAD_DIM = 128
CA_DIM = CA_HEADS * CA_HEAD_DIM
D_FF = 2816
EPS = 1e-6

LANES = 128
HALF = LANES // 2
DT_PAD = LANES
COL_Z = 0
COL_XBC = COL_Z + SSD_D_INNER
COL_Q = COL_XBC + CONV_DIM
COL_K = COL_Q + ATTN_DIM
COL_V = COL_K + KV_DIM
COL_DT = COL_V + KV_DIM
IN_COLS = COL_DT + DT_PAD
NEG = -1e30
SAMPLE_SEQS = 16
VMEM_LIMIT = 56 * 1024 * 1024
FF_SPLITS = ((0, 1024), (1024, 2048), (2048, D_FF))
SECOND_STREAM_LEAD = 0.5
FF_PIECES = tuple((lo, min(lo + 512, D_FF)) for lo in range(0, D_FF, 512))
PROJ_PIECES = ((COL_Z, COL_XBC), (COL_XBC, COL_XBC + 512), (COL_XBC + 512, COL_Q), (COL_Q, COL_K), (COL_K, IN_COLS))


def _rms(x, g):
    return x * lax.rsqrt(jnp.mean(x * x, axis=-1, keepdims=True) + EPS) * g


def _silu(x):
    return x * jax.nn.sigmoid(x)


def _softplus(x):
    return jnp.maximum(x, 0.0) + jnp.log1p(jnp.exp(-jnp.abs(x)))


def _dot(a, b):
    return jnp.dot(a, b, preferred_element_type=F32)


def _dot_nt(a, b):
    return lax.dot_general(a, b, (((1,), (1,)), ((), ())), preferred_element_type=F32)


def _split3(a):
    hi = a.astype(BF16)
    r = a - hi.astype(F32)
    mid = r.astype(BF16)
    lo = (r - mid.astype(F32)).astype(BF16)
    return hi, mid, lo


def _sel_left(t01, a):
    hi, mid, lo = _split3(a)
    return _dot(t01, hi) + _dot(t01, mid) + _dot(t01, lo)


def _sel_right(a, e01):
    hi, mid, lo = _split3(a)
    return _dot(hi, e01) + _dot(mid, e01) + _dot(lo, e01)


def _lane_lo():
    return lax.broadcasted_iota(jnp.int32, (1, LANES), 1) < HALF


def _t5_bucket_np(dist):
    n = np.maximum(dist, 0)
    ratio = np.log(np.maximum(n, 1).astype(np.float32) / np.float32(MAX_EXACT))
    large = MAX_EXACT + (ratio / np.float32(math.log(MAX_DISTANCE / MAX_EXACT))
                         * np.float32(N_BUCKETS - MAX_EXACT)).astype(np.int32)
    large = np.minimum(large, N_BUCKETS - 1)
    return np.where(n < MAX_EXACT, n, large).astype(np.int32)


def _tril_np(n):
    return np.tril(np.ones((n, n), np.float32))


def _expand_np():
    e = np.zeros((LANES, SSD_D_INNER), np.float32)
    for h in range(SSD_HEADS):
        e[h, h * SSD_HEAD_DIM:(h + 1) * SSD_HEAD_DIM] = 1.0
    return e


def _full_spec(shape):
    nd = len(shape)
    return pl.BlockSpec(shape, lambda *_: (0,) * nd, pipeline_mode=pl.Buffered(1))


def _smem_spec():
    return pl.BlockSpec(memory_space=pltpu.SMEM)


def _params(sem):
    return pltpu.CompilerParams(dimension_semantics=sem, vmem_limit_bytes=VMEM_LIMIT)


def _in_proj_body(x_ref, g_ref, w_ref, z_ref, xbc_ref, q_ref, k_ref, v_ref, dt_ref):
    xn = _rms(x_ref[...], g_ref[...]).astype(BF16)

    def seg(lo, hi):
        return _dot(xn, w_ref[:, lo:hi])

    z_ref[...] = seg(COL_Z, COL_XBC)
    xbc_ref[...] = seg(COL_XBC, COL_Q)
    q_ref[...] = (seg(COL_Q, COL_K) * (ATTN_HEAD_DIM ** -0.5)).astype(BF16)
    k_ref[...] = seg(COL_K, COL_V)
    v_ref[...] = seg(COL_V, COL_DT)
    dt_ref[...] = seg(COL_DT, IN_COLS)


def _in_proj(x2, g_mix, w_in_r, tm):
    t = x2.shape[0]
    row = lambda w: pl.BlockSpec((tm, w), lambda i: (i, 0))
    outs = [(SSD_D_INNER, F32), (CONV_DIM, F32), (ATTN_DIM, BF16), (KV_DIM, F32), (KV_DIM, F32), (DT_PAD, F32)]
    return pl.pallas_call(
        _in_proj_body,
        grid=(t // tm,),
        in_specs=[row(D_MODEL), _full_spec((1, D_MODEL)), _full_spec((D_MODEL, IN_COLS))],
        out_specs=[row(w) for w, _ in outs],
        out_shape=[jax.ShapeDtypeStruct((t, w), d) for w, d in outs],
        compiler_params=_params(("parallel",)),
        name="in_proj",
    )(x2, g_mix, w_in_r)


def _conv_taps(cw_ref, cb_ref, taps):
    acc = cb_ref[...] + taps[0] * cw_ref[0:1, :]
    for k in range(1, CONV_K):
        acc = acc + taps[k] * cw_ref[k:k + 1, :]
    return _silu(acc)


def _ssd_prepare(conv, dt_raw, dtb_ref, alog_ref, tcum, total_fn, extra_fn, expand):
    xs = conv[:, :SSD_D_INNER]
    bm = conv[:, SSD_D_INNER:SSD_D_INNER + SSD_GROUPS * D_STATE]
    cm = conv[:, SSD_D_INNER + SSD_GROUPS * D_STATE:]
    dt = _softplus(dt_raw + dtb_ref[...])
    a = dt * (-jnp.exp(alog_ref[...]))
    cs = _sel_left(tcum, a)
    total = total_fn(cs, a)
    pieces = [dt, jnp.exp(total - cs), jnp.exp(cs)] + extra_fn(total)
    rows = cs.shape[0]
    ex = _sel_right(jnp.concatenate(pieces, axis=0), expand)
    ex = [ex[i * rows:(i + 1) * rows] for i in range(len(pieces))]
    return xs, bm, cm, cs, cs.T, ex


def _ssd_diag(cs, cs_t, cb_g, xdt, mask, g):
    lo = _lane_lo()
    out = []
    for pr in range(2):
        h0 = g * 4 + 2 * pr
        xp = xdt[:, (h0 // 2) * LANES:(h0 // 2 + 1) * LANES]
        x_lo = jnp.where(lo, xp, 0.0).astype(BF16)
        x_hi = jnp.where(lo, 0.0, xp).astype(BF16)
        acc = None
        for h, xh in ((h0, x_lo), (h0 + 1, x_hi)):
            diff = cs[:, h:h + 1] - cs_t[h:h + 1, :]
            decay = jnp.exp(jnp.where(mask, diff, -jnp.inf))
            part = _dot((cb_g * decay).astype(BF16), xh)
            acc = part if acc is None else acc + part
        out.append(acc)
    return jnp.concatenate(out, axis=-1)


def _gated_norm(y, z, gn_ref):
    yf = y * _silu(z)
    parts = []
    for g in range(SSD_GROUPS):
        yg = yf[:, g * GROUP_W:(g + 1) * GROUP_W]
        parts.append(yg * lax.rsqrt(jnp.mean(yg * yg, axis=-1, keepdims=True) + EPS))
    return jnp.concatenate(parts, axis=-1) * gn_ref[...]


def _build_bias(bias_ref, bucket_of, relb_ref, n_tables):
    for i in range(n_tables):
        for h in range(N_HEADS):
            bias_ref[i, h] = jnp.full(bias_ref.shape[2:], NEG, F32)

    def body(t, carry):
        for i in range(n_tables):
            hit = bucket_of(i) == t
            for h in range(N_HEADS):
                bias_ref[i, h] = jnp.where(hit, relb_ref[t, h], bias_ref[i, h])
        return carry

    lax.fori_loop(0, N_BUCKETS, body, 0)


def _alternate(first, second):
    streams = [[0.0, 1.0, first], [0.0, SECOND_STREAM_LEAD, second]]
    while streams:
        entry = min(streams, key=lambda e: e[0])
        try:
            entry[0] += entry[1] * next(entry[2])
        except StopIteration:
            streams.remove(entry)


def _layer_step_body(x_ref, mk_ref, mv_ref, gmix_ref, win_ref, wout_ref, gc_ref, wcq_ref,
                     wco_ref, gf_ref, wg_ref, wu_ref, wd_ref, gfin_ref,
                     cw_ref, cb_ref, dtb_ref, alog_ref, dskip_ref, gn_ref,
                     tril_ref, bucket_ref, relb_ref, sink_ref,
                     y_ref, conv_out, ssm_out, k_out, v_out,
                     xpad, state_t, kbuf, vbuf, bias, mix, hbuf, proj, *, n_chunks, tiles_per_seq):
    step = pl.program_id(0)
    n_tiles = pl.num_programs(0) - 1
    c = lax.rem(jnp.minimum(step, n_tiles - 1), tiles_per_seq)
    last = tiles_per_seq - 1
    tq = n_chunks * CHUNK

    @pl.when(step == 0)
    def _():
        _build_bias(bias, lambda i: bucket_ref[i], relb_ref, 2)
        hbuf[...] = jnp.zeros_like(hbuf)
        mix[...] = jnp.zeros_like(mix)

    @pl.when(c == 0)
    def _():
        xpad[0:8, :] = jnp.zeros((8, CONV_DIM), F32)
        state_t[...] = jnp.zeros_like(state_t)
        kbuf[:, 0:CHUNK, :] = jnp.zeros((2, CHUNK, KV_DIM), BF16)
        vbuf[:, 0:CHUNK, :] = jnp.zeros((2, CHUNK, KV_DIM), BF16)

    def back():
        h1 = hbuf[...] + _dot(mix[...], wout_ref[...])
        yield 4
        qc = _dot(_rms(h1, gc_ref[...]).astype(BF16), wcq_ref[...]).astype(BF16)
        yield 2
        heads = []
        for h in range(CA_HEADS):
            hs = slice(h * CA_HEAD_DIM, (h + 1) * CA_HEAD_DIM)
            sc = _dot_nt(qc[:, hs], mk_ref[0, :, hs]) * (CA_HEAD_DIM ** -0.5)
            m = jnp.max(sc, axis=-1, keepdims=True)
            p = jnp.exp(sc - m)
            heads.append(_dot(p.astype(BF16), mv_ref[0, :, hs]) / jnp.sum(p, axis=-1, keepdims=True))
            yield 3
        o = jnp.concatenate(heads, axis=-1).astype(BF16)
        h2 = h1 + _dot(o, wco_ref[...])
        hn = _rms(h2, gf_ref[...]).astype(BF16)
        yield 4
        acc = h2
        for f_lo, f_hi in FF_PIECES:
            width = (f_hi - f_lo) / 512
            gate = _dot(hn, wg_ref[:, f_lo:f_hi])
            yield 5 * width
            act = (_silu(gate) * _dot(hn, wu_ref[:, f_lo:f_hi])).astype(BF16)
            yield 6 * width
            acc = acc + _dot(act, wd_ref[f_lo:f_hi, :])
            yield 5 * width
        y_ref[0] = _rms(acc, gfin_ref[...])

    def front():
        x = x_ref[0]
        xn = _rms(x, gmix_ref[...]).astype(BF16)
        for p_lo, p_hi in PROJ_PIECES:
            proj[:, p_lo:p_hi] = _dot(xn, win_ref[:, p_lo:p_hi])
            yield 4

        xpad[8:8 + tq, :] = proj[:, COL_XBC:COL_Q]
        conv_halves = []
        half_w = CONV_DIM // 2
        for hf in range(2):
            cs_ = slice(hf * half_w, (hf + 1) * half_w)
            acc = cb_ref[:, cs_] + xpad[5:5 + tq, cs_] * cw_ref[0:1, cs_]
            for kk in range(1, CONV_K):
                acc = acc + xpad[5 + kk:5 + kk + tq, cs_] * cw_ref[kk:kk + 1, cs_]
            conv_halves.append(_silu(acc))
            yield 6
        xpad[5:8, :] = xpad[5 + tq:8 + tq, :]
        xs_all = conv_halves[0]
        bc_all = conv_halves[1]

        k_new = proj[:, COL_K:COL_V]
        v_new = proj[:, COL_V:COL_DT]
        kbuf[0, CHUNK:, :] = k_new.astype(BF16)
        kbuf[1, CHUNK:, :] = pltpu.roll(k_new, HALF, 1).astype(BF16)
        vbuf[0, CHUNK:, :] = v_new.astype(BF16)
        vbuf[1, CHUNK:, :] = pltpu.roll(v_new, HALF, 1).astype(BF16)
        q = (proj[:, COL_Q:COL_K] * (ATTN_HEAD_DIM ** -0.5)).astype(BF16)
        yield 2

        row = lax.broadcasted_iota(jnp.int32, (CHUNK, CHUNK), 0)
        col = lax.broadcasted_iota(jnp.int32, (CHUNK, CHUNK), 1)
        causal = col <= row
        lo = _lane_lo()
        for ci in range(n_chunks):
            rs = slice(ci * CHUNK, (ci + 1) * CHUNK)
            xs = xs_all[rs]
            bm = bc_all[rs, 0:SSD_GROUPS * D_STATE]
            cm = bc_all[rs, SSD_GROUPS * D_STATE:]
            dt = _softplus(proj[rs, COL_DT:IN_COLS] + dtb_ref[...])
            a = dt * (-jnp.exp(alog_ref[...]))
            cs = _sel_left(tril_ref[...], a)
            cs_t = cs.T
            dt_t = dt.T
            total = cs[CHUNK - 1:CHUNK, :]
            ecs = jnp.exp(cs)
            w_end = dt * jnp.exp(total - cs)
            bb = bm.astype(BF16)
            cb = cm.astype(BF16)
            yield 5
            y_parts = []
            for g in range(SSD_GROUPS):
                gs = slice(g * GROUP_W, (g + 1) * GROUP_W)
                ns = slice(g * D_STATE, (g + 1) * D_STATE)
                cb_g = _dot_nt(cb[:, ns], bb[:, ns])
                y_diag, xw, ecs_g = [], [], []
                for pr in range(2):
                    h0 = g * 4 + 2 * pr
                    xp = xs[:, (h0 // 2) * LANES:(h0 // 2 + 1) * LANES]
                    ecs_g.append(jnp.where(lo, ecs[:, h0:h0 + 1], ecs[:, h0 + 1:h0 + 2]))
                    xw.append((xp * jnp.where(lo, w_end[:, h0:h0 + 1], w_end[:, h0 + 1:h0 + 2])).astype(BF16))
                    x2 = jnp.concatenate([jnp.where(lo, xp, 0.0), jnp.where(lo, 0.0, xp)], axis=0).astype(BF16)
                    gmat = []
                    for h in (h0, h0 + 1):
                        diff = cs[:, h:h + 1] - cs_t[h:h + 1, :]
                        decay = jnp.exp(jnp.where(causal, diff, -jnp.inf))
                        gmat.append((cb_g * decay * dt_t[h:h + 1, :]).astype(BF16))
                    y_diag.append(_dot(jnp.concatenate(gmat, axis=1), x2))
                ecs_x = jnp.concatenate(ecs_g, axis=-1)
                st_old = state_t[:, gs]
                y_off = _dot(cb[:, ns], st_old.astype(BF16)) * ecs_x
                y_parts.append(jnp.concatenate(y_diag, axis=-1) + y_off)
                b_t = bm[:, ns].T.astype(BF16)
                state_t[:, gs] = ecs_x[CHUNK - 1:CHUNK, :] * st_old + _dot(b_t, jnp.concatenate(xw, axis=-1))
                yield 9
            y = jnp.concatenate(y_parts, axis=-1) + dskip_ref[...] * xs
            mix[rs, 0:SSD_D_INNER] = _gated_norm(y, proj[rs, COL_Z:COL_XBC], gn_ref).astype(BF16)
            yield 3

            keys = slice(ci * CHUNK, (ci + 2) * CHUNK)
            table = jnp.minimum(c, 1) if ci == 0 else 1
            probs, denoms = {}, {}
            for variant in range(2):
                hv = [h for h in range(N_HEADS) if (h // Q_PER_KV + h % 2) % 2 == variant]
                qms = []
                for h in hv:
                    q_pair = q[rs, (h // 2) * LANES:(h // 2 + 1) * LANES]
                    qms.append(jnp.where(lo, q_pair, 0.0) if h % 2 == 0 else jnp.where(lo, 0.0, q_pair))
                sc_all = _dot_nt(jnp.concatenate(qms, axis=0).astype(BF16), kbuf[variant, keys, :])
                for i, h in enumerate(hv):
                    sc = sc_all[i * CHUNK:(i + 1) * CHUNK] + bias[table, h]
                    sink = sink_ref[h]
                    m = jnp.maximum(jnp.max(sc, axis=-1, keepdims=True), sink)
                    p = jnp.exp(sc - m)
                    denoms[h] = jnp.sum(p, axis=-1, keepdims=True) + jnp.exp(sink - m)
                    probs[h] = p.astype(BF16)
                yield 10
            parts = {}
            for variant in range(2):
                vv = vbuf[variant, keys, :]
                for par in range(2):
                    hv = [h for h in range(N_HEADS) if (h // Q_PER_KV + h % 2) % 2 == variant and h % 2 == par]
                    vm = jnp.where(lo, vv, 0.0) if par == 0 else jnp.where(lo, 0.0, vv)
                    o_all = _dot(jnp.concatenate([probs[h] for h in hv], axis=0), vm.astype(BF16))
                    for i, h in enumerate(hv):
                        parts[h] = o_all[i * CHUNK:(i + 1) * CHUNK] / denoms[h]
            for pair in range(N_HEADS // 2):
                mix[rs, SSD_D_INNER + pair * LANES:SSD_D_INNER + (pair + 1) * LANES] = (
                    parts[2 * pair] + parts[2 * pair + 1]).astype(BF16)
            yield 4
        kbuf[:, 0:CHUNK, :] = kbuf[:, tq:tq + CHUNK, :]
        vbuf[:, 0:CHUNK, :] = vbuf[:, tq:tq + CHUNK, :]

        hbuf[...] = x

    _alternate(back(), front())

    @pl.when((c == last) & (step < n_tiles))
    def _():
        conv_out[0] = xpad[5:8, :]
        ssm_out[0] = state_t[...].T
        k_out[0] = proj[tq - WINDOW:tq, COL_K:COL_V]
        v_out[0] = proj[tq - WINDOW:tq, COL_V:COL_DT]


def _prompt_layer(x3, mk_b, mv_b, dense, small, consts, n_chunks):
    cw, cb, dtb, alog, dskip_x, gn, relb, sinks = small
    tril, buckets = consts
    batch, seq, _ = x3.shape
    tq = n_chunks * CHUNK
    tps = seq // tq
    n_tiles = batch * tps
    front = lambda s: jnp.minimum(s, n_tiles - 1)
    back = lambda s: jnp.maximum(s - 1, 0)
    x_spec = pl.BlockSpec((1, tq, D_MODEL), lambda s: (front(s) // tps, front(s) % tps, 0))
    y_spec = pl.BlockSpec((1, tq, D_MODEL), lambda s: (back(s) // tps, back(s) % tps, 0))
    mem_spec = pl.BlockSpec((1, N_MEM, CA_DIM), lambda s: (back(s) // tps, 0, 0))
    per_b = lambda shp: pl.BlockSpec((1,) + shp, lambda s: (front(s) // tps,) + (0,) * len(shp))
    out_shape = [
        jax.ShapeDtypeStruct((batch, seq, D_MODEL), F32),
        jax.ShapeDtypeStruct((batch, CONV_K - 1, CONV_DIM), F32),
        jax.ShapeDtypeStruct((batch, SSD_D_INNER, D_STATE), F32),
        jax.ShapeDtypeStruct((batch, WINDOW, KV_DIM), F32),
        jax.ShapeDtypeStruct((batch, WINDOW, KV_DIM), F32),
    ]
    full = list(dense) + [cw, cb, dtb, alog, dskip_x, gn, tril, buckets]
    return pl.pallas_call(
        functools.partial(_layer_step_body, n_chunks=n_chunks, tiles_per_seq=tps),
        grid=(n_tiles + 1,),
        in_specs=[x_spec, mem_spec, mem_spec] + [_full_spec(a.shape) for a in full] + [_smem_spec(), _smem_spec()],
        out_specs=[y_spec, per_b((CONV_K - 1, CONV_DIM)), per_b((SSD_D_INNER, D_STATE)),
                   per_b((WINDOW, KV_DIM)), per_b((WINDOW, KV_DIM))],
        out_shape=out_shape,
        scratch_shapes=[
            pltpu.VMEM((8 + tq, CONV_DIM), F32),
            pltpu.VMEM((D_STATE, SSD_D_INNER), F32),
            pltpu.VMEM((2, CHUNK + tq, KV_DIM), BF16),
            pltpu.VMEM((2, CHUNK + tq, KV_DIM), BF16),
            pltpu.VMEM((2, N_HEADS, WINDOW, 2 * WINDOW), F32),
            pltpu.VMEM((tq, D_MODEL), BF16),
            pltpu.VMEM((tq, D_MODEL), F32),
            pltpu.VMEM((tq, IN_COLS), F32),
        ],
        compiler_params=_params(("arbitrary",)),
        name="prompt_layer",
    )(x3, mk_b, mv_b, *full, relb, sinks)


def _sample_mixer_body(z_ref, xbc_ref, q_ref, k_ref, v_ref, dt_ref,
                       sconv_ref, sssm_ref, ck_ref, cv_ref,
                       cw_ref, cb_ref, dtb_ref, alog_ref, dskip_ref, gn_ref,
                       tcum_ref, tseq_ref, expand_ref, bkt_c_ref, bkt_n_ref, relb_ref, sink_ref,
                       mix_ref, conv_out, ssm_out, k_out, v_out,
                       xpad, bias_c, bias_n, *, dec_seq):
    step = pl.program_id(0)
    n_seq = SAMPLE_SEQS
    rows = n_seq * dec_seq

    @pl.when(step == 0)
    def _():
        _build_bias(bias_c, lambda i: bkt_c_ref[...], relb_ref, 1)
        _build_bias(bias_n, lambda i: bkt_n_ref[...], relb_ref, 1)

    xbc = xbc_ref[...]
    xpad[:, 8:8 + dec_seq, :] = xbc.reshape(n_seq, dec_seq, CONV_DIM)
    for r in range(CONV_K - 1):
        xpad[:, 5 + r, :] = sconv_ref[r]
    taps = [xpad[:, 5 + k:5 + k + dec_seq, :].reshape(rows, CONV_DIM) for k in range(CONV_K - 1)] + [xbc]
    conv = _conv_taps(cw_ref, cb_ref, taps)
    for r in range(CONV_K - 1):
        conv_out[r] = xpad[:, 5 + dec_seq + r, :]

    row = lax.broadcasted_iota(jnp.int32, (rows, rows), 0)
    col = lax.broadcasted_iota(jnp.int32, (rows, rows), 1)
    tseq = tseq_ref[...]
    same_seq = tseq > 0
    causal = same_seq & (col <= row)
    xs, bm, cm, cs, cs_t, (dt_x, dend_x, ecs_x, seqdec_x) = _ssd_prepare(
        conv, dt_ref[...], dtb_ref, alog_ref, tcum_ref[...],
        lambda cs_, a_: _sel_left(tseq, a_), lambda total: [jnp.exp(total)], expand_ref[...])
    xdt = xs * dt_x
    bb = bm.astype(BF16)
    cb = cm.astype(BF16)
    seq_of_row = lax.broadcasted_iota(jnp.int32, (rows, 1), 0) // dec_seq
    seq_of_lane = lax.broadcasted_iota(jnp.int32, (1, rows), 1) // dec_seq
    xde_t = (xdt * dend_x).T
    y_off = [None] * SSD_GROUPS
    for i in range(n_seq):
        st = sssm_ref[i]
        dec = jnp.concatenate(
            [jnp.broadcast_to(seqdec_x[i * dec_seq:i * dec_seq + 1, h * SSD_HEAD_DIM:h * SSD_HEAD_DIM + 1],
                              (SSD_HEAD_DIM, D_STATE)) for h in range(SSD_HEADS)], axis=0)
        new_parts = []
        for g in range(SSD_GROUPS):
            gs = slice(g * GROUP_W, (g + 1) * GROUP_W)
            ns = slice(g * D_STATE, (g + 1) * D_STATE)
            c_i = jnp.where(seq_of_row == i, cm[:, ns], 0.0).astype(BF16)
            part = _dot_nt(c_i, st[gs].astype(BF16))
            y_off[g] = part if y_off[g] is None else y_off[g] + part
            x_i = jnp.where(seq_of_lane == i, xde_t[gs], 0.0).astype(BF16)
            new_parts.append(dec[gs] * st[gs] + _dot(x_i, bb[:, ns]))
        ssm_out[i] = jnp.concatenate(new_parts, axis=0)
    y_parts = []
    for g in range(SSD_GROUPS):
        gs = slice(g * GROUP_W, (g + 1) * GROUP_W)
        ns = slice(g * D_STATE, (g + 1) * D_STATE)
        cb_g = _dot_nt(cb[:, ns], bb[:, ns])
        y_parts.append(_ssd_diag(cs, cs_t, cb_g, xdt, causal, g) + y_off[g] * ecs_x[:, gs])
    y = jnp.concatenate(y_parts, axis=-1) + dskip_ref[...] * xs
    mix_ref[:, 0:SSD_D_INNER] = _gated_norm(y, z_ref[...], gn_ref).astype(BF16)

    lo = _lane_lo()
    k_new = k_ref[...]
    v_new = v_ref[...]
    k_var = [k_new.astype(BF16), pltpu.roll(k_new, HALF, 1).astype(BF16)]
    v_new_r = pltpu.roll(v_new, HALF, 1)
    v_dup = [jnp.where(lo, v_new, v_new_r).astype(BF16), jnp.where(lo, v_new_r, v_new).astype(BF16)]
    qf = q_ref[...].astype(F32)
    q_masked = []
    s_new = []
    for h in range(N_HEADS):
        pair, par = h // 2, h % 2
        j = h // Q_PER_KV
        q_pair = qf[:, pair * LANES:(pair + 1) * LANES]
        qm = jnp.where(lo, q_pair, 0.0) if par == 0 else jnp.where(lo, 0.0, q_pair)
        q_masked.append(qm)
        s_new.append(_dot_nt(qm.astype(BF16), k_var[(j + par) % 2]) + bias_n[0, h])
    stack_rows = lax.broadcasted_iota(jnp.int32, (Q_PER_KV * dec_seq, 1), 0) // dec_seq
    k_new_t = k_new.T
    v_new_t = v_new.T
    keep = WINDOW - dec_seq
    old_lane = lax.broadcasted_iota(jnp.int32, (1, WINDOW), 1) < keep
    att_rows = []
    for i in range(n_seq):
        rs = slice(i * dec_seq, (i + 1) * dec_seq)
        kc_t = ck_ref[i]
        vc_t = cv_ref[i]
        pieces = []
        for j in range(N_KV_HEADS):
            heads = range(j * Q_PER_KV, (j + 1) * Q_PER_KV)
            cj = slice(j * ATTN_HEAD_DIM, (j + 1) * ATTN_HEAD_DIM)
            kdup_t = jnp.concatenate([kc_t[cj], kc_t[cj]], axis=0).astype(BF16)
            vdup_t = jnp.concatenate([vc_t[cj], vc_t[cj]], axis=0).astype(BF16)
            qs = jnp.concatenate([q_masked[h][rs] for h in heads], axis=0).astype(BF16)
            sc = _dot(qs, kdup_t) + jnp.concatenate([bias_c[0, h] for h in heads], axis=0)
            sn = jnp.concatenate([s_new[h][rs] for h in heads], axis=0)
            sink = jnp.zeros((Q_PER_KV * dec_seq, 1), F32)
            for hh, h in enumerate(heads):
                sink = jnp.where(stack_rows == hh, sink_ref[h], sink)
            m = jnp.maximum(jnp.maximum(jnp.max(sc, axis=-1, keepdims=True),
                                        jnp.max(sn, axis=-1, keepdims=True)), sink)
            pc = jnp.exp(sc - m)
            pn = jnp.exp(sn - m)
            denom = (jnp.sum(pc, axis=-1, keepdims=True) + jnp.sum(pn, axis=-1, keepdims=True)
                     + jnp.exp(sink - m))
            o = (_dot_nt(pc.astype(BF16), vdup_t) + _dot(pn.astype(BF16), v_dup[j])) / denom
            for pr in range(Q_PER_KV // 2):
                even = o[(2 * pr) * dec_seq:(2 * pr + 1) * dec_seq]
                odd = o[(2 * pr + 1) * dec_seq:(2 * pr + 2) * dec_seq]
                pieces.append(jnp.where(lo, even, odd))
        att_rows.append(jnp.concatenate(pieces, axis=-1))
        new_shift = (keep - i * dec_seq) % WINDOW
        k_out[i] = jnp.where(old_lane, pltpu.roll(kc_t, keep, 1), pltpu.roll(k_new_t, new_shift, 1))
        v_out[i] = jnp.where(old_lane, pltpu.roll(vc_t, keep, 1), pltpu.roll(v_new_t, new_shift, 1))
    mix_ref[:, SSD_D_INNER:] = jnp.concatenate(att_rows, axis=0).astype(BF16)


def _sample_mixer(z, xbc, q, k, v, dt, sconv, sssm, ck, cv, small, consts, n_seq_total, dec_seq):
    cw, cb, dtb, alog, dskip_x, gn, relb, sinks = small
    tcum, tseq, expand, bkt_c, bkt_n = consts
    rows = SAMPLE_SEQS * dec_seq
    tok = lambda w: pl.BlockSpec((rows, w), lambda i: (i, 0))
    per_s = lambda s: pl.BlockSpec((SAMPLE_SEQS,) + s, lambda i: (i,) + (0,) * len(s))
    out_shape = [
        jax.ShapeDtypeStruct((n_seq_total * dec_seq, D_MODEL), BF16),
        jax.ShapeDtypeStruct((CONV_K - 1, n_seq_total, CONV_DIM), F32),
        jax.ShapeDtypeStruct((n_seq_total, SSD_D_INNER, D_STATE), F32),
        jax.ShapeDtypeStruct((n_seq_total, WINDOW, KV_DIM), F32),
        jax.ShapeDtypeStruct((n_seq_total, WINDOW, KV_DIM), F32),
    ]
    conv_spec = pl.BlockSpec((CONV_K - 1, SAMPLE_SEQS, CONV_DIM), lambda i: (0, i, 0))
    return pl.pallas_call(
        functools.partial(_sample_mixer_body, dec_seq=dec_seq),
        grid=(n_seq_total // SAMPLE_SEQS,),
        in_specs=[tok(SSD_D_INNER), tok(CONV_DIM), tok(ATTN_DIM), tok(KV_DIM), tok(KV_DIM), tok(DT_PAD),
                  conv_spec, per_s((SSD_D_INNER, D_STATE)),
                  per_s((WINDOW, KV_DIM)), per_s((WINDOW, KV_DIM)),
                  _full_spec(cw.shape), _full_spec(cb.shape), _full_spec(dtb.shape), _full_spec(alog.shape),
                  _full_spec(dskip_x.shape), _full_spec(gn.shape),
                  _full_spec(tcum.shape), _full_spec(tseq.shape), _full_spec(expand.shape),
                  _full_spec(bkt_c.shape), _full_spec(bkt_n.shape), _smem_spec(), _smem_spec()],
        out_specs=[tok(D_MODEL), conv_spec, per_s((SSD_D_INNER, D_STATE)),
                   per_s((WINDOW, KV_DIM)), per_s((WINDOW, KV_DIM))],
        out_shape=out_shape,
        scratch_shapes=[
            pltpu.VMEM((SAMPLE_SEQS, 8 + dec_seq, CONV_DIM), F32),
            pltpu.VMEM((1, N_HEADS, dec_seq, WINDOW), F32),
            pltpu.VMEM((1, N_HEADS, rows, rows), F32),
        ],
        compiler_params=_params(("arbitrary",)),
        name="sample_mixer",
    )(z, xbc, q, k, v, dt, sconv, sssm, ck, cv, cw, cb, dtb, alog, dskip_x, gn,
      tcum, tseq, expand, bkt_c, bkt_n, relb, sinks)


def _post1_body(x_ref, mix_ref, wout_ref, gc_ref, wcq_ref, h_ref, qc_ref):
    h = x_ref[...] + _dot(mix_ref[...], wout_ref[...])
    h_ref[...] = h
    qc_ref[...] = _dot(_rms(h, gc_ref[...]).astype(BF16), wcq_ref[...]).astype(BF16)


def _post1(x2, mix, w_out, g_cross, w_cq, tm):
    t = x2.shape[0]
    row = lambda w: pl.BlockSpec((tm, w), lambda i: (i, 0))
    return pl.pallas_call(
        _post1_body,
        grid=(t // tm,),
        in_specs=[row(D_MODEL), row(D_MODEL), _full_spec(w_out.shape), _full_spec(g_cross.shape),
                  _full_spec(w_cq.shape)],
        out_specs=[row(D_MODEL), row(CA_DIM)],
        out_shape=[jax.ShapeDtypeStruct((t, D_MODEL), F32), jax.ShapeDtypeStruct((t, CA_DIM), BF16)],
        compiler_params=_params(("parallel",)),
        name="out_proj",
    )(x2, mix, w_out, g_cross, w_cq)


def _mem_kv_body(mem_ref, g_ref, wk_ref, wv_ref, k_ref, v_ref, kb_ref, vb_ref):
    mn = _rms(mem_ref[...], g_ref[...]).astype(BF16)
    k = _dot(mn, wk_ref[...])
    v = _dot(mn, wv_ref[...])
    rows = k.shape[0]
    for h in range(CA_HEADS):
        hs = slice(h * CA_HEAD_DIM, (h + 1) * CA_HEAD_DIM)
        k_ref[pl.ds(h, rows, stride=CA_HEADS), :] = k[:, hs]
        v_ref[pl.ds(h, rows, stride=CA_HEADS), :] = v[:, hs]
    kb_ref[...] = k.astype(BF16)
    vb_ref[...] = v.astype(BF16)


def _mem_kv(mem2, g_mem, w_ck, w_cv, tm):
    t = mem2.shape[0]
    row = lambda w: pl.BlockSpec((tm, w), lambda i: (i, 0))
    return pl.pallas_call(
        _mem_kv_body,
        grid=(t // tm,),
        in_specs=[row(D_MODEL), _full_spec(g_mem.shape), _full_spec(w_ck.shape), _full_spec(w_cv.shape)],
        out_specs=[pl.BlockSpec((tm * CA_HEADS, CA_HEAD_DIM), lambda i: (i, 0))] * 2 + [row(CA_DIM)] * 2,
        out_shape=([jax.ShapeDtypeStruct((t * CA_HEADS, CA_HEAD_DIM), F32)] * 2
                   + [jax.ShapeDtypeStruct((t, CA_DIM), BF16)] * 2),
        compiler_params=_params(("parallel",)),
        name="mem_kv",
    )(mem2, g_mem, w_ck, w_cv)


def _cross_sample_body(q_ref, k_ref, v_ref, o_ref, *, n_seq, dec_seq):
    q = q_ref[...].astype(F32)
    n_keys = N_MEM * CA_HEADS
    col_head = lax.broadcasted_iota(jnp.int32, (1, n_keys), 1) & (CA_HEADS - 1)
    row_head = lax.broadcasted_iota(jnp.int32, (CA_HEADS * dec_seq, 1), 0) // dec_seq
    own = col_head == row_head
    rows = []
    for i in range(n_seq):
        qi = q[i * dec_seq:(i + 1) * dec_seq]
        qs = jnp.concatenate([qi[:, h * CA_HEAD_DIM:(h + 1) * CA_HEAD_DIM] for h in range(CA_HEADS)], axis=0)
        s = _dot_nt(qs.astype(BF16), k_ref[i].astype(BF16)) * (CA_HEAD_DIM ** -0.5)
        s = jnp.where(own, s, NEG)
        m = jnp.max(s, axis=-1, keepdims=True)
        p = jnp.exp(s - m)
        o = _dot(p.astype(BF16), v_ref[i].astype(BF16)) / jnp.sum(p, axis=-1, keepdims=True)
        rows.append(jnp.concatenate([o[h * dec_seq:(h + 1) * dec_seq] for h in range(CA_HEADS)], axis=-1))
    o_ref[...] = jnp.concatenate(rows, axis=0).astype(BF16)


def _cross_sample(qc, ck, cv, n_seq_total, dec_seq, n_seq):
    rows = n_seq * dec_seq
    tok = pl.BlockSpec((rows, CA_DIM), lambda i: (i, 0))
    mem = pl.BlockSpec((n_seq, N_MEM * CA_HEADS, CA_HEAD_DIM), lambda i: (i, 0, 0))
    return pl.pallas_call(
        functools.partial(_cross_sample_body, n_seq=n_seq, dec_seq=dec_seq),
        grid=(n_seq_total // n_seq,),
        in_specs=[tok, mem, mem],
        out_specs=tok,
        out_shape=jax.ShapeDtypeStruct((n_seq_total * dec_seq, CA_DIM), BF16),
        compiler_params=_params(("parallel",)),
        name="cross_sample",
    )(qc, ck, cv)


def _post2_body(h_ref, o_ref, wco_ref, gf_ref, wg_ref, wu_ref, wd_ref, gfin_ref, y_ref):
    h = h_ref[...] + _dot(o_ref[...], wco_ref[...])
    hn = _rms(h, gf_ref[...]).astype(BF16)
    acc = h
    for lo, hi in FF_SPLITS:
        act = _silu(_dot(hn, wg_ref[:, lo:hi])) * _dot(hn, wu_ref[:, lo:hi])
        acc = acc + _dot(act.astype(BF16), wd_ref[lo:hi, :])
    y_ref[...] = _rms(acc, gfin_ref[...])


def _post2(h1, o, w_co, g_ffn, w_gate, w_up, w_down, g_final, tm):
    t = h1.shape[0]
    row = lambda w: pl.BlockSpec((tm, w), lambda i: (i, 0))
    return pl.pallas_call(
        _post2_body,
        grid=(t // tm,),
        in_specs=[row(D_MODEL), row(CA_DIM), _full_spec(w_co.shape), _full_spec(g_ffn.shape),
                  _full_spec(w_gate.shape), _full_spec(w_up.shape), _full_spec(w_down.shape),
                  _full_spec(g_final.shape)],
        out_specs=row(D_MODEL),
        out_shape=jax.ShapeDtypeStruct((t, D_MODEL), F32),
        compiler_params=_params(("parallel",)),
        name="ffn",
    )(h1, o, w_co, g_ffn, w_gate, w_up, w_down, g_final)


def _prompt_consts():
    qi = np.arange(WINDOW)[:, None]
    ji = np.arange(2 * WINDOW)[None, :]
    dist = qi + WINDOW - ji
    inband = (dist >= 0) & (dist < WINDOW)
    bucket = np.where(inband, _t5_bucket_np(dist), -1)
    first = np.where(ji >= WINDOW, bucket, -1)
    buckets = np.stack([first, bucket]).astype(np.int32)
    return (jnp.asarray(_tril_np(CHUNK), BF16), jnp.asarray(buckets))


def _sample_consts(dec_seq, cache_len):
    rows = SAMPLE_SEQS * dec_seq
    r = np.arange(rows)
    same = (r[:, None] // dec_seq) == (r[None, :] // dec_seq)
    tcum = (same & (r[None, :] <= r[:, None])).astype(np.float32)
    tseq = same.astype(np.float32)
    t = np.arange(dec_seq)[:, None]
    j = np.arange(cache_len)[None, :]
    dist_c = t + cache_len - j
    bkt_c = np.where((dist_c >= 0) & (dist_c < WINDOW), _t5_bucket_np(dist_c), -1).astype(np.int32)
    dist_n = (r[:, None] % dec_seq) - (r[None, :] % dec_seq)
    ok = same & (dist_n >= 0) & (dist_n < WINDOW)
    bkt_n = np.where(ok, _t5_bucket_np(dist_n), -1).astype(np.int32)
    return (jnp.asarray(tcum, BF16), jnp.asarray(tseq, BF16), jnp.asarray(_expand_np(), BF16),
            jnp.asarray(bkt_c), jnp.asarray(bkt_n))


def _pick_tile(t, pref):
    tm = min(t, pref)
    while t % tm:
        tm //= 2
    return tm


def kernel(x_prompt, x_sample, mem_prompt, state_conv, state_ssm, cache_swa_k, cache_swa_v, cache_mem_k, cache_mem_v, rel_bias, g_mix, w_in, conv_w, conv_b, dt_bias, a_log, d_skip, g_ssd_norm, sinks, w_out, g_cross, g_mem, w_cq, w_ck, w_cv, w_co, g_ffn, w_gate, w_up, w_down, g_final):
    assert g_mix.shape[0] == 1, "single-layer trunk"
    batch, seq, _ = x_prompt.shape
    n_dec, dec_seq, _ = x_sample.shape
    cache_len = cache_swa_k.shape[2]
    assert seq % CHUNK == 0 and cache_len == WINDOW and n_dec % SAMPLE_SEQS == 0 and dec_seq == 8

    wi = w_in[0]
    s0, s1, s2, s3, s4 = (SSD_D_INNER, SSD_D_INNER + CONV_DIM, SSD_D_INNER + CONV_DIM + SSD_HEADS,
                          SSD_D_INNER + CONV_DIM + SSD_HEADS + ATTN_DIM,
                          SSD_D_INNER + CONV_DIM + SSD_HEADS + ATTN_DIM + KV_DIM)
    w_in_r = jnp.concatenate(
        [wi[:, :s0], wi[:, s0:s1], wi[:, s2:s3], wi[:, s3:s4], wi[:, s4:],
         jnp.pad(wi[:, s1:s2], ((0, 0), (0, DT_PAD - SSD_HEADS)))], axis=1).astype(BF16)
    row = lambda a: a.reshape(1, -1).astype(F32)
    pad_h = lambda a: jnp.pad(a.reshape(1, -1).astype(F32), ((0, 0), (0, DT_PAD - SSD_HEADS)))
    small = (conv_w[0].astype(F32), row(conv_b[0]), pad_h(dt_bias[0]), pad_h(a_log[0]),
             jnp.repeat(d_skip[0].astype(F32), SSD_HEAD_DIM).reshape(1, -1), row(g_ssd_norm[0]),
             rel_bias.astype(F32), sinks[0].astype(F32))
    bf = lambda w: w[0].astype(BF16)
    w_out_b, w_cq_b, w_ck_b, w_cv_b, w_co_b = bf(w_out), bf(w_cq), bf(w_ck), bf(w_cv), bf(w_co)
    w_gate_b, w_up_b, w_down_b = bf(w_gate), bf(w_up), bf(w_down)
    g_mix_r, g_cross_r, g_mem_r, g_ffn_r, g_fin_r = row(g_mix[0]), row(g_cross[0]), row(g_mem[0]), row(g_ffn[0]), row(g_final)

    n_chunks = 2 if seq % (2 * CHUNK) == 0 else 1
    mem2 = mem_prompt.reshape(batch * N_MEM, D_MODEL)
    mk, mv, mk_b, mv_b = _mem_kv(mem2, g_mem_r, w_ck_b, w_cv_b, _pick_tile(batch * N_MEM, 512))
    dense = (g_mix_r, w_in_r, w_out_b, g_cross_r, w_cq_b, w_co_b, g_ffn_r, w_gate_b, w_up_b, w_down_b, g_fin_r)
    y_prompt, p_conv, p_ssm, p_k, p_v = _prompt_layer(
        x_prompt, mk_b.reshape(batch, N_MEM, CA_DIM), mv_b.reshape(batch, N_MEM, CA_DIM),
        dense, small, _prompt_consts(), n_chunks)

    def channel_major(cache):
        return jnp.transpose(cache, (0, 2, 3, 1)).reshape(n_dec, KV_DIM, cache_len)

    def position_major(cache_t):
        return jnp.transpose(cache_t.reshape(n_dec, N_KV_HEADS, ATTN_HEAD_DIM, cache_len), (0, 3, 1, 2))

    ts = n_dec * dec_seq
    xs2 = x_sample.reshape(ts, D_MODEL)
    tm_s = _pick_tile(ts, 512)
    z, xbc, q, k, v, dt = _in_proj(xs2, g_mix_r, w_in_r, tm_s)
    mix_s, s_conv, s_ssm, s_k, s_v = _sample_mixer(
        z, xbc, q, k, v, dt, jnp.transpose(state_conv[0], (1, 0, 2)),
        state_ssm[0].reshape(n_dec, SSD_D_INNER, D_STATE),
        channel_major(cache_swa_k[0]), channel_major(cache_swa_v[0]),
        small, _sample_consts(dec_seq, cache_len), n_dec, dec_seq)
    h1s, qcs = _post1(xs2, mix_s, w_out_b, g_cross_r, w_cq_b, tm_s)
    os_ = _cross_sample(qcs, cache_mem_k[0].reshape(n_dec, N_MEM * CA_HEADS, CA_HEAD_DIM),
                        cache_mem_v[0].reshape(n_dec, N_MEM * CA_HEADS, CA_HEAD_DIM), n_dec, dec_seq, 8)
    y_sample = _post2(h1s, os_, w_co_b, g_ffn_r, w_gate_b, w_up_b, w_down_b, g_fin_r, tm_s)

    return (y_prompt.reshape(batch, seq, D_MODEL), y_sample.reshape(n_dec, dec_seq, D_MODEL),
            p_conv[None], p_ssm.reshape(1, batch, SSD_HEADS, SSD_HEAD_DIM, D_STATE),
            p_k.reshape(1, batch, WINDOW, N_KV_HEADS, ATTN_HEAD_DIM),
            p_v.reshape(1, batch, WINDOW, N_KV_HEADS, ATTN_HEAD_DIM),
            mk.reshape(1, batch, N_MEM, CA_HEADS, CA_HEAD_DIM), mv.reshape(1, batch, N_MEM, CA_HEADS, CA_HEAD_DIM),
            jnp.transpose(s_conv, (1, 0, 2))[None], s_ssm.reshape(1, n_dec, SSD_HEADS, SSD_HEAD_DIM, D_STATE),
            position_major(s_k)[None], position_major(s_v)[None])
```

```python
import functools
import math

import numpy as np
import jax
import jax.numpy as jnp
from jax import lax
from jax.experimental import pallas as pl
from jax.experimental.pallas import tpu as pltpu

F32 = jnp.float32
BF16 = jnp.bfloat16

D_MODEL = 1024
SSD_D_INNER = 512
SSD_HEAD_DIM = 64
SSD_HEADS = 8
SSD_GROUPS = 2
GROUP_W = SSD_D_INNER // SSD_GROUPS
D_STATE = 128
CONV_K = 4
CONV_DIM = SSD_D_INNER + 2 * SSD_GROUPS * D_STATE
CHUNK = 128
ATTN_DIM = 512
ATTN_HEAD_DIM = 64
N_HEADS = 8
N_KV_HEADS = 2
Q_PER_KV = N_HEADS // N_KV_HEADS
KV_DIM = N_KV_HEADS * ATTN_HEAD_DIM
WINDOW = 128
N_BUCKETS = 32
MAX_EXACT = N_BUCKETS // 2
MAX_DISTANCE = 128
N_MEM = 256
CA_HEADS = 4
CA_HEAD_DIM = 128
CA_DIM = CA_HEADS * CA_HEAD_DIM
D_FF = 2816
EPS = 1e-6

LANES = 128
HALF = LANES // 2
DT_PAD = LANES
COL_Z = 0
COL_XBC = COL_Z + SSD_D_INNER
COL_Q = COL_XBC + CONV_DIM
COL_K = COL_Q + ATTN_DIM
COL_V = COL_K + KV_DIM
COL_DT = COL_V + KV_DIM
IN_COLS = COL_DT + DT_PAD
NEG = -1e30
SAMPLE_SEQS = 16
VMEM_LIMIT = 56 * 1024 * 1024
FF_SPLITS = ((0, 1024), (1024, 2048), (2048, D_FF))
SECOND_STREAM_LEAD = 0.5
FF_PIECES = tuple((lo, min(lo + 512, D_FF)) for lo in range(0, D_FF, 512))
PROJ_PIECES = ((COL_Z, COL_XBC), (COL_XBC, COL_XBC + 512), (COL_XBC + 512, COL_Q), (COL_Q, COL_K), (COL_K, IN_COLS))


def _rms(x, g):
    return x * lax.rsqrt(jnp.mean(x * x, axis=-1, keepdims=True) + EPS) * g


def _silu(x):
    return x * jax.nn.sigmoid(x)


def _softplus(x):
    return jnp.maximum(x, 0.0) + jnp.log1p(jnp.exp(-jnp.abs(x)))


def _dot(a, b):
    return jnp.dot(a, b, preferred_element_type=F32)


def _dot_nt(a, b):
    return lax.dot_general(a, b, (((1,), (1,)), ((), ())), preferred_element_type=F32)


def _split3(a):
    hi = a.astype(BF16)
    r = a - hi.astype(F32)
    mid = r.astype(BF16)
    lo = (r - mid.astype(F32)).astype(BF16)
    return hi, mid, lo


def _sel_left(t01, a):
    hi, mid, lo = _split3(a)
    return _dot(t01, hi) + _dot(t01, mid) + _dot(t01, lo)


def _sel_right(a, e01):
    hi, mid, lo = _split3(a)
    return _dot(hi, e01) + _dot(mid, e01) + _dot(lo, e01)


def _lane_lo():
    return lax.broadcasted_iota(jnp.int32, (1, LANES), 1) < HALF


def _t5_bucket_np(dist):
    n = np.maximum(dist, 0)
    ratio = np.log(np.maximum(n, 1).astype(np.float32) / np.float32(MAX_EXACT))
    large = MAX_EXACT + (ratio / np.float32(math.log(MAX_DISTANCE / MAX_EXACT))
                         * np.float32(N_BUCKETS - MAX_EXACT)).astype(np.int32)
    large = np.minimum(large, N_BUCKETS - 1)
    return np.where(n < MAX_EXACT, n, large).astype(np.int32)


def _tril_np(n):
    return np.tril(np.ones((n, n), np.float32))


def _expand_np():
    e = np.zeros((LANES, SSD_D_INNER), np.float32)
    for h in range(SSD_HEADS):
        e[h, h * SSD_HEAD_DIM:(h + 1) * SSD_HEAD_DIM] = 1.0
    return e


def _full_spec(shape):
    nd = len(shape)
    return pl.BlockSpec(shape, lambda *_: (0,) * nd, pipeline_mode=pl.Buffered(1))


def _smem_spec():
    return pl.BlockSpec(memory_space=pltpu.SMEM)


def _params(sem):
    return pltpu.CompilerParams(dimension_semantics=sem, vmem_limit_bytes=VMEM_LIMIT)


def _in_proj_body(x_ref, g_ref, w_ref, z_ref, xbc_ref, q_ref, k_ref, v_ref, dt_ref):
    xn = _rms(x_ref[...], g_ref[...]).astype(BF16)

    def seg(lo, hi):
        return _dot(xn, w_ref[:, lo:hi])

    z_ref[...] = seg(COL_Z, COL_XBC)
    xbc_ref[...] = seg(COL_XBC, COL_Q)
    q_ref[...] = (seg(COL_Q, COL_K) * (ATTN_HEAD_DIM ** -0.5)).astype(BF16)
    k_ref[...] = seg(COL_K, COL_V)
    v_ref[...] = seg(COL_V, COL_DT)
    dt_ref[...] = seg(COL_DT, IN_COLS)


def _in_proj(x2, g_mix, w_in_r, tm):
    t = x2.shape[0]
    row = lambda w: pl.BlockSpec((tm, w), lambda i: (i, 0))
    outs = [(SSD_D_INNER, F32), (CONV_DIM, F32), (ATTN_DIM, BF16), (KV_DIM, F32), (KV_DIM, F32), (DT_PAD, F32)]
    return pl.pallas_call(
        _in_proj_body,
        grid=(t // tm,),
        in_specs=[row(D_MODEL), _full_spec((1, D_MODEL)), _full_spec((D_MODEL, IN_COLS))],
        out_specs=[row(w) for w, _ in outs],
        out_shape=[jax.ShapeDtypeStruct((t, w), d) for w, d in outs],
        compiler_params=_params(("parallel",)),
        name="in_proj",
    )(x2, g_mix, w_in_r)


def _conv_taps(cw_ref, cb_ref, taps):
    acc = cb_ref[...] + taps[0] * cw_ref[0:1, :]
    for k in range(1, CONV_K):
        acc = acc + taps[k] * cw_ref[k:k + 1, :]
    return _silu(acc)


def _ssd_prepare(conv, dt_raw, dtb_ref, alog_ref, tcum, total_fn, extra_fn, expand):
    xs = conv[:, :SSD_D_INNER]
    bm = conv[:, SSD_D_INNER:SSD_D_INNER + SSD_GROUPS * D_STATE]
    cm = conv[:, SSD_D_INNER + SSD_GROUPS * D_STATE:]
    dt = _softplus(dt_raw + dtb_ref[...])
    a = dt * (-jnp.exp(alog_ref[...]))
    cs = _sel_left(tcum, a)
    total = total_fn(cs, a)
    pieces = [dt, jnp.exp(total - cs), jnp.exp(cs)] + extra_fn(total)
    rows = cs.shape[0]
    ex = _sel_right(jnp.concatenate(pieces, axis=0), expand)
    ex = [ex[i * rows:(i + 1) * rows] for i in range(len(pieces))]
    return xs, bm, cm, cs, cs.T, ex


def _ssd_diag(cs, cs_t, cb_g, xdt, mask, g):
    lo = _lane_lo()
    out = []
    for pr in range(2):
        h0 = g * 4 + 2 * pr
        xp = xdt[:, (h0 // 2) * LANES:(h0 // 2 + 1) * LANES]
        x_lo = jnp.where(lo, xp, 0.0).astype(BF16)
        x_hi = jnp.where(lo, 0.0, xp).astype(BF16)
        acc = None
        for h, xh in ((h0, x_lo), (h0 + 1, x_hi)):
            diff = cs[:, h:h + 1] - cs_t[h:h + 1, :]
            decay = jnp.exp(jnp.where(mask, diff, -jnp.inf))
            part = _dot((cb_g * decay).astype(BF16), xh)
            acc = part if acc is None else acc + part
        out.append(acc)
    return jnp.concatenate(out, axis=-1)


def _gated_norm(y, z, gn_ref):
    yf = y * _silu(z)
    parts = []
    for g in range(SSD_GROUPS):
        yg = yf[:, g * GROUP_W:(g + 1) * GROUP_W]
        parts.append(yg * lax.rsqrt(jnp.mean(yg * yg, axis=-1, keepdims=True) + EPS))
    return jnp.concatenate(parts, axis=-1) * gn_ref[...]


def _build_bias(bias_ref, bucket_of, relb_ref, n_tables):
    for i in range(n_tables):
        for h in range(N_HEADS):
            bias_ref[i, h] = jnp.full(bias_ref.shape[2:], NEG, F32)

    def body(t, carry):
        for i in range(n_tables):
            hit = bucket_of(i) == t
            for h in range(N_HEADS):
                bias_ref[i, h] = jnp.where(hit, relb_ref[t, h], bias_ref[i, h])
        return carry

    lax.fori_loop(0, N_BUCKETS, body, 0)


def _alternate(first, second):
    streams = [[0.0, 1.0, first], [0.0, SECOND_STREAM_LEAD, second]]
    while streams:
        entry = min(streams, key=lambda e: e[0])
        try:
            entry[0] += entry[1] * next(entry[2])
        except StopIteration:
            streams.remove(entry)


def _layer_step_body(x_ref, mk_ref, mv_ref, gmix_ref, win_ref, wout_ref, gc_ref, wcq_ref,
                     wco_ref, gf_ref, wg_ref, wu_ref, wd_ref, gfin_ref,
                     cw_ref, cb_ref, dtb_ref, alog_ref, dskip_ref, gn_ref,
                     tril_ref, bucket_ref, relb_ref, sink_ref,
                     y_ref, conv_out, ssm_out, k_out, v_out,
                     xpad, state_t, kbuf, vbuf, bias, mix, hbuf, proj, *, n_chunks, tiles_per_seq):
    step = pl.program_id(0)
    n_tiles = pl.num_programs(0) - 1
    c = lax.rem(jnp.minimum(step, n_tiles - 1), tiles_per_seq)
    last = tiles_per_seq - 1
    tq = n_chunks * CHUNK

    @pl.when(step == 0)
    def _():
        _build_bias(bias, lambda i: bucket_ref[i], relb_ref, 2)
        hbuf[...] = jnp.zeros_like(hbuf)
        mix[...] = jnp.zeros_like(mix)

    @pl.when(c == 0)
    def _():
        xpad[0:8, :] = jnp.zeros((8, CONV_DIM), F32)
        state_t[...] = jnp.zeros_like(state_t)
        kbuf[:, 0:CHUNK, :] = jnp.zeros((2, CHUNK, KV_DIM), BF16)
        vbuf[:, 0:CHUNK, :] = jnp.zeros((2, CHUNK, KV_DIM), BF16)

    def back():
        h1 = hbuf[...] + _dot(mix[...], wout_ref[...])
        yield 4
        qc = _dot(_rms(h1, gc_ref[...]).astype(BF16), wcq_ref[...]).astype(BF16)
        yield 2
        heads = []
        for h in range(CA_HEADS):
            hs = slice(h * CA_HEAD_DIM, (h + 1) * CA_HEAD_DIM)
            sc = _dot_nt(qc[:, hs], mk_ref[0, :, hs]) * (CA_HEAD_DIM ** -0.5)
            m = jnp.max(sc, axis=-1, keepdims=True)
            p = jnp.exp(sc - m)
            heads.append(_dot(p.astype(BF16), mv_ref[0, :, hs]) / jnp.sum(p, axis=-1, keepdims=True))
            yield 3
        o = jnp.concatenate(heads, axis=-1).astype(BF16)
        h2 = h1 + _dot(o, wco_ref[...])
        hn = _rms(h2, gf_ref[...]).astype(BF16)
        yield 4
        acc = h2
        for f_lo, f_hi in FF_PIECES:
            width = (f_hi - f_lo) / 512
            gate = _dot(hn, wg_ref[:, f_lo:f_hi])
            yield 5 * width
            act = (_silu(gate) * _dot(hn, wu_ref[:, f_lo:f_hi])).astype(BF16)
            yield 6 * width
            acc = acc + _dot(act, wd_ref[f_lo:f_hi, :])
            yield 5 * width
        y_ref[0] = _rms(acc, gfin_ref[...])

    def front():
        x = x_ref[0]
        xn = _rms(x, gmix_ref[...]).astype(BF16)
        for p_lo, p_hi in PROJ_PIECES:
            proj[:, p_lo:p_hi] = _dot(xn, win_ref[:, p_lo:p_hi])
            yield 4

        xpad[8:8 + tq, :] = proj[:, COL_XBC:COL_Q]
        conv_halves = []
        half_w = CONV_DIM // 2
        for hf in range(2):
            cs_ = slice(hf * half_w, (hf + 1) * half_w)
            acc = cb_ref[:, cs_] + xpad[5:5 + tq, cs_] * cw_ref[0:1, cs_]
            for kk in range(1, CONV_K):
                acc = acc + xpad[5 + kk:5 + kk + tq, cs_] * cw_ref[kk:kk + 1, cs_]
            conv_halves.append(_silu(acc))
            yield 6
        xpad[5:8, :] = xpad[5 + tq:8 + tq, :]
        xs_all = conv_halves[0]
        bc_all = conv_halves[1]

        k_new = proj[:, COL_K:COL_V]
        v_new = proj[:, COL_V:COL_DT]
        kbuf[0, CHUNK:, :] = k_new.astype(BF16)
        kbuf[1, CHUNK:, :] = pltpu.roll(k_new, HALF, 1).astype(BF16)
        vbuf[0, CHUNK:, :] = v_new.astype(BF16)
        vbuf[1, CHUNK:, :] = pltpu.roll(v_new, HALF, 1).astype(BF16)
        q = (proj[:, COL_Q:COL_K] * (ATTN_HEAD_DIM ** -0.5)).astype(BF16)
        yield 2

        row = lax.broadcasted_iota(jnp.int32, (CHUNK, CHUNK), 0)
        col = lax.broadcasted_iota(jnp.int32, (CHUNK, CHUNK), 1)
        causal = col <= row
        lo = _lane_lo()
        for ci in range(n_chunks):
            rs = slice(ci * CHUNK, (ci + 1) * CHUNK)
            xs = xs_all[rs]
            bm = bc_all[rs, 0:SSD_GROUPS * D_STATE]
            cm = bc_all[rs, SSD_GROUPS * D_STATE:]
            dt = _softplus(proj[rs, COL_DT:IN_COLS] + dtb_ref[...])
            a = dt * (-jnp.exp(alog_ref[...]))
            cs = _sel_left(tril_ref[...], a)
            cs_t = cs.T
            dt_t = dt.T
            total = cs[CHUNK - 1:CHUNK, :]
            ecs = jnp.exp(cs)
            w_end = dt * jnp.exp(total - cs)
            bb = bm.astype(BF16)
            cb = cm.astype(BF16)
            yield 5
            y_parts = []
            for g in range(SSD_GROUPS):
                gs = slice(g * GROUP_W, (g + 1) * GROUP_W)
                ns = slice(g * D_STATE, (g + 1) * D_STATE)
                cb_g = _dot_nt(cb[:, ns], bb[:, ns])
                y_diag, xw, ecs_g = [], [], []
                for pr in range(2):
                    h0 = g * 4 + 2 * pr
                    xp = xs[:, (h0 // 2) * LANES:(h0 // 2 + 1) * LANES]
                    ecs_g.append(jnp.where(lo, ecs[:, h0:h0 + 1], ecs[:, h0 + 1:h0 + 2]))
                    xw.append((xp * jnp.where(lo, w_end[:, h0:h0 + 1], w_end[:, h0 + 1:h0 + 2])).astype(BF16))
                    x2 = jnp.concatenate([jnp.where(lo, xp, 0.0), jnp.where(lo, 0.0, xp)], axis=0).astype(BF16)
                    gmat = []
                    for h in (h0, h0 + 1):
                        diff = cs[:, h:h + 1] - cs_t[h:h + 1, :]
                        decay = jnp.exp(jnp.where(causal, diff, -jnp.inf))
                        gmat.append((cb_g * decay * dt_t[h:h + 1, :]).astype(BF16))
                    y_diag.append(_dot(jnp.concatenate(gmat, axis=1), x2))
                ecs_x = jnp.concatenate(ecs_g, axis=-1)
                st_old = state_t[:, gs]
                y_off = _dot(cb[:, ns], st_old.astype(BF16)) * ecs_x
                y_parts.append(jnp.concatenate(y_diag, axis=-1) + y_off)
                b_t = bm[:, ns].T.astype(BF16)
                state_t[:, gs] = ecs_x[CHUNK - 1:CHUNK, :] * st_old + _dot(b_t, jnp.concatenate(xw, axis=-1))
                yield 9
            y = jnp.concatenate(y_parts, axis=-1) + dskip_ref[...] * xs
            mix[rs, 0:SSD_D_INNER] = _gated_norm(y, proj[rs, COL_Z:COL_XBC], gn_ref).astype(BF16)
            yield 3

            keys = slice(ci * CHUNK, (ci + 2) * CHUNK)
            table = jnp.minimum(c, 1) if ci == 0 else 1
            probs, denoms = {}, {}
            for variant in range(2):
                hv = [h for h in range(N_HEADS) if (h // Q_PER_KV + h % 2) % 2 == variant]
                qms = []
                for h in hv:
                    q_pair = q[rs, (h // 2) * LANES:(h // 2 + 1) * LANES]
                    qms.append(jnp.where(lo, q_pair, 0.0) if h % 2 == 0 else jnp.where(lo, 0.0, q_pair))
                sc_all = _dot_nt(jnp.concatenate(qms, axis=0).astype(BF16), kbuf[variant, keys, :])
                for i, h in enumerate(hv):
                    sc = sc_all[i * CHUNK:(i + 1) * CHUNK] + bias[table, h]
                    sink = sink_ref[h]
                    m = jnp.maximum(jnp.max(sc, axis=-1, keepdims=True), sink)
                    p = jnp.exp(sc - m)
                    denoms[h] = jnp.sum(p, axis=-1, keepdims=True) + jnp.exp(sink - m)
                    probs[h] = p.astype(BF16)
                yield 10
            parts = {}
            for variant in range(2):
                vv = vbuf[variant, keys, :]
                for par in range(2):
                    hv = [h for h in range(N_HEADS) if (h // Q_PER_KV + h % 2) % 2 == variant and h % 2 == par]
                    vm = jnp.where(lo, vv, 0.0) if par == 0 else jnp.where(lo, 0.0, vv)
                    o_all = _dot(jnp.concatenate([probs[h] for h in hv], axis=0), vm.astype(BF16))
                    for i, h in enumerate(hv):
                        parts[h] = o_all[i * CHUNK:(i + 1) * CHUNK] / denoms[h]
            for pair in range(N_HEADS // 2):
                mix[rs, SSD_D_INNER + pair * LANES:SSD_D_INNER + (pair + 1) * LANES] = (
                    parts[2 * pair] + parts[2 * pair + 1]).astype(BF16)
            yield 4
        kbuf[:, 0:CHUNK, :] = kbuf[:, tq:tq + CHUNK, :]
        vbuf[:, 0:CHUNK, :] = vbuf[:, tq:tq + CHUNK, :]

        hbuf[...] = x

    _alternate(back(), front())

    @pl.when((c == last) & (step < n_tiles))
    def _():
        conv_out[0] = xpad[5:8, :]
        ssm_out[0] = state_t[...].T
        k_out[0] = proj[tq - WINDOW:tq, COL_K:COL_V]
        v_out[0] = proj[tq - WINDOW:tq, COL_V:COL_DT]


def _prompt_layer(x3, mk_b, mv_b, dense, small, consts, n_chunks):
    cw, cb, dtb, alog, dskip_x, gn, relb, sinks = small
    tril, buckets = consts
    batch, seq, _ = x3.shape
    tq = n_chunks * CHUNK
    tps = seq // tq
    n_tiles = batch * tps
    front = lambda s: jnp.minimum(s, n_tiles - 1)
    back = lambda s: jnp.maximum(s - 1, 0)
    x_spec = pl.BlockSpec((1, tq, D_MODEL), lambda s: (front(s) // tps, front(s) % tps, 0))
    y_spec = pl.BlockSpec((1, tq, D_MODEL), lambda s: (back(s) // tps, back(s) % tps, 0))
    mem_spec = pl.BlockSpec((1, N_MEM, CA_DIM), lambda s: (back(s) // tps, 0, 0))
    per_b = lambda shp: pl.BlockSpec((1,) + shp, lambda s: (front(s) // tps,) + (0,) * len(shp))
    out_shape = [
        jax.ShapeDtypeStruct((batch, seq, D_MODEL), F32),
        jax.ShapeDtypeStruct((batch, CONV_K - 1, CONV_DIM), F32),
        jax.ShapeDtypeStruct((batch, SSD_D_INNER, D_STATE), F32),
        jax.ShapeDtypeStruct((batch, WINDOW, KV_DIM), F32),
        jax.ShapeDtypeStruct((batch, WINDOW, KV_DIM), F32),
    ]
    full = list(dense) + [cw, cb, dtb, alog, dskip_x, gn, tril, buckets]
    return pl.pallas_call(
        functools.partial(_layer_step_body, n_chunks=n_chunks, tiles_per_seq=tps),
        grid=(n_tiles + 1,),
        in_specs=[x_spec, mem_spec, mem_spec] + [_full_spec(a.shape) for a in full] + [_smem_spec(), _smem_spec()],
        out_specs=[y_spec, per_b((CONV_K - 1, CONV_DIM)), per_b((SSD_D_INNER, D_STATE)),
                   per_b((WINDOW, KV_DIM)), per_b((WINDOW, KV_DIM))],
        out_shape=out_shape,
        scratch_shapes=[
            pltpu.VMEM((8 + tq, CONV_DIM), F32),
            pltpu.VMEM((D_STATE, SSD_D_INNER), F32),
            pltpu.VMEM((2, CHUNK + tq, KV_DIM), BF16),
            pltpu.VMEM((2, CHUNK + tq, KV_DIM), BF16),
            pltpu.VMEM((2, N_HEADS, WINDOW, 2 * WINDOW), F32),
            pltpu.VMEM((tq, D_MODEL), BF16),
            pltpu.VMEM((tq, D_MODEL), F32),
            pltpu.VMEM((tq, IN_COLS), F32),
        ],
        compiler_params=_params(("arbitrary",)),
        name="prompt_layer",
    )(x3, mk_b, mv_b, *full, relb, sinks)


def _sample_mixer_body(z_ref, xbc_ref, q_ref, k_ref, v_ref, dt_ref,
                       sconv_ref, sssm_ref, ck_ref, cv_ref,
                       cw_ref, cb_ref, dtb_ref, alog_ref, dskip_ref, gn_ref,
                       tcum_ref, tseq_ref, expand_ref, bkt_c_ref, bkt_n_ref, relb_ref, sink_ref,
                       mix_ref, conv_out, ssm_out, k_out, v_out,
                       xpad, bias_c, bias_n, *, dec_seq):
    step = pl.program_id(0)
    n_seq = SAMPLE_SEQS
    rows = n_seq * dec_seq

    @pl.when(step == 0)
    def _():
        _build_bias(bias_c, lambda i: bkt_c_ref[...], relb_ref, 1)
        _build_bias(bias_n, lambda i: bkt_n_ref[...], relb_ref, 1)

    xbc = xbc_ref[...]
    xpad[:, 8:8 + dec_seq, :] = xbc.reshape(n_seq, dec_seq, CONV_DIM)
    for r in range(CONV_K - 1):
        xpad[:, 5 + r, :] = sconv_ref[r]
    taps = [xpad[:, 5 + k:5 + k + dec_seq, :].reshape(rows, CONV_DIM) for k in range(CONV_K - 1)] + [xbc]
    conv = _conv_taps(cw_ref, cb_ref, taps)
    for r in range(CONV_K - 1):
        conv_out[r] = xpad[:, 5 + dec_seq + r, :]

    row = lax.broadcasted_iota(jnp.int32, (rows, rows), 0)
    col = lax.broadcasted_iota(jnp.int32, (rows, rows), 1)
    tseq = tseq_ref[...]
    same_seq = tseq > 0
    causal = same_seq & (col <= row)
    xs, bm, cm, cs, cs_t, (dt_x, dend_x, ecs_x, seqdec_x) = _ssd_prepare(
        conv, dt_ref[...], dtb_ref, alog_ref, tcum_ref[...],
        lambda cs_, a_: _sel_left(tseq, a_), lambda total: [jnp.exp(total)], expand_ref[...])
    xdt = xs * dt_x
    bb = bm.astype(BF16)
    cb = cm.astype(BF16)
    seq_of_row = lax.broadcasted_iota(jnp.int32, (rows, 1), 0) // dec_seq
    seq_of_lane = lax.broadcasted_iota(jnp.int32, (1, rows), 1) // dec_seq
    xde_t = (xdt * dend_x).T
    y_off = [None] * SSD_GROUPS
    for i in range(n_seq):
        st = sssm_ref[i]
        dec = jnp.concatenate(
            [jnp.broadcast_to(seqdec_x[i * dec_seq:i * dec_seq + 1, h * SSD_HEAD_DIM:h * SSD_HEAD_DIM + 1],
                              (SSD_HEAD_DIM, D_STATE)) for h in range(SSD_HEADS)], axis=0)
        new_parts = []
        for g in range(SSD_GROUPS):
            gs = slice(g * GROUP_W, (g + 1) * GROUP_W)
            ns = slice(g * D_STATE, (g + 1) * D_STATE)
            c_i = jnp.where(seq_of_row == i, cm[:, ns], 0.0).astype(BF16)
            part = _dot_nt(c_i, st[gs].astype(BF16))
            y_off[g] = part if y_off[g] is None else y_off[g] + part
            x_i = jnp.where(seq_of_lane == i, xde_t[gs], 0.0).astype(BF16)
            new_parts.append(dec[gs] * st[gs] + _dot(x_i, bb[:, ns]))
        ssm_out[i] = jnp.concatenate(new_parts, axis=0)
    y_parts = []
    for g in range(SSD_GROUPS):
        gs = slice(g * GROUP_W, (g + 1) * GROUP_W)
        ns = slice(g * D_STATE, (g + 1) * D_STATE)
        cb_g = _dot_nt(cb[:, ns], bb[:, ns])
        y_parts.append(_ssd_diag(cs, cs_t, cb_g, xdt, causal, g) + y_off[g] * ecs_x[:, gs])
    y = jnp.concatenate(y_parts, axis=-1) + dskip_ref[...] * xs
    mix_ref[:, 0:SSD_D_INNER] = _gated_norm(y, z_ref[...], gn_ref).astype(BF16)

    lo = _lane_lo()
    k_new = k_ref[...]
    v_new = v_ref[...]
    k_var = [k_new.astype(BF16), pltpu.roll(k_new, HALF, 1).astype(BF16)]
    v_new_r = pltpu.roll(v_new, HALF, 1)
    v_dup = [jnp.where(lo, v_new, v_new_r).astype(BF16), jnp.where(lo, v_new_r, v_new).astype(BF16)]
    qf = q_ref[...].astype(F32)
    q_masked = []
    s_new = []
    for h in range(N_HEADS):
        pair, par = h // 2, h % 2
        j = h // Q_PER_KV
        q_pair = qf[:, pair * LANES:(pair + 1) * LANES]
        qm = jnp.where(lo, q_pair, 0.0) if par == 0 else jnp.where(lo, 0.0, q_pair)
        q_masked.append(qm)
        s_new.append(_dot_nt(qm.astype(BF16), k_var[(j + par) % 2]) + bias_n[0, h])
    stack_rows = lax.broadcasted_iota(jnp.int32, (Q_PER_KV * dec_seq, 1), 0) // dec_seq
    k_new_t = k_new.T
    v_new_t = v_new.T
    keep = WINDOW - dec_seq
    old_lane = lax.broadcasted_iota(jnp.int32, (1, WINDOW), 1) < keep
    grp = Q_PER_KV * dec_seq
    sc_parts, sn_parts, sink_parts = [], [], []
    for i in range(n_seq):
        rs = slice(i * dec_seq, (i + 1) * dec_seq)
        kc_t = ck_ref[i]
        for j in range(N_KV_HEADS):
            heads = range(j * Q_PER_KV, (j + 1) * Q_PER_KV)
            cj = slice(j * ATTN_HEAD_DIM, (j + 1) * ATTN_HEAD_DIM)
            kdup_t = jnp.concatenate([kc_t[cj], kc_t[cj]], axis=0).astype(BF16)
            qs = jnp.concatenate([q_masked[h][rs] for h in heads], axis=0).astype(BF16)
            sc_parts.append(_dot(qs, kdup_t) + jnp.concatenate([bias_c[0, h] for h in heads], axis=0))
            sn_parts.append(jnp.concatenate([s_new[h][rs] for h in heads], axis=0))
            if i == 0:
                sink_j = jnp.zeros((grp, 1), F32)
                for hh, h in enumerate(heads):
                    sink_j = jnp.where(stack_rows == hh, sink_ref[h], sink_j)
                sink_parts.append(sink_j)
    sc = jnp.concatenate(sc_parts, axis=0)
    sn = jnp.concatenate(sn_parts, axis=0)
    sink = jnp.concatenate(sink_parts * n_seq, axis=0)
    m = jnp.maximum(jnp.maximum(jnp.max(sc, axis=-1, keepdims=True), jnp.max(sn, axis=-1, keepdims=True)), sink)
    pc = jnp.exp(sc - m)
    pn = jnp.exp(sn - m)
    rdenom = 1.0 / (jnp.sum(pc, axis=-1, keepdims=True) + jnp.sum(pn, axis=-1, keepdims=True) + jnp.exp(sink - m))
    pc = pc.astype(BF16)
    pn = pn.astype(BF16)
    att_rows = []
    for i in range(n_seq):
        vc_t = cv_ref[i]
        pieces = []
        for j in range(N_KV_HEADS):
            cj = slice(j * ATTN_HEAD_DIM, (j + 1) * ATTN_HEAD_DIM)
            gr = slice((i * N_KV_HEADS + j) * grp, (i * N_KV_HEADS + j + 1) * grp)
            vdup_t = jnp.concatenate([vc_t[cj], vc_t[cj]], axis=0).astype(BF16)
            o = (_dot_nt(pc[gr], vdup_t) + _dot(pn[gr], v_dup[j])) * rdenom[gr]
            for pr in range(Q_PER_KV // 2):
                even = o[(2 * pr) * dec_seq:(2 * pr + 1) * dec_seq]
                odd = o[(2 * pr + 1) * dec_seq:(2 * pr + 2) * dec_seq]
                pieces.append(jnp.where(lo, even, odd))
        att_rows.append(jnp.concatenate(pieces, axis=-1))
        new_shift = (keep - i * dec_seq) % WINDOW
        k_out[i] = jnp.where(old_lane, pltpu.roll(ck_ref[i], keep, 1), pltpu.roll(k_new_t, new_shift, 1))
        v_out[i] = jnp.where(old_lane, pltpu.roll(vc_t, keep, 1), pltpu.roll(v_new_t, new_shift, 1))
    mix_ref[:, SSD_D_INNER:] = jnp.concatenate(att_rows, axis=0).astype(BF16)


def _sample_mixer(z, xbc, q, k, v, dt, sconv, sssm, ck, cv, small, consts, n_seq_total, dec_seq):
    cw, cb, dtb, alog, dskip_x, gn, relb, sinks = small
    tcum, tseq, expand, bkt_c, bkt_n = consts
    rows = SAMPLE_SEQS * dec_seq
    tok = lambda w: pl.BlockSpec((rows, w), lambda i: (i, 0))
    per_s = lambda s: pl.BlockSpec((SAMPLE_SEQS,) + s, lambda i: (i,) + (0,) * len(s))
    out_shape = [
        jax.ShapeDtypeStruct((n_seq_total * dec_seq, D_MODEL), BF16),
        jax.ShapeDtypeStruct((CONV_K - 1, n_seq_total, CONV_DIM), F32),
        jax.ShapeDtypeStruct((n_seq_total, SSD_D_INNER, D_STATE), F32),
        jax.ShapeDtypeStruct((n_seq_total, WINDOW, KV_DIM), F32),
        jax.ShapeDtypeStruct((n_seq_total, WINDOW, KV_DIM), F32),
    ]
    conv_spec = pl.BlockSpec((CONV_K - 1, SAMPLE_SEQS, CONV_DIM), lambda i: (0, i, 0))
    return pl.pallas_call(
        functools.partial(_sample_mixer_body, dec_seq=dec_seq),
        grid=(n_seq_total // SAMPLE_SEQS,),
        in_specs=[tok(SSD_D_INNER), tok(CONV_DIM), tok(ATTN_DIM), tok(KV_DIM), tok(KV_DIM), tok(DT_PAD),
                  conv_spec, per_s((SSD_D_INNER, D_STATE)),
                  per_s((WINDOW, KV_DIM)), per_s((WINDOW, KV_DIM)),
                  _full_spec(cw.shape), _full_spec(cb.shape), _full_spec(dtb.shape), _full_spec(alog.shape),
                  _full_spec(dskip_x.shape), _full_spec(gn.shape),
                  _full_spec(tcum.shape), _full_spec(tseq.shape), _full_spec(expand.shape),
                  _full_spec(bkt_c.shape), _full_spec(bkt_n.shape), _smem_spec(), _smem_spec()],
        out_specs=[tok(D_MODEL), conv_spec, per_s((SSD_D_INNER, D_STATE)),
                   per_s((WINDOW, KV_DIM)), per_s((WINDOW, KV_DIM))],
        out_shape=out_shape,
        scratch_shapes=[
            pltpu.VMEM((SAMPLE_SEQS, 8 + dec_seq, CONV_DIM), F32),
            pltpu.VMEM((1, N_HEADS, dec_seq, WINDOW), F32),
            pltpu.VMEM((1, N_HEADS, rows, rows), F32),
        ],
        compiler_params=_params(("arbitrary",)),
        name="sample_mixer",
    )(z, xbc, q, k, v, dt, sconv, sssm, ck, cv, cw, cb, dtb, alog, dskip_x, gn,
      tcum, tseq, expand, bkt_c, bkt_n, relb, sinks)


def _post1_body(x_ref, mix_ref, wout_ref, gc_ref, wcq_ref, h_ref, qc_ref):
    h = x_ref[...] + _dot(mix_ref[...], wout_ref[...])
    h_ref[...] = h
    qc_ref[...] = _dot(_rms(h, gc_ref[...]).astype(BF16), wcq_ref[...]).astype(BF16)


def _post1(x2, mix, w_out, g_cross, w_cq, tm):
    t = x2.shape[0]
    row = lambda w: pl.BlockSpec((tm, w), lambda i: (i, 0))
    return pl.pallas_call(
        _post1_body,
        grid=(t // tm,),
        in_specs=[row(D_MODEL), row(D_MODEL), _full_spec(w_out.shape), _full_spec(g_cross.shape),
                  _full_spec(w_cq.shape)],
        out_specs=[row(D_MODEL), row(CA_DIM)],
        out_shape=[jax.ShapeDtypeStruct((t, D_MODEL), F32), jax.ShapeDtypeStruct((t, CA_DIM), BF16)],
        compiler_params=_params(("parallel",)),
        name="out_proj",
    )(x2, mix, w_out, g_cross, w_cq)


def _mem_kv_body(mem_ref, g_ref, wk_ref, wv_ref, k_ref, v_ref, kb_ref, vb_ref):
    mn = _rms(mem_ref[...], g_ref[...]).astype(BF16)
    k = _dot(mn, wk_ref[...])
    v = _dot(mn, wv_ref[...])
    rows = k.shape[0]
    for h in range(CA_HEADS):
        hs = slice(h * CA_HEAD_DIM, (h + 1) * CA_HEAD_DIM)
        k_ref[pl.ds(h, rows, stride=CA_HEADS), :] = k[:, hs]
        v_ref[pl.ds(h, rows, stride=CA_HEADS), :] = v[:, hs]
    kb_ref[...] = k.astype(BF16)
    vb_ref[...] = v.astype(BF16)


def _mem_kv(mem2, g_mem, w_ck, w_cv, tm):
    t = mem2.shape[0]
    row = lambda w: pl.BlockSpec((tm, w), lambda i: (i, 0))
    return pl.pallas_call(
        _mem_kv_body,
        grid=(t // tm,),
        in_specs=[row(D_MODEL), _full_spec(g_mem.shape), _full_spec(w_ck.shape), _full_spec(w_cv.shape)],
        out_specs=[pl.BlockSpec((tm * CA_HEADS, CA_HEAD_DIM), lambda i: (i, 0))] * 2 + [row(CA_DIM)] * 2,
        out_shape=([jax.ShapeDtypeStruct((t * CA_HEADS, CA_HEAD_DIM), F32)] * 2
                   + [jax.ShapeDtypeStruct((t, CA_DIM), BF16)] * 2),
        compiler_params=_params(("parallel",)),
        name="mem_kv",
    )(mem2, g_mem, w_ck, w_cv)


def _cross_sample_body(q_ref, k_ref, v_ref, o_ref, *, n_seq, dec_seq):
    q = q_ref[...].astype(F32)
    n_keys = N_MEM * CA_HEADS
    col_head = lax.broadcasted_iota(jnp.int32, (1, n_keys), 1) & (CA_HEADS - 1)
    row_head = lax.broadcasted_iota(jnp.int32, (CA_HEADS * dec_seq, 1), 0) // dec_seq
    own = col_head == row_head
    rows = []
    for i in range(n_seq):
        qi = q[i * dec_seq:(i + 1) * dec_seq]
        qs = jnp.concatenate([qi[:, h * CA_HEAD_DIM:(h + 1) * CA_HEAD_DIM] for h in range(CA_HEADS)], axis=0)
        s = _dot_nt(qs.astype(BF16), k_ref[i].astype(BF16)) * (CA_HEAD_DIM ** -0.5)
        s = jnp.where(own, s, NEG)
        m = jnp.max(s, axis=-1, keepdims=True)
        p = jnp.exp(s - m)
        o = _dot(p.astype(BF16), v_ref[i].astype(BF16)) / jnp.sum(p, axis=-1, keepdims=True)
        rows.append(jnp.concatenate([o[h * dec_seq:(h + 1) * dec_seq] for h in range(CA_HEADS)], axis=-1))
    o_ref[...] = jnp.concatenate(rows, axis=0).astype(BF16)


def _cross_sample(qc, ck, cv, n_seq_total, dec_seq, n_seq):
    rows = n_seq * dec_seq
    tok = pl.BlockSpec((rows, CA_DIM), lambda i: (i, 0))
    mem = pl.BlockSpec((n_seq, N_MEM * CA_HEADS, CA_HEAD_DIM), lambda i: (i, 0, 0))
    return pl.pallas_call(
        functools.partial(_cross_sample_body, n_seq=n_seq, dec_seq=dec_seq),
        grid=(n_seq_total // n_seq,),
        in_specs=[tok, mem, mem],
        out_specs=tok,
        out_shape=jax.ShapeDtypeStruct((n_seq_total * dec_seq, CA_DIM), BF16),
        compiler_params=_params(("parallel",)),
        name="cross_sample",
    )(qc, ck, cv)


def _post2_body(h_ref, o_ref, wco_ref, gf_ref, wg_ref, wu_ref, wd_ref, gfin_ref, y_ref):
    h = h_ref[...] + _dot(o_ref[...], wco_ref[...])
    hn = _rms(h, gf_ref[...]).astype(BF16)
    acc = h
    for lo, hi in FF_SPLITS:
        act = _silu(_dot(hn, wg_ref[:, lo:hi])) * _dot(hn, wu_ref[:, lo:hi])
        acc = acc + _dot(act.astype(BF16), wd_ref[lo:hi, :])
    y_ref[...] = _rms(acc, gfin_ref[...])


def _post2(h1, o, w_co, g_ffn, w_gate, w_up, w_down, g_final, tm):
    t = h1.shape[0]
    row = lambda w: pl.BlockSpec((tm, w), lambda i: (i, 0))
    return pl.pallas_call(
        _post2_body,
        grid=(t // tm,),
        in_specs=[row(D_MODEL), row(CA_DIM), _full_spec(w_co.shape), _full_spec(g_ffn.shape),
                  _full_spec(w_gate.shape), _full_spec(w_up.shape), _full_spec(w_down.shape),
                  _full_spec(g_final.shape)],
        out_specs=row(D_MODEL),
        out_shape=jax.ShapeDtypeStruct((t, D_MODEL), F32),
        compiler_params=_params(("parallel",)),
        name="ffn",
    )(h1, o, w_co, g_ffn, w_gate, w_up, w_down, g_final)


def _prompt_consts():
    qi = np.arange(WINDOW)[:, None]
    ji = np.arange(2 * WINDOW)[None, :]
    dist = qi + WINDOW - ji
    inband = (dist >= 0) & (dist < WINDOW)
    bucket = np.where(inband, _t5_bucket_np(dist), -1)
    first = np.where(ji >= WINDOW, bucket, -1)
    buckets = np.stack([first, bucket]).astype(np.int32)
    return (jnp.asarray(_tril_np(CHUNK), BF16), jnp.asarray(buckets))


def _sample_consts(dec_seq, cache_len):
    rows = SAMPLE_SEQS * dec_seq
    r = np.arange(rows)
    same = (r[:, None] // dec_seq) == (r[None, :] // dec_seq)
    tcum = (same & (r[None, :] <= r[:, None])).astype(np.float32)
    tseq = same.astype(np.float32)
    t = np.arange(dec_seq)[:, None]
    j = np.arange(cache_len)[None, :]
    dist_c = t + cache_len - j
    bkt_c = np.where((dist_c >= 0) & (dist_c < WINDOW), _t5_bucket_np(dist_c), -1).astype(np.int32)
    dist_n = (r[:, None] % dec_seq) - (r[None, :] % dec_seq)
    ok = same & (dist_n >= 0) & (dist_n < WINDOW)
    bkt_n = np.where(ok, _t5_bucket_np(dist_n), -1).astype(np.int32)
    return (jnp.asarray(tcum, BF16), jnp.asarray(tseq, BF16), jnp.asarray(_expand_np(), BF16),
            jnp.asarray(bkt_c), jnp.asarray(bkt_n))


def _pick_tile(t, pref):
    tm = min(t, pref)
    while t % tm:
        tm //= 2
    return tm


def kernel(x_prompt, x_sample, mem_prompt, state_conv, state_ssm, cache_swa_k, cache_swa_v, cache_mem_k, cache_mem_v, rel_bias, g_mix, w_in, conv_w, conv_b, dt_bias, a_log, d_skip, g_ssd_norm, sinks, w_out, g_cross, g_mem, w_cq, w_ck, w_cv, w_co, g_ffn, w_gate, w_up, w_down, g_final):
    assert g_mix.shape[0] == 1, "single-layer trunk"
    batch, seq, _ = x_prompt.shape
    n_dec, dec_seq, _ = x_sample.shape
    cache_len = cache_swa_k.shape[2]
    assert seq % CHUNK == 0 and cache_len == WINDOW and n_dec % SAMPLE_SEQS == 0 and dec_seq == 8

    wi = w_in[0]
    s1 = SSD_D_INNER + CONV_DIM
    s2 = s1 + SSD_HEADS
    w_in_r = jnp.concatenate(
        [wi[:, :s1], wi[:, s2:], jnp.pad(wi[:, s1:s2], ((0, 0), (0, DT_PAD - SSD_HEADS)))], axis=1).astype(BF16)
    row = lambda a: a.reshape(1, -1).astype(F32)
    pad_h = lambda a: jnp.pad(a.reshape(1, -1).astype(F32), ((0, 0), (0, DT_PAD - SSD_HEADS)))
    small = (conv_w[0].astype(F32), row(conv_b[0]), pad_h(dt_bias[0]), pad_h(a_log[0]),
             jnp.repeat(d_skip[0].astype(F32), SSD_HEAD_DIM).reshape(1, -1), row(g_ssd_norm[0]),
             rel_bias.astype(F32), sinks[0].astype(F32))
    bf = lambda w: w[0].astype(BF16)
    w_out_b, w_cq_b, w_ck_b, w_cv_b, w_co_b = bf(w_out), bf(w_cq), bf(w_ck), bf(w_cv), bf(w_co)
    w_gate_b, w_up_b, w_down_b = bf(w_gate), bf(w_up), bf(w_down)
    g_mix_r, g_cross_r, g_mem_r, g_ffn_r, g_fin_r = row(g_mix[0]), row(g_cross[0]), row(g_mem[0]), row(g_ffn[0]), row(g_final)

    n_chunks = 2 if seq % (2 * CHUNK) == 0 else 1
    mem2 = mem_prompt.reshape(batch * N_MEM, D_MODEL)
    mk, mv, mk_b, mv_b = _mem_kv(mem2, g_mem_r, w_ck_b, w_cv_b, _pick_tile(batch * N_MEM, 512))
    dense = (g_mix_r, w_in_r, w_out_b, g_cross_r, w_cq_b, w_co_b, g_ffn_r, w_gate_b, w_up_b, w_down_b, g_fin_r)
    y_prompt, p_conv, p_ssm, p_k, p_v = _prompt_layer(
        x_prompt, mk_b.reshape(batch, N_MEM, CA_DIM), mv_b.reshape(batch, N_MEM, CA_DIM),
        dense, small, _prompt_consts(), n_chunks)

    def channel_major(cache):
        return jnp.transpose(cache, (0, 2, 3, 1)).reshape(n_dec, KV_DIM, cache_len)

    def position_major(cache_t):
        return jnp.transpose(cache_t.reshape(n_dec, N_KV_HEADS, ATTN_HEAD_DIM, cache_len), (0, 3, 1, 2))

    ts = n_dec * dec_seq
    xs2 = x_sample.reshape(ts, D_MODEL)
    tm_s = _pick_tile(ts, 512)
    z, xbc, q, k, v, dt = _in_proj(xs2, g_mix_r, w_in_r, tm_s)
    mix_s, s_conv, s_ssm, s_k, s_v = _sample_mixer(
        z, xbc, q, k, v, dt, jnp.transpose(state_conv[0], (1, 0, 2)),
        state_ssm[0].reshape(n_dec, SSD_D_INNER, D_STATE),
        channel_major(cache_swa_k[0]), channel_major(cache_swa_v[0]),
        small, _sample_consts(dec_seq, cache_len), n_dec, dec_seq)
    h1s, qcs = _post1(xs2, mix_s, w_out_b, g_cross_r, w_cq_b, tm_s)
    os_ = _cross_sample(qcs, cache_mem_k[0].reshape(n_dec, N_MEM * CA_HEADS, CA_HEAD_DIM),
                        cache_mem_v[0].reshape(n_dec, N_MEM * CA_HEADS, CA_HEAD_DIM), n_dec, dec_seq, SAMPLE_SEQS)
    y_sample = _post2(h1s, os_, w_co_b, g_ffn_r, w_gate_b, w_up_b, w_down_b, g_fin_r, tm_s)

    return (y_prompt.reshape(batch, seq, D_MODEL), y_sample.reshape(n_dec, dec_seq, D_MODEL),
            p_conv[None], p_ssm.reshape(1, batch, SSD_HEADS, SSD_HEAD_DIM, D_STATE),
            p_k.reshape(1, batch, WINDOW, N_KV_HEADS, ATTN_HEAD_DIM),
            p_v.reshape(1, batch, WINDOW, N_KV_HEADS, ATTN_HEAD_DIM),
            mk.reshape(1, batch, N_MEM, CA_HEADS, CA_HEAD_DIM), mv.reshape(1, batch, N_MEM, CA_HEADS, CA_HEAD_DIM),
            jnp.transpose(s_conv, (1, 0, 2))[None], s_ssm.reshape(1, n_dec, SSD_HEADS, SSD_HEAD_DIM, D_STATE),
            position_major(s_k)[None], position_major(s_v)[None])
```

```python
import functools
import math

import numpy as np
import jax
import jax.numpy as jnp
from jax import lax
from jax.experimental import pallas as pl
from jax.experimental.pallas import tpu as pltpu

F32 = jnp.float32
BF16 = jnp.bfloat16

D_MODEL = 1024
SSD_D_INNER = 512
SSD_HEAD_DIM = 64
SSD_HEADS = 8
SSD_GROUPS = 2
GROUP_W = SSD_D_INNER // SSD_GROUPS
D_STATE = 128
CONV_K = 4
CONV_DIM = SSD_D_INNER + 2 * SSD_GROUPS * D_STATE
CHUNK = 128
ATTN_DIM = 512
ATTN_HEAD_DIM = 64
N_HEADS = 8
N_KV_HEADS = 2
Q_PER_KV = N_HEADS // N_KV_HEADS
KV_DIM = N_KV_HEADS * ATTN_HEAD_DIM
WINDOW = 128
N_BUCKETS = 32
MAX_EXACT = N_BUCKETS // 2
MAX_DISTANCE = 128
N_MEM = 256
CA_HEADS = 4
CA_HEAD_DIM = 128
CA_DIM = CA_HEADS * CA_HEAD_DIM
D_FF = 2816
EPS = 1e-6

LANES = 128
HALF = LANES // 2
DT_PAD = LANES
COL_Z = 0
COL_XBC = COL_Z + SSD_D_INNER
COL_Q = COL_XBC + CONV_DIM
COL_K = COL_Q + ATTN_DIM
COL_V = COL_K + KV_DIM
COL_DT = COL_V + KV_DIM
IN_COLS = COL_DT + DT_PAD
NEG = -1e30
SAMPLE_SEQS = 16
VMEM_LIMIT = 56 * 1024 * 1024
FF_SPLITS = ((0, 1024), (1024, 2048), (2048, D_FF))
SECOND_STREAM_LEAD = 0.5
FF_PIECES = tuple((lo, min(lo + 512, D_FF)) for lo in range(0, D_FF, 512))
PROJ_PIECES = ((COL_Z, COL_XBC), (COL_XBC, COL_XBC + 512), (COL_XBC + 512, COL_Q), (COL_Q, COL_K), (COL_K, IN_COLS))


def _rms(x, g):
    return x * lax.rsqrt(jnp.mean(x * x, axis=-1, keepdims=True) + EPS) * g


def _silu(x):
    return x * jax.nn.sigmoid(x)


def _softplus(x):
    return jnp.maximum(x, 0.0) + jnp.log1p(jnp.exp(-jnp.abs(x)))


def _dot(a, b):
    return jnp.dot(a, b, preferred_element_type=F32)


def _dot_nt(a, b):
    return lax.dot_general(a, b, (((1,), (1,)), ((), ())), preferred_element_type=F32)


def _split3(a):
    hi = a.astype(BF16)
    r = a - hi.astype(F32)
    mid = r.astype(BF16)
    lo = (r - mid.astype(F32)).astype(BF16)
    return hi, mid, lo


def _sel_left(t01, a):
    hi, mid, lo = _split3(a)
    return _dot(t01, hi) + _dot(t01, mid) + _dot(t01, lo)


def _sel_right(a, e01):
    hi, mid, lo = _split3(a)
    return _dot(hi, e01) + _dot(mid, e01) + _dot(lo, e01)


def _lane_lo():
    return lax.broadcasted_iota(jnp.int32, (1, LANES), 1) < HALF


def _t5_bucket_np(dist):
    n = np.maximum(dist, 0)
    ratio = np.log(np.maximum(n, 1).astype(np.float32) / np.float32(MAX_EXACT))
    large = MAX_EXACT + (ratio / np.float32(math.log(MAX_DISTANCE / MAX_EXACT))
                         * np.float32(N_BUCKETS - MAX_EXACT)).astype(np.int32)
    large = np.minimum(large, N_BUCKETS - 1)
    return np.where(n < MAX_EXACT, n, large).astype(np.int32)


def _tril_np(n):
    return np.tril(np.ones((n, n), np.float32))


def _expand_np():
    e = np.zeros((LANES, SSD_D_INNER), np.float32)
    for h in range(SSD_HEADS):
        e[h, h * SSD_HEAD_DIM:(h + 1) * SSD_HEAD_DIM] = 1.0
    return e


def _full_spec(shape):
    nd = len(shape)
    return pl.BlockSpec(shape, lambda *_: (0,) * nd, pipeline_mode=pl.Buffered(1))


def _smem_spec():
    return pl.BlockSpec(memory_space=pltpu.SMEM)


def _params(sem):
    return pltpu.CompilerParams(dimension_semantics=sem, vmem_limit_bytes=VMEM_LIMIT)


def _in_proj_body(x_ref, g_ref, w_ref, z_ref, xbc_ref, q_ref, k_ref, v_ref, dt_ref):
    xn = _rms(x_ref[...], g_ref[...]).astype(BF16)

    def seg(lo, hi):
        return _dot(xn, w_ref[:, lo:hi])

    z_ref[...] = seg(COL_Z, COL_XBC)
    xbc_ref[...] = seg(COL_XBC, COL_Q)
    q_ref[...] = (seg(COL_Q, COL_K) * (ATTN_HEAD_DIM ** -0.5)).astype(BF16)
    k_ref[...] = seg(COL_K, COL_V)
    v_ref[...] = seg(COL_V, COL_DT)
    dt_ref[...] = seg(COL_DT, IN_COLS)


def _in_proj(x2, g_mix, w_in_r, tm):
    t = x2.shape[0]
    row = lambda w: pl.BlockSpec((tm, w), lambda i: (i, 0))
    outs = [(SSD_D_INNER, F32), (CONV_DIM, F32), (ATTN_DIM, BF16), (KV_DIM, F32), (KV_DIM, F32), (DT_PAD, F32)]
    return pl.pallas_call(
        _in_proj_body,
        grid=(t // tm,),
        in_specs=[row(D_MODEL), _full_spec((1, D_MODEL)), _full_spec((D_MODEL, IN_COLS))],
        out_specs=[row(w) for w, _ in outs],
        out_shape=[jax.ShapeDtypeStruct((t, w), d) for w, d in outs],
        compiler_params=_params(("parallel",)),
        name="in_proj",
    )(x2, g_mix, w_in_r)


def _conv_taps(cw_ref, cb_ref, taps):
    acc = cb_ref[...] + taps[0] * cw_ref[0:1, :]
    for k in range(1, CONV_K):
        acc = acc + taps[k] * cw_ref[k:k + 1, :]
    return _silu(acc)


def _ssd_prepare(conv, dt_raw, dtb_ref, alog_ref, tcum, total_fn, extra_fn, expand):
    xs = conv[:, :SSD_D_INNER]
    bm = conv[:, SSD_D_INNER:SSD_D_INNER + SSD_GROUPS * D_STATE]
    cm = conv[:, SSD_D_INNER + SSD_GROUPS * D_STATE:]
    dt = _softplus(dt_raw + dtb_ref[...])
    a = dt * (-jnp.exp(alog_ref[...]))
    cs = _sel_left(tcum, a)
    total = total_fn(cs, a)
    pieces = [dt, jnp.exp(total - cs), jnp.exp(cs)] + extra_fn(total)
    rows = cs.shape[0]
    ex = _sel_right(jnp.concatenate(pieces, axis=0), expand)
    ex = [ex[i * rows:(i + 1) * rows] for i in range(len(pieces))]
    return xs, bm, cm, cs, cs.T, ex


def _ssd_diag(cs, cs_t, cb_g, xdt, mask, g):
    lo = _lane_lo()
    out = []
    for pr in range(2):
        h0 = g * 4 + 2 * pr
        xp = xdt[:, (h0 // 2) * LANES:(h0 // 2 + 1) * LANES]
        x_lo = jnp.where(lo, xp, 0.0).astype(BF16)
        x_hi = jnp.where(lo, 0.0, xp).astype(BF16)
        acc = None
        for h, xh in ((h0, x_lo), (h0 + 1, x_hi)):
            diff = cs[:, h:h + 1] - cs_t[h:h + 1, :]
            decay = jnp.exp(jnp.where(mask, diff, -jnp.inf))
            part = _dot((cb_g * decay).astype(BF16), xh)
            acc = part if acc is None else acc + part
        out.append(acc)
    return jnp.concatenate(out, axis=-1)


def _gated_norm(y, z, gn_ref):
    yf = y * _silu(z)
    parts = []
    for g in range(SSD_GROUPS):
        yg = yf[:, g * GROUP_W:(g + 1) * GROUP_W]
        parts.append(yg * lax.rsqrt(jnp.mean(yg * yg, axis=-1, keepdims=True) + EPS))
    return jnp.concatenate(parts, axis=-1) * gn_ref[...]


def _build_bias(bias_ref, bucket_of, relb_ref, n_tables):
    for i in range(n_tables):
        for h in range(N_HEADS):
            bias_ref[i, h] = jnp.full(bias_ref.shape[2:], NEG, F32)

    def body(t, carry):
        for i in range(n_tables):
            hit = bucket_of(i) == t
            for h in range(N_HEADS):
                bias_ref[i, h] = jnp.where(hit, relb_ref[t, h], bias_ref[i, h])
        return carry

    lax.fori_loop(0, N_BUCKETS, body, 0)


def _alternate(first, second):
    streams = [[0.0, 1.0, first], [0.0, SECOND_STREAM_LEAD, second]]
    while streams:
        entry = min(streams, key=lambda e: e[0])
        try:
            entry[0] += entry[1] * next(entry[2])
        except StopIteration:
            streams.remove(entry)


def _layer_step_body(x_ref, mk_ref, mv_ref, gmix_ref, win_ref, wout_ref, gc_ref, wcq_ref,
                     wco_ref, gf_ref, wg_ref, wu_ref, wd_ref, gfin_ref,
                     cw_ref, cb_ref, dtb_ref, alog_ref, dskip_ref, gn_ref,
                     tril_ref, bucket_ref, relb_ref, sink_ref,
                     y_ref, conv_out, ssm_out, k_out, v_out,
                     xpad, state_t, kbuf, vbuf, bias, mix, hbuf, proj, *, n_chunks, tiles_per_seq):
    step = pl.program_id(0)
    n_tiles = pl.num_programs(0) - 1
    c = lax.rem(jnp.minimum(step, n_tiles - 1), tiles_per_seq)
    last = tiles_per_seq - 1
    tq = n_chunks * CHUNK

    @pl.when(step == 0)
    def _():
        _build_bias(bias, lambda i: bucket_ref[i], relb_ref, 2)
        hbuf[...] = jnp.zeros_like(hbuf)
        mix[...] = jnp.zeros_like(mix)

    @pl.when(c == 0)
    def _():
        xpad[0:8, :] = jnp.zeros((8, CONV_DIM), F32)
        state_t[...] = jnp.zeros_like(state_t)
        kbuf[:, 0:CHUNK, :] = jnp.zeros((2, CHUNK, KV_DIM), BF16)
        vbuf[:, 0:CHUNK, :] = jnp.zeros((2, CHUNK, KV_DIM), BF16)

    def back():
        h1 = hbuf[...] + _dot(mix[...], wout_ref[...])
        yield 4
        qc = _dot(_rms(h1, gc_ref[...]).astype(BF16), wcq_ref[...]).astype(BF16)
        yield 2
        heads = []
        for h in range(CA_HEADS):
            hs = slice(h * CA_HEAD_DIM, (h + 1) * CA_HEAD_DIM)
            sc = _dot_nt(qc[:, hs], mk_ref[0, :, hs]) * (CA_HEAD_DIM ** -0.5)
            m = jnp.max(sc, axis=-1, keepdims=True)
            p = jnp.exp(sc - m)
            heads.append(_dot(p.astype(BF16), mv_ref[0, :, hs]) / jnp.sum(p, axis=-1, keepdims=True))
            yield 3
        o = jnp.concatenate(heads, axis=-1).astype(BF16)
        h2 = h1 + _dot(o, wco_ref[...])
        hn = _rms(h2, gf_ref[...]).astype(BF16)
        yield 4
        acc = h2
        for f_lo, f_hi in FF_PIECES:
            width = (f_hi - f_lo) / 512
            gate = _dot(hn, wg_ref[:, f_lo:f_hi])
            yield 5 * width
            act = (_silu(gate) * _dot(hn, wu_ref[:, f_lo:f_hi])).astype(BF16)
            yield 6 * width
            acc = acc + _dot(act, wd_ref[f_lo:f_hi, :])
            yield 5 * width
        y_ref[0] = _rms(acc, gfin_ref[...])

    def front():
        x = x_ref[0]
        xn = _rms(x, gmix_ref[...]).astype(BF16)
        for p_lo, p_hi in PROJ_PIECES:
            proj[:, p_lo:p_hi] = _dot(xn, win_ref[:, p_lo:p_hi])
            yield 4

        xpad[8:8 + tq, :] = proj[:, COL_XBC:COL_Q]
        conv_halves = []
        half_w = CONV_DIM // 2
        for hf in range(2):
            cs_ = slice(hf * half_w, (hf + 1) * half_w)
            acc = cb_ref[:, cs_] + xpad[5:5 + tq, cs_] * cw_ref[0:1, cs_]
            for kk in range(1, CONV_K):
                acc = acc + xpad[5 + kk:5 + kk + tq, cs_] * cw_ref[kk:kk + 1, cs_]
            conv_halves.append(_silu(acc))
            yield 6
        xpad[5:8, :] = xpad[5 + tq:8 + tq, :]
        xs_all = conv_halves[0]
        bc_all = conv_halves[1]

        k_new = proj[:, COL_K:COL_V]
        v_new = proj[:, COL_V:COL_DT]
        kbuf[0, CHUNK:, :] = k_new.astype(BF16)
        kbuf[1, CHUNK:, :] = pltpu.roll(k_new, HALF, 1).astype(BF16)
        vbuf[0, CHUNK:, :] = v_new.astype(BF16)
        vbuf[1, CHUNK:, :] = pltpu.roll(v_new, HALF, 1).astype(BF16)
        q = (proj[:, COL_Q:COL_K] * (ATTN_HEAD_DIM ** -0.5)).astype(BF16)
        yield 2

        row = lax.broadcasted_iota(jnp.int32, (CHUNK, CHUNK), 0)
        col = lax.broadcasted_iota(jnp.int32, (CHUNK, CHUNK), 1)
        causal = col <= row
        lo = _lane_lo()
        for ci in range(n_chunks):
            rs = slice(ci * CHUNK, (ci + 1) * CHUNK)
            xs = xs_all[rs]
            bm = bc_all[rs, 0:SSD_GROUPS * D_STATE]
            cm = bc_all[rs, SSD_GROUPS * D_STATE:]
            dt = _softplus(proj[rs, COL_DT:IN_COLS] + dtb_ref[...])
            a = dt * (-jnp.exp(alog_ref[...]))
            cs = _sel_left(tril_ref[...], a)
            cs_t = cs.T
            dt_t = dt.T
            total = cs[CHUNK - 1:CHUNK, :]
            ecs = jnp.exp(cs)
            w_end = dt * jnp.exp(total - cs)
            bb = bm.astype(BF16)
            cb = cm.astype(BF16)
            yield 5
            y_parts = []
            for g in range(SSD_GROUPS):
                gs = slice(g * GROUP_W, (g + 1) * GROUP_W)
                ns = slice(g * D_STATE, (g + 1) * D_STATE)
                cb_g = _dot_nt(cb[:, ns], bb[:, ns])
                y_diag, xw, ecs_g = [], [], []
                for pr in range(2):
                    h0 = g * 4 + 2 * pr
                    xp = xs[:, (h0 // 2) * LANES:(h0 // 2 + 1) * LANES]
                    ecs_g.append(jnp.where(lo, ecs[:, h0:h0 + 1], ecs[:, h0 + 1:h0 + 2]))
                    xw.append((xp * jnp.where(lo, w_end[:, h0:h0 + 1], w_end[:, h0 + 1:h0 + 2])).astype(BF16))
                    x2 = jnp.concatenate([jnp.where(lo, xp, 0.0), jnp.where(lo, 0.0, xp)], axis=0).astype(BF16)
                    gmat = []
                    for h in (h0, h0 + 1):
                        diff = cs[:, h:h + 1] - cs_t[h:h + 1, :]
                        decay = jnp.exp(jnp.where(causal, diff, -jnp.inf))
                        gmat.append((cb_g * decay * dt_t[h:h + 1, :]).astype(BF16))
                    y_diag.append(_dot(jnp.concatenate(gmat, axis=1), x2))
                ecs_x = jnp.concatenate(ecs_g, axis=-1)
                st_old = state_t[:, gs]
                y_off = _dot(cb[:, ns], st_old.astype(BF16)) * ecs_x
                y_parts.append(jnp.concatenate(y_diag, axis=-1) + y_off)
                b_t = bm[:, ns].T.astype(BF16)
                state_t[:, gs] = ecs_x[CHUNK - 1:CHUNK, :] * st_old + _dot(b_t, jnp.concatenate(xw, axis=-1))
                yield 9
            y = jnp.concatenate(y_parts, axis=-1) + dskip_ref[...] * xs
            mix[rs, 0:SSD_D_INNER] = _gated_norm(y, proj[rs, COL_Z:COL_XBC], gn_ref).astype(BF16)
            yield 3

            keys = slice(ci * CHUNK, (ci + 2) * CHUNK)
            table = jnp.minimum(c, 1) if ci == 0 else 1
            probs, denoms = {}, {}
            for variant in range(2):
                hv = [h for h in range(N_HEADS) if (h // Q_PER_KV + h % 2) % 2 == variant]
                qms = []
                for h in hv:
                    q_pair = q[rs, (h // 2) * LANES:(h // 2 + 1) * LANES]
                    qms.append(jnp.where(lo, q_pair, 0.0) if h % 2 == 0 else jnp.where(lo, 0.0, q_pair))
                sc_all = _dot_nt(jnp.concatenate(qms, axis=0).astype(BF16), kbuf[variant, keys, :])
                for i, h in enumerate(hv):
                    sc = sc_all[i * CHUNK:(i + 1) * CHUNK] + bias[table, h]
                    sink = sink_ref[h]
                    m = jnp.maximum(jnp.max(sc, axis=-1, keepdims=True), sink)
                    p = jnp.exp(sc - m)
                    denoms[h] = jnp.sum(p, axis=-1, keepdims=True) + jnp.exp(sink - m)
                    probs[h] = p.astype(BF16)
                yield 10
            parts = {}
            for variant in range(2):
                vv = vbuf[variant, keys, :]
                for par in range(2):
                    hv = [h for h in range(N_HEADS) if (h // Q_PER_KV + h % 2) % 2 == variant and h % 2 == par]
                    vm = jnp.where(lo, vv, 0.0) if par == 0 else jnp.where(lo, 0.0, vv)
                    o_all = _dot(jnp.concatenate([probs[h] for h in hv], axis=0), vm.astype(BF16))
                    for i, h in enumerate(hv):
                        parts[h] = o_all[i * CHUNK:(i + 1) * CHUNK] / denoms[h]
            for pair in range(N_HEADS // 2):
                mix[rs, SSD_D_INNER + pair * LANES:SSD_D_INNER + (pair + 1) * LANES] = (
                    parts[2 * pair] + parts[2 * pair + 1]).astype(BF16)
            yield 4
        kbuf[:, 0:CHUNK, :] = kbuf[:, tq:tq + CHUNK, :]
        vbuf[:, 0:CHUNK, :] = vbuf[:, tq:tq + CHUNK, :]

        hbuf[...] = x

    _alternate(back(), front())

    @pl.when((c == last) & (step < n_tiles))
    def _():
        conv_out[0] = xpad[5:8, :]
        ssm_out[0] = state_t[...].T
        k_out[0] = proj[tq - WINDOW:tq, COL_K:COL_V]
        v_out[0] = proj[tq - WINDOW:tq, COL_V:COL_DT]


def _prompt_layer(x3, mk_b, mv_b, dense, small, consts, n_chunks):
    cw, cb, dtb, alog, dskip_x, gn, relb, sinks = small
    tril, buckets = consts
    batch, seq, _ = x3.shape
    tq = n_chunks * CHUNK
    tps = seq // tq
    n_tiles = batch * tps
    front = lambda s: jnp.minimum(s, n_tiles - 1)
    back = lambda s: jnp.maximum(s - 1, 0)
    x_spec = pl.BlockSpec((1, tq, D_MODEL), lambda s: (front(s) // tps, front(s) % tps, 0))
    y_spec = pl.BlockSpec((1, tq, D_MODEL), lambda s: (back(s) // tps, back(s) % tps, 0))
    mem_spec = pl.BlockSpec((1, N_MEM, CA_DIM), lambda s: (back(s) // tps, 0, 0))
    per_b = lambda shp: pl.BlockSpec((1,) + shp, lambda s: (front(s) // tps,) + (0,) * len(shp))
    out_shape = [
        jax.ShapeDtypeStruct((batch, seq, D_MODEL), F32),
        jax.ShapeDtypeStruct((batch, CONV_K - 1, CONV_DIM), F32),
        jax.ShapeDtypeStruct((batch, SSD_D_INNER, D_STATE), F32),
        jax.ShapeDtypeStruct((batch, WINDOW, KV_DIM), F32),
        jax.ShapeDtypeStruct((batch, WINDOW, KV_DIM), F32),
    ]
    full = list(dense) + [cw, cb, dtb, alog, dskip_x, gn, tril, buckets]
    return pl.pallas_call(
        functools.partial(_layer_step_body, n_chunks=n_chunks, tiles_per_seq=tps),
        grid=(n_tiles + 1,),
        in_specs=[x_spec, mem_spec, mem_spec] + [_full_spec(a.shape) for a in full] + [_smem_spec(), _smem_spec()],
        out_specs=[y_spec, per_b((CONV_K - 1, CONV_DIM)), per_b((SSD_D_INNER, D_STATE)),
                   per_b((WINDOW, KV_DIM)), per_b((WINDOW, KV_DIM))],
        out_shape=out_shape,
        scratch_shapes=[
            pltpu.VMEM((8 + tq, CONV_DIM), F32),
            pltpu.VMEM((D_STATE, SSD_D_INNER), F32),
            pltpu.VMEM((2, CHUNK + tq, KV_DIM), BF16),
            pltpu.VMEM((2, CHUNK + tq, KV_DIM), BF16),
            pltpu.VMEM((2, N_HEADS, WINDOW, 2 * WINDOW), F32),
            pltpu.VMEM((tq, D_MODEL), BF16),
            pltpu.VMEM((tq, D_MODEL), F32),
            pltpu.VMEM((tq, IN_COLS), F32),
        ],
        compiler_params=_params(("arbitrary",)),
        name="prompt_layer",
    )(x3, mk_b, mv_b, *full, relb, sinks)


def _sample_mixer_body(z_ref, xbc_ref, q_ref, k_ref, v_ref, dt_ref,
                       sconv_ref, sssm_ref, ck_ref, cv_ref,
                       cw_ref, cb_ref, dtb_ref, alog_ref, dskip_ref, gn_ref,
                       tcum_ref, tseq_ref, expand_ref, bkt_c_ref, bkt_n_ref, relb_ref, sink_ref,
                       mix_ref, conv_out, ssm_out, k_out, v_out,
                       xpad, bias_c, bias_n, *, dec_seq):
    step = pl.program_id(0)
    n_seq = SAMPLE_SEQS
    rows = n_seq * dec_seq

    @pl.when(step == 0)
    def _():
        _build_bias(bias_c, lambda i: bkt_c_ref[...], relb_ref, 1)
        _build_bias(bias_n, lambda i: bkt_n_ref[...], relb_ref, 1)

    xbc = xbc_ref[...]
    xpad[:, 8:8 + dec_seq, :] = xbc.reshape(n_seq, dec_seq, CONV_DIM)
    for r in range(CONV_K - 1):
        xpad[:, 5 + r, :] = sconv_ref[r]
    taps = [xpad[:, 5 + k:5 + k + dec_seq, :].reshape(rows, CONV_DIM) for k in range(CONV_K - 1)] + [xbc]
    conv = _conv_taps(cw_ref, cb_ref, taps)
    for r in range(CONV_K - 1):
        conv_out[r] = xpad[:, 5 + dec_seq + r, :]

    row = lax.broadcasted_iota(jnp.int32, (rows, rows), 0)
    col = lax.broadcasted_iota(jnp.int32, (rows, rows), 1)
    tseq = tseq_ref[...]
    same_seq = tseq > 0
    causal = same_seq & (col <= row)
    xs, bm, cm, cs, cs_t, (dt_x, dend_x, ecs_x, seqdec_x) = _ssd_prepare(
        conv, dt_ref[...], dtb_ref, alog_ref, tcum_ref[...],
        lambda cs_, a_: _sel_left(tseq, a_), lambda total: [jnp.exp(total)], expand_ref[...])
    xdt = xs * dt_x
    bb = bm.astype(BF16)
    cb = cm.astype(BF16)
    seq_of_row = lax.broadcasted_iota(jnp.int32, (rows, 1), 0) // dec_seq
    seq_of_lane = lax.broadcasted_iota(jnp.int32, (1, rows), 1) // dec_seq
    xde_t = (xdt * dend_x).T
    y_off = [None] * SSD_GROUPS
    for i in range(n_seq):
        st = sssm_ref[i]
        dec = jnp.concatenate(
            [jnp.broadcast_to(seqdec_x[i * dec_seq:i * dec_seq + 1, h * SSD_HEAD_DIM:h * SSD_HEAD_DIM + 1],
                              (SSD_HEAD_DIM, D_STATE)) for h in range(SSD_HEADS)], axis=0)
        new_parts = []
        for g in range(SSD_GROUPS):
            gs = slice(g * GROUP_W, (g + 1) * GROUP_W)
            ns = slice(g * D_STATE, (g + 1) * D_STATE)
            c_i = jnp.where(seq_of_row == i, cm[:, ns], 0.0).astype(BF16)
            part = _dot_nt(c_i, st[gs].astype(BF16))
            y_off[g] = part if y_off[g] is None else y_off[g] + part
            x_i = jnp.where(seq_of_lane == i, xde_t[gs], 0.0).astype(BF16)
            new_parts.append(dec[gs] * st[gs] + _dot(x_i, bb[:, ns]))
        ssm_out[i] = jnp.concatenate(new_parts, axis=0)
    y_parts = []
    for g in range(SSD_GROUPS):
        gs = slice(g * GROUP_W, (g + 1) * GROUP_W)
        ns = slice(g * D_STATE, (g + 1) * D_STATE)
        cb_g = _dot_nt(cb[:, ns], bb[:, ns])
        y_parts.append(_ssd_diag(cs, cs_t, cb_g, xdt, causal, g) + y_off[g] * ecs_x[:, gs])
    y = jnp.concatenate(y_parts, axis=-1) + dskip_ref[...] * xs
    mix_ref[:, 0:SSD_D_INNER] = _gated_norm(y, z_ref[...], gn_ref).astype(BF16)

    lo = _lane_lo()
    k_new = k_ref[...]
    v_new = v_ref[...]
    k_var = [k_new.astype(BF16), pltpu.roll(k_new, HALF, 1).astype(BF16)]
    v_new_r = pltpu.roll(v_new, HALF, 1)
    v_dup = [jnp.where(lo, v_new, v_new_r).astype(BF16), jnp.where(lo, v_new_r, v_new).astype(BF16)]
    qf = q_ref[...].astype(F32)
    q_masked = []
    s_new = []
    for h in range(N_HEADS):
        pair, par = h // 2, h % 2
        j = h // Q_PER_KV
        q_pair = qf[:, pair * LANES:(pair + 1) * LANES]
        qm = jnp.where(lo, q_pair, 0.0) if par == 0 else jnp.where(lo, 0.0, q_pair)
        q_masked.append(qm)
        s_new.append(_dot_nt(qm.astype(BF16), k_var[(j + par) % 2]) + bias_n[0, h])
    stack_rows = lax.broadcasted_iota(jnp.int32, (Q_PER_KV * dec_seq, 1), 0) // dec_seq
    k_new_t = k_new.T
    v_new_t = v_new.T
    keep = WINDOW - dec_seq
    old_lane = lax.broadcasted_iota(jnp.int32, (1, WINDOW), 1) < keep
    grp = Q_PER_KV * dec_seq
    sc_parts, sn_parts, sink_parts = [], [], []
    for i in range(n_seq):
        rs = slice(i * dec_seq, (i + 1) * dec_seq)
        kc_t = ck_ref[i]
        for j in range(N_KV_HEADS):
            heads = range(j * Q_PER_KV, (j + 1) * Q_PER_KV)
            cj = slice(j * ATTN_HEAD_DIM, (j + 1) * ATTN_HEAD_DIM)
            kdup_t = jnp.concatenate([kc_t[cj], kc_t[cj]], axis=0).astype(BF16)
            qs = jnp.concatenate([q_masked[h][rs] for h in heads], axis=0).astype(BF16)
            sc_parts.append(_dot(qs, kdup_t) + jnp.concatenate([bias_c[0, h] for h in heads], axis=0))
            sn_parts.append(jnp.concatenate([s_new[h][rs] for h in heads], axis=0))
            if i == 0:
                sink_j = jnp.zeros((grp, 1), F32)
                for hh, h in enumerate(heads):
                    sink_j = jnp.where(stack_rows == hh, sink_ref[h], sink_j)
                sink_parts.append(sink_j)
    sc = jnp.concatenate(sc_parts, axis=0)
    sn = jnp.concatenate(sn_parts, axis=0)
    sink = jnp.concatenate(sink_parts * n_seq, axis=0)
    m = jnp.maximum(jnp.maximum(jnp.max(sc, axis=-1, keepdims=True), jnp.max(sn, axis=-1, keepdims=True)), sink)
    pc = jnp.exp(sc - m)
    pn = jnp.exp(sn - m)
    rdenom = 1.0 / (jnp.sum(pc, axis=-1, keepdims=True) + jnp.sum(pn, axis=-1, keepdims=True) + jnp.exp(sink - m))
    pc = pc.astype(BF16)
    pn = pn.astype(BF16)
    att_rows = []
    for i in range(n_seq):
        vc_t = cv_ref[i]
        pieces = []
        for j in range(N_KV_HEADS):
            cj = slice(j * ATTN_HEAD_DIM, (j + 1) * ATTN_HEAD_DIM)
            gr = slice((i * N_KV_HEADS + j) * grp, (i * N_KV_HEADS + j + 1) * grp)
            vdup_t = jnp.concatenate([vc_t[cj], vc_t[cj]], axis=0).astype(BF16)
            o = (_dot_nt(pc[gr], vdup_t) + _dot(pn[gr], v_dup[j])) * rdenom[gr]
            for pr in range(Q_PER_KV // 2):
                even = o[(2 * pr) * dec_seq:(2 * pr + 1) * dec_seq]
                odd = o[(2 * pr + 1) * dec_seq:(2 * pr + 2) * dec_seq]
                pieces.append(jnp.where(lo, even, odd))
        att_rows.append(jnp.concatenate(pieces, axis=-1))
        new_shift = (keep - i * dec_seq) % WINDOW
        k_out[i] = jnp.where(old_lane, pltpu.roll(ck_ref[i], keep, 1), pltpu.roll(k_new_t, new_shift, 1))
        v_out[i] = jnp.where(old_lane, pltpu.roll(vc_t, keep, 1), pltpu.roll(v_new_t, new_shift, 1))
    mix_ref[:, SSD_D_INNER:] = jnp.concatenate(att_rows, axis=0).astype(BF16)


def _sample_mixer(z, xbc, q, k, v, dt, sconv, sssm, ck, cv, small, consts, n_seq_total, dec_seq):
    cw, cb, dtb, alog, dskip_x, gn, relb, sinks = small
    tcum, tseq, expand, bkt_c, bkt_n = consts
    rows = SAMPLE_SEQS * dec_seq
    tok = lambda w: pl.BlockSpec((rows, w), lambda i: (i, 0))
    per_s = lambda s: pl.BlockSpec((SAMPLE_SEQS,) + s, lambda i: (i,) + (0,) * len(s))
    out_shape = [
        jax.ShapeDtypeStruct((n_seq_total * dec_seq, D_MODEL), BF16),
        jax.ShapeDtypeStruct((CONV_K - 1, n_seq_total, CONV_DIM), F32),
        jax.ShapeDtypeStruct((n_seq_total, SSD_D_INNER, D_STATE), F32),
        jax.ShapeDtypeStruct((n_seq_total, WINDOW, KV_DIM), F32),
        jax.ShapeDtypeStruct((n_seq_total, WINDOW, KV_DIM), F32),
    ]
    conv_spec = pl.BlockSpec((CONV_K - 1, SAMPLE_SEQS, CONV_DIM), lambda i: (0, i, 0))
    return pl.pallas_call(
        functools.partial(_sample_mixer_body, dec_seq=dec_seq),
        grid=(n_seq_total // SAMPLE_SEQS,),
        in_specs=[tok(SSD_D_INNER), tok(CONV_DIM), tok(ATTN_DIM), tok(KV_DIM), tok(KV_DIM), tok(DT_PAD),
                  conv_spec, per_s((SSD_D_INNER, D_STATE)),
                  per_s((WINDOW, KV_DIM)), per_s((WINDOW, KV_DIM)),
                  _full_spec(cw.shape), _full_spec(cb.shape), _full_spec(dtb.shape), _full_spec(alog.shape),
                  _full_spec(dskip_x.shape), _full_spec(gn.shape),
                  _full_spec(tcum.shape), _full_spec(tseq.shape), _full_spec(expand.shape),
                  _full_spec(bkt_c.shape), _full_spec(bkt_n.shape), _smem_spec(), _smem_spec()],
        out_specs=[tok(D_MODEL), conv_spec, per_s((SSD_D_INNER, D_STATE)),
                   per_s((WINDOW, KV_DIM)), per_s((WINDOW, KV_DIM))],
        out_shape=out_shape,
        scratch_shapes=[
            pltpu.VMEM((SAMPLE_SEQS, 8 + dec_seq, CONV_DIM), F32),
            pltpu.VMEM((1, N_HEADS, dec_seq, WINDOW), F32),
            pltpu.VMEM((1, N_HEADS, rows, rows), F32),
        ],
        compiler_params=_params(("arbitrary",)),
        name="sample_mixer",
    )(z, xbc, q, k, v, dt, sconv, sssm, ck, cv, cw, cb, dtb, alog, dskip_x, gn,
      tcum, tseq, expand, bkt_c, bkt_n, relb, sinks)


def _post1_body(x_ref, mix_ref, wout_ref, gc_ref, wcq_ref, h_ref, qc_ref):
    h = x_ref[...] + _dot(mix_ref[...], wout_ref[...])
    h_ref[...] = h
    qc_ref[...] = _dot(_rms(h, gc_ref[...]).astype(BF16), wcq_ref[...]).astype(BF16)


def _post1(x2, mix, w_out, g_cross, w_cq, tm):
    t = x2.shape[0]
    row = lambda w: pl.BlockSpec((tm, w), lambda i: (i, 0))
    return pl.pallas_call(
        _post1_body,
        grid=(t // tm,),
        in_specs=[row(D_MODEL), row(D_MODEL), _full_spec(w_out.shape), _full_spec(g_cross.shape),
                  _full_spec(w_cq.shape)],
        out_specs=[row(D_MODEL), row(CA_DIM)],
        out_shape=[jax.ShapeDtypeStruct((t, D_MODEL), F32), jax.ShapeDtypeStruct((t, CA_DIM), BF16)],
        compiler_params=_params(("parallel",)),
        name="out_proj",
    )(x2, mix, w_out, g_cross, w_cq)


def _mem_kv_body(mem_ref, g_ref, wk_ref, wv_ref, k_ref, v_ref, kb_ref, vb_ref):
    mn = _rms(mem_ref[...], g_ref[...]).astype(BF16)
    k = _dot(mn, wk_ref[...])
    v = _dot(mn, wv_ref[...])
    rows = k.shape[0]
    for h in range(CA_HEADS):
        hs = slice(h * CA_HEAD_DIM, (h + 1) * CA_HEAD_DIM)
        k_ref[pl.ds(h, rows, stride=CA_HEADS), :] = k[:, hs]
        v_ref[pl.ds(h, rows, stride=CA_HEADS), :] = v[:, hs]
    kb_ref[...] = k.astype(BF16)
    vb_ref[...] = v.astype(BF16)


def _mem_kv(mem2, g_mem, w_ck, w_cv, tm):
    t = mem2.shape[0]
    row = lambda w: pl.BlockSpec((tm, w), lambda i: (i, 0))
    return pl.pallas_call(
        _mem_kv_body,
        grid=(t // tm,),
        in_specs=[row(D_MODEL), _full_spec(g_mem.shape), _full_spec(w_ck.shape), _full_spec(w_cv.shape)],
        out_specs=[pl.BlockSpec((tm * CA_HEADS, CA_HEAD_DIM), lambda i: (i, 0))] * 2 + [row(CA_DIM)] * 2,
        out_shape=([jax.ShapeDtypeStruct((t * CA_HEADS, CA_HEAD_DIM), F32)] * 2
                   + [jax.ShapeDtypeStruct((t, CA_DIM), BF16)] * 2),
        compiler_params=_params(("parallel",)),
        name="mem_kv",
    )(mem2, g_mem, w_ck, w_cv)


def _cross_sample_body(q_ref, k_ref, v_ref, o_ref, *, n_seq, dec_seq):
    q = q_ref[...].astype(F32)
    n_keys = N_MEM * CA_HEADS
    col_head = lax.broadcasted_iota(jnp.int32, (1, n_keys), 1) & (CA_HEADS - 1)
    grp = CA_HEADS * dec_seq
    row_head = (lax.broadcasted_iota(jnp.int32, (n_seq * grp, 1), 0) // dec_seq) & (CA_HEADS - 1)
    own = col_head == row_head
    parts = []
    for i in range(n_seq):
        qi = q[i * dec_seq:(i + 1) * dec_seq]
        qs = jnp.concatenate([qi[:, h * CA_HEAD_DIM:(h + 1) * CA_HEAD_DIM] for h in range(CA_HEADS)], axis=0)
        parts.append(_dot_nt(qs.astype(BF16), k_ref[i].astype(BF16)))
    s = jnp.where(own, jnp.concatenate(parts, axis=0) * (CA_HEAD_DIM ** -0.5), NEG)
    m = jnp.max(s, axis=-1, keepdims=True)
    p = jnp.exp(s - m)
    rdenom = 1.0 / jnp.sum(p, axis=-1, keepdims=True)
    p = p.astype(BF16)
    rows = []
    for i in range(n_seq):
        gr = slice(i * grp, (i + 1) * grp)
        o = _dot(p[gr], v_ref[i].astype(BF16)) * rdenom[gr]
        rows.append(jnp.concatenate([o[h * dec_seq:(h + 1) * dec_seq] for h in range(CA_HEADS)], axis=-1))
    o_ref[...] = jnp.concatenate(rows, axis=0).astype(BF16)


def _cross_sample(qc, ck, cv, n_seq_total, dec_seq, n_seq):
    rows = n_seq * dec_seq
    tok = pl.BlockSpec((rows, CA_DIM), lambda i: (i, 0))
    mem = pl.BlockSpec((n_seq, N_MEM * CA_HEADS, CA_HEAD_DIM), lambda i: (i, 0, 0))
    return pl.pallas_call(
        functools.partial(_cross_sample_body, n_seq=n_seq, dec_seq=dec_seq),
        grid=(n_seq_total // n_seq,),
        in_specs=[tok, mem, mem],
        out_specs=tok,
        out_shape=jax.ShapeDtypeStruct((n_seq_total * dec_seq, CA_DIM), BF16),
        compiler_params=_params(("parallel",)),
        name="cross_sample",
    )(qc, ck, cv)


def _post2_body(h_ref, o_ref, wco_ref, gf_ref, wg_ref, wu_ref, wd_ref, gfin_ref, y_ref):
    h = h_ref[...] + _dot(o_ref[...], wco_ref[...])
    hn = _rms(h, gf_ref[...]).astype(BF16)
    acc = h
    for lo, hi in FF_SPLITS:
        act = _silu(_dot(hn, wg_ref[:, lo:hi])) * _dot(hn, wu_ref[:, lo:hi])
        acc = acc + _dot(act.astype(BF16), wd_ref[lo:hi, :])
    y_ref[...] = _rms(acc, gfin_ref[...])


def _post2(h1, o, w_co, g_ffn, w_gate, w_up, w_down, g_final, tm):
    t = h1.shape[0]
    row = lambda w: pl.BlockSpec((tm, w), lambda i: (i, 0))
    return pl.pallas_call(
        _post2_body,
        grid=(t // tm,),
        in_specs=[row(D_MODEL), row(CA_DIM), _full_spec(w_co.shape), _full_spec(g_ffn.shape),
                  _full_spec(w_gate.shape), _full_spec(w_up.shape), _full_spec(w_down.shape),
                  _full_spec(g_final.shape)],
        out_specs=row(D_MODEL),
        out_shape=jax.ShapeDtypeStruct((t, D_MODEL), F32),
        compiler_params=_params(("parallel",)),
        name="ffn",
    )(h1, o, w_co, g_ffn, w_gate, w_up, w_down, g_final)


def _prompt_consts():
    qi = np.arange(WINDOW)[:, None]
    ji = np.arange(2 * WINDOW)[None, :]
    dist = qi + WINDOW - ji
    inband = (dist >= 0) & (dist < WINDOW)
    bucket = np.where(inband, _t5_bucket_np(dist), -1)
    first = np.where(ji >= WINDOW, bucket, -1)
    buckets = np.stack([first, bucket]).astype(np.int32)
    return (jnp.asarray(_tril_np(CHUNK), BF16), jnp.asarray(buckets))


def _sample_consts(dec_seq, cache_len):
    rows = SAMPLE_SEQS * dec_seq
    r = np.arange(rows)
    same = (r[:, None] // dec_seq) == (r[None, :] // dec_seq)
    tcum = (same & (r[None, :] <= r[:, None])).astype(np.float32)
    tseq = same.astype(np.float32)
    t = np.arange(dec_seq)[:, None]
    j = np.arange(cache_len)[None, :]
    dist_c = t + cache_len - j
    bkt_c = np.where((dist_c >= 0) & (dist_c < WINDOW), _t5_bucket_np(dist_c), -1).astype(np.int32)
    dist_n = (r[:, None] % dec_seq) - (r[None, :] % dec_seq)
    ok = same & (dist_n >= 0) & (dist_n < WINDOW)
    bkt_n = np.where(ok, _t5_bucket_np(dist_n), -1).astype(np.int32)
    return (jnp.asarray(tcum, BF16), jnp.asarray(tseq, BF16), jnp.asarray(_expand_np(), BF16),
            jnp.asarray(bkt_c), jnp.asarray(bkt_n))


def _pick_tile(t, pref):
    tm = min(t, pref)
    while t % tm:
        tm //= 2
    return tm


def kernel(x_prompt, x_sample, mem_prompt, state_conv, state_ssm, cache_swa_k, cache_swa_v, cache_mem_k, cache_mem_v, rel_bias, g_mix, w_in, conv_w, conv_b, dt_bias, a_log, d_skip, g_ssd_norm, sinks, w_out, g_cross, g_mem, w_cq, w_ck, w_cv, w_co, g_ffn, w_gate, w_up, w_down, g_final):
    assert g_mix.shape[0] == 1, "single-layer trunk"
    batch, seq, _ = x_prompt.shape
    n_dec, dec_seq, _ = x_sample.shape
    cache_len = cache_swa_k.shape[2]
    assert seq % CHUNK == 0 and cache_len == WINDOW and n_dec % SAMPLE_SEQS == 0 and dec_seq == 8

    wi = w_in[0]
    s1 = SSD_D_INNER + CONV_DIM
    s2 = s1 + SSD_HEADS
    w_in_r = jnp.concatenate(
        [wi[:, :s1], wi[:, s2:], jnp.pad(wi[:, s1:s2], ((0, 0), (0, DT_PAD - SSD_HEADS)))], axis=1).astype(BF16)
    row = lambda a: a.reshape(1, -1).astype(F32)
    pad_h = lambda a: jnp.pad(a.reshape(1, -1).astype(F32), ((0, 0), (0, DT_PAD - SSD_HEADS)))
    small = (conv_w[0].astype(F32), row(conv_b[0]), pad_h(dt_bias[0]), pad_h(a_log[0]),
             jnp.repeat(d_skip[0].astype(F32), SSD_HEAD_DIM).reshape(1, -1), row(g_ssd_norm[0]),
             rel_bias.astype(F32), sinks[0].astype(F32))
    bf = lambda w: w[0].astype(BF16)
    w_out_b, w_cq_b, w_ck_b, w_cv_b, w_co_b = bf(w_out), bf(w_cq), bf(w_ck), bf(w_cv), bf(w_co)
    w_gate_b, w_up_b, w_down_b = bf(w_gate), bf(w_up), bf(w_down)
    g_mix_r, g_cross_r, g_mem_r, g_ffn_r, g_fin_r = row(g_mix[0]), row(g_cross[0]), row(g_mem[0]), row(g_ffn[0]), row(g_final)

    n_chunks = 2 if seq % (2 * CHUNK) == 0 else 1
    mem2 = mem_prompt.reshape(batch * N_MEM, D_MODEL)
    mk, mv, mk_b, mv_b = _mem_kv(mem2, g_mem_r, w_ck_b, w_cv_b, _pick_tile(batch * N_MEM, 512))
    dense = (g_mix_r, w_in_r, w_out_b, g_cross_r, w_cq_b, w_co_b, g_ffn_r, w_gate_b, w_up_b, w_down_b, g_fin_r)
    y_prompt, p_conv, p_ssm, p_k, p_v = _prompt_layer(
        x_prompt, mk_b.reshape(batch, N_MEM, CA_DIM), mv_b.reshape(batch, N_MEM, CA_DIM),
        dense, small, _prompt_consts(), n_chunks)

    def channel_major(cache):
        return jnp.transpose(cache, (0, 2, 3, 1)).reshape(n_dec, KV_DIM, cache_len)

    def position_major(cache_t):
        return jnp.transpose(cache_t.reshape(n_dec, N_KV_HEADS, ATTN_HEAD_DIM, cache_len), (0, 3, 1, 2))

    ts = n_dec * dec_seq
    xs2 = x_sample.reshape(ts, D_MODEL)
    tm_s = _pick_tile(ts, 512)
    z, xbc, q, k, v, dt = _in_proj(xs2, g_mix_r, w_in_r, tm_s)
    mix_s, s_conv, s_ssm, s_k, s_v = _sample_mixer(
        z, xbc, q, k, v, dt, jnp.transpose(state_conv[0], (1, 0, 2)),
        state_ssm[0].reshape(n_dec, SSD_D_INNER, D_STATE),
        channel_major(cache_swa_k[0]), channel_major(cache_swa_v[0]),
        small, _sample_consts(dec_seq, cache_len), n_dec, dec_seq)
    h1s, qcs = _post1(xs2, mix_s, w_out_b, g_cross_r, w_cq_b, tm_s)
    os_ = _cross_sample(qcs, cache_mem_k[0].reshape(n_dec, N_MEM * CA_HEADS, CA_HEAD_DIM),
                        cache_mem_v[0].reshape(n_dec, N_MEM * CA_HEADS, CA_HEAD_DIM), n_dec, dec_seq, SAMPLE_SEQS)
    y_sample = _post2(h1s, os_, w_co_b, g_ffn_r, w_gate_b, w_up_b, w_down_b, g_fin_r, tm_s)

    return (y_prompt.reshape(batch, seq, D_MODEL), y_sample.reshape(n_dec, dec_seq, D_MODEL),
            p_conv[None], p_ssm.reshape(1, batch, SSD_HEADS, SSD_HEAD_DIM, D_STATE),
            p_k.reshape(1, batch, WINDOW, N_KV_HEADS, ATTN_HEAD_DIM),
            p_v.reshape(1, batch, WINDOW, N_KV_HEADS, ATTN_HEAD_DIM),
            mk.reshape(1, batch, N_MEM, CA_HEADS, CA_HEAD_DIM), mv.reshape(1, batch, N_MEM, CA_HEADS, CA_HEAD_DIM),
            jnp.transpose(s_conv, (1, 0, 2))[None], s_ssm.reshape(1, n_dec, SSD_HEADS, SSD_HEAD_DIM, D_STATE),
            position_major(s_k)[None], position_major(s_v)[None])
```

```python
import functools
import math

import numpy as np
import jax
import jax.numpy as jnp
from jax import lax
from jax.experimental import pallas as pl
from jax.experimental.pallas import tpu as pltpu

F32 = jnp.float32
BF16 = jnp.bfloat16

D_MODEL = 1024
SSD_D_INNER = 512
SSD_HEAD_DIM = 64
SSD_HEADS = 8
SSD_GROUPS = 2
GROUP_W = SSD_D_INNER // SSD_GROUPS
D_STATE = 128
CONV_K = 4
CONV_DIM = SSD_D_INNER + 2 * SSD_GROUPS * D_STATE
CHUNK = 128
ATTN_DIM = 512
ATTN_HEAD_DIM = 64
N_HEADS = 8
N_KV_HEADS = 2
Q_PER_KV = N_HEADS // N_KV_HEADS
KV_DIM = N_KV_HEADS * ATTN_HEAD_DIM
WINDOW = 128
N_BUCKETS = 32
MAX_EXACT = N_BUCKETS // 2
MAX_DISTANCE = 128
N_MEM = 256
CA_HEADS = 4
CA_HEAD_DIM = 128
CA_DIM = CA_HEADS * CA_HEAD_DIM
D_FF = 2816
EPS = 1e-6

LANES = 128
HALF = LANES // 2
DT_PAD = LANES
COL_Z = 0
COL_XBC = COL_Z + SSD_D_INNER
COL_Q = COL_XBC + CONV_DIM
COL_K = COL_Q + ATTN_DIM
COL_V = COL_K + KV_DIM
COL_DT = COL_V + KV_DIM
IN_COLS = COL_DT + DT_PAD
NEG = -1e30
SAMPLE_SEQS = 16
VMEM_LIMIT = 56 * 1024 * 1024
FF_SPLITS = ((0, 1024), (1024, 2048), (2048, D_FF))
SECOND_STREAM_LEAD = 0.5
FF_PIECES = tuple((lo, min(lo + 512, D_FF)) for lo in range(0, D_FF, 512))
PROJ_PIECES = ((COL_Z, COL_XBC), (COL_XBC, COL_XBC + 512), (COL_XBC + 512, COL_Q), (COL_Q, COL_K), (COL_K, IN_COLS))


def _rms(x, g):
    return x * lax.rsqrt(jnp.mean(x * x, axis=-1, keepdims=True) + EPS) * g


def _silu(x):
    return x * jax.nn.sigmoid(x)


def _softplus(x):
    return jnp.maximum(x, 0.0) + jnp.log1p(jnp.exp(-jnp.abs(x)))


def _dot(a, b):
    return jnp.dot(a, b, preferred_element_type=F32)


def _dot_nt(a, b):
    return lax.dot_general(a, b, (((1,), (1,)), ((), ())), preferred_element_type=F32)


def _split3(a):
    hi = a.astype(BF16)
    r = a - hi.astype(F32)
    mid = r.astype(BF16)
    lo = (r - mid.astype(F32)).astype(BF16)
    return hi, mid, lo


def _sel_left(t01, a):
    hi, mid, lo = _split3(a)
    return _dot(t01, hi) + _dot(t01, mid) + _dot(t01, lo)


def _sel_right(a, e01):
    hi, mid, lo = _split3(a)
    return _dot(hi, e01) + _dot(mid, e01) + _dot(lo, e01)


def _lane_lo():
    return lax.broadcasted_iota(jnp.int32, (1, LANES), 1) < HALF


def _t5_bucket_np(dist):
    n = np.maximum(dist, 0)
    ratio = np.log(np.maximum(n, 1).astype(np.float32) / np.float32(MAX_EXACT))
    large = MAX_EXACT + (ratio / np.float32(math.log(MAX_DISTANCE / MAX_EXACT))
                         * np.float32(N_BUCKETS - MAX_EXACT)).astype(np.int32)
    large = np.minimum(large, N_BUCKETS - 1)
    return np.where(n < MAX_EXACT, n, large).astype(np.int32)


def _tril_np(n):
    return np.tril(np.ones((n, n), np.float32))


def _expand_np():
    e = np.zeros((LANES, SSD_D_INNER), np.float32)
    for h in range(SSD_HEADS):
        e[h, h * SSD_HEAD_DIM:(h + 1) * SSD_HEAD_DIM] = 1.0
    return e


def _full_spec(shape):
    nd = len(shape)
    return pl.BlockSpec(shape, lambda *_: (0,) * nd, pipeline_mode=pl.Buffered(1))


def _smem_spec():
    return pl.BlockSpec(memory_space=pltpu.SMEM)


def _params(sem):
    return pltpu.CompilerParams(dimension_semantics=sem, vmem_limit_bytes=VMEM_LIMIT)


def _in_proj_body(x_ref, g_ref, w_ref, z_ref, xbc_ref, q_ref, k_ref, v_ref, dt_ref):
    xn = _rms(x_ref[...], g_ref[...]).astype(BF16)

    def seg(lo, hi):
        return _dot(xn, w_ref[:, lo:hi])

    z_ref[...] = seg(COL_Z, COL_XBC)
    xbc_ref[...] = seg(COL_XBC, COL_Q)
    q_ref[...] = (seg(COL_Q, COL_K) * (ATTN_HEAD_DIM ** -0.5)).astype(BF16)
    k_ref[...] = seg(COL_K, COL_V)
    v_ref[...] = seg(COL_V, COL_DT)
    dt_ref[...] = seg(COL_DT, IN_COLS)


def _in_proj(x2, g_mix, w_in_r, tm):
    t = x2.shape[0]
    row = lambda w: pl.BlockSpec((tm, w), lambda i: (i, 0))
    outs = [(SSD_D_INNER, F32), (CONV_DIM, F32), (ATTN_DIM, BF16), (KV_DIM, F32), (KV_DIM, F32), (DT_PAD, F32)]
    return pl.pallas_call(
        _in_proj_body,
        grid=(t // tm,),
        in_specs=[row(D_MODEL), _full_spec((1, D_MODEL)), _full_spec((D_MODEL, IN_COLS))],
        out_specs=[row(w) for w, _ in outs],
        out_shape=[jax.ShapeDtypeStruct((t, w), d) for w, d in outs],
        compiler_params=_params(("parallel",)),
        name="in_proj",
    )(x2, g_mix, w_in_r)


def _conv_taps(cw_ref, cb_ref, taps):
    acc = cb_ref[...] + taps[0] * cw_ref[0:1, :]
    for k in range(1, CONV_K):
        acc = acc + taps[k] * cw_ref[k:k + 1, :]
    return _silu(acc)


def _ssd_prepare(conv, dt_raw, dtb_ref, alog_ref, tcum, total_fn, extra_fn, expand):
    xs = conv[:, :SSD_D_INNER]
    bm = conv[:, SSD_D_INNER:SSD_D_INNER + SSD_GROUPS * D_STATE]
    cm = conv[:, SSD_D_INNER + SSD_GROUPS * D_STATE:]
    dt = _softplus(dt_raw + dtb_ref[...])
    a = dt * (-jnp.exp(alog_ref[...]))
    cs = _sel_left(tcum, a)
    total = total_fn(cs, a)
    pieces = [dt, jnp.exp(total - cs), jnp.exp(cs)] + extra_fn(total)
    rows = cs.shape[0]
    ex = _sel_right(jnp.concatenate(pieces, axis=0), expand)
    ex = [ex[i * rows:(i + 1) * rows] for i in range(len(pieces))]
    return xs, bm, cm, cs, cs.T, ex


def _ssd_diag(cs, cs_t, cb_g, xdt, mask, g):
    lo = _lane_lo()
    out = []
    for pr in range(2):
        h0 = g * 4 + 2 * pr
        xp = xdt[:, (h0 // 2) * LANES:(h0 // 2 + 1) * LANES]
        x_lo = jnp.where(lo, xp, 0.0).astype(BF16)
        x_hi = jnp.where(lo, 0.0, xp).astype(BF16)
        acc = None
        for h, xh in ((h0, x_lo), (h0 + 1, x_hi)):
            diff = cs[:, h:h + 1] - cs_t[h:h + 1, :]
            decay = jnp.exp(jnp.where(mask, diff, -jnp.inf))
            part = _dot((cb_g * decay).astype(BF16), xh)
            acc = part if acc is None else acc + part
        out.append(acc)
    return jnp.concatenate(out, axis=-1)


def _gated_norm(y, z, gn_ref):
    yf = y * _silu(z)
    parts = []
    for g in range(SSD_GROUPS):
        yg = yf[:, g * GROUP_W:(g + 1) * GROUP_W]
        parts.append(yg * lax.rsqrt(jnp.mean(yg * yg, axis=-1, keepdims=True) + EPS))
    return jnp.concatenate(parts, axis=-1) * gn_ref[...]


def _build_bias(bias_ref, bucket_of, relb_ref, n_tables):
    for i in range(n_tables):
        for h in range(N_HEADS):
            bias_ref[i, h] = jnp.full(bias_ref.shape[2:], NEG, F32)

    def body(t, carry):
        for i in range(n_tables):
            hit = bucket_of(i) == t
            for h in range(N_HEADS):
                bias_ref[i, h] = jnp.where(hit, relb_ref[t, h], bias_ref[i, h])
        return carry

    lax.fori_loop(0, N_BUCKETS, body, 0)


def _alternate(first, second):
    streams = [[0.0, 1.0, first], [0.0, SECOND_STREAM_LEAD, second]]
    while streams:
        entry = min(streams, key=lambda e: e[0])
        try:
            entry[0] += entry[1] * next(entry[2])
        except StopIteration:
            streams.remove(entry)


def _layer_step_body(x_ref, xprev_ref, mk_ref, mv_ref, gmix_ref, win_ref, wout_ref, gc_ref, wcq_ref,
                     wco_ref, gf_ref, wg_ref, wu_ref, wd_ref, gfin_ref,
                     cw_ref, cb_ref, dtb_ref, alog_ref, dskip_ref, gn_ref,
                     tril_ref, bucket_ref, relb_ref, sink_ref,
                     y_ref, conv_out, ssm_out, k_out, v_out,
                     xpad, state_t, kbuf, vbuf, bias, mix, proj, *, n_chunks, tiles_per_seq):
    step = pl.program_id(0)
    n_tiles = pl.num_programs(0) - 1
    c = lax.rem(jnp.minimum(step, n_tiles - 1), tiles_per_seq)
    last = tiles_per_seq - 1
    tq = n_chunks * CHUNK

    @pl.when(step == 0)
    def _():
        _build_bias(bias, lambda i: bucket_ref[i], relb_ref, 2)
        mix[...] = jnp.zeros_like(mix)

    @pl.when(c == 0)
    def _():
        xpad[0:8, :] = jnp.zeros((8, CONV_DIM), F32)
        state_t[...] = jnp.zeros_like(state_t)
        kbuf[:, 0:CHUNK, :] = jnp.zeros((2, CHUNK, KV_DIM), BF16)
        vbuf[:, 0:CHUNK, :] = jnp.zeros((2, CHUNK, KV_DIM), BF16)

    def back():
        h1 = xprev_ref[0] + _dot(mix[...], wout_ref[...])
        yield 4
        qc = _dot(_rms(h1, gc_ref[...]).astype(BF16), wcq_ref[...]).astype(BF16)
        yield 2
        heads = []
        for h in range(CA_HEADS):
            hs = slice(h * CA_HEAD_DIM, (h + 1) * CA_HEAD_DIM)
            sc = _dot_nt(qc[:, hs], mk_ref[0, :, hs]) * (CA_HEAD_DIM ** -0.5)
            m = jnp.max(sc, axis=-1, keepdims=True)
            p = jnp.exp(sc - m)
            heads.append(_dot(p.astype(BF16), mv_ref[0, :, hs]) / jnp.sum(p, axis=-1, keepdims=True))
            yield 3
        o = jnp.concatenate(heads, axis=-1).astype(BF16)
        h2 = h1 + _dot(o, wco_ref[...])
        hn = _rms(h2, gf_ref[...]).astype(BF16)
        yield 4
        acc = h2
        for f_lo, f_hi in FF_PIECES:
            width = (f_hi - f_lo) / 512
            gate = _dot(hn, wg_ref[:, f_lo:f_hi])
            yield 5 * width
            act = (_silu(gate) * _dot(hn, wu_ref[:, f_lo:f_hi])).astype(BF16)
            yield 6 * width
            acc = acc + _dot(act, wd_ref[f_lo:f_hi, :])
            yield 5 * width
        y_ref[0] = _rms(acc, gfin_ref[...])

    def front():
        x = x_ref[0]
        xn = _rms(x, gmix_ref[...]).astype(BF16)
        for p_lo, p_hi in PROJ_PIECES:
            if COL_XBC <= p_lo and p_hi <= COL_Q:
                xpad[8:8 + tq, p_lo - COL_XBC:p_hi - COL_XBC] = _dot(xn, win_ref[:, p_lo:p_hi])
            else:
                proj[:, p_lo:p_hi] = _dot(xn, win_ref[:, p_lo:p_hi])
            yield 4

        conv_halves = []
        half_w = CONV_DIM // 2
        for hf in range(2):
            cs_ = slice(hf * half_w, (hf + 1) * half_w)
            acc = cb_ref[:, cs_] + xpad[5:5 + tq, cs_] * cw_ref[0:1, cs_]
            for kk in range(1, CONV_K):
                acc = acc + xpad[5 + kk:5 + kk + tq, cs_] * cw_ref[kk:kk + 1, cs_]
            conv_halves.append(_silu(acc))
            yield 6
        xpad[5:8, :] = xpad[5 + tq:8 + tq, :]
        xs_all = conv_halves[0]
        bc_all = conv_halves[1]

        k_new = proj[:, COL_K:COL_V]
        v_new = proj[:, COL_V:COL_DT]
        kbuf[0, CHUNK:, :] = k_new.astype(BF16)
        kbuf[1, CHUNK:, :] = pltpu.roll(k_new, HALF, 1).astype(BF16)
        vbuf[0, CHUNK:, :] = v_new.astype(BF16)
        vbuf[1, CHUNK:, :] = pltpu.roll(v_new, HALF, 1).astype(BF16)
        q = (proj[:, COL_Q:COL_K] * (ATTN_HEAD_DIM ** -0.5)).astype(BF16)
        yield 2

        row = lax.broadcasted_iota(jnp.int32, (CHUNK, CHUNK), 0)
        col = lax.broadcasted_iota(jnp.int32, (CHUNK, CHUNK), 1)
        causal = col <= row
        lo = _lane_lo()
        for ci in range(n_chunks):
            rs = slice(ci * CHUNK, (ci + 1) * CHUNK)
            xs = xs_all[rs]
            bm = bc_all[rs, 0:SSD_GROUPS * D_STATE]
            cm = bc_all[rs, SSD_GROUPS * D_STATE:]
            dt = _softplus(proj[rs, COL_DT:IN_COLS] + dtb_ref[...])
            a = dt * (-jnp.exp(alog_ref[...]))
            cs = _sel_left(tril_ref[...], a)
            cs_t = cs.T
            dt_t = dt.T
            total = cs[CHUNK - 1:CHUNK, :]
            ecs = jnp.exp(cs)
            w_end = dt * jnp.exp(total - cs)
            bb = bm.astype(BF16)
            cb = cm.astype(BF16)
            yield 5
            y_parts = []
            for g in range(SSD_GROUPS):
                gs = slice(g * GROUP_W, (g + 1) * GROUP_W)
                ns = slice(g * D_STATE, (g + 1) * D_STATE)
                cb_g = _dot_nt(cb[:, ns], bb[:, ns])
                y_diag, xw, ecs_g = [], [], []
                for pr in range(2):
                    h0 = g * 4 + 2 * pr
                    xp = xs[:, (h0 // 2) * LANES:(h0 // 2 + 1) * LANES]
                    ecs_g.append(jnp.where(lo, ecs[:, h0:h0 + 1], ecs[:, h0 + 1:h0 + 2]))
                    xw.append((xp * jnp.where(lo, w_end[:, h0:h0 + 1], w_end[:, h0 + 1:h0 + 2])).astype(BF16))
                    x2 = jnp.concatenate([jnp.where(lo, xp, 0.0), jnp.where(lo, 0.0, xp)], axis=0).astype(BF16)
                    gmat = []
                    for h in (h0, h0 + 1):
                        diff = cs[:, h:h + 1] - cs_t[h:h + 1, :]
                        decay = jnp.exp(jnp.where(causal, diff, -jnp.inf))
                        gmat.append((cb_g * decay * dt_t[h:h + 1, :]).astype(BF16))
                    y_diag.append(_dot(jnp.concatenate(gmat, axis=1), x2))
                ecs_x = jnp.concatenate(ecs_g, axis=-1)
                st_old = state_t[:, gs]
                y_off = _dot(cb[:, ns], st_old.astype(BF16)) * ecs_x
                y_parts.append(jnp.concatenate(y_diag, axis=-1) + y_off)
                b_t = bm[:, ns].T.astype(BF16)
                state_t[:, gs] = ecs_x[CHUNK - 1:CHUNK, :] * st_old + _dot(b_t, jnp.concatenate(xw, axis=-1))
                yield 9
            y = jnp.concatenate(y_parts, axis=-1) + dskip_ref[...] * xs
            mix[rs, 0:SSD_D_INNER] = _gated_norm(y, proj[rs, COL_Z:COL_XBC], gn_ref).astype(BF16)
            yield 3

            keys = slice(ci * CHUNK, (ci + 2) * CHUNK)
            table = jnp.minimum(c, 1) if ci == 0 else 1
            probs, denoms = {}, {}
            for variant in range(2):
                hv = [h for h in range(N_HEADS) if (h // Q_PER_KV + h % 2) % 2 == variant]
                qms = []
                for h in hv:
                    q_pair = q[rs, (h // 2) * LANES:(h // 2 + 1) * LANES]
                    qms.append(jnp.where(lo, q_pair, 0.0) if h % 2 == 0 else jnp.where(lo, 0.0, q_pair))
                sc_all = _dot_nt(jnp.concatenate(qms, axis=0).astype(BF16), kbuf[variant, keys, :])
                for i, h in enumerate(hv):
                    sc = sc_all[i * CHUNK:(i + 1) * CHUNK] + bias[table, h]
                    sink = sink_ref[h]
                    m = jnp.maximum(jnp.max(sc, axis=-1, keepdims=True), sink)
                    p = jnp.exp(sc - m)
                    denoms[h] = jnp.sum(p, axis=-1, keepdims=True) + jnp.exp(sink - m)
                    probs[h] = p.astype(BF16)
                yield 10
            parts = {}
            for variant in range(2):
                vv = vbuf[variant, keys, :]
                for par in range(2):
                    hv = [h for h in range(N_HEADS) if (h // Q_PER_KV + h % 2) % 2 == variant and h % 2 == par]
                    vm = jnp.where(lo, vv, 0.0) if par == 0 else jnp.where(lo, 0.0, vv)
                    o_all = _dot(jnp.concatenate([probs[h] for h in hv], axis=0), vm.astype(BF16))
                    for i, h in enumerate(hv):
                        parts[h] = o_all[i * CHUNK:(i + 1) * CHUNK] / denoms[h]
            for pair in range(N_HEADS // 2):
                mix[rs, SSD_D_INNER + pair * LANES:SSD_D_INNER + (pair + 1) * LANES] = (
                    parts[2 * pair] + parts[2 * pair + 1]).astype(BF16)
            yield 4
        kbuf[:, 0:CHUNK, :] = kbuf[:, tq:tq + CHUNK, :]
        vbuf[:, 0:CHUNK, :] = vbuf[:, tq:tq + CHUNK, :]

    _alternate(back(), front())

    @pl.when((c == last) & (step < n_tiles))
    def _():
        conv_out[0] = xpad[5:8, :]
        ssm_out[0] = state_t[...].T
        k_out[0] = proj[tq - WINDOW:tq, COL_K:COL_V]
        v_out[0] = proj[tq - WINDOW:tq, COL_V:COL_DT]


def _prompt_layer(x3, mk_b, mv_b, dense, small, consts, n_chunks):
    cw, cb, dtb, alog, dskip_x, gn, relb, sinks = small
    tril, buckets = consts
    batch, seq, _ = x3.shape
    tq = n_chunks * CHUNK
    tps = seq // tq
    n_tiles = batch * tps
    front = lambda s: jnp.minimum(s, n_tiles - 1)
    back = lambda s: jnp.maximum(s - 1, 0)
    x_spec = pl.BlockSpec((1, tq, D_MODEL), lambda s: (front(s) // tps, front(s) % tps, 0))
    xprev_spec = pl.BlockSpec((1, tq, D_MODEL), lambda s: (back(s) // tps, back(s) % tps, 0))
    y_spec = pl.BlockSpec((1, tq, D_MODEL), lambda s: (back(s) // tps, back(s) % tps, 0))
    mem_spec = pl.BlockSpec((1, N_MEM, CA_DIM), lambda s: (back(s) // tps, 0, 0))
    per_b = lambda shp: pl.BlockSpec((1,) + shp, lambda s: (front(s) // tps,) + (0,) * len(shp))
    out_shape = [
        jax.ShapeDtypeStruct((batch, seq, D_MODEL), F32),
        jax.ShapeDtypeStruct((batch, CONV_K - 1, CONV_DIM), F32),
        jax.ShapeDtypeStruct((batch, SSD_D_INNER, D_STATE), F32),
        jax.ShapeDtypeStruct((batch, WINDOW, KV_DIM), F32),
        jax.ShapeDtypeStruct((batch, WINDOW, KV_DIM), F32),
    ]
    full = list(dense) + [cw, cb, dtb, alog, dskip_x, gn, tril, buckets]
    return pl.pallas_call(
        functools.partial(_layer_step_body, n_chunks=n_chunks, tiles_per_seq=tps),
        grid=(n_tiles + 1,),
        in_specs=([x_spec, xprev_spec, mem_spec, mem_spec] + [_full_spec(a.shape) for a in full]
                  + [_smem_spec(), _smem_spec()]),
        out_specs=[y_spec, per_b((CONV_K - 1, CONV_DIM)), per_b((SSD_D_INNER, D_STATE)),
                   per_b((WINDOW, KV_DIM)), per_b((WINDOW, KV_DIM))],
        out_shape=out_shape,
        scratch_shapes=[
            pltpu.VMEM((8 + tq, CONV_DIM), F32),
            pltpu.VMEM((D_STATE, SSD_D_INNER), F32),
            pltpu.VMEM((2, CHUNK + tq, KV_DIM), BF16),
            pltpu.VMEM((2, CHUNK + tq, KV_DIM), BF16),
            pltpu.VMEM((2, N_HEADS, WINDOW, 2 * WINDOW), F32),
            pltpu.VMEM((tq, D_MODEL), BF16),
            pltpu.VMEM((tq, IN_COLS), F32),
        ],
        compiler_params=_params(("arbitrary",)),
        name="prompt_layer",
    )(x3, x3, mk_b, mv_b, *full, relb, sinks)


def _sample_mixer_body(z_ref, xbc_ref, q_ref, k_ref, v_ref, dt_ref,
                       sconv_ref, sssm_ref, ck_ref, cv_ref,
                       cw_ref, cb_ref, dtb_ref, alog_ref, dskip_ref, gn_ref,
                       tcum_ref, tseq_ref, expand_ref, bkt_c_ref, bkt_n_ref, relb_ref, sink_ref,
                       mix_ref, conv_out, ssm_out, k_out, v_out,
                       xpad, bias_c, bias_n, *, dec_seq):
    step = pl.program_id(0)
    n_seq = SAMPLE_SEQS
    rows = n_seq * dec_seq

    @pl.when(step == 0)
    def _():
        _build_bias(bias_c, lambda i: bkt_c_ref[...], relb_ref, 1)
        _build_bias(bias_n, lambda i: bkt_n_ref[...], relb_ref, 1)

    xbc = xbc_ref[...]
    xpad[:, 8:8 + dec_seq, :] = xbc.reshape(n_seq, dec_seq, CONV_DIM)
    for r in range(CONV_K - 1):
        xpad[:, 5 + r, :] = sconv_ref[r]
    taps = [xpad[:, 5 + k:5 + k + dec_seq, :].reshape(rows, CONV_DIM) for k in range(CONV_K - 1)] + [xbc]
    conv = _conv_taps(cw_ref, cb_ref, taps)
    for r in range(CONV_K - 1):
        conv_out[r] = xpad[:, 5 + dec_seq + r, :]

    row = lax.broadcasted_iota(jnp.int32, (rows, rows), 0)
    col = lax.broadcasted_iota(jnp.int32, (rows, rows), 1)
    tseq = tseq_ref[...]
    same_seq = tseq > 0
    causal = same_seq & (col <= row)
    xs, bm, cm, cs, cs_t, (dt_x, dend_x, ecs_x, seqdec_x) = _ssd_prepare(
        conv, dt_ref[...], dtb_ref, alog_ref, tcum_ref[...],
        lambda cs_, a_: _sel_left(tseq, a_), lambda total: [jnp.exp(total)], expand_ref[...])
    xdt = xs * dt_x
    bb = bm.astype(BF16)
    cb = cm.astype(BF16)
    seq_of_row = lax.broadcasted_iota(jnp.int32, (rows, 1), 0) // dec_seq
    seq_of_lane = lax.broadcasted_iota(jnp.int32, (1, rows), 1) // dec_seq
    xde_t = (xdt * dend_x).T
    y_off = [None] * SSD_GROUPS
    for i in range(n_seq):
        st = sssm_ref[i]
        dec = jnp.concatenate(
            [jnp.broadcast_to(seqdec_x[i * dec_seq:i * dec_seq + 1, h * SSD_HEAD_DIM:h * SSD_HEAD_DIM + 1],
                              (SSD_HEAD_DIM, D_STATE)) for h in range(SSD_HEADS)], axis=0)
        new_parts = []
        for g in range(SSD_GROUPS):
            gs = slice(g * GROUP_W, (g + 1) * GROUP_W)
            ns = slice(g * D_STATE, (g + 1) * D_STATE)
            c_i = jnp.where(seq_of_row == i, cm[:, ns], 0.0).astype(BF16)
            part = _dot_nt(c_i, st[gs].astype(BF16))
            y_off[g] = part if y_off[g] is None else y_off[g] + part
            x_i = jnp.where(seq_of_lane == i, xde_t[gs], 0.0).astype(BF16)
            new_parts.append(dec[gs] * st[gs] + _dot(x_i, bb[:, ns]))
        ssm_out[i] = jnp.concatenate(new_parts, axis=0)
    y_parts = []
    for g in range(SSD_GROUPS):
        gs = slice(g * GROUP_W, (g + 1) * GROUP_W)
        ns = slice(g * D_STATE, (g + 1) * D_STATE)
        cb_g = _dot_nt(cb[:, ns], bb[:, ns])
        y_parts.append(_ssd_diag(cs, cs_t, cb_g, xdt, causal, g) + y_off[g] * ecs_x[:, gs])
    y = jnp.concatenate(y_parts, axis=-1) + dskip_ref[...] * xs
    mix_ref[:, 0:SSD_D_INNER] = _gated_norm(y, z_ref[...], gn_ref).astype(BF16)

    lo = _lane_lo()
    k_new = k_ref[...]
    v_new = v_ref[...]
    k_var = [k_new.astype(BF16), pltpu.roll(k_new, HALF, 1).astype(BF16)]
    v_new_r = pltpu.roll(v_new, HALF, 1)
    v_dup = [jnp.where(lo, v_new, v_new_r).astype(BF16), jnp.where(lo, v_new_r, v_new).astype(BF16)]
    qf = q_ref[...].astype(F32)
    q_masked = []
    s_new = []
    for h in range(N_HEADS):
        pair, par = h // 2, h % 2
        j = h // Q_PER_KV
        q_pair = qf[:, pair * LANES:(pair + 1) * LANES]
        qm = jnp.where(lo, q_pair, 0.0) if par == 0 else jnp.where(lo, 0.0, q_pair)
        q_masked.append(qm)
        s_new.append(_dot_nt(qm.astype(BF16), k_var[(j + par) % 2]) + bias_n[0, h])
    stack_rows = lax.broadcasted_iota(jnp.int32, (Q_PER_KV * dec_seq, 1), 0) // dec_seq
    k_new_t = k_new.T
    v_new_t = v_new.T
    keep = WINDOW - dec_seq
    old_lane = lax.broadcasted_iota(jnp.int32, (1, WINDOW), 1) < keep
    grp = Q_PER_KV * dec_seq
    sc_parts, sn_parts, sink_parts = [], [], []
    for i in range(n_seq):
        rs = slice(i * dec_seq, (i + 1) * dec_seq)
        kc_t = ck_ref[i]
        for j in range(N_KV_HEADS):
            heads = range(j * Q_PER_KV, (j + 1) * Q_PER_KV)
            cj = slice(j * ATTN_HEAD_DIM, (j + 1) * ATTN_HEAD_DIM)
            kdup_t = jnp.concatenate([kc_t[cj], kc_t[cj]], axis=0).astype(BF16)
            qs = jnp.concatenate([q_masked[h][rs] for h in heads], axis=0).astype(BF16)
            sc_parts.append(_dot(qs, kdup_t) + jnp.concatenate([bias_c[0, h] for h in heads], axis=0))
            sn_parts.append(jnp.concatenate([s_new[h][rs] for h in heads], axis=0))
            if i == 0:
                sink_j = jnp.zeros((grp, 1), F32)
                for hh, h in enumerate(heads):
                    sink_j = jnp.where(stack_rows == hh, sink_ref[h], sink_j)
                sink_parts.append(sink_j)
    sc = jnp.concatenate(sc_parts, axis=0)
    sn = jnp.concatenate(sn_parts, axis=0)
    sink = jnp.concatenate(sink_parts * n_seq, axis=0)
    m = jnp.maximum(jnp.maximum(jnp.max(sc, axis=-1, keepdims=True), jnp.max(sn, axis=-1, keepdims=True)), sink)
    pc = jnp.exp(sc - m)
    pn = jnp.exp(sn - m)
    rdenom = 1.0 / (jnp.sum(pc, axis=-1, keepdims=True) + jnp.sum(pn, axis=-1, keepdims=True) + jnp.exp(sink - m))
    pc = pc.astype(BF16)
    pn = pn.astype(BF16)
    att_rows = []
    for i in range(n_seq):
        vc_t = cv_ref[i]
        pieces = []
        for j in range(N_KV_HEADS):
            cj = slice(j * ATTN_HEAD_DIM, (j + 1) * ATTN_HEAD_DIM)
            gr = slice((i * N_KV_HEADS + j) * grp, (i * N_KV_HEADS + j + 1) * grp)
            vdup_t = jnp.concatenate([vc_t[cj], vc_t[cj]], axis=0).astype(BF16)
            o = (_dot_nt(pc[gr], vdup_t) + _dot(pn[gr], v_dup[j])) * rdenom[gr]
            for pr in range(Q_PER_KV // 2):
                even = o[(2 * pr) * dec_seq:(2 * pr + 1) * dec_seq]
                odd = o[(2 * pr + 1) * dec_seq:(2 * pr + 2) * dec_seq]
                pieces.append(jnp.where(lo, even, odd))
        att_rows.append(jnp.concatenate(pieces, axis=-1))
        new_shift = (keep - i * dec_seq) % WINDOW
        k_out[i] = jnp.where(old_lane, pltpu.roll(ck_ref[i], keep, 1), pltpu.roll(k_new_t, new_shift, 1))
        v_out[i] = jnp.where(old_lane, pltpu.roll(vc_t, keep, 1), pltpu.roll(v_new_t, new_shift, 1))
    mix_ref[:, SSD_D_INNER:] = jnp.concatenate(att_rows, axis=0).astype(BF16)


def _sample_mixer(z, xbc, q, k, v, dt, sconv, sssm, ck, cv, small, consts, n_seq_total, dec_seq):
    cw, cb, dtb, alog, dskip_x, gn, relb, sinks = small
    tcum, tseq, expand, bkt_c, bkt_n = consts
    rows = SAMPLE_SEQS * dec_seq
    tok = lambda w: pl.BlockSpec((rows, w), lambda i: (i, 0))
    per_s = lambda s: pl.BlockSpec((SAMPLE_SEQS,) + s, lambda i: (i,) + (0,) * len(s))
    out_shape = [
        jax.ShapeDtypeStruct((n_seq_total * dec_seq, D_MODEL), BF16),
        jax.ShapeDtypeStruct((CONV_K - 1, n_seq_total, CONV_DIM), F32),
        jax.ShapeDtypeStruct((n_seq_total, SSD_D_INNER, D_STATE), F32),
        jax.ShapeDtypeStruct((n_seq_total, WINDOW, KV_DIM), F32),
        jax.ShapeDtypeStruct((n_seq_total, WINDOW, KV_DIM), F32),
    ]
    conv_spec = pl.BlockSpec((CONV_K - 1, SAMPLE_SEQS, CONV_DIM), lambda i: (0, i, 0))
    return pl.pallas_call(
        functools.partial(_sample_mixer_body, dec_seq=dec_seq),
        grid=(n_seq_total // SAMPLE_SEQS,),
        in_specs=[tok(SSD_D_INNER), tok(CONV_DIM), tok(ATTN_DIM), tok(KV_DIM), tok(KV_DIM), tok(DT_PAD),
                  conv_spec, per_s((SSD_D_INNER, D_STATE)),
                  per_s((WINDOW, KV_DIM)), per_s((WINDOW, KV_DIM)),
                  _full_spec(cw.shape), _full_spec(cb.shape), _full_spec(dtb.shape), _full_spec(alog.shape),
                  _full_spec(dskip_x.shape), _full_spec(gn.shape),
                  _full_spec(tcum.shape), _full_spec(tseq.shape), _full_spec(expand.shape),
                  _full_spec(bkt_c.shape), _full_spec(bkt_n.shape), _smem_spec(), _smem_spec()],
        out_specs=[tok(D_MODEL), conv_spec, per_s((SSD_D_INNER, D_STATE)),
                   per_s((WINDOW, KV_DIM)), per_s((WINDOW, KV_DIM))],
        out_shape=out_shape,
        scratch_shapes=[
            pltpu.VMEM((SAMPLE_SEQS, 8 + dec_seq, CONV_DIM), F32),
            pltpu.VMEM((1, N_HEADS, dec_seq, WINDOW), F32),
            pltpu.VMEM((1, N_HEADS, rows, rows), F32),
        ],
        compiler_params=_params(("arbitrary",)),
        name="sample_mixer",
    )(z, xbc, q, k, v, dt, sconv, sssm, ck, cv, cw, cb, dtb, alog, dskip_x, gn,
      tcum, tseq, expand, bkt_c, bkt_n, relb, sinks)


def _post1_body(x_ref, mix_ref, wout_ref, gc_ref, wcq_ref, h_ref, qc_ref):
    h = x_ref[...] + _dot(mix_ref[...], wout_ref[...])
    h_ref[...] = h
    qc_ref[...] = _dot(_rms(h, gc_ref[...]).astype(BF16), wcq_ref[...]).astype(BF16)


def _post1(x2, mix, w_out, g_cross, w_cq, tm):
    t = x2.shape[0]
    row = lambda w: pl.BlockSpec((tm, w), lambda i: (i, 0))
    return pl.pallas_call(
        _post1_body,
        grid=(t // tm,),
        in_specs=[row(D_MODEL), row(D_MODEL), _full_spec(w_out.shape), _full_spec(g_cross.shape),
                  _full_spec(w_cq.shape)],
        out_specs=[row(D_MODEL), row(CA_DIM)],
        out_shape=[jax.ShapeDtypeStruct((t, D_MODEL), F32), jax.ShapeDtypeStruct((t, CA_DIM), BF16)],
        compiler_params=_params(("parallel",)),
        name="out_proj",
    )(x2, mix, w_out, g_cross, w_cq)


def _mem_kv_body(mem_ref, g_ref, wk_ref, wv_ref, k_ref, v_ref, kb_ref, vb_ref):
    mn = _rms(mem_ref[...], g_ref[...]).astype(BF16)
    k = _dot(mn, wk_ref[...])
    v = _dot(mn, wv_ref[...])
    rows = k.shape[0]
    for h in range(CA_HEADS):
        hs = slice(h * CA_HEAD_DIM, (h + 1) * CA_HEAD_DIM)
        k_ref[pl.ds(h, rows, stride=CA_HEADS), :] = k[:, hs]
        v_ref[pl.ds(h, rows, stride=CA_HEADS), :] = v[:, hs]
    kb_ref[...] = k.astype(BF16)
    vb_ref[...] = v.astype(BF16)


def _mem_kv(mem2, g_mem, w_ck, w_cv, tm):
    t = mem2.shape[0]
    row = lambda w: pl.BlockSpec((tm, w), lambda i: (i, 0))
    return pl.pallas_call(
        _mem_kv_body,
        grid=(t // tm,),
        in_specs=[row(D_MODEL), _full_spec(g_mem.shape), _full_spec(w_ck.shape), _full_spec(w_cv.shape)],
        out_specs=[pl.BlockSpec((tm * CA_HEADS, CA_HEAD_DIM), lambda i: (i, 0))] * 2 + [row(CA_DIM)] * 2,
        out_shape=([jax.ShapeDtypeStruct((t * CA_HEADS, CA_HEAD_DIM), F32)] * 2
                   + [jax.ShapeDtypeStruct((t, CA_DIM), BF16)] * 2),
        compiler_params=_params(("parallel",)),
        name="mem_kv",
    )(mem2, g_mem, w_ck, w_cv)


def _cross_sample_body(q_ref, k_ref, v_ref, o_ref, *, n_seq, dec_seq):
    q = q_ref[...].astype(F32)
    n_keys = N_MEM * CA_HEADS
    col_head = lax.broadcasted_iota(jnp.int32, (1, n_keys), 1) & (CA_HEADS - 1)
    grp = CA_HEADS * dec_seq
    row_head = (lax.broadcasted_iota(jnp.int32, (n_seq * grp, 1), 0) // dec_seq) & (CA_HEADS - 1)
    own = col_head == row_head
    parts = []
    for i in range(n_seq):
        qi = q[i * dec_seq:(i + 1) * dec_seq]
        qs = jnp.concatenate([qi[:, h * CA_HEAD_DIM:(h + 1) * CA_HEAD_DIM] for h in range(CA_HEADS)], axis=0)
        parts.append(_dot_nt(qs.astype(BF16), k_ref[i].astype(BF16)))
    s = jnp.where(own, jnp.concatenate(parts, axis=0) * (CA_HEAD_DIM ** -0.5), NEG)
    m = jnp.max(s, axis=-1, keepdims=True)
    p = jnp.exp(s - m)
    rdenom = 1.0 / jnp.sum(p, axis=-1, keepdims=True)
    p = p.astype(BF16)
    rows = []
    for i in range(n_seq):
        gr = slice(i * grp, (i + 1) * grp)
        o = _dot(p[gr], v_ref[i].astype(BF16)) * rdenom[gr]
        rows.append(jnp.concatenate([o[h * dec_seq:(h + 1) * dec_seq] for h in range(CA_HEADS)], axis=-1))
    o_ref[...] = jnp.concatenate(rows, axis=0).astype(BF16)


def _cross_sample(qc, ck, cv, n_seq_total, dec_seq, n_seq):
    rows = n_seq * dec_seq
    tok = pl.BlockSpec((rows, CA_DIM), lambda i: (i, 0))
    mem = pl.BlockSpec((n_seq, N_MEM * CA_HEADS, CA_HEAD_DIM), lambda i: (i, 0, 0))
    return pl.pallas_call(
        functools.partial(_cross_sample_body, n_seq=n_seq, dec_seq=dec_seq),
        grid=(n_seq_total // n_seq,),
        in_specs=[tok, mem, mem],
        out_specs=tok,
        out_shape=jax.ShapeDtypeStruct((n_seq_total * dec_seq, CA_DIM), BF16),
        compiler_params=_params(("parallel",)),
        name="cross_sample",
    )(qc, ck, cv)


def _post2_body(h_ref, o_ref, wco_ref, gf_ref, wg_ref, wu_ref, wd_ref, gfin_ref, y_ref):
    h = h_ref[...] + _dot(o_ref[...], wco_ref[...])
    hn = _rms(h, gf_ref[...]).astype(BF16)
    acc = h
    for lo, hi in FF_SPLITS:
        act = _silu(_dot(hn, wg_ref[:, lo:hi])) * _dot(hn, wu_ref[:, lo:hi])
        acc = acc + _dot(act.astype(BF16), wd_ref[lo:hi, :])
    y_ref[...] = _rms(acc, gfin_ref[...])


def _post2(h1, o, w_co, g_ffn, w_gate, w_up, w_down, g_final, tm):
    t = h1.shape[0]
    row = lambda w: pl.BlockSpec((tm, w), lambda i: (i, 0))
    return pl.pallas_call(
        _post2_body,
        grid=(t // tm,),
        in_specs=[row(D_MODEL), row(CA_DIM), _full_spec(w_co.shape), _full_spec(g_ffn.shape),
                  _full_spec(w_gate.shape), _full_spec(w_up.shape), _full_spec(w_down.shape),
                  _full_spec(g_final.shape)],
        out_specs=row(D_MODEL),
        out_shape=jax.ShapeDtypeStruct((t, D_MODEL), F32),
        compiler_params=_params(("parallel",)),
        name="ffn",
    )(h1, o, w_co, g_ffn, w_gate, w_up, w_down, g_final)


def _prompt_consts():
    qi = np.arange(WINDOW)[:, None]
    ji = np.arange(2 * WINDOW)[None, :]
    dist = qi + WINDOW - ji
    inband = (dist >= 0) & (dist < WINDOW)
    bucket = np.where(inband, _t5_bucket_np(dist), -1)
    first = np.where(ji >= WINDOW, bucket, -1)
    buckets = np.stack([first, bucket]).astype(np.int32)
    return (jnp.asarray(_tril_np(CHUNK), BF16), jnp.asarray(buckets))


def _sample_consts(dec_seq, cache_len):
    rows = SAMPLE_SEQS * dec_seq
    r = np.arange(rows)
    same = (r[:, None] // dec_seq) == (r[None, :] // dec_seq)
    tcum = (same & (r[None, :] <= r[:, None])).astype(np.float32)
    tseq = same.astype(np.float32)
    t = np.arange(dec_seq)[:, None]
    j = np.arange(cache_len)[None, :]
    dist_c = t + cache_len - j
    bkt_c = np.where((dist_c >= 0) & (dist_c < WINDOW), _t5_bucket_np(dist_c), -1).astype(np.int32)
    dist_n = (r[:, None] % dec_seq) - (r[None, :] % dec_seq)
    ok = same & (dist_n >= 0) & (dist_n < WINDOW)
    bkt_n = np.where(ok, _t5_bucket_np(dist_n), -1).astype(np.int32)
    return (jnp.asarray(tcum, BF16), jnp.asarray(tseq, BF16), jnp.asarray(_expand_np(), BF16),
            jnp.asarray(bkt_c), jnp.asarray(bkt_n))


def _pick_tile(t, pref):
    tm = min(t, pref)
    while t % tm:
        tm //= 2
    return tm


def kernel(x_prompt, x_sample, mem_prompt, state_conv, state_ssm, cache_swa_k, cache_swa_v, cache_mem_k, cache_mem_v, rel_bias, g_mix, w_in, conv_w, conv_b, dt_bias, a_log, d_skip, g_ssd_norm, sinks, w_out, g_cross, g_mem, w_cq, w_ck, w_cv, w_co, g_ffn, w_gate, w_up, w_down, g_final):
    assert g_mix.shape[0] == 1, "single-layer trunk"
    batch, seq, _ = x_prompt.shape
    n_dec, dec_seq, _ = x_sample.shape
    cache_len = cache_swa_k.shape[2]
    assert seq % CHUNK == 0 and cache_len == WINDOW and n_dec % SAMPLE_SEQS == 0 and dec_seq == 8

    wi = w_in[0]
    s1 = SSD_D_INNER + CONV_DIM
    s2 = s1 + SSD_HEADS
    w_in_r = jnp.concatenate(
        [wi[:, :s1], wi[:, s2:], jnp.pad(wi[:, s1:s2], ((0, 0), (0, DT_PAD - SSD_HEADS)))], axis=1).astype(BF16)
    row = lambda a: a.reshape(1, -1).astype(F32)
    pad_h = lambda a: jnp.pad(a.reshape(1, -1).astype(F32), ((0, 0), (0, DT_PAD - SSD_HEADS)))
    small = (conv_w[0].astype(F32), row(conv_b[0]), pad_h(dt_bias[0]), pad_h(a_log[0]),
             jnp.repeat(d_skip[0].astype(F32), SSD_HEAD_DIM).reshape(1, -1), row(g_ssd_norm[0]),
             rel_bias.astype(F32), sinks[0].astype(F32))
    bf = lambda w: w[0].astype(BF16)
    w_out_b, w_cq_b, w_ck_b, w_cv_b, w_co_b = bf(w_out), bf(w_cq), bf(w_ck), bf(w_cv), bf(w_co)
    w_gate_b, w_up_b, w_down_b = bf(w_gate), bf(w_up), bf(w_down)
    g_mix_r, g_cross_r, g_mem_r, g_ffn_r, g_fin_r = row(g_mix[0]), row(g_cross[0]), row(g_mem[0]), row(g_ffn[0]), row(g_final)

    n_chunks = 2 if seq % (2 * CHUNK) == 0 else 1
    mem2 = mem_prompt.reshape(batch * N_MEM, D_MODEL)
    mk, mv, mk_b, mv_b = _mem_kv(mem2, g_mem_r, w_ck_b, w_cv_b, _pick_tile(batch * N_MEM, 512))
    dense = (g_mix_r, w_in_r, w_out_b, g_cross_r, w_cq_b, w_co_b, g_ffn_r, w_gate_b, w_up_b, w_down_b, g_fin_r)
    y_prompt, p_conv, p_ssm, p_k, p_v = _prompt_layer(
        x_prompt, mk_b.reshape(batch, N_MEM, CA_DIM), mv_b.reshape(batch, N_MEM, CA_DIM),
        dense, small, _prompt_consts(), n_chunks)

    def channel_major(cache):
        return jnp.transpose(cache, (0, 2, 3, 1)).reshape(n_dec, KV_DIM, cache_len)

    def position_major(cache_t):
        return jnp.transpose(cache_t.reshape(n_dec, N_KV_HEADS, ATTN_HEAD_DIM, cache_len), (0, 3, 1, 2))

    ts = n_dec * dec_seq
    xs2 = x_sample.reshape(ts, D_MODEL)
    tm_s = _pick_tile(ts, 512)
    z, xbc, q, k, v, dt = _in_proj(xs2, g_mix_r, w_in_r, tm_s)
    mix_s, s_conv, s_ssm, s_k, s_v = _sample_mixer(
        z, xbc, q, k, v, dt, jnp.transpose(state_conv[0], (1, 0, 2)),
        state_ssm[0].reshape(n_dec, SSD_D_INNER, D_STATE),
        channel_major(cache_swa_k[0]), channel_major(cache_swa_v[0]),
        small, _sample_consts(dec_seq, cache_len), n_dec, dec_seq)
    h1s, qcs = _post1(xs2, mix_s, w_out_b, g_cross_r, w_cq_b, tm_s)
    os_ = _cross_sample(qcs, cache_mem_k[0].reshape(n_dec, N_MEM * CA_HEADS, CA_HEAD_DIM),
                        cache_mem_v[0].reshape(n_dec, N_MEM * CA_HEADS, CA_HEAD_DIM), n_dec, dec_seq, SAMPLE_SEQS)
    y_sample = _post2(h1s, os_, w_co_b, g_ffn_r, w_gate_b, w_up_b, w_down_b, g_fin_r, tm_s)

    return (y_prompt.reshape(batch, seq, D_MODEL), y_sample.reshape(n_dec, dec_seq, D_MODEL),
            p_conv[None], p_ssm.reshape(1, batch, SSD_HEADS, SSD_HEAD_DIM, D_STATE),
            p_k.reshape(1, batch, WINDOW, N_KV_HEADS, ATTN_HEAD_DIM),
            p_v.reshape(1, batch, WINDOW, N_KV_HEADS, ATTN_HEAD_DIM),
            mk.reshape(1, batch, N_MEM, CA_HEADS, CA_HEAD_DIM), mv.reshape(1, batch, N_MEM, CA_HEADS, CA_HEAD_DIM),
            jnp.transpose(s_conv, (1, 0, 2))[None], s_ssm.reshape(1, n_dec, SSD_HEADS, SSD_HEAD_DIM, D_STATE),
            position_major(s_k)[None], position_major(s_v)[None])
```

```python
import functools
import math

import numpy as np
import jax
import jax.numpy as jnp
from jax import lax
from jax.experimental import pallas as pl
from jax.experimental.pallas import tpu as pltpu

F32 = jnp.float32
BF16 = jnp.bfloat16

D_MODEL = 1024
SSD_D_INNER = 512
SSD_HEAD_DIM = 64
SSD_HEADS = 8
SSD_GROUPS = 2
GROUP_W = SSD_D_INNER // SSD_GROUPS
D_STATE = 128
CONV_K = 4
CONV_DIM = SSD_D_INNER + 2 * SSD_GROUPS * D_STATE
CHUNK = 128
ATTN_DIM = 512
ATTN_HEAD_DIM = 64
N_HEADS = 8
N_KV_HEADS = 2
Q_PER_KV = N_HEADS // N_KV_HEADS
KV_DIM = N_KV_HEADS * ATTN_HEAD_DIM
WINDOW = 128
N_BUCKETS = 32
MAX_EXACT = N_BUCKETS // 2
MAX_DISTANCE = 128
N_MEM = 256
CA_HEADS = 4
CA_HEAD_DIM = 128
CA_DIM = CA_HEADS * CA_HEAD_DIM
D_FF = 2816
EPS = 1e-6

LANES = 128
HALF = LANES // 2
DT_PAD = LANES
COL_Z = 0
COL_XBC = COL_Z + SSD_D_INNER
COL_Q = COL_XBC + CONV_DIM
COL_K = COL_Q + ATTN_DIM
COL_V = COL_K + KV_DIM
COL_DT = COL_V + KV_DIM
IN_COLS = COL_DT + DT_PAD
NEG = -1e30
SAMPLE_SEQS = 16
VMEM_LIMIT = 56 * 1024 * 1024
FF_SPLITS = ((0, 1024), (1024, 2048), (2048, D_FF))
SECOND_STREAM_LEAD = 0.5
FF_PIECES = tuple((lo, min(lo + 512, D_FF)) for lo in range(0, D_FF, 512))
PROJ_PIECES = ((COL_Z, COL_XBC), (COL_XBC, COL_XBC + 512), (COL_XBC + 512, COL_Q), (COL_Q, COL_K), (COL_K, IN_COLS))


def _rms(x, g):
    return x * lax.rsqrt(jnp.mean(x * x, axis=-1, keepdims=True) + EPS) * g


def _silu(x):
    return x * jax.nn.sigmoid(x)


def _softplus(x):
    return jnp.maximum(x, 0.0) + jnp.log1p(jnp.exp(-jnp.abs(x)))


def _dot(a, b):
    return jnp.dot(a, b, preferred_element_type=F32)


def _dot_nt(a, b):
    return lax.dot_general(a, b, (((1,), (1,)), ((), ())), preferred_element_type=F32)


def _split3(a):
    hi = a.astype(BF16)
    r = a - hi.astype(F32)
    mid = r.astype(BF16)
    lo = (r - mid.astype(F32)).astype(BF16)
    return hi, mid, lo


def _sel_left(t01, a):
    hi, mid, lo = _split3(a)
    return _dot(t01, hi) + _dot(t01, mid) + _dot(t01, lo)


def _sel_right(a, e01):
    hi, mid, lo = _split3(a)
    return _dot(hi, e01) + _dot(mid, e01) + _dot(lo, e01)


def _lane_lo():
    return lax.broadcasted_iota(jnp.int32, (1, LANES), 1) < HALF


def _t5_bucket_np(dist):
    n = np.maximum(dist, 0)
    ratio = np.log(np.maximum(n, 1).astype(np.float32) / np.float32(MAX_EXACT))
    large = MAX_EXACT + (ratio / np.float32(math.log(MAX_DISTANCE / MAX_EXACT))
                         * np.float32(N_BUCKETS - MAX_EXACT)).astype(np.int32)
    large = np.minimum(large, N_BUCKETS - 1)
    return np.where(n < MAX_EXACT, n, large).astype(np.int32)


def _tril_np(n):
    return np.tril(np.ones((n, n), np.float32))


def _expand_np():
    e = np.zeros((LANES, SSD_D_INNER), np.float32)
    for h in range(SSD_HEADS):
        e[h, h * SSD_HEAD_DIM:(h + 1) * SSD_HEAD_DIM] = 1.0
    return e


def _full_spec(shape):
    nd = len(shape)
    return pl.BlockSpec(shape, lambda *_: (0,) * nd, pipeline_mode=pl.Buffered(1))


def _smem_spec():
    return pl.BlockSpec(memory_space=pltpu.SMEM)


def _params(sem):
    return pltpu.CompilerParams(dimension_semantics=sem, vmem_limit_bytes=VMEM_LIMIT)


def _in_proj_body(x_ref, g_ref, w_ref, z_ref, xbc_ref, q_ref, k_ref, v_ref, dt_ref):
    xn = _rms(x_ref[...], g_ref[...]).astype(BF16)

    def seg(lo, hi):
        return _dot(xn, w_ref[:, lo:hi])

    z_ref[...] = seg(COL_Z, COL_XBC)
    xbc_ref[...] = seg(COL_XBC, COL_Q)
    q_ref[...] = (seg(COL_Q, COL_K) * (ATTN_HEAD_DIM ** -0.5)).astype(BF16)
    k_ref[...] = seg(COL_K, COL_V)
    v_ref[...] = seg(COL_V, COL_DT)
    dt_ref[...] = seg(COL_DT, IN_COLS)


def _in_proj(x2, g_mix, w_in_r, tm):
    t = x2.shape[0]
    row = lambda w: pl.BlockSpec((tm, w), lambda i: (i, 0))
    outs = [(SSD_D_INNER, F32), (CONV_DIM, F32), (ATTN_DIM, BF16), (KV_DIM, F32), (KV_DIM, F32), (DT_PAD, F32)]
    return pl.pallas_call(
        _in_proj_body,
        grid=(t // tm,),
        in_specs=[row(D_MODEL), _full_spec((1, D_MODEL)), _full_spec((D_MODEL, IN_COLS))],
        out_specs=[row(w) for w, _ in outs],
        out_shape=[jax.ShapeDtypeStruct((t, w), d) for w, d in outs],
        compiler_params=_params(("parallel",)),
        name="in_proj",
    )(x2, g_mix, w_in_r)


def _conv_taps(cw_ref, cb_ref, taps):
    acc = cb_ref[...] + taps[0] * cw_ref[0:1, :]
    for k in range(1, CONV_K):
        acc = acc + taps[k] * cw_ref[k:k + 1, :]
    return _silu(acc)


def _ssd_prepare(conv, dt_raw, dtb_ref, alog_ref, tcum, total_fn, extra_fn, expand):
    xs = conv[:, :SSD_D_INNER]
    bm = conv[:, SSD_D_INNER:SSD_D_INNER + SSD_GROUPS * D_STATE]
    cm = conv[:, SSD_D_INNER + SSD_GROUPS * D_STATE:]
    dt = _softplus(dt_raw + dtb_ref[...])
    a = dt * (-jnp.exp(alog_ref[...]))
    cs = _sel_left(tcum, a)
    total = total_fn(cs, a)
    pieces = [dt, jnp.exp(total - cs), jnp.exp(cs)] + extra_fn(total)
    rows = cs.shape[0]
    ex = _sel_right(jnp.concatenate(pieces, axis=0), expand)
    ex = [ex[i * rows:(i + 1) * rows] for i in range(len(pieces))]
    return xs, bm, cm, cs, cs.T, ex


def _ssd_diag(cs, cs_t, cb_g, xdt, mask, g):
    lo = _lane_lo()
    out = []
    for pr in range(2):
        h0 = g * 4 + 2 * pr
        xp = xdt[:, (h0 // 2) * LANES:(h0 // 2 + 1) * LANES]
        x_lo = jnp.where(lo, xp, 0.0).astype(BF16)
        x_hi = jnp.where(lo, 0.0, xp).astype(BF16)
        acc = None
        for h, xh in ((h0, x_lo), (h0 + 1, x_hi)):
            diff = cs[:, h:h + 1] - cs_t[h:h + 1, :]
            decay = jnp.exp(jnp.where(mask, diff, -jnp.inf))
            part = _dot((cb_g * decay).astype(BF16), xh)
            acc = part if acc is None else acc + part
        out.append(acc)
    return jnp.concatenate(out, axis=-1)


def _gated_norm(y, z, gn_ref):
    yf = y * _silu(z)
    parts = []
    for g in range(SSD_GROUPS):
        yg = yf[:, g * GROUP_W:(g + 1) * GROUP_W]
        parts.append(yg * lax.rsqrt(jnp.mean(yg * yg, axis=-1, keepdims=True) + EPS))
    return jnp.concatenate(parts, axis=-1) * gn_ref[...]


def _build_bias(bias_ref, bucket_of, relb_ref, n_tables):
    for i in range(n_tables):
        for h in range(N_HEADS):
            bias_ref[i, h] = jnp.full(bias_ref.shape[2:], NEG, F32)

    def body(t, carry):
        for i in range(n_tables):
            hit = bucket_of(i) == t
            for h in range(N_HEADS):
                bias_ref[i, h] = jnp.where(hit, relb_ref[t, h], bias_ref[i, h])
        return carry

    lax.fori_loop(0, N_BUCKETS, body, 0)


def _alternate(first, second):
    streams = [[0.0, 1.0, first], [0.0, SECOND_STREAM_LEAD, second]]
    while streams:
        entry = min(streams, key=lambda e: e[0])
        try:
            entry[0] += entry[1] * next(entry[2])
        except StopIteration:
            streams.remove(entry)


def _layer_step_body(x_ref, xprev_ref, mk_ref, mv_ref, gmix_ref, win_ref, wout_ref, gc_ref, wcq_ref,
                     wco_ref, gf_ref, wg_ref, wu_ref, wd_ref, gfin_ref,
                     cw_ref, cb_ref, dtb_ref, alog_ref, dskip_ref, gn_ref,
                     tril_ref, bucket_ref, relb_ref, sink_ref,
                     y_ref, conv_out, ssm_out, k_out, v_out,
                     xpad, state_t, kbuf, vbuf, bias, mix, proj, *, n_chunks, tiles_per_seq):
    step = pl.program_id(0)
    n_tiles = pl.num_programs(0) - 1
    c = lax.rem(jnp.minimum(step, n_tiles - 1), tiles_per_seq)
    last = tiles_per_seq - 1
    tq = n_chunks * CHUNK

    @pl.when(step == 0)
    def _():
        _build_bias(bias, lambda i: bucket_ref[i], relb_ref, 2)
        mix[...] = jnp.zeros_like(mix)

    @pl.when(c == 0)
    def _():
        xpad[0:8, :] = jnp.zeros((8, CONV_DIM), F32)
        state_t[...] = jnp.zeros_like(state_t)
        kbuf[:, 0:CHUNK, :] = jnp.zeros((2, CHUNK, KV_DIM), BF16)
        vbuf[:, 0:CHUNK, :] = jnp.zeros((2, CHUNK, KV_DIM), BF16)

    def back():
        h1 = xprev_ref[0] + _dot(mix[...], wout_ref[...])
        yield 4
        qc = _dot(_rms(h1, gc_ref[...]).astype(BF16), wcq_ref[...]).astype(BF16)
        yield 2
        heads = []
        for h in range(CA_HEADS):
            hs = slice(h * CA_HEAD_DIM, (h + 1) * CA_HEAD_DIM)
            sc = _dot_nt(qc[:, hs], mk_ref[0, :, hs]) * (CA_HEAD_DIM ** -0.5)
            m = jnp.max(sc, axis=-1, keepdims=True)
            p = jnp.exp(sc - m)
            heads.append(_dot(p.astype(BF16), mv_ref[0, :, hs]) / jnp.sum(p, axis=-1, keepdims=True))
            yield 3
        o = jnp.concatenate(heads, axis=-1).astype(BF16)
        h2 = h1 + _dot(o, wco_ref[...])
        hn = _rms(h2, gf_ref[...]).astype(BF16)
        yield 4
        acc = h2
        for f_lo, f_hi in FF_PIECES:
            width = (f_hi - f_lo) / 512
            gate = _dot(hn, wg_ref[:, f_lo:f_hi])
            yield 5 * width
            act = (_silu(gate) * _dot(hn, wu_ref[:, f_lo:f_hi])).astype(BF16)
            yield 6 * width
            acc = acc + _dot(act, wd_ref[f_lo:f_hi, :])
            yield 5 * width
        y_ref[0] = _rms(acc, gfin_ref[...])

    def front():
        x = x_ref[0]
        xn = _rms(x, gmix_ref[...]).astype(BF16)
        for p_lo, p_hi in PROJ_PIECES:
            if COL_XBC <= p_lo and p_hi <= COL_Q:
                xpad[8:8 + tq, p_lo - COL_XBC:p_hi - COL_XBC] = _dot(xn, win_ref[:, p_lo:p_hi])
            else:
                proj[:, p_lo:p_hi] = _dot(xn, win_ref[:, p_lo:p_hi])
            yield 4

        conv_halves = []
        half_w = CONV_DIM // 2
        for hf in range(2):
            cs_ = slice(hf * half_w, (hf + 1) * half_w)
            acc = cb_ref[:, cs_] + xpad[5:5 + tq, cs_] * cw_ref[0:1, cs_]
            for kk in range(1, CONV_K):
                acc = acc + xpad[5 + kk:5 + kk + tq, cs_] * cw_ref[kk:kk + 1, cs_]
            conv_halves.append(_silu(acc))
            yield 6
        xpad[5:8, :] = xpad[5 + tq:8 + tq, :]
        xs_all = conv_halves[0]
        bc_all = conv_halves[1]

        k_new = proj[:, COL_K:COL_V]
        v_new = proj[:, COL_V:COL_DT]
        kbuf[0, CHUNK:, :] = k_new.astype(BF16)
        kbuf[1, CHUNK:, :] = pltpu.roll(k_new, HALF, 1).astype(BF16)
        vbuf[0, CHUNK:, :] = v_new.astype(BF16)
        vbuf[1, CHUNK:, :] = pltpu.roll(v_new, HALF, 1).astype(BF16)
        q = (proj[:, COL_Q:COL_K] * (ATTN_HEAD_DIM ** -0.5)).astype(BF16)
        yield 2

        row = lax.broadcasted_iota(jnp.int32, (CHUNK, CHUNK), 0)
        col = lax.broadcasted_iota(jnp.int32, (CHUNK, CHUNK), 1)
        causal = col <= row
        lo = _lane_lo()
        for ci in range(n_chunks):
            rs = slice(ci * CHUNK, (ci + 1) * CHUNK)
            xs = xs_all[rs]
            bm = bc_all[rs, 0:SSD_GROUPS * D_STATE]
            cm = bc_all[rs, SSD_GROUPS * D_STATE:]
            dt = _softplus(proj[rs, COL_DT:IN_COLS] + dtb_ref[...])
            a = dt * (-jnp.exp(alog_ref[...]))
            cs = _sel_left(tril_ref[...], a)
            cs_t = cs.T
            dt_t = dt.T
            total = cs[CHUNK - 1:CHUNK, :]
            ecs = jnp.exp(cs)
            w_end = dt * jnp.exp(total - cs)
            bb = bm.astype(BF16)
            cb = cm.astype(BF16)
            yield 5
            y_parts = []
            for g in range(SSD_GROUPS):
                gs = slice(g * GROUP_W, (g + 1) * GROUP_W)
                ns = slice(g * D_STATE, (g + 1) * D_STATE)
                cb_g = _dot_nt(cb[:, ns], bb[:, ns])
                y_diag, xw, ecs_g = [], [], []
                for pr in range(2):
                    h0 = g * 4 + 2 * pr
                    xp = xs[:, (h0 // 2) * LANES:(h0 // 2 + 1) * LANES]
                    ecs_g.append(jnp.where(lo, ecs[:, h0:h0 + 1], ecs[:, h0 + 1:h0 + 2]))
                    xw.append((xp * jnp.where(lo, w_end[:, h0:h0 + 1], w_end[:, h0 + 1:h0 + 2])).astype(BF16))
                    x2 = jnp.concatenate([jnp.where(lo, xp, 0.0), jnp.where(lo, 0.0, xp)], axis=0).astype(BF16)
                    gmat = []
                    for h in (h0, h0 + 1):
                        diff = cs[:, h:h + 1] - cs_t[h:h + 1, :]
                        decay = jnp.exp(jnp.where(causal, diff, -jnp.inf))
                        gmat.append((cb_g * decay * dt_t[h:h + 1, :]).astype(BF16))
                    y_diag.append(_dot(jnp.concatenate(gmat, axis=1), x2))
                ecs_x = jnp.concatenate(ecs_g, axis=-1)
                st_old = state_t[:, gs]
                y_off = _dot(cb[:, ns], st_old.astype(BF16)) * ecs_x
                y_parts.append(jnp.concatenate(y_diag, axis=-1) + y_off)
                b_t = bm[:, ns].T.astype(BF16)
                state_t[:, gs] = ecs_x[CHUNK - 1:CHUNK, :] * st_old + _dot(b_t, jnp.concatenate(xw, axis=-1))
                yield 9
            y = jnp.concatenate(y_parts, axis=-1) + dskip_ref[...] * xs
            mix[rs, 0:SSD_D_INNER] = _gated_norm(y, proj[rs, COL_Z:COL_XBC], gn_ref).astype(BF16)
            yield 3

            keys = slice(ci * CHUNK, (ci + 2) * CHUNK)
            table = jnp.minimum(c, 1) if ci == 0 else 1
            probs, denoms = {}, {}
            for variant in range(2):
                hv = [h for h in range(N_HEADS) if (h // Q_PER_KV + h % 2) % 2 == variant]
                qms = []
                for h in hv:
                    q_pair = q[rs, (h // 2) * LANES:(h // 2 + 1) * LANES]
                    qms.append(jnp.where(lo, q_pair, 0.0) if h % 2 == 0 else jnp.where(lo, 0.0, q_pair))
                sc_all = _dot_nt(jnp.concatenate(qms, axis=0).astype(BF16), kbuf[variant, keys, :])
                for i, h in enumerate(hv):
                    sc = sc_all[i * CHUNK:(i + 1) * CHUNK] + bias[table, h]
                    sink = sink_ref[h]
                    m = jnp.maximum(jnp.max(sc, axis=-1, keepdims=True), sink)
                    p = jnp.exp(sc - m)
                    denoms[h] = jnp.sum(p, axis=-1, keepdims=True) + jnp.exp(sink - m)
                    probs[h] = p.astype(BF16)
                yield 10
            parts = {}
            for variant in range(2):
                vv = vbuf[variant, keys, :]
                for par in range(2):
                    hv = [h for h in range(N_HEADS) if (h // Q_PER_KV + h % 2) % 2 == variant and h % 2 == par]
                    vm = jnp.where(lo, vv, 0.0) if par == 0 else jnp.where(lo, 0.0, vv)
                    o_all = _dot(jnp.concatenate([probs[h] for h in hv], axis=0), vm.astype(BF16))
                    for i, h in enumerate(hv):
                        parts[h] = o_all[i * CHUNK:(i + 1) * CHUNK] / denoms[h]
            for pair in range(N_HEADS // 2):
                mix[rs, SSD_D_INNER + pair * LANES:SSD_D_INNER + (pair + 1) * LANES] = (
                    parts[2 * pair] + parts[2 * pair + 1]).astype(BF16)
            yield 4
        kbuf[:, 0:CHUNK, :] = kbuf[:, tq:tq + CHUNK, :]
        vbuf[:, 0:CHUNK, :] = vbuf[:, tq:tq + CHUNK, :]

    _alternate(back(), front())

    @pl.when((c == last) & (step < n_tiles))
    def _():
        conv_out[0] = xpad[5:8, :]
        ssm_out[0] = state_t[...].T
        k_out[0] = proj[tq - WINDOW:tq, COL_K:COL_V]
        v_out[0] = proj[tq - WINDOW:tq, COL_V:COL_DT]


def _prompt_layer(x3, mk_b, mv_b, dense, small, consts, n_chunks):
    cw, cb, dtb, alog, dskip_x, gn, relb, sinks = small
    tril, buckets = consts
    batch, seq, _ = x3.shape
    tq = n_chunks * CHUNK
    tps = seq // tq
    n_tiles = batch * tps
    front = lambda s: jnp.minimum(s, n_tiles - 1)
    back = lambda s: jnp.maximum(s - 1, 0)
    x_spec = pl.BlockSpec((1, tq, D_MODEL), lambda s: (front(s) // tps, front(s) % tps, 0))
    xprev_spec = pl.BlockSpec((1, tq, D_MODEL), lambda s: (back(s) // tps, back(s) % tps, 0))
    y_spec = pl.BlockSpec((1, tq, D_MODEL), lambda s: (back(s) // tps, back(s) % tps, 0))
    mem_spec = pl.BlockSpec((1, N_MEM, CA_DIM), lambda s: (back(s) // tps, 0, 0))
    per_b = lambda shp: pl.BlockSpec((1,) + shp, lambda s: (front(s) // tps,) + (0,) * len(shp))
    out_shape = [
        jax.ShapeDtypeStruct((batch, seq, D_MODEL), F32),
        jax.ShapeDtypeStruct((batch, CONV_K - 1, CONV_DIM), F32),
        jax.ShapeDtypeStruct((batch, SSD_D_INNER, D_STATE), F32),
        jax.ShapeDtypeStruct((batch, WINDOW, KV_DIM), F32),
        jax.ShapeDtypeStruct((batch, WINDOW, KV_DIM), F32),
    ]
    full = list(dense) + [cw, cb, dtb, alog, dskip_x, gn, tril, buckets]
    return pl.pallas_call(
        functools.partial(_layer_step_body, n_chunks=n_chunks, tiles_per_seq=tps),
        grid=(n_tiles + 1,),
        in_specs=([x_spec, xprev_spec, mem_spec, mem_spec] + [_full_spec(a.shape) for a in full]
                  + [_smem_spec(), _smem_spec()]),
        out_specs=[y_spec, per_b((CONV_K - 1, CONV_DIM)), per_b((SSD_D_INNER, D_STATE)),
                   per_b((WINDOW, KV_DIM)), per_b((WINDOW, KV_DIM))],
        out_shape=out_shape,
        scratch_shapes=[
            pltpu.VMEM((8 + tq, CONV_DIM), F32),
            pltpu.VMEM((D_STATE, SSD_D_INNER), F32),
            pltpu.VMEM((2, CHUNK + tq, KV_DIM), BF16),
            pltpu.VMEM((2, CHUNK + tq, KV_DIM), BF16),
            pltpu.VMEM((2, N_HEADS, WINDOW, 2 * WINDOW), F32),
            pltpu.VMEM((tq, D_MODEL), BF16),
            pltpu.VMEM((tq, IN_COLS), F32),
        ],
        compiler_params=_params(("arbitrary",)),
        name="prompt_layer",
    )(x3, x3, mk_b, mv_b, *full, relb, sinks)


def _sample_mixer_body(x_ref, gmix_ref, win_ref,
                       sconv_ref, sssm_ref, ck_ref, cv_ref,
                       cw_ref, cb_ref, dtb_ref, alog_ref, dskip_ref, gn_ref,
                       tcum_ref, tseq_ref, expand_ref, bkt_c_ref, bkt_n_ref, relb_ref, sink_ref,
                       mix_ref, conv_out, ssm_out, k_out, v_out,
                       xpad, bias_c, bias_n, *, dec_seq):
    step = pl.program_id(0)
    n_seq = SAMPLE_SEQS
    rows = n_seq * dec_seq

    @pl.when(step == 0)
    def _():
        _build_bias(bias_c, lambda i: bkt_c_ref[...], relb_ref, 1)
        _build_bias(bias_n, lambda i: bkt_n_ref[...], relb_ref, 1)

    xn = _rms(x_ref[...], gmix_ref[...]).astype(BF16)

    def seg(lo, hi):
        return _dot(xn, win_ref[:, lo:hi])

    z = seg(COL_Z, COL_XBC)
    xbc = seg(COL_XBC, COL_Q)
    qf = (seg(COL_Q, COL_K) * (ATTN_HEAD_DIM ** -0.5)).astype(BF16).astype(F32)
    k_new = seg(COL_K, COL_V)
    v_new = seg(COL_V, COL_DT)
    dt_raw = seg(COL_DT, IN_COLS)

    xpad[:, 8:8 + dec_seq, :] = xbc.reshape(n_seq, dec_seq, CONV_DIM)
    for r in range(CONV_K - 1):
        xpad[:, 5 + r, :] = sconv_ref[r]
    taps = [xpad[:, 5 + k:5 + k + dec_seq, :].reshape(rows, CONV_DIM) for k in range(CONV_K - 1)] + [xbc]
    conv = _conv_taps(cw_ref, cb_ref, taps)
    for r in range(CONV_K - 1):
        conv_out[r] = xpad[:, 5 + dec_seq + r, :]

    row = lax.broadcasted_iota(jnp.int32, (rows, rows), 0)
    col = lax.broadcasted_iota(jnp.int32, (rows, rows), 1)
    tseq = tseq_ref[...]
    same_seq = tseq > 0
    causal = same_seq & (col <= row)
    xs, bm, cm, cs, cs_t, (dt_x, dend_x, ecs_x, seqdec_x) = _ssd_prepare(
        conv, dt_raw, dtb_ref, alog_ref, tcum_ref[...],
        lambda cs_, a_: _sel_left(tseq, a_), lambda total: [jnp.exp(total)], expand_ref[...])
    xdt = xs * dt_x
    bb = bm.astype(BF16)
    cb = cm.astype(BF16)
    seq_of_row = lax.broadcasted_iota(jnp.int32, (rows, 1), 0) // dec_seq
    seq_of_lane = lax.broadcasted_iota(jnp.int32, (1, rows), 1) // dec_seq
    xde_t = (xdt * dend_x).T
    y_off = [None] * SSD_GROUPS
    for i in range(n_seq):
        st = sssm_ref[i]
        dec = jnp.concatenate(
            [jnp.broadcast_to(seqdec_x[i * dec_seq:i * dec_seq + 1, h * SSD_HEAD_DIM:h * SSD_HEAD_DIM + 1],
                              (SSD_HEAD_DIM, D_STATE)) for h in range(SSD_HEADS)], axis=0)
        new_parts = []
        for g in range(SSD_GROUPS):
            gs = slice(g * GROUP_W, (g + 1) * GROUP_W)
            ns = slice(g * D_STATE, (g + 1) * D_STATE)
            c_i = jnp.where(seq_of_row == i, cm[:, ns], 0.0).astype(BF16)
            part = _dot_nt(c_i, st[gs].astype(BF16))
            y_off[g] = part if y_off[g] is None else y_off[g] + part
            x_i = jnp.where(seq_of_lane == i, xde_t[gs], 0.0).astype(BF16)
            new_parts.append(dec[gs] * st[gs] + _dot(x_i, bb[:, ns]))
        ssm_out[i] = jnp.concatenate(new_parts, axis=0)
    y_parts = []
    for g in range(SSD_GROUPS):
        gs = slice(g * GROUP_W, (g + 1) * GROUP_W)
        ns = slice(g * D_STATE, (g + 1) * D_STATE)
        cb_g = _dot_nt(cb[:, ns], bb[:, ns])
        y_parts.append(_ssd_diag(cs, cs_t, cb_g, xdt, causal, g) + y_off[g] * ecs_x[:, gs])
    y = jnp.concatenate(y_parts, axis=-1) + dskip_ref[...] * xs
    mix_ref[:, 0:SSD_D_INNER] = _gated_norm(y, z, gn_ref).astype(BF16)

    lo = _lane_lo()
    k_var =[k_new.astype(BF16), pltpu.roll(k_new, HALF, 1).astype(BF16)]
    v_new_r = pltpu.roll(v_new, HALF, 1)
    v_dup = [jnp.where(lo, v_new, v_new_r).astype(BF16), jnp.where(lo, v_new_r, v_new).astype(BF16)]
    q_masked = []
    s_new = []
    for h in range(N_HEADS):
        pair, par = h // 2, h % 2
        j = h // Q_PER_KV
        q_pair = qf[:, pair * LANES:(pair + 1) * LANES]
        qm = jnp.where(lo, q_pair, 0.0) if par == 0 else jnp.where(lo, 0.0, q_pair)
        q_masked.append(qm)
        s_new.append(_dot_nt(qm.astype(BF16), k_var[(j + par) % 2]) + bias_n[0, h])
    stack_rows = lax.broadcasted_iota(jnp.int32, (Q_PER_KV * dec_seq, 1), 0) // dec_seq
    k_new_t = k_new.T
    v_new_t = v_new.T
    keep = WINDOW - dec_seq
    old_lane = lax.broadcasted_iota(jnp.int32, (1, WINDOW), 1) < keep
    grp = Q_PER_KV * dec_seq
    sc_parts, sn_parts, sink_parts = [], [], []
    for i in range(n_seq):
        rs = slice(i * dec_seq, (i + 1) * dec_seq)
        kc_t = ck_ref[i]
        for j in range(N_KV_HEADS):
            heads = range(j * Q_PER_KV, (j + 1) * Q_PER_KV)
            cj = slice(j * ATTN_HEAD_DIM, (j + 1) * ATTN_HEAD_DIM)
            kdup_t = jnp.concatenate([kc_t[cj], kc_t[cj]], axis=0).astype(BF16)
            qs = jnp.concatenate([q_masked[h][rs] for h in heads], axis=0).astype(BF16)
            sc_parts.append(_dot(qs, kdup_t) + jnp.concatenate([bias_c[0, h] for h in heads], axis=0))
            sn_parts.append(jnp.concatenate([s_new[h][rs] for h in heads], axis=0))
            if i == 0:
                sink_j = jnp.zeros((grp, 1), F32)
                for hh, h in enumerate(heads):
                    sink_j = jnp.where(stack_rows == hh, sink_ref[h], sink_j)
                sink_parts.append(sink_j)
    sc = jnp.concatenate(sc_parts, axis=0)
    sn = jnp.concatenate(sn_parts, axis=0)
    sink = jnp.concatenate(sink_parts * n_seq, axis=0)
    m = jnp.maximum(jnp.maximum(jnp.max(sc, axis=-1, keepdims=True), jnp.max(sn, axis=-1, keepdims=True)), sink)
    pc = jnp.exp(sc - m)
    pn = jnp.exp(sn - m)
    rdenom = 1.0 / (jnp.sum(pc, axis=-1, keepdims=True) + jnp.sum(pn, axis=-1, keepdims=True) + jnp.exp(sink - m))
    pc = pc.astype(BF16)
    pn = pn.astype(BF16)
    att_rows = []
    for i in range(n_seq):
        vc_t = cv_ref[i]
        pieces = []
        for j in range(N_KV_HEADS):
            cj = slice(j * ATTN_HEAD_DIM, (j + 1) * ATTN_HEAD_DIM)
            gr = slice((i * N_KV_HEADS + j) * grp, (i * N_KV_HEADS + j + 1) * grp)
            vdup_t = jnp.concatenate([vc_t[cj], vc_t[cj]], axis=0).astype(BF16)
            o = (_dot_nt(pc[gr], vdup_t) + _dot(pn[gr], v_dup[j])) * rdenom[gr]
            for pr in range(Q_PER_KV // 2):
                even = o[(2 * pr) * dec_seq:(2 * pr + 1) * dec_seq]
                odd = o[(2 * pr + 1) * dec_seq:(2 * pr + 2) * dec_seq]
                pieces.append(jnp.where(lo, even, odd))
        att_rows.append(jnp.concatenate(pieces, axis=-1))
        new_shift = (keep - i * dec_seq) % WINDOW
        k_out[i] = jnp.where(old_lane, pltpu.roll(ck_ref[i], keep, 1), pltpu.roll(k_new_t, new_shift, 1))
        v_out[i] = jnp.where(old_lane, pltpu.roll(vc_t, keep, 1), pltpu.roll(v_new_t, new_shift, 1))
    mix_ref[:, SSD_D_INNER:] = jnp.concatenate(att_rows, axis=0).astype(BF16)


def _sample_mixer(x2, g_mix, w_in_r, sconv, sssm, ck, cv, small, consts, n_seq_total, dec_seq):
    cw, cb, dtb, alog, dskip_x, gn, relb, sinks = small
    tcum, tseq, expand, bkt_c, bkt_n = consts
    rows = SAMPLE_SEQS * dec_seq
    tok = lambda w: pl.BlockSpec((rows, w), lambda i: (i, 0))
    per_s = lambda s: pl.BlockSpec((SAMPLE_SEQS,) + s, lambda i: (i,) + (0,) * len(s))
    out_shape = [
        jax.ShapeDtypeStruct((n_seq_total * dec_seq, D_MODEL), BF16),
        jax.ShapeDtypeStruct((CONV_K - 1, n_seq_total, CONV_DIM), F32),
        jax.ShapeDtypeStruct((n_seq_total, SSD_D_INNER, D_STATE), F32),
        jax.ShapeDtypeStruct((n_seq_total, WINDOW, KV_DIM), F32),
        jax.ShapeDtypeStruct((n_seq_total, WINDOW, KV_DIM), F32),
    ]
    conv_spec = pl.BlockSpec((CONV_K - 1, SAMPLE_SEQS, CONV_DIM), lambda i: (0, i, 0))
    return pl.pallas_call(
        functools.partial(_sample_mixer_body, dec_seq=dec_seq),
        grid=(n_seq_total // SAMPLE_SEQS,),
        in_specs=[tok(D_MODEL), _full_spec(g_mix.shape), _full_spec(w_in_r.shape),
                  conv_spec, per_s((SSD_D_INNER, D_STATE)),
                  per_s((WINDOW, KV_DIM)), per_s((WINDOW, KV_DIM)),
                  _full_spec(cw.shape), _full_spec(cb.shape), _full_spec(dtb.shape), _full_spec(alog.shape),
                  _full_spec(dskip_x.shape), _full_spec(gn.shape),
                  _full_spec(tcum.shape), _full_spec(tseq.shape), _full_spec(expand.shape),
                  _full_spec(bkt_c.shape), _full_spec(bkt_n.shape), _smem_spec(), _smem_spec()],
        out_specs=[tok(D_MODEL), conv_spec, per_s((SSD_D_INNER, D_STATE)),
                   per_s((WINDOW, KV_DIM)), per_s((WINDOW, KV_DIM))],
        out_shape=out_shape,
        scratch_shapes=[
            pltpu.VMEM((SAMPLE_SEQS, 8 + dec_seq, CONV_DIM), F32),
            pltpu.VMEM((1, N_HEADS, dec_seq, WINDOW), F32),
            pltpu.VMEM((1, N_HEADS, rows, rows), F32),
        ],
        compiler_params=_params(("arbitrary",)),
        name="sample_mixer",
    )(x2, g_mix, w_in_r, sconv, sssm, ck, cv, cw, cb, dtb, alog, dskip_x, gn,
      tcum, tseq, expand, bkt_c, bkt_n, relb, sinks)


def _post1_body(x_ref, mix_ref, wout_ref, gc_ref, wcq_ref, h_ref, qc_ref):
    h = x_ref[...] + _dot(mix_ref[...], wout_ref[...])
    h_ref[...] = h
    qc_ref[...] = _dot(_rms(h, gc_ref[...]).astype(BF16), wcq_ref[...]).astype(BF16)


def _post1(x2, mix, w_out, g_cross, w_cq, tm):
    t = x2.shape[0]
    row = lambda w: pl.BlockSpec((tm, w), lambda i: (i, 0))
    return pl.pallas_call(
        _post1_body,
        grid=(t // tm,),
        in_specs=[row(D_MODEL), row(D_MODEL), _full_spec(w_out.shape), _full_spec(g_cross.shape),
                  _full_spec(w_cq.shape)],
        out_specs=[row(D_MODEL), row(CA_DIM)],
        out_shape=[jax.ShapeDtypeStruct((t, D_MODEL), F32), jax.ShapeDtypeStruct((t, CA_DIM), BF16)],
        compiler_params=_params(("parallel",)),
        name="out_proj",
    )(x2, mix, w_out, g_cross, w_cq)


def _mem_kv_body(mem_ref, g_ref, wk_ref, wv_ref, k_ref, v_ref, kb_ref, vb_ref):
    mn = _rms(mem_ref[...], g_ref[...]).astype(BF16)
    k = _dot(mn, wk_ref[...])
    v = _dot(mn, wv_ref[...])
    rows = k.shape[0]
    for h in range(CA_HEADS):
        hs = slice(h * CA_HEAD_DIM, (h + 1) * CA_HEAD_DIM)
        k_ref[pl.ds(h, rows, stride=CA_HEADS), :] = k[:, hs]
        v_ref[pl.ds(h, rows, stride=CA_HEADS), :] = v[:, hs]
    kb_ref[...] = k.astype(BF16)
    vb_ref[...] = v.astype(BF16)


def _mem_kv(mem2, g_mem, w_ck, w_cv, tm):
    t = mem2.shape[0]
    row = lambda w: pl.BlockSpec((tm, w), lambda i: (i, 0))
    return pl.pallas_call(
        _mem_kv_body,
        grid=(t // tm,),
        in_specs=[row(D_MODEL), _full_spec(g_mem.shape), _full_spec(w_ck.shape), _full_spec(w_cv.shape)],
        out_specs=[pl.BlockSpec((tm * CA_HEADS, CA_HEAD_DIM), lambda i: (i, 0))] * 2 + [row(CA_DIM)] * 2,
        out_shape=([jax.ShapeDtypeStruct((t * CA_HEADS, CA_HEAD_DIM), F32)] * 2
                   + [jax.ShapeDtypeStruct((t, CA_DIM), BF16)] * 2),
        compiler_params=_params(("parallel",)),
        name="mem_kv",
    )(mem2, g_mem, w_ck, w_cv)


def _cross_sample_body(x_ref, mix_ref, wout_ref, gc_ref, wcq_ref, k_ref, v_ref, h_ref, o_ref, *, n_seq, dec_seq):
    h1 = x_ref[...] + _dot(mix_ref[...], wout_ref[...])
    h_ref[...] = h1
    q = _dot(_rms(h1, gc_ref[...]).astype(BF16), wcq_ref[...]).astype(BF16).astype(F32)
    n_keys = N_MEM * CA_HEADS
    col_head = lax.broadcasted_iota(jnp.int32, (1, n_keys), 1) & (CA_HEADS - 1)
    grp = CA_HEADS * dec_seq
    row_head = (lax.broadcasted_iota(jnp.int32, (n_seq * grp, 1), 0) // dec_seq) & (CA_HEADS - 1)
    own = col_head == row_head
    parts = []
    for i in range(n_seq):
        qi = q[i * dec_seq:(i + 1) * dec_seq]
        qs = jnp.concatenate([qi[:, h * CA_HEAD_DIM:(h + 1) * CA_HEAD_DIM] for h in range(CA_HEADS)], axis=0)
        parts.append(_dot_nt(qs.astype(BF16), k_ref[i].astype(BF16)))
    s = jnp.where(own, jnp.concatenate(parts, axis=0) * (CA_HEAD_DIM ** -0.5), NEG)
    m = jnp.max(s, axis=-1, keepdims=True)
    p = jnp.exp(s - m)
    rdenom = 1.0 / jnp.sum(p, axis=-1, keepdims=True)
    p = p.astype(BF16)
    rows = []
    for i in range(n_seq):
        gr = slice(i * grp, (i + 1) * grp)
        o = _dot(p[gr], v_ref[i].astype(BF16)) * rdenom[gr]
        rows.append(jnp.concatenate([o[h * dec_seq:(h + 1) * dec_seq] for h in range(CA_HEADS)], axis=-1))
    o_ref[...] = jnp.concatenate(rows, axis=0).astype(BF16)


def _cross_sample(x2, mix, w_out, g_cross, w_cq, ck, cv, n_seq_total, dec_seq, n_seq):
    rows = n_seq * dec_seq
    tok = lambda w: pl.BlockSpec((rows, w), lambda i: (i, 0))
    mem = pl.BlockSpec((n_seq, N_MEM * CA_HEADS, CA_HEAD_DIM), lambda i: (i, 0, 0))
    t = n_seq_total * dec_seq
    return pl.pallas_call(
        functools.partial(_cross_sample_body, n_seq=n_seq, dec_seq=dec_seq),
        grid=(n_seq_total // n_seq,),
        in_specs=[tok(D_MODEL), tok(D_MODEL), _full_spec(w_out.shape), _full_spec(g_cross.shape),
                  _full_spec(w_cq.shape), mem, mem],
        out_specs=[tok(D_MODEL), tok(CA_DIM)],
        out_shape=[jax.ShapeDtypeStruct((t, D_MODEL), F32), jax.ShapeDtypeStruct((t, CA_DIM), BF16)],
        compiler_params=_params(("parallel",)),
        name="cross_sample",
    )(x2, mix, w_out, g_cross, w_cq, ck, cv)


def _post2_body(h_ref, o_ref, wco_ref, gf_ref, wg_ref, wu_ref, wd_ref, gfin_ref, y_ref):
    h = h_ref[...] + _dot(o_ref[...], wco_ref[...])
    hn = _rms(h, gf_ref[...]).astype(BF16)
    acc = h
    for lo, hi in FF_SPLITS:
        act = _silu(_dot(hn, wg_ref[:, lo:hi])) * _dot(hn, wu_ref[:, lo:hi])
        acc = acc + _dot(act.astype(BF16), wd_ref[lo:hi, :])
    y_ref[...] = _rms(acc, gfin_ref[...])


def _post2(h1, o, w_co, g_ffn, w_gate, w_up, w_down, g_final, tm):
    t = h1.shape[0]
    row = lambda w: pl.BlockSpec((tm, w), lambda i: (i, 0))
    return pl.pallas_call(
        _post2_body,
        grid=(t // tm,),
        in_specs=[row(D_MODEL), row(CA_DIM), _full_spec(w_co.shape), _full_spec(g_ffn.shape),
                  _full_spec(w_gate.shape), _full_spec(w_up.shape), _full_spec(w_down.shape),
                  _full_spec(g_final.shape)],
        out_specs=row(D_MODEL),
        out_shape=jax.ShapeDtypeStruct((t, D_MODEL), F32),
        compiler_params=_params(("parallel",)),
        name="ffn",
    )(h1, o, w_co, g_ffn, w_gate, w_up, w_down, g_final)


def _prompt_consts():
    qi = np.arange(WINDOW)[:, None]
    ji = np.arange(2 * WINDOW)[None, :]
    dist = qi + WINDOW - ji
    inband = (dist >= 0) & (dist < WINDOW)
    bucket = np.where(inband, _t5_bucket_np(dist), -1)
    first = np.where(ji >= WINDOW, bucket, -1)
    buckets = np.stack([first, bucket]).astype(np.int32)
    return (jnp.asarray(_tril_np(CHUNK), BF16), jnp.asarray(buckets))


def _sample_consts(dec_seq, cache_len):
    rows = SAMPLE_SEQS * dec_seq
    r = np.arange(rows)
    same = (r[:, None] // dec_seq) == (r[None, :] // dec_seq)
    tcum = (same & (r[None, :] <= r[:, None])).astype(np.float32)
    tseq = same.astype(np.float32)
    t = np.arange(dec_seq)[:, None]
    j = np.arange(cache_len)[None, :]
    dist_c = t + cache_len - j
    bkt_c = np.where((dist_c >= 0) & (dist_c < WINDOW), _t5_bucket_np(dist_c), -1).astype(np.int32)
    dist_n = (r[:, None] % dec_seq) - (r[None, :] % dec_seq)
    ok = same & (dist_n >= 0) & (dist_n < WINDOW)
    bkt_n = np.where(ok, _t5_bucket_np(dist_n), -1).astype(np.int32)
    return (jnp.asarray(tcum, BF16), jnp.asarray(tseq, BF16), jnp.asarray(_expand_np(), BF16),
            jnp.asarray(bkt_c), jnp.asarray(bkt_n))


def _pick_tile(t, pref):
    tm = min(t, pref)
    while t % tm:
        tm //= 2
    return tm


def kernel(x_prompt, x_sample, mem_prompt, state_conv, state_ssm, cache_swa_k, cache_swa_v, cache_mem_k, cache_mem_v, rel_bias, g_mix, w_in, conv_w, conv_b, dt_bias, a_log, d_skip, g_ssd_norm, sinks, w_out, g_cross, g_mem, w_cq, w_ck, w_cv, w_co, g_ffn, w_gate, w_up, w_down, g_final):
    assert g_mix.shape[0] == 1, "single-layer trunk"
    batch, seq, _ = x_prompt.shape
    n_dec, dec_seq, _ = x_sample.shape
    cache_len = cache_swa_k.shape[2]
    assert seq % CHUNK == 0 and cache_len == WINDOW and n_dec % SAMPLE_SEQS == 0 and dec_seq == 8

    wi = w_in[0]
    s1 = SSD_D_INNER + CONV_DIM
    s2 = s1 + SSD_HEADS
    w_in_r = jnp.concatenate(
        [wi[:, :s1], wi[:, s2:], jnp.pad(wi[:, s1:s2], ((0, 0), (0, DT_PAD - SSD_HEADS)))], axis=1).astype(BF16)
    row = lambda a: a.reshape(1, -1).astype(F32)
    pad_h = lambda a: jnp.pad(a.reshape(1, -1).astype(F32), ((0, 0), (0, DT_PAD - SSD_HEADS)))
    small = (conv_w[0].astype(F32), row(conv_b[0]), pad_h(dt_bias[0]), pad_h(a_log[0]),
             jnp.repeat(d_skip[0].astype(F32), SSD_HEAD_DIM).reshape(1, -1), row(g_ssd_norm[0]),
             rel_bias.astype(F32), sinks[0].astype(F32))
    bf = lambda w: w[0].astype(BF16)
    w_out_b, w_cq_b, w_ck_b, w_cv_b, w_co_b = bf(w_out), bf(w_cq), bf(w_ck), bf(w_cv), bf(w_co)
    w_gate_b, w_up_b, w_down_b = bf(w_gate), bf(w_up), bf(w_down)
    g_mix_r, g_cross_r, g_mem_r, g_ffn_r, g_fin_r = row(g_mix[0]), row(g_cross[0]), row(g_mem[0]), row(g_ffn[0]), row(g_final)

    n_chunks = 2 if seq % (2 * CHUNK) == 0 else 1
    mem2 = mem_prompt.reshape(batch * N_MEM, D_MODEL)
    mk, mv, mk_b, mv_b = _mem_kv(mem2, g_mem_r, w_ck_b, w_cv_b, _pick_tile(batch * N_MEM, 512))
    dense = (g_mix_r, w_in_r, w_out_b, g_cross_r, w_cq_b, w_co_b, g_ffn_r, w_gate_b, w_up_b, w_down_b, g_fin_r)
    y_prompt, p_conv, p_ssm, p_k, p_v = _prompt_layer(
        x_prompt, mk_b.reshape(batch, N_MEM, CA_DIM), mv_b.reshape(batch, N_MEM, CA_DIM),
        dense, small, _prompt_consts(), n_chunks)

    def channel_major(cache):
        return jnp.transpose(cache, (0, 2, 3, 1)).reshape(n_dec, KV_DIM, cache_len)

    def position_major(cache_t):
        return jnp.transpose(cache_t.reshape(n_dec, N_KV_HEADS, ATTN_HEAD_DIM, cache_len), (0, 3, 1, 2))

    ts = n_dec * dec_seq
    xs2 = x_sample.reshape(ts, D_MODEL)
    tm_s = _pick_tile(ts, 512)
    mix_s, s_conv, s_ssm, s_k, s_v = _sample_mixer(
        xs2, g_mix_r, w_in_r, jnp.transpose(state_conv[0], (1, 0, 2)),
        state_ssm[0].reshape(n_dec, SSD_D_INNER, D_STATE),
        channel_major(cache_swa_k[0]), channel_major(cache_swa_v[0]),
        small, _sample_consts(dec_seq, cache_len), n_dec, dec_seq)
    h1s, os_ = _cross_sample(xs2, mix_s, w_out_b, g_cross_r, w_cq_b,
                             cache_mem_k[0].reshape(n_dec, N_MEM * CA_HEADS, CA_HEAD_DIM),
                             cache_mem_v[0].reshape(n_dec, N_MEM * CA_HEADS, CA_HEAD_DIM), n_dec, dec_seq, SAMPLE_SEQS)
    y_sample = _post2(h1s, os_, w_co_b, g_ffn_r, w_gate_b, w_up_b, w_down_b, g_fin_r, tm_s)

    return (y_prompt.reshape(batch, seq, D_MODEL), y_sample.reshape(n_dec, dec_seq, D_MODEL),
            p_conv[None], p_ssm.reshape(1, batch, SSD_HEADS, SSD_HEAD_DIM, D_STATE),
            p_k.reshape(1, batch, WINDOW, N_KV_HEADS, ATTN_HEAD_DIM),
            p_v.reshape(1, batch, WINDOW, N_KV_HEADS, ATTN_HEAD_DIM),
            mk.reshape(1, batch, N_MEM, CA_HEADS, CA_HEAD_DIM), mv.reshape(1, batch, N_MEM, CA_HEADS, CA_HEAD_DIM),
            jnp.transpose(s_conv, (1, 0, 2))[None], s_ssm.reshape(1, n_dec, SSD_HEADS, SSD_HEAD_DIM, D_STATE),
            position_major(s_k)[None], position_major(s_v)[None])
```

```python
import functools
import math

import numpy as np
import jax
import jax.numpy as jnp
from jax import lax
from jax.experimental import pallas as pl
from jax.experimental.pallas import tpu as pltpu

F32 = jnp.float32
BF16 = jnp.bfloat16

D_MODEL = 1024
SSD_D_INNER = 512
SSD_HEAD_DIM = 64
SSD_HEADS = 8
SSD_GROUPS = 2
GROUP_W = SSD_D_INNER // SSD_GROUPS
D_STATE = 128
CONV_K = 4
CONV_DIM = SSD_D_INNER + 2 * SSD_GROUPS * D_STATE
CHUNK = 128
ATTN_DIM = 512
ATTN_HEAD_DIM = 64
N_HEADS = 8
N_KV_HEADS = 2
Q_PER_KV = N_HEADS // N_KV_HEADS
KV_DIM = N_KV_HEADS * ATTN_HEAD_DIM
WINDOW = 128
N_BUCKETS = 32
MAX_EXACT = N_BUCKETS // 2
MAX_DISTANCE = 128
N_MEM = 256
CA_HEADS = 4
CA_HEAD_DIM = 128
CA_DIM = CA_HEADS * CA_HEAD_DIM
D_FF = 2816
EPS = 1e-6

LANES = 128
HALF = LANES // 2
DT_PAD = LANES
COL_Z = 0
COL_XBC = COL_Z + SSD_D_INNER
COL_Q = COL_XBC + CONV_DIM
COL_K = COL_Q + ATTN_DIM
COL_V = COL_K + KV_DIM
COL_DT = COL_V + KV_DIM
IN_COLS = COL_DT + DT_PAD
NEG = -1e30
SAMPLE_SEQS = 16
VMEM_LIMIT = 56 * 1024 * 1024
FF_SPLITS = ((0, 1024), (1024, 2048), (2048, D_FF))
SECOND_STREAM_LEAD = 0.5
FF_PIECES = tuple((lo, min(lo + 512, D_FF)) for lo in range(0, D_FF, 512))
PROJ_PIECES = ((COL_Z, COL_XBC), (COL_XBC, COL_XBC + 512), (COL_XBC + 512, COL_Q), (COL_Q, COL_K), (COL_K, IN_COLS))


def _rms(x, g):
    return x * lax.rsqrt(jnp.mean(x * x, axis=-1, keepdims=True) + EPS) * g


def _silu(x):
    return x * jax.nn.sigmoid(x)


def _softplus(x):
    return jnp.maximum(x, 0.0) + jnp.log1p(jnp.exp(-jnp.abs(x)))


def _dot(a, b):
    return jnp.dot(a, b, preferred_element_type=F32)


def _dot_nt(a, b):
    return lax.dot_general(a, b, (((1,), (1,)), ((), ())), preferred_element_type=F32)


def _split3(a):
    hi = a.astype(BF16)
    r = a - hi.astype(F32)
    mid = r.astype(BF16)
    lo = (r - mid.astype(F32)).astype(BF16)
    return hi, mid, lo


def _sel_left(t01, a):
    hi, mid, lo = _split3(a)
    return _dot(t01, hi) + _dot(t01, mid) + _dot(t01, lo)


def _sel_right(a, e01):
    hi, mid, lo = _split3(a)
    return _dot(hi, e01) + _dot(mid, e01) + _dot(lo, e01)


def _lane_lo():
    return lax.broadcasted_iota(jnp.int32, (1, LANES), 1) < HALF


def _t5_bucket_np(dist):
    n = np.maximum(dist, 0)
    ratio = np.log(np.maximum(n, 1).astype(np.float32) / np.float32(MAX_EXACT))
    large = MAX_EXACT + (ratio / np.float32(math.log(MAX_DISTANCE / MAX_EXACT))
                         * np.float32(N_BUCKETS - MAX_EXACT)).astype(np.int32)
    large = np.minimum(large, N_BUCKETS - 1)
    return np.where(n < MAX_EXACT, n, large).astype(np.int32)


def _tril_np(n):
    return np.tril(np.ones((n, n), np.float32))


def _expand_np():
    e = np.zeros((LANES, SSD_D_INNER), np.float32)
    for h in range(SSD_HEADS):
        e[h, h * SSD_HEAD_DIM:(h + 1) * SSD_HEAD_DIM] = 1.0
    return e


def _full_spec(shape):
    nd = len(shape)
    return pl.BlockSpec(shape, lambda *_: (0,) * nd, pipeline_mode=pl.Buffered(1))


def _smem_spec():
    return pl.BlockSpec(memory_space=pltpu.SMEM)


def _params(sem):
    return pltpu.CompilerParams(dimension_semantics=sem, vmem_limit_bytes=VMEM_LIMIT)


def _conv_taps(cw_ref, cb_ref, taps):
    acc = cb_ref[...] + taps[0] * cw_ref[0:1, :]
    for k in range(1, CONV_K):
        acc = acc + taps[k] * cw_ref[k:k + 1, :]
    return _silu(acc)


def _ssd_prepare(conv, dt_raw, dtb_ref, alog_ref, tcum, total_fn, extra_fn, expand):
    xs = conv[:, :SSD_D_INNER]
    bm = conv[:, SSD_D_INNER:SSD_D_INNER + SSD_GROUPS * D_STATE]
    cm = conv[:, SSD_D_INNER + SSD_GROUPS * D_STATE:]
    dt = _softplus(dt_raw + dtb_ref[...])
    a = dt * (-jnp.exp(alog_ref[...]))
    cs = _sel_left(tcum, a)
    total = total_fn(cs, a)
    pieces = [dt, jnp.exp(total - cs), jnp.exp(cs)] + extra_fn(total)
    rows = cs.shape[0]
    ex = _sel_right(jnp.concatenate(pieces, axis=0), expand)
    ex = [ex[i * rows:(i + 1) * rows] for i in range(len(pieces))]
    return xs, bm, cm, cs, cs.T, ex


def _ssd_diag(cs, cs_t, cb_g, xdt, mask, g):
    lo = _lane_lo()
    out = []
    for pr in range(2):
        h0 = g * 4 + 2 * pr
        xp = xdt[:, (h0 // 2) * LANES:(h0 // 2 + 1) * LANES]
        x_lo = jnp.where(lo, xp, 0.0).astype(BF16)
        x_hi = jnp.where(lo, 0.0, xp).astype(BF16)
        acc = None
        for h, xh in ((h0, x_lo), (h0 + 1, x_hi)):
            diff = cs[:, h:h + 1] - cs_t[h:h + 1, :]
            decay = jnp.exp(jnp.where(mask, diff, -jnp.inf))
            part = _dot((cb_g * decay).astype(BF16), xh)
            acc = part if acc is None else acc + part
        out.append(acc)
    return jnp.concatenate(out, axis=-1)


def _gated_norm(y, z, gn_ref):
    yf = y * _silu(z)
    parts = []
    for g in range(SSD_GROUPS):
        yg = yf[:, g * GROUP_W:(g + 1) * GROUP_W]
        parts.append(yg * lax.rsqrt(jnp.mean(yg * yg, axis=-1, keepdims=True) + EPS))
    return jnp.concatenate(parts, axis=-1) * gn_ref[...]


def _build_bias(bias_ref, bucket_of, relb_ref, n_tables):
    for i in range(n_tables):
        for h in range(N_HEADS):
            bias_ref[i, h] = jnp.full(bias_ref.shape[2:], NEG, F32)

    def body(t, carry):
        for i in range(n_tables):
            hit = bucket_of(i) == t
            for h in range(N_HEADS):
                bias_ref[i, h] = jnp.where(hit, relb_ref[t, h], bias_ref[i, h])
        return carry

    lax.fori_loop(0, N_BUCKETS, body, 0)


def _alternate(first, second):
    streams = [[0.0, 1.0, first], [0.0, SECOND_STREAM_LEAD, second]]
    while streams:
        entry = min(streams, key=lambda e: e[0])
        try:
            entry[0] += entry[1] * next(entry[2])
        except StopIteration:
            streams.remove(entry)


def _layer_step_body(x_ref, xprev_ref, mk_ref, mv_ref, gmix_ref, win_ref, wout_ref, gc_ref, wcq_ref,
                     wco_ref, gf_ref, wg_ref, wu_ref, wd_ref, gfin_ref,
                     cw_ref, cb_ref, dtb_ref, alog_ref, dskip_ref, gn_ref,
                     tril_ref, bucket_ref, relb_ref, sink_ref,
                     y_ref, conv_out, ssm_out, k_out, v_out,
                     xpad, state_t, kbuf, vbuf, bias, mix, proj, *, n_chunks, tiles_per_seq):
    step = pl.program_id(0)
    n_tiles = pl.num_programs(0) - 1
    c = lax.rem(jnp.minimum(step, n_tiles - 1), tiles_per_seq)
    last = tiles_per_seq - 1
    tq = n_chunks * CHUNK

    @pl.when(step == 0)
    def _():
        _build_bias(bias, lambda i: bucket_ref[i], relb_ref, 2)
        mix[...] = jnp.zeros_like(mix)

    @pl.when(c == 0)
    def _():
        xpad[0:8, :] = jnp.zeros((8, CONV_DIM), F32)
        state_t[...] = jnp.zeros_like(state_t)
        kbuf[:, 0:CHUNK, :] = jnp.zeros((2, CHUNK, KV_DIM), BF16)
        vbuf[:, 0:CHUNK, :] = jnp.zeros((2, CHUNK, KV_DIM), BF16)

    def back():
        h1 = xprev_ref[0] + _dot(mix[...], wout_ref[...])
        yield 4
        qc = _dot(_rms(h1, gc_ref[...]).astype(BF16), wcq_ref[...]).astype(BF16)
        yield 2
        heads = []
        for h in range(CA_HEADS):
            hs = slice(h * CA_HEAD_DIM, (h + 1) * CA_HEAD_DIM)
            sc = _dot_nt(qc[:, hs], mk_ref[0, :, hs]) * (CA_HEAD_DIM ** -0.5)
            m = jnp.max(sc, axis=-1, keepdims=True)
            p = jnp.exp(sc - m)
            heads.append(_dot(p.astype(BF16), mv_ref[0, :, hs]) / jnp.sum(p, axis=-1, keepdims=True))
            yield 3
        o = jnp.concatenate(heads, axis=-1).astype(BF16)
        h2 = h1 + _dot(o, wco_ref[...])
        hn = _rms(h2, gf_ref[...]).astype(BF16)
        yield 4
        acc = h2
        for f_lo, f_hi in FF_PIECES:
            width = (f_hi - f_lo) / 512
            gate = _dot(hn, wg_ref[:, f_lo:f_hi])
            yield 5 * width
            act = (_silu(gate) * _dot(hn, wu_ref[:, f_lo:f_hi])).astype(BF16)
            yield 6 * width
            acc = acc + _dot(act, wd_ref[f_lo:f_hi, :])
            yield 5 * width
        y_ref[0] = _rms(acc, gfin_ref[...])

    def front():
        x = x_ref[0]
        xn = _rms(x, gmix_ref[...]).astype(BF16)
        for p_lo, p_hi in PROJ_PIECES:
            if COL_XBC <= p_lo and p_hi <= COL_Q:
                xpad[8:8 + tq, p_lo - COL_XBC:p_hi - COL_XBC] = _dot(xn, win_ref[:, p_lo:p_hi])
            else:
                proj[:, p_lo:p_hi] = _dot(xn, win_ref[:, p_lo:p_hi])
            yield 4

        conv_halves = []
        half_w = CONV_DIM // 2
        for hf in range(2):
            cs_ = slice(hf * half_w, (hf + 1) * half_w)
            acc = cb_ref[:, cs_] + xpad[5:5 + tq, cs_] * cw_ref[0:1, cs_]
            for kk in range(1, CONV_K):
                acc = acc + xpad[5 + kk:5 + kk + tq, cs_] * cw_ref[kk:kk + 1, cs_]
            conv_halves.append(_silu(acc))
            yield 6
        xpad[5:8, :] = xpad[5 + tq:8 + tq, :]
        xs_all = conv_halves[0]
        bc_all = conv_halves[1]

        k_new = proj[:, COL_K:COL_V]
        v_new = proj[:, COL_V:COL_DT]
        kbuf[0, CHUNK:, :] = k_new.astype(BF16)
        kbuf[1, CHUNK:, :] = pltpu.roll(k_new, HALF, 1).astype(BF16)
        vbuf[0, CHUNK:, :] = v_new.astype(BF16)
        vbuf[1, CHUNK:, :] = pltpu.roll(v_new, HALF, 1).astype(BF16)
        q = (proj[:, COL_Q:COL_K] * (ATTN_HEAD_DIM ** -0.5)).astype(BF16)
        yield 2

        row = lax.broadcasted_iota(jnp.int32, (CHUNK, CHUNK), 0)
        col = lax.broadcasted_iota(jnp.int32, (CHUNK, CHUNK), 1)
        causal = col <= row
        lo = _lane_lo()
        for ci in range(n_chunks):
            rs = slice(ci * CHUNK, (ci + 1) * CHUNK)
            xs = xs_all[rs]
            bm = bc_all[rs, 0:SSD_GROUPS * D_STATE]
            cm = bc_all[rs, SSD_GROUPS * D_STATE:]
            dt = _softplus(proj[rs, COL_DT:IN_COLS] + dtb_ref[...])
            a = dt * (-jnp.exp(alog_ref[...]))
            cs = _sel_left(tril_ref[...], a)
            cs_t = cs.T
            dt_t = dt.T
            total = cs[CHUNK - 1:CHUNK, :]
            ecs = jnp.exp(cs)
            w_end = dt * jnp.exp(total - cs)
            bb = bm.astype(BF16)
            cb = cm.astype(BF16)
            yield 5
            y_parts = []
            for g in range(SSD_GROUPS):
                gs = slice(g * GROUP_W, (g + 1) * GROUP_W)
                ns = slice(g * D_STATE, (g + 1) * D_STATE)
                cb_g = _dot_nt(cb[:, ns], bb[:, ns])
                y_diag, xw, ecs_g = [], [], []
                for pr in range(2):
                    h0 = g * 4 + 2 * pr
                    xp = xs[:, (h0 // 2) * LANES:(h0 // 2 + 1) * LANES]
                    ecs_g.append(jnp.where(lo, ecs[:, h0:h0 + 1], ecs[:, h0 + 1:h0 + 2]))
                    xw.append((xp * jnp.where(lo, w_end[:, h0:h0 + 1], w_end[:, h0 + 1:h0 + 2])).astype(BF16))
                    x2 = jnp.concatenate([jnp.where(lo, xp, 0.0), jnp.where(lo, 0.0, xp)], axis=0).astype(BF16)
                    gmat = []
                    for h in (h0, h0 + 1):
                        diff = cs[:, h:h + 1] - cs_t[h:h + 1, :]
                        decay = jnp.exp(jnp.where(causal, diff, -jnp.inf))
                        gmat.append((cb_g * decay * dt_t[h:h + 1, :]).astype(BF16))
                    y_diag.append(_dot(jnp.concatenate(gmat, axis=1), x2))
                ecs_x = jnp.concatenate(ecs_g, axis=-1)
                st_old = state_t[:, gs]
                y_off = _dot(cb[:, ns], st_old.astype(BF16)) * ecs_x
                y_parts.append(jnp.concatenate(y_diag, axis=-1) + y_off)
                b_t = bm[:, ns].T.astype(BF16)
                state_t[:, gs] = ecs_x[CHUNK - 1:CHUNK, :] * st_old + _dot(b_t, jnp.concatenate(xw, axis=-1))
                yield 9
            y = jnp.concatenate(y_parts, axis=-1) + dskip_ref[...] * xs
            mix[rs, 0:SSD_D_INNER] = _gated_norm(y, proj[rs, COL_Z:COL_XBC], gn_ref).astype(BF16)
            yield 3

            keys = slice(ci * CHUNK, (ci + 2) * CHUNK)
            table = jnp.minimum(c, 1) if ci == 0 else 1
            probs, denoms = {}, {}
            for variant in range(2):
                hv = [h for h in range(N_HEADS) if (h // Q_PER_KV + h % 2) % 2 == variant]
                qms = []
                for h in hv:
                    q_pair = q[rs, (h // 2) * LANES:(h // 2 + 1) * LANES]
                    qms.append(jnp.where(lo, q_pair, 0.0) if h % 2 == 0 else jnp.where(lo, 0.0, q_pair))
                sc_all = _dot_nt(jnp.concatenate(qms, axis=0).astype(BF16), kbuf[variant, keys, :])
                for i, h in enumerate(hv):
                    sc = sc_all[i * CHUNK:(i + 1) * CHUNK] + bias[table, h]
                    sink = sink_ref[h]
                    m = jnp.maximum(jnp.max(sc, axis=-1, keepdims=True), sink)
                    p = jnp.exp(sc - m)
                    denoms[h] = jnp.sum(p, axis=-1, keepdims=True) + jnp.exp(sink - m)
                    probs[h] = p.astype(BF16)
                yield 10
            parts = {}
            for variant in range(2):
                vv = vbuf[variant, keys, :]
                for par in range(2):
                    hv = [h for h in range(N_HEADS) if (h // Q_PER_KV + h % 2) % 2 == variant and h % 2 == par]
                    vm = jnp.where(lo, vv, 0.0) if par == 0 else jnp.where(lo, 0.0, vv)
                    o_all = _dot(jnp.concatenate([probs[h] for h in hv], axis=0), vm.astype(BF16))
                    for i, h in enumerate(hv):
                        parts[h] = o_all[i * CHUNK:(i + 1) * CHUNK] / denoms[h]
            for pair in range(N_HEADS // 2):
                mix[rs, SSD_D_INNER + pair * LANES:SSD_D_INNER + (pair + 1) * LANES] = (
                    parts[2 * pair] + parts[2 * pair + 1]).astype(BF16)
            yield 4
        kbuf[:, 0:CHUNK, :] = kbuf[:, tq:tq + CHUNK, :]
        vbuf[:, 0:CHUNK, :] = vbuf[:, tq:tq + CHUNK, :]

    _alternate(back(), front())

    @pl.when((c == last) & (step < n_tiles))
    def _():
        conv_out[0] = xpad[5:8, :]
        ssm_out[0] = state_t[...].T
        k_out[0] = proj[tq - WINDOW:tq, COL_K:COL_V]
        v_out[0] = proj[tq - WINDOW:tq, COL_V:COL_DT]


def _prompt_layer(x3, mk_b, mv_b, dense, small, consts, n_chunks):
    cw, cb, dtb, alog, dskip_x, gn, relb, sinks = small
    tril, buckets = consts
    batch, seq, _ = x3.shape
    tq = n_chunks * CHUNK
    tps = seq // tq
    n_tiles = batch * tps
    front = lambda s: jnp.minimum(s, n_tiles - 1)
    back = lambda s: jnp.maximum(s - 1, 0)
    x_spec = pl.BlockSpec((1, tq, D_MODEL), lambda s: (front(s) // tps, front(s) % tps, 0))
    xprev_spec = pl.BlockSpec((1, tq, D_MODEL), lambda s: (back(s) // tps, back(s) % tps, 0))
    y_spec = pl.BlockSpec((1, tq, D_MODEL), lambda s: (back(s) // tps, back(s) % tps, 0))
    mem_spec = pl.BlockSpec((1, N_MEM, CA_DIM), lambda s: (back(s) // tps, 0, 0))
    per_b = lambda shp: pl.BlockSpec((1,) + shp, lambda s: (front(s) // tps,) + (0,) * len(shp))
    out_shape = [
        jax.ShapeDtypeStruct((batch, seq, D_MODEL), F32),
        jax.ShapeDtypeStruct((batch, CONV_K - 1, CONV_DIM), F32),
        jax.ShapeDtypeStruct((batch, SSD_D_INNER, D_STATE), F32),
        jax.ShapeDtypeStruct((batch, WINDOW, KV_DIM), F32),
        jax.ShapeDtypeStruct((batch, WINDOW, KV_DIM), F32),
    ]
    full = list(dense) + [cw, cb, dtb, alog, dskip_x, gn, tril, buckets]
    return pl.pallas_call(
        functools.partial(_layer_step_body, n_chunks=n_chunks, tiles_per_seq=tps),
        grid=(n_tiles + 1,),
        in_specs=([x_spec, xprev_spec, mem_spec, mem_spec] + [_full_spec(a.shape) for a in full]
                  + [_smem_spec(), _smem_spec()]),
        out_specs=[y_spec, per_b((CONV_K - 1, CONV_DIM)), per_b((SSD_D_INNER, D_STATE)),
                   per_b((WINDOW, KV_DIM)), per_b((WINDOW, KV_DIM))],
        out_shape=out_shape,
        scratch_shapes=[
            pltpu.VMEM((8 + tq, CONV_DIM), F32),
            pltpu.VMEM((D_STATE, SSD_D_INNER), F32),
            pltpu.VMEM((2, CHUNK + tq, KV_DIM), BF16),
            pltpu.VMEM((2, CHUNK + tq, KV_DIM), BF16),
            pltpu.VMEM((2, N_HEADS, WINDOW, 2 * WINDOW), F32),
            pltpu.VMEM((tq, D_MODEL), BF16),
            pltpu.VMEM((tq, IN_COLS), F32),
        ],
        compiler_params=_params(("arbitrary",)),
        name="prompt_layer",
    )(x3, x3, mk_b, mv_b, *full, relb, sinks)


def _sample_mixer_body(x_ref, gmix_ref, win_ref,
                       sconv_ref, sssm_ref, ck_ref, cv_ref,
                       cw_ref, cb_ref, dtb_ref, alog_ref, dskip_ref, gn_ref,
                       tcum_ref, tseq_ref, expand_ref, bkt_c_ref, bkt_n_ref, relb_ref, sink_ref,
                       mix_ref, conv_out, ssm_out, k_out, v_out,
                       xpad, bias_c, bias_n, *, dec_seq):
    step = pl.program_id(0)
    n_seq = SAMPLE_SEQS
    rows = n_seq * dec_seq

    @pl.when(step == 0)
    def _():
        _build_bias(bias_c, lambda i: bkt_c_ref[...], relb_ref, 1)
        _build_bias(bias_n, lambda i: bkt_n_ref[...], relb_ref, 1)

    xn = _rms(x_ref[...], gmix_ref[...]).astype(BF16)

    def seg(lo, hi):
        return _dot(xn, win_ref[:, lo:hi])

    z = seg(COL_Z, COL_XBC)
    xbc = seg(COL_XBC, COL_Q)
    qf = (seg(COL_Q, COL_K) * (ATTN_HEAD_DIM ** -0.5)).astype(BF16).astype(F32)
    k_new = seg(COL_K, COL_V)
    v_new = seg(COL_V, COL_DT)
    dt_raw = seg(COL_DT, IN_COLS)

    xpad[:, 8:8 + dec_seq, :] = xbc.reshape(n_seq, dec_seq, CONV_DIM)
    for r in range(CONV_K - 1):
        xpad[:, 5 + r, :] = sconv_ref[r]
    taps = [xpad[:, 5 + k:5 + k + dec_seq, :].reshape(rows, CONV_DIM) for k in range(CONV_K - 1)] + [xbc]
    conv = _conv_taps(cw_ref, cb_ref, taps)
    for r in range(CONV_K - 1):
        conv_out[r] = xpad[:, 5 + dec_seq + r, :]

    row = lax.broadcasted_iota(jnp.int32, (rows, rows), 0)
    col = lax.broadcasted_iota(jnp.int32, (rows, rows), 1)
    tseq = tseq_ref[...]
    same_seq = tseq > 0
    causal = same_seq & (col <= row)
    xs, bm, cm, cs, cs_t, (dt_x, dend_x, ecs_x, seqdec_x) = _ssd_prepare(
        conv, dt_raw, dtb_ref, alog_ref, tcum_ref[...],
        lambda cs_, a_: _sel_left(tseq, a_), lambda total: [jnp.exp(total)], expand_ref[...])
    xdt = xs * dt_x
    bb = bm.astype(BF16)
    cb = cm.astype(BF16)
    seq_of_row = lax.broadcasted_iota(jnp.int32, (rows, 1), 0) // dec_seq
    seq_of_lane = lax.broadcasted_iota(jnp.int32, (1, rows), 1) // dec_seq
    xde_t = (xdt * dend_x).T
    y_off = [None] * SSD_GROUPS
    for i in range(n_seq):
        st = sssm_ref[i]
        dec = jnp.concatenate(
            [jnp.broadcast_to(seqdec_x[i * dec_seq:i * dec_seq + 1, h * SSD_HEAD_DIM:h * SSD_HEAD_DIM + 1],
                              (SSD_HEAD_DIM, D_STATE)) for h in range(SSD_HEADS)], axis=0)
        new_parts = []
        for g in range(SSD_GROUPS):
            gs = slice(g * GROUP_W, (g + 1) * GROUP_W)
            ns = slice(g * D_STATE, (g + 1) * D_STATE)
            c_i = jnp.where(seq_of_row == i, cm[:, ns], 0.0).astype(BF16)
            part = _dot_nt(c_i, st[gs].astype(BF16))
            y_off[g] = part if y_off[g] is None else y_off[g] + part
            x_i = jnp.where(seq_of_lane == i, xde_t[gs], 0.0).astype(BF16)
            new_parts.append(dec[gs] * st[gs] + _dot(x_i, bb[:, ns]))
        ssm_out[i] = jnp.concatenate(new_parts, axis=0)
    y_parts = []
    for g in range(SSD_GROUPS):
        gs = slice(g * GROUP_W, (g + 1) * GROUP_W)
        ns = slice(g * D_STATE, (g + 1) * D_STATE)
        cb_g = _dot_nt(cb[:, ns], bb[:, ns])
        y_parts.append(_ssd_diag(cs, cs_t, cb_g, xdt, causal, g) + y_off[g] * ecs_x[:, gs])
    y = jnp.concatenate(y_parts, axis=-1) + dskip_ref[...] * xs
    mix_ref[:, 0:SSD_D_INNER] = _gated_norm(y, z, gn_ref).astype(BF16)

    lo = _lane_lo()
    k_var =[k_new.astype(BF16), pltpu.roll(k_new, HALF, 1).astype(BF16)]
    v_new_r = pltpu.roll(v_new, HALF, 1)
    v_dup = [jnp.where(lo, v_new, v_new_r).astype(BF16), jnp.where(lo, v_new_r, v_new).astype(BF16)]
    q_masked = []
    s_new = []
    for h in range(N_HEADS):
        pair, par = h // 2, h % 2
        j = h // Q_PER_KV
        q_pair = qf[:, pair * LANES:(pair + 1) * LANES]
        qm = jnp.where(lo, q_pair, 0.0) if par == 0 else jnp.where(lo, 0.0, q_pair)
        q_masked.append(qm)
        s_new.append(_dot_nt(qm.astype(BF16), k_var[(j + par) % 2]) + bias_n[0, h])
    stack_rows = lax.broadcasted_iota(jnp.int32, (Q_PER_KV * dec_seq, 1), 0) // dec_seq
    k_new_t = k_new.T
    v_new_t = v_new.T
    keep = WINDOW - dec_seq
    old_lane = lax.broadcasted_iota(jnp.int32, (1, WINDOW), 1) < keep
    grp = Q_PER_KV * dec_seq
    sc_parts, sn_parts, sink_parts = [], [], []
    for i in range(n_seq):
        rs = slice(i * dec_seq, (i + 1) * dec_seq)
        kc_t = ck_ref[i]
        for j in range(N_KV_HEADS):
            heads = range(j * Q_PER_KV, (j + 1) * Q_PER_KV)
            cj = slice(j * ATTN_HEAD_DIM, (j + 1) * ATTN_HEAD_DIM)
            kdup_t = jnp.concatenate([kc_t[cj], kc_t[cj]], axis=0).astype(BF16)
            qs = jnp.concatenate([q_masked[h][rs] for h in heads], axis=0).astype(BF16)
            sc_parts.append(_dot(qs, kdup_t) + jnp.concatenate([bias_c[0, h] for h in heads], axis=0))
            sn_parts.append(jnp.concatenate([s_new[h][rs] for h in heads], axis=0))
            if i == 0:
                sink_j = jnp.zeros((grp, 1), F32)
                for hh, h in enumerate(heads):
                    sink_j = jnp.where(stack_rows == hh, sink_ref[h], sink_j)
                sink_parts.append(sink_j)
    sc = jnp.concatenate(sc_parts, axis=0)
    sn = jnp.concatenate(sn_parts, axis=0)
    sink = jnp.concatenate(sink_parts * n_seq, axis=0)
    m = jnp.maximum(jnp.maximum(jnp.max(sc, axis=-1, keepdims=True), jnp.max(sn, axis=-1, keepdims=True)), sink)
    pc = jnp.exp(sc - m)
    pn = jnp.exp(sn - m)
    rdenom = 1.0 / (jnp.sum(pc, axis=-1, keepdims=True) + jnp.sum(pn, axis=-1, keepdims=True) + jnp.exp(sink - m))
    pc = pc.astype(BF16)
    pn = pn.astype(BF16)
    att_rows = []
    for i in range(n_seq):
        vc_t = cv_ref[i]
        pieces = []
        for j in range(N_KV_HEADS):
            cj = slice(j * ATTN_HEAD_DIM, (j + 1) * ATTN_HEAD_DIM)
            gr = slice((i * N_KV_HEADS + j) * grp, (i * N_KV_HEADS + j + 1) * grp)
            vdup_t = jnp.concatenate([vc_t[cj], vc_t[cj]], axis=0).astype(BF16)
            o = (_dot_nt(pc[gr], vdup_t) + _dot(pn[gr], v_dup[j])) * rdenom[gr]
            for pr in range(Q_PER_KV // 2):
                even = o[(2 * pr) * dec_seq:(2 * pr + 1) * dec_seq]
                odd = o[(2 * pr + 1) * dec_seq:(2 * pr + 2) * dec_seq]
                pieces.append(jnp.where(lo, even, odd))
        att_rows.append(jnp.concatenate(pieces, axis=-1))
        new_shift = (keep - i * dec_seq) % WINDOW
        k_out[i] = jnp.where(old_lane, pltpu.roll(ck_ref[i], keep, 1), pltpu.roll(k_new_t, new_shift, 1))
        v_out[i] = jnp.where(old_lane, pltpu.roll(vc_t, keep, 1), pltpu.roll(v_new_t, new_shift, 1))
    mix_ref[:, SSD_D_INNER:] = jnp.concatenate(att_rows, axis=0).astype(BF16)


def _sample_mixer(x2, g_mix, w_in_r, sconv, sssm, ck, cv, small, consts, n_seq_total, dec_seq):
    cw, cb, dtb, alog, dskip_x, gn, relb, sinks = small
    tcum, tseq, expand, bkt_c, bkt_n = consts
    rows = SAMPLE_SEQS * dec_seq
    tok = lambda w: pl.BlockSpec((rows, w), lambda i: (i, 0))
    per_s = lambda s: pl.BlockSpec((SAMPLE_SEQS,) + s, lambda i: (i,) + (0,) * len(s))
    out_shape = [
        jax.ShapeDtypeStruct((n_seq_total * dec_seq, D_MODEL), BF16),
        jax.ShapeDtypeStruct((CONV_K - 1, n_seq_total, CONV_DIM), F32),
        jax.ShapeDtypeStruct((n_seq_total, SSD_D_INNER, D_STATE), F32),
        jax.ShapeDtypeStruct((n_seq_total, WINDOW, KV_DIM), F32),
        jax.ShapeDtypeStruct((n_seq_total, WINDOW, KV_DIM), F32),
    ]
    conv_spec = pl.BlockSpec((CONV_K - 1, SAMPLE_SEQS, CONV_DIM), lambda i: (0, i, 0))
    return pl.pallas_call(
        functools.partial(_sample_mixer_body, dec_seq=dec_seq),
        grid=(n_seq_total // SAMPLE_SEQS,),
        in_specs=[tok(D_MODEL), _full_spec(g_mix.shape), _full_spec(w_in_r.shape),
                  conv_spec, per_s((SSD_D_INNER, D_STATE)),
                  per_s((WINDOW, KV_DIM)), per_s((WINDOW, KV_DIM)),
                  _full_spec(cw.shape), _full_spec(cb.shape), _full_spec(dtb.shape), _full_spec(alog.shape),
                  _full_spec(dskip_x.shape), _full_spec(gn.shape),
                  _full_spec(tcum.shape), _full_spec(tseq.shape), _full_spec(expand.shape),
                  _full_spec(bkt_c.shape), _full_spec(bkt_n.shape), _smem_spec(), _smem_spec()],
        out_specs=[tok(D_MODEL), conv_spec, per_s((SSD_D_INNER, D_STATE)),
                   per_s((WINDOW, KV_DIM)), per_s((WINDOW, KV_DIM))],
        out_shape=out_shape,
        scratch_shapes=[
            pltpu.VMEM((SAMPLE_SEQS, 8 + dec_seq, CONV_DIM), F32),
            pltpu.VMEM((1, N_HEADS, dec_seq, WINDOW), F32),
            pltpu.VMEM((1, N_HEADS, rows, rows), F32),
        ],
        compiler_params=_params(("arbitrary",)),
        name="sample_mixer",
    )(x2, g_mix, w_in_r, sconv, sssm, ck, cv, cw, cb, dtb, alog, dskip_x, gn,
      tcum, tseq, expand, bkt_c, bkt_n, relb, sinks)


def _mem_kv_body(mem_ref, g_ref, wk_ref, wv_ref, k_ref, v_ref, kb_ref, vb_ref):
    mn = _rms(mem_ref[...], g_ref[...]).astype(BF16)
    k = _dot(mn, wk_ref[...])
    v = _dot(mn, wv_ref[...])
    rows = k.shape[0]
    for h in range(CA_HEADS):
        hs = slice(h * CA_HEAD_DIM, (h + 1) * CA_HEAD_DIM)
        k_ref[pl.ds(h, rows, stride=CA_HEADS), :] = k[:, hs]
        v_ref[pl.ds(h, rows, stride=CA_HEADS), :] = v[:, hs]
    kb_ref[...] = k.astype(BF16)
    vb_ref[...] = v.astype(BF16)


def _mem_kv(mem2, g_mem, w_ck, w_cv, tm):
    t = mem2.shape[0]
    row = lambda w: pl.BlockSpec((tm, w), lambda i: (i, 0))
    return pl.pallas_call(
        _mem_kv_body,
        grid=(t // tm,),
        in_specs=[row(D_MODEL), _full_spec(g_mem.shape), _full_spec(w_ck.shape), _full_spec(w_cv.shape)],
        out_specs=[pl.BlockSpec((tm * CA_HEADS, CA_HEAD_DIM), lambda i: (i, 0))] * 2 + [row(CA_DIM)] * 2,
        out_shape=([jax.ShapeDtypeStruct((t * CA_HEADS, CA_HEAD_DIM), F32)] * 2
                   + [jax.ShapeDtypeStruct((t, CA_DIM), BF16)] * 2),
        compiler_params=_params(("parallel",)),
        name="mem_kv",
    )(mem2, g_mem, w_ck, w_cv)


def _cross_sample_body(x_ref, mix_ref, wout_ref, gc_ref, wcq_ref, k_ref, v_ref, h_ref, o_ref, *, n_seq, dec_seq):
    h1 = x_ref[...] + _dot(mix_ref[...], wout_ref[...])
    h_ref[...] = h1
    q = _dot(_rms(h1, gc_ref[...]).astype(BF16), wcq_ref[...]).astype(BF16).astype(F32)
    n_keys = N_MEM * CA_HEADS
    col_head = lax.broadcasted_iota(jnp.int32, (1, n_keys), 1) & (CA_HEADS - 1)
    grp = CA_HEADS * dec_seq
    row_head = (lax.broadcasted_iota(jnp.int32, (n_seq * grp, 1), 0) // dec_seq) & (CA_HEADS - 1)
    own = col_head == row_head
    parts = []
    for i in range(n_seq):
        qi = q[i * dec_seq:(i + 1) * dec_seq]
        qs = jnp.concatenate([qi[:, h * CA_HEAD_DIM:(h + 1) * CA_HEAD_DIM] for h in range(CA_HEADS)], axis=0)
        parts.append(_dot_nt(qs.astype(BF16), k_ref[i].astype(BF16)))
    s = jnp.where(own, jnp.concatenate(parts, axis=0) * (CA_HEAD_DIM ** -0.5), NEG)
    m = jnp.max(s, axis=-1, keepdims=True)
    p = jnp.exp(s - m)
    rdenom = 1.0 / jnp.sum(p, axis=-1, keepdims=True)
    p = p.astype(BF16)
    rows = []
    for i in range(n_seq):
        gr = slice(i * grp, (i + 1) * grp)
        o = _dot(p[gr], v_ref[i].astype(BF16)) * rdenom[gr]
        rows.append(jnp.concatenate([o[h * dec_seq:(h + 1) * dec_seq] for h in range(CA_HEADS)], axis=-1))
    o_ref[...] = jnp.concatenate(rows, axis=0).astype(BF16)


def _cross_sample(x2, mix, w_out, g_cross, w_cq, ck, cv, n_seq_total, dec_seq, n_seq):
    rows = n_seq * dec_seq
    tok = lambda w: pl.BlockSpec((rows, w), lambda i: (i, 0))
    mem = pl.BlockSpec((n_seq, N_MEM * CA_HEADS, CA_HEAD_DIM), lambda i: (i, 0, 0))
    t = n_seq_total * dec_seq
    return pl.pallas_call(
        functools.partial(_cross_sample_body, n_seq=n_seq, dec_seq=dec_seq),
        grid=(n_seq_total // n_seq,),
        in_specs=[tok(D_MODEL), tok(D_MODEL), _full_spec(w_out.shape), _full_spec(g_cross.shape),
                  _full_spec(w_cq.shape), mem, mem],
        out_specs=[tok(D_MODEL), tok(CA_DIM)],
        out_shape=[jax.ShapeDtypeStruct((t, D_MODEL), F32), jax.ShapeDtypeStruct((t, CA_DIM), BF16)],
        compiler_params=_params(("parallel",)),
        name="cross_sample",
    )(x2, mix, w_out, g_cross, w_cq, ck, cv)


def _post2_body(h_ref, o_ref, wco_ref, gf_ref, wg_ref, wu_ref, wd_ref, gfin_ref, y_ref):
    h = h_ref[...] + _dot(o_ref[...], wco_ref[...])
    hn = _rms(h, gf_ref[...]).astype(BF16)
    acc = h
    for lo, hi in FF_SPLITS:
        act = _silu(_dot(hn, wg_ref[:, lo:hi])) * _dot(hn, wu_ref[:, lo:hi])
        acc = acc + _dot(act.astype(BF16), wd_ref[lo:hi, :])
    y_ref[...] = _rms(acc, gfin_ref[...])


def _post2(h1, o, w_co, g_ffn, w_gate, w_up, w_down, g_final, tm):
    t = h1.shape[0]
    row = lambda w: pl.BlockSpec((tm, w), lambda i: (i, 0))
    return pl.pallas_call(
        _post2_body,
        grid=(t // tm,),
        in_specs=[row(D_MODEL), row(CA_DIM), _full_spec(w_co.shape), _full_spec(g_ffn.shape),
                  _full_spec(w_gate.shape), _full_spec(w_up.shape), _full_spec(w_down.shape),
                  _full_spec(g_final.shape)],
        out_specs=row(D_MODEL),
        out_shape=jax.ShapeDtypeStruct((t, D_MODEL), F32),
        compiler_params=_params(("parallel",)),
        name="ffn",
    )(h1, o, w_co, g_ffn, w_gate, w_up, w_down, g_final)


def _prompt_consts():
    qi = np.arange(WINDOW)[:, None]
    ji = np.arange(2 * WINDOW)[None, :]
    dist = qi + WINDOW - ji
    inband = (dist >= 0) & (dist < WINDOW)
    bucket = np.where(inband, _t5_bucket_np(dist), -1)
    first = np.where(ji >= WINDOW, bucket, -1)
    buckets = np.stack([first, bucket]).astype(np.int32)
    return (jnp.asarray(_tril_np(CHUNK), BF16), jnp.asarray(buckets))


def _sample_consts(dec_seq, cache_len):
    rows = SAMPLE_SEQS * dec_seq
    r = np.arange(rows)
    same = (r[:, None] // dec_seq) == (r[None, :] // dec_seq)
    tcum = (same & (r[None, :] <= r[:, None])).astype(np.float32)
    tseq = same.astype(np.float32)
    t = np.arange(dec_seq)[:, None]
    j = np.arange(cache_len)[None, :]
    dist_c = t + cache_len - j
    bkt_c = np.where((dist_c >= 0) & (dist_c < WINDOW), _t5_bucket_np(dist_c), -1).astype(np.int32)
    dist_n = (r[:, None] % dec_seq) - (r[None, :] % dec_seq)
    ok = same & (dist_n >= 0) & (dist_n < WINDOW)
    bkt_n = np.where(ok, _t5_bucket_np(dist_n), -1).astype(np.int32)
    return (jnp.asarray(tcum, BF16), jnp.asarray(tseq, BF16), jnp.asarray(_expand_np(), BF16),
            jnp.asarray(bkt_c), jnp.asarray(bkt_n))


def _pick_tile(t, pref):
    tm = min(t, pref)
    while t % tm:
        tm //= 2
    return tm


def kernel(x_prompt, x_sample, mem_prompt, state_conv, state_ssm, cache_swa_k, cache_swa_v, cache_mem_k, cache_mem_v, rel_bias, g_mix, w_in, conv_w, conv_b, dt_bias, a_log, d_skip, g_ssd_norm, sinks, w_out, g_cross, g_mem, w_cq, w_ck, w_cv, w_co, g_ffn, w_gate, w_up, w_down, g_final):
    assert g_mix.shape[0] == 1, "single-layer trunk"
    batch, seq, _ = x_prompt.shape
    n_dec, dec_seq, _ = x_sample.shape
    cache_len = cache_swa_k.shape[2]
    assert seq % CHUNK == 0 and cache_len == WINDOW and n_dec % SAMPLE_SEQS == 0 and dec_seq == 8

    wi = w_in[0]
    s1 = SSD_D_INNER + CONV_DIM
    s2 = s1 + SSD_HEADS
    w_in_r = jnp.concatenate(
        [wi[:, :s1], wi[:, s2:], jnp.pad(wi[:, s1:s2], ((0, 0), (0, DT_PAD - SSD_HEADS)))], axis=1).astype(BF16)
    row = lambda a: a.reshape(1, -1).astype(F32)
    pad_h = lambda a: jnp.pad(a.reshape(1, -1).astype(F32), ((0, 0), (0, DT_PAD - SSD_HEADS)))
    small = (conv_w[0].astype(F32), row(conv_b[0]), pad_h(dt_bias[0]), pad_h(a_log[0]),
             jnp.repeat(d_skip[0].astype(F32), SSD_HEAD_DIM).reshape(1, -1), row(g_ssd_norm[0]),
             rel_bias.astype(F32), sinks[0].astype(F32))
    bf = lambda w: w[0].astype(BF16)
    w_out_b, w_cq_b, w_ck_b, w_cv_b, w_co_b = bf(w_out), bf(w_cq), bf(w_ck), bf(w_cv), bf(w_co)
    w_gate_b, w_up_b, w_down_b = bf(w_gate), bf(w_up), bf(w_down)
    g_mix_r, g_cross_r, g_mem_r, g_ffn_r, g_fin_r = row(g_mix[0]), row(g_cross[0]), row(g_mem[0]), row(g_ffn[0]), row(g_final)

    n_chunks = 2 if seq % (2 * CHUNK) == 0 else 1
    mem2 = mem_prompt.reshape(batch * N_MEM, D_MODEL)
    mk, mv, mk_b, mv_b = _mem_kv(mem2, g_mem_r, w_ck_b, w_cv_b, _pick_tile(batch * N_MEM, 512))
    dense = (g_mix_r, w_in_r, w_out_b, g_cross_r, w_cq_b, w_co_b, g_ffn_r, w_gate_b, w_up_b, w_down_b, g_fin_r)
    y_prompt, p_conv, p_ssm, p_k, p_v = _prompt_layer(
        x_prompt, mk_b.reshape(batch, N_MEM, CA_DIM), mv_b.reshape(batch, N_MEM, CA_DIM),
        dense, small, _prompt_consts(), n_chunks)

    def channel_major(cache):
        return jnp.transpose(cache, (0, 2, 3, 1)).reshape(n_dec, KV_DIM, cache_len)

    def position_major(cache_t):
        return jnp.transpose(cache_t.reshape(n_dec, N_KV_HEADS, ATTN_HEAD_DIM, cache_len), (0, 3, 1, 2))

    ts = n_dec * dec_seq
    xs2 = x_sample.reshape(ts, D_MODEL)
    tm_s = _pick_tile(ts, 512)
    mix_s, s_conv, s_ssm, s_k, s_v = _sample_mixer(
        xs2, g_mix_r, w_in_r, jnp.transpose(state_conv[0], (1, 0, 2)),
        state_ssm[0].reshape(n_dec, SSD_D_INNER, D_STATE),
        channel_major(cache_swa_k[0]), channel_major(cache_swa_v[0]),
        small, _sample_consts(dec_seq, cache_len), n_dec, dec_seq)
    h1s, os_ = _cross_sample(xs2, mix_s, w_out_b, g_cross_r, w_cq_b,
                             cache_mem_k[0].reshape(n_dec, N_MEM * CA_HEADS, CA_HEAD_DIM),
                             cache_mem_v[0].reshape(n_dec, N_MEM * CA_HEADS, CA_HEAD_DIM), n_dec, dec_seq, SAMPLE_SEQS)
    y_sample = _post2(h1s, os_, w_co_b, g_ffn_r, w_gate_b, w_up_b, w_down_b, g_fin_r, tm_s)

    return (y_prompt.reshape(batch, seq, D_MODEL), y_sample.reshape(n_dec, dec_seq, D_MODEL),
            p_conv[None], p_ssm.reshape(1, batch, SSD_HEADS, SSD_HEAD_DIM, D_STATE),
            p_k.reshape(1, batch, WINDOW, N_KV_HEADS, ATTN_HEAD_DIM),
            p_v.reshape(1, batch, WINDOW, N_KV_HEADS, ATTN_HEAD_DIM),
            mk.reshape(1, batch, N_MEM, CA_HEADS, CA_HEAD_DIM), mv.reshape(1, batch, N_MEM, CA_HEADS, CA_HEAD_DIM),
            jnp.transpose(s_conv, (1, 0, 2))[None], s_ssm.reshape(1, n_dec, SSD_HEADS, SSD_HEAD_DIM, D_STATE),
            position_major(s_k)[None], position_major(s_v)[None])
```

```python
import functools
import math

import numpy as np
import jax
import jax.numpy as jnp
from jax import lax
from jax.experimental import pallas as pl
from jax.experimental.pallas import tpu as pltpu

F32 = jnp.float32
BF16 = jnp.bfloat16

D_MODEL = 1024
SSD_D_INNER = 512
SSD_HEAD_DIM = 64
SSD_HEADS = 8
SSD_GROUPS = 2
GROUP_W = SSD_D_INNER // SSD_GROUPS
D_STATE = 128
CONV_K = 4
CONV_DIM = SSD_D_INNER + 2 * SSD_GROUPS * D_STATE
CHUNK = 128
ATTN_DIM = 512
ATTN_HEAD_DIM = 64
N_HEADS = 8
N_KV_HEADS = 2
Q_PER_KV = N_HEADS // N_KV_HEADS
KV_DIM = N_KV_HEADS * ATTN_HEAD_DIM
WINDOW = 128
N_BUCKETS = 32
MAX_EXACT = N_BUCKETS // 2
MAX_DISTANCE = 128
N_MEM = 256
CA_HEADS = 4
CA_HEAD_DIM = 128
CA_DIM = CA_HEADS * CA_HEAD_DIM
D_FF = 2816
EPS = 1e-6

LANES = 128
HALF = LANES // 2
DT_PAD = LANES
COL_Z = 0
COL_XBC = COL_Z + SSD_D_INNER
COL_Q = COL_XBC + CONV_DIM
COL_K = COL_Q + ATTN_DIM
COL_V = COL_K + KV_DIM
COL_DT = COL_V + KV_DIM
IN_COLS = COL_DT + DT_PAD
NEG = -1e30
SAMPLE_SEQS = 16
VMEM_LIMIT = 48 * 1024 * 1024
FF_SPLITS = ((0, 1024), (1024, 2048), (2048, D_FF))
SECOND_STREAM_LEAD = 0.5
FF_PIECES = tuple((lo, min(lo + 512, D_FF)) for lo in range(0, D_FF, 512))
PROJ_PIECES = ((COL_Z, COL_XBC), (COL_XBC, COL_XBC + 512), (COL_XBC + 512, COL_Q), (COL_Q, COL_K), (COL_K, IN_COLS))


def _rms(x, g):
    return x * lax.rsqrt(jnp.mean(x * x, axis=-1, keepdims=True) + EPS) * g


def _silu(x):
    return x * jax.nn.sigmoid(x)


def _softplus(x):
    return jnp.maximum(x, 0.0) + jnp.log1p(jnp.exp(-jnp.abs(x)))


def _dot(a, b):
    return jnp.dot(a, b, preferred_element_type=F32)


def _dot_nt(a, b):
    return lax.dot_general(a, b, (((1,), (1,)), ((), ())), preferred_element_type=F32)


def _split3(a):
    hi = a.astype(BF16)
    r = a - hi.astype(F32)
    mid = r.astype(BF16)
    lo = (r - mid.astype(F32)).astype(BF16)
    return hi, mid, lo


def _sel_left(t01, a):
    hi, mid, lo = _split3(a)
    return _dot(t01, hi) + _dot(t01, mid) + _dot(t01, lo)


def _sel_right(a, e01):
    hi, mid, lo = _split3(a)
    return _dot(hi, e01) + _dot(mid, e01) + _dot(lo, e01)


def _lane_lo():
    return lax.broadcasted_iota(jnp.int32, (1, LANES), 1) < HALF


def _t5_bucket_np(dist):
    n = np.maximum(dist, 0)
    ratio = np.log(np.maximum(n, 1).astype(np.float32) / np.float32(MAX_EXACT))
    large = MAX_EXACT + (ratio / np.float32(math.log(MAX_DISTANCE / MAX_EXACT))
                         * np.float32(N_BUCKETS - MAX_EXACT)).astype(np.int32)
    large = np.minimum(large, N_BUCKETS - 1)
    return np.where(n < MAX_EXACT, n, large).astype(np.int32)


def _tril_np(n):
    return np.tril(np.ones((n, n), np.float32))


def _expand_np():
    e = np.zeros((LANES, SSD_D_INNER), np.float32)
    for h in range(SSD_HEADS):
        e[h, h * SSD_HEAD_DIM:(h + 1) * SSD_HEAD_DIM] = 1.0
    return e


def _full_spec(shape):
    nd = len(shape)
    return pl.BlockSpec(shape, lambda *_: (0,) * nd, pipeline_mode=pl.Buffered(1))


def _smem_spec():
    return pl.BlockSpec(memory_space=pltpu.SMEM)


def _params(sem):
    return pltpu.CompilerParams(dimension_semantics=sem, vmem_limit_bytes=VMEM_LIMIT)


def _in_proj_body(x_ref, g_ref, w_ref, z_ref, xbc_ref, q_ref, k_ref, v_ref, dt_ref):
    xn = _rms(x_ref[...], g_ref[...]).astype(BF16)

    def seg(lo, hi):
        return _dot(xn, w_ref[:, lo:hi])

    z_ref[...] = seg(COL_Z, COL_XBC)
    xbc_ref[...] = seg(COL_XBC, COL_Q)
    q_ref[...] = (seg(COL_Q, COL_K) * (ATTN_HEAD_DIM ** -0.5)).astype(BF16)
    k_ref[...] = seg(COL_K, COL_V)
    v_ref[...] = seg(COL_V, COL_DT)
    dt_ref[...] = seg(COL_DT, IN_COLS)


def _in_proj(x2, g_mix, w_in_r, tm):
    t = x2.shape[0]
    row = lambda w: pl.BlockSpec((tm, w), lambda i: (i, 0))
    outs = [(SSD_D_INNER, F32), (CONV_DIM, F32), (ATTN_DIM, BF16), (KV_DIM, F32), (KV_DIM, F32), (DT_PAD, F32)]
    return pl.pallas_call(
        _in_proj_body,
        grid=(t // tm,),
        in_specs=[row(D_MODEL), _full_spec((1, D_MODEL)), _full_spec((D_MODEL, IN_COLS))],
        out_specs=[row(w) for w, _ in outs],
        out_shape=[jax.ShapeDtypeStruct((t, w), d) for w, d in outs],
        compiler_params=_params(("parallel",)),
        name="in_proj",
    )(x2, g_mix, w_in_r)


def _conv_taps(cw_ref, cb_ref, taps):
    acc = cb_ref[...] + taps[0] * cw_ref[0:1, :]
    for k in range(1, CONV_K):
        acc = acc + taps[k] * cw_ref[k:k + 1, :]
    return _silu(acc)


def _ssd_prepare(conv, dt_raw, dtb_ref, alog_ref, tcum, total_fn, extra_fn, expand):
    xs = conv[:, :SSD_D_INNER]
    bm = conv[:, SSD_D_INNER:SSD_D_INNER + SSD_GROUPS * D_STATE]
    cm = conv[:, SSD_D_INNER + SSD_GROUPS * D_STATE:]
    dt = _softplus(dt_raw + dtb_ref[...])
    a = dt * (-jnp.exp(alog_ref[...]))
    cs = _sel_left(tcum, a)
    total = total_fn(cs, a)
    pieces = [dt, jnp.exp(total - cs), jnp.exp(cs)] + extra_fn(total)
    rows = cs.shape[0]
    ex = _sel_right(jnp.concatenate(pieces, axis=0), expand)
    ex = [ex[i * rows:(i + 1) * rows] for i in range(len(pieces))]
    return xs, bm, cm, cs, cs.T, ex


def _ssd_diag(cs, cs_t, cb_g, xdt, mask, g):
    lo = _lane_lo()
    out = []
    for pr in range(2):
        h0 = g * 4 + 2 * pr
        xp = xdt[:, (h0 // 2) * LANES:(h0 // 2 + 1) * LANES]
        x_lo = jnp.where(lo, xp, 0.0).astype(BF16)
        x_hi = jnp.where(lo, 0.0, xp).astype(BF16)
        acc = None
        for h, xh in ((h0, x_lo), (h0 + 1, x_hi)):
            diff = cs[:, h:h + 1] - cs_t[h:h + 1, :]
            decay = jnp.exp(jnp.where(mask, diff, -jnp.inf))
            part = _dot((cb_g * decay).astype(BF16), xh)
            acc = part if acc is None else acc + part
        out.append(acc)
    return jnp.concatenate(out, axis=-1)


def _gated_norm(y, z, gn_ref):
    yf = y * _silu(z)
    parts = []
    for g in range(SSD_GROUPS):
        yg = yf[:, g * GROUP_W:(g + 1) * GROUP_W]
        parts.append(yg * lax.rsqrt(jnp.mean(yg * yg, axis=-1, keepdims=True) + EPS))
    return jnp.concatenate(parts, axis=-1) * gn_ref[...]


def _build_bias(bias_ref, bucket_of, relb_ref, n_tables):
    for i in range(n_tables):
        for h in range(N_HEADS):
            bias_ref[i, h] = jnp.full(bias_ref.shape[2:], NEG, F32)

    def body(t, carry):
        for i in range(n_tables):
            hit = bucket_of(i) == t
            for h in range(N_HEADS):
                bias_ref[i, h] = jnp.where(hit, relb_ref[t, h], bias_ref[i, h])
        return carry

    lax.fori_loop(0, N_BUCKETS, body, 0)


def _alternate(first, second):
    streams = [[0.0, 1.0, first], [0.0, SECOND_STREAM_LEAD, second]]
    while streams:
        entry = min(streams, key=lambda e: e[0])
        try:
            entry[0] += entry[1] * next(entry[2])
        except StopIteration:
            streams.remove(entry)


def _layer_step_body(x_ref, xprev_ref, mk_ref, mv_ref, gmix_ref, win_ref, wout_ref, gc_ref, wcq_ref,
                     wco_ref, gf_ref, wg_ref, wu_ref, wd_ref, gfin_ref,
                     cw_ref, cb_ref, dtb_ref, alog_ref, dskip_ref, gn_ref,
                     tril_ref, bucket_ref, relb_ref, sink_ref,
                     y_ref, conv_out, ssm_out, k_out, v_out,
                     xpad, state_t, kbuf, vbuf, bias, mix, proj, *, n_chunks, tiles_per_seq):
    step = pl.program_id(0)
    n_tiles = pl.num_programs(0) - 1
    c = lax.rem(jnp.minimum(step, n_tiles - 1), tiles_per_seq)
    last = tiles_per_seq - 1
    tq = n_chunks * CHUNK

    @pl.when(step == 0)
    def _():
        _build_bias(bias, lambda i: bucket_ref[i], relb_ref, 2)
        mix[...] = jnp.zeros_like(mix)

    @pl.when(c == 0)
    def _():
        xpad[0:8, :] = jnp.zeros((8, CONV_DIM), F32)
        state_t[...] = jnp.zeros_like(state_t)
        kbuf[:, 0:CHUNK, :] = jnp.zeros((2, CHUNK, KV_DIM), BF16)
        vbuf[:, 0:CHUNK, :] = jnp.zeros((2, CHUNK, KV_DIM), BF16)

    def back():
        h1 = xprev_ref[0] + _dot(mix[...], wout_ref[...])
        yield 4
        qc = _dot(_rms(h1, gc_ref[...]).astype(BF16), wcq_ref[...]).astype(BF16)
        yield 2
        heads = []
        for h in range(CA_HEADS):
            hs = slice(h * CA_HEAD_DIM, (h + 1) * CA_HEAD_DIM)
            sc = _dot_nt(qc[:, hs], mk_ref[0, :, hs]) * (CA_HEAD_DIM ** -0.5)
            m = jnp.max(sc, axis=-1, keepdims=True)
            p = jnp.exp(sc - m)
            heads.append(_dot(p.astype(BF16), mv_ref[0, :, hs]) / jnp.sum(p, axis=-1, keepdims=True))
            yield 3
        o = jnp.concatenate(heads, axis=-1).astype(BF16)
        h2 = h1 + _dot(o, wco_ref[...])
        hn = _rms(h2, gf_ref[...]).astype(BF16)
        yield 4
        acc = h2
        for f_lo, f_hi in FF_PIECES:
            width = (f_hi - f_lo) / 512
            gate = _dot(hn, wg_ref[:, f_lo:f_hi])
            yield 5 * width
            act = (_silu(gate) * _dot(hn, wu_ref[:, f_lo:f_hi])).astype(BF16)
            yield 6 * width
            acc = acc + _dot(act, wd_ref[f_lo:f_hi, :])
            yield 5 * width
        y_ref[0] = _rms(acc, gfin_ref[...])

    def front():
        x = x_ref[0]
        xn = _rms(x, gmix_ref[...]).astype(BF16)
        for p_lo, p_hi in PROJ_PIECES:
            if COL_XBC <= p_lo and p_hi <= COL_Q:
                xpad[8:8 + tq, p_lo - COL_XBC:p_hi - COL_XBC] = _dot(xn, win_ref[:, p_lo:p_hi])
            else:
                proj[:, p_lo:p_hi] = _dot(xn, win_ref[:, p_lo:p_hi])
            yield 4

        conv_halves = []
        half_w = CONV_DIM // 2
        for hf in range(2):
            cs_ = slice(hf * half_w, (hf + 1) * half_w)
            acc = cb_ref[:, cs_] + xpad[5:5 + tq, cs_] * cw_ref[0:1, cs_]
            for kk in range(1, CONV_K):
                acc = acc + xpad[5 + kk:5 + kk + tq, cs_] * cw_ref[kk:kk + 1, cs_]
            conv_halves.append(_silu(acc))
            yield 6
        xpad[5:8, :] = xpad[5 + tq:8 + tq, :]
        xs_all = conv_halves[0]
        bc_all = conv_halves[1]

        k_new = proj[:, COL_K:COL_V]
        v_new = proj[:, COL_V:COL_DT]
        kbuf[0, CHUNK:, :] = k_new.astype(BF16)
        kbuf[1, CHUNK:, :] = pltpu.roll(k_new, HALF, 1).astype(BF16)
        vbuf[0, CHUNK:, :] = v_new.astype(BF16)
        vbuf[1, CHUNK:, :] = pltpu.roll(v_new, HALF, 1).astype(BF16)
        q = (proj[:, COL_Q:COL_K] * (ATTN_HEAD_DIM ** -0.5)).astype(BF16)
        yield 2

        row = lax.broadcasted_iota(jnp.int32, (CHUNK, CHUNK), 0)
        col = lax.broadcasted_iota(jnp.int32, (CHUNK, CHUNK), 1)
        causal = col <= row
        lo = _lane_lo()
        for ci in range(n_chunks):
            rs = slice(ci * CHUNK, (ci + 1) * CHUNK)
            xs = xs_all[rs]
            bm = bc_all[rs, 0:SSD_GROUPS * D_STATE]
            cm = bc_all[rs, SSD_GROUPS * D_STATE:]
            dt = _softplus(proj[rs, COL_DT:IN_COLS] + dtb_ref[...])
            a = dt * (-jnp.exp(alog_ref[...]))
            cs = _sel_left(tril_ref[...], a)
            cs_t = cs.T
            dt_t = dt.T
            total = cs[CHUNK - 1:CHUNK, :]
            ecs = jnp.exp(cs)
            w_end = dt * jnp.exp(total - cs)
            bb = bm.astype(BF16)
            cb = cm.astype(BF16)
            yield 5
            y_parts = []
            for g in range(SSD_GROUPS):
                gs = slice(g * GROUP_W, (g + 1) * GROUP_W)
                ns = slice(g * D_STATE, (g + 1) * D_STATE)
                cb_g = _dot_nt(cb[:, ns], bb[:, ns])
                y_diag, xw, ecs_g = [], [], []
                for pr in range(2):
                    h0 = g * 4 + 2 * pr
                    xp = xs[:, (h0 // 2) * LANES:(h0 // 2 + 1) * LANES]
                    ecs_g.append(jnp.where(lo, ecs[:, h0:h0 + 1], ecs[:, h0 + 1:h0 + 2]))
                    xw.append((xp * jnp.where(lo, w_end[:, h0:h0 + 1], w_end[:, h0 + 1:h0 + 2])).astype(BF16))
                    x2 = jnp.concatenate([jnp.where(lo, xp, 0.0), jnp.where(lo, 0.0, xp)], axis=0).astype(BF16)
                    gmat = []
                    for h in (h0, h0 + 1):
                        diff = cs[:, h:h + 1] - cs_t[h:h + 1, :]
                        decay = jnp.exp(jnp.where(causal, diff, -jnp.inf))
                        gmat.append((cb_g * decay * dt_t[h:h + 1, :]).astype(BF16))
                    y_diag.append(_dot(jnp.concatenate(gmat, axis=1), x2))
                ecs_x = jnp.concatenate(ecs_g, axis=-1)
                st_old = state_t[:, gs]
                y_off = _dot(cb[:, ns], st_old.astype(BF16)) * ecs_x
                y_parts.append(jnp.concatenate(y_diag, axis=-1) + y_off)
                b_t = bm[:, ns].T.astype(BF16)
                state_t[:, gs] = ecs_x[CHUNK - 1:CHUNK, :] * st_old + _dot(b_t, jnp.concatenate(xw, axis=-1))
                yield 9
            y = jnp.concatenate(y_parts, axis=-1) + dskip_ref[...] * xs
            mix[rs, 0:SSD_D_INNER] = _gated_norm(y, proj[rs, COL_Z:COL_XBC], gn_ref).astype(BF16)
            yield 3

            keys = slice(ci * CHUNK, (ci + 2) * CHUNK)
            table = jnp.minimum(c, 1) if ci == 0 else 1
            probs, denoms = {}, {}
            for variant in range(2):
                hv = [h for h in range(N_HEADS) if (h // Q_PER_KV + h % 2) % 2 == variant]
                qms = []
                for h in hv:
                    q_pair = q[rs, (h // 2) * LANES:(h // 2 + 1) * LANES]
                    qms.append(jnp.where(lo, q_pair, 0.0) if h % 2 == 0 else jnp.where(lo, 0.0, q_pair))
                sc_all = _dot_nt(jnp.concatenate(qms, axis=0).astype(BF16), kbuf[variant, keys, :])
                for i, h in enumerate(hv):
                    sc = sc_all[i * CHUNK:(i + 1) * CHUNK] + bias[table, h]
                    sink = sink_ref[h]
                    m = jnp.maximum(jnp.max(sc, axis=-1, keepdims=True), sink)
                    p = jnp.exp(sc - m)
                    denoms[h] = jnp.sum(p, axis=-1, keepdims=True) + jnp.exp(sink - m)
                    probs[h] = p.astype(BF16)
                yield 10
            parts = {}
            for variant in range(2):
                vv = vbuf[variant, keys, :]
                for par in range(2):
                    hv = [h for h in range(N_HEADS) if (h // Q_PER_KV + h % 2) % 2 == variant and h % 2 == par]
                    vm = jnp.where(lo, vv, 0.0) if par == 0 else jnp.where(lo, 0.0, vv)
                    o_all = _dot(jnp.concatenate([probs[h] for h in hv], axis=0), vm.astype(BF16))
                    for i, h in enumerate(hv):
                        parts[h] = o_all[i * CHUNK:(i + 1) * CHUNK] / denoms[h]
            for pair in range(N_HEADS // 2):
                mix[rs, SSD_D_INNER + pair * LANES:SSD_D_INNER + (pair + 1) * LANES] = (
                    parts[2 * pair] + parts[2 * pair + 1]).astype(BF16)
            yield 4
        kbuf[:, 0:CHUNK, :] = kbuf[:, tq:tq + CHUNK, :]
        vbuf[:, 0:CHUNK, :] = vbuf[:, tq:tq + CHUNK, :]

    _alternate(back(), front())

    @pl.when((c == last) & (step < n_tiles))
    def _():
        conv_out[0] = xpad[5:8, :]
        ssm_out[0] = state_t[...].T
        k_out[0] = proj[tq - WINDOW:tq, COL_K:COL_V]
        v_out[0] = proj[tq - WINDOW:tq, COL_V:COL_DT]


def _prompt_layer(x3, mk_b, mv_b, dense, small, consts, n_chunks):
    cw, cb, dtb, alog, dskip_x, gn, relb, sinks = small
    tril, buckets = consts
    batch, seq, _ = x3.shape
    tq = n_chunks * CHUNK
    tps = seq // tq
    n_tiles = batch * tps
    front = lambda s: jnp.minimum(s, n_tiles - 1)
    back = lambda s: jnp.maximum(s - 1, 0)
    x_spec = pl.BlockSpec((1, tq, D_MODEL), lambda s: (front(s) // tps, front(s) % tps, 0))
    xprev_spec = pl.BlockSpec((1, tq, D_MODEL), lambda s: (back(s) // tps, back(s) % tps, 0))
    y_spec = pl.BlockSpec((1, tq, D_MODEL), lambda s: (back(s) // tps, back(s) % tps, 0))
    mem_spec = pl.BlockSpec((1, N_MEM, CA_DIM), lambda s: (back(s) // tps, 0, 0))
    per_b = lambda shp: pl.BlockSpec((1,) + shp, lambda s: (front(s) // tps,) + (0,) * len(shp))
    out_shape = [
        jax.ShapeDtypeStruct((batch, seq, D_MODEL), F32),
        jax.ShapeDtypeStruct((batch, CONV_K - 1, CONV_DIM), F32),
        jax.ShapeDtypeStruct((batch, SSD_D_INNER, D_STATE), F32),
        jax.ShapeDtypeStruct((batch, WINDOW, KV_DIM), F32),
        jax.ShapeDtypeStruct((batch, WINDOW, KV_DIM), F32),
    ]
    full = list(dense) + [cw, cb, dtb, alog, dskip_x, gn, tril, buckets]
    return pl.pallas_call(
        functools.partial(_layer_step_body, n_chunks=n_chunks, tiles_per_seq=tps),
        grid=(n_tiles + 1,),
        in_specs=([x_spec, xprev_spec, mem_spec, mem_spec] + [_full_spec(a.shape) for a in full]
                  + [_smem_spec(), _smem_spec()]),
        out_specs=[y_spec, per_b((CONV_K - 1, CONV_DIM)), per_b((SSD_D_INNER, D_STATE)),
                   per_b((WINDOW, KV_DIM)), per_b((WINDOW, KV_DIM))],
        out_shape=out_shape,
        scratch_shapes=[
            pltpu.VMEM((8 + tq, CONV_DIM), F32),
            pltpu.VMEM((D_STATE, SSD_D_INNER), F32),
            pltpu.VMEM((2, CHUNK + tq, KV_DIM), BF16),
            pltpu.VMEM((2, CHUNK + tq, KV_DIM), BF16),
            pltpu.VMEM((2, N_HEADS, WINDOW, 2 * WINDOW), F32),
            pltpu.VMEM((tq, D_MODEL), BF16),
            pltpu.VMEM((tq, IN_COLS), F32),
        ],
        compiler_params=_params(("arbitrary",)),
        name="prompt_layer",
    )(x3, x3, mk_b, mv_b, *full, relb, sinks)


def _sample_mixer_body(z_ref, xbc_ref, q_ref, k_ref, v_ref, dt_ref,
                       sconv_ref, sssm_ref, ck_ref, cv_ref,
                       cw_ref, cb_ref, dtb_ref, alog_ref, dskip_ref, gn_ref,
                       tcum_ref, tseq_ref, expand_ref, bkt_c_ref, bkt_n_ref, relb_ref, sink_ref,
                       mix_ref, conv_out, ssm_out, k_out, v_out,
                       xpad, bias_c, bias_n, *, dec_seq):
    step = pl.program_id(0)
    n_seq = SAMPLE_SEQS
    rows = n_seq * dec_seq

    @pl.when(step == 0)
    def _():
        _build_bias(bias_c, lambda i: bkt_c_ref[...], relb_ref, 1)
        _build_bias(bias_n, lambda i: bkt_n_ref[...], relb_ref, 1)

    xbc = xbc_ref[...]
    xpad[:, 8:8 + dec_seq, :] = xbc.reshape(n_seq, dec_seq, CONV_DIM)
    for r in range(CONV_K - 1):
        xpad[:, 5 + r, :] = sconv_ref[r]
    taps = [xpad[:, 5 + k:5 + k + dec_seq, :].reshape(rows, CONV_DIM) for k in range(CONV_K - 1)] + [xbc]
    conv = _conv_taps(cw_ref, cb_ref, taps)
    for r in range(CONV_K - 1):
        conv_out[r] = xpad[:, 5 + dec_seq + r, :]

    row = lax.broadcasted_iota(jnp.int32, (rows, rows), 0)
    col = lax.broadcasted_iota(jnp.int32, (rows, rows), 1)
    tseq = tseq_ref[...]
    same_seq = tseq > 0
    causal = same_seq & (col <= row)
    xs, bm, cm, cs, cs_t, (dt_x, dend_x, ecs_x, seqdec_x) = _ssd_prepare(
        conv, dt_ref[...], dtb_ref, alog_ref, tcum_ref[...],
        lambda cs_, a_: _sel_left(tseq, a_), lambda total: [jnp.exp(total)], expand_ref[...])
    xdt = xs * dt_x
    bb = bm.astype(BF16)
    cb = cm.astype(BF16)
    seq_of_row = lax.broadcasted_iota(jnp.int32, (rows, 1), 0) // dec_seq
    seq_of_lane = lax.broadcasted_iota(jnp.int32, (1, rows), 1) // dec_seq
    xde_t = (xdt * dend_x).T
    y_off = [None] * SSD_GROUPS
    for i in range(n_seq):
        st = sssm_ref[i]
        dec = jnp.concatenate(
            [jnp.broadcast_to(seqdec_x[i * dec_seq:i * dec_seq + 1, h * SSD_HEAD_DIM:h * SSD_HEAD_DIM + 1],
                              (SSD_HEAD_DIM, D_STATE)) for h in range(SSD_HEADS)], axis=0)
        new_parts = []
        for g in range(SSD_GROUPS):
            gs = slice(g * GROUP_W, (g + 1) * GROUP_W)
            ns = slice(g * D_STATE, (g + 1) * D_STATE)
            c_i = jnp.where(seq_of_row == i, cm[:, ns], 0.0).astype(BF16)
            part = _dot_nt(c_i, st[gs].astype(BF16))
            y_off[g] = part if y_off[g] is None else y_off[g] + part
            x_i = jnp.where(seq_of_lane == i, xde_t[gs], 0.0).astype(BF16)
            new_parts.append(dec[gs] * st[gs] + _dot(x_i, bb[:, ns]))
        ssm_out[i] = jnp.concatenate(new_parts, axis=0)
    y_parts = []
    for g in range(SSD_GROUPS):
        gs = slice(g * GROUP_W, (g + 1) * GROUP_W)
        ns = slice(g * D_STATE, (g + 1) * D_STATE)
        cb_g = _dot_nt(cb[:, ns], bb[:, ns])
        y_parts.append(_ssd_diag(cs, cs_t, cb_g, xdt, causal, g) + y_off[g] * ecs_x[:, gs])
    y = jnp.concatenate(y_parts, axis=-1) + dskip_ref[...] * xs
    mix_ref[:, 0:SSD_D_INNER] = _gated_norm(y, z_ref[...], gn_ref).astype(BF16)

    lo = _lane_lo()
    k_new = k_ref[...]
    v_new = v_ref[...]
    k_var = [k_new.astype(BF16), pltpu.roll(k_new, HALF, 1).astype(BF16)]
    v_new_r = pltpu.roll(v_new, HALF, 1)
    v_dup = [jnp.where(lo, v_new, v_new_r).astype(BF16), jnp.where(lo, v_new_r, v_new).astype(BF16)]
    qf = q_ref[...].astype(F32)
    q_masked = []
    s_new = []
    for h in range(N_HEADS):
        pair, par = h // 2, h % 2
        j = h // Q_PER_KV
        q_pair = qf[:, pair * LANES:(pair + 1) * LANES]
        qm = jnp.where(lo, q_pair, 0.0) if par == 0 else jnp.where(lo, 0.0, q_pair)
        q_masked.append(qm)
        s_new.append(_dot_nt(qm.astype(BF16), k_var[(j + par) % 2]) + bias_n[0, h])
    stack_rows = lax.broadcasted_iota(jnp.int32, (Q_PER_KV * dec_seq, 1), 0) // dec_seq
    k_new_t = k_new.T
    v_new_t = v_new.T
    keep = WINDOW - dec_seq
    old_lane = lax.broadcasted_iota(jnp.int32, (1, WINDOW), 1) < keep
    grp = Q_PER_KV * dec_seq
    sc_parts, sn_parts, sink_parts = [], [], []
    for i in range(n_seq):
        rs = slice(i * dec_seq, (i + 1) * dec_seq)
        kc_t = ck_ref[i]
        for j in range(N_KV_HEADS):
            heads = range(j * Q_PER_KV, (j + 1) * Q_PER_KV)
            cj = slice(j * ATTN_HEAD_DIM, (j + 1) * ATTN_HEAD_DIM)
            kdup_t = jnp.concatenate([kc_t[cj], kc_t[cj]], axis=0).astype(BF16)
            qs = jnp.concatenate([q_masked[h][rs] for h in heads], axis=0).astype(BF16)
            sc_parts.append(_dot(qs, kdup_t) + jnp.concatenate([bias_c[0, h] for h in heads], axis=0))
            sn_parts.append(jnp.concatenate([s_new[h][rs] for h in heads], axis=0))
            if i == 0:
                sink_j = jnp.zeros((grp, 1), F32)
                for hh, h in enumerate(heads):
                    sink_j = jnp.where(stack_rows == hh, sink_ref[h], sink_j)
                sink_parts.append(sink_j)
    sc = jnp.concatenate(sc_parts, axis=0)
    sn = jnp.concatenate(sn_parts, axis=0)
    sink = jnp.concatenate(sink_parts * n_seq, axis=0)
    m = jnp.maximum(jnp.maximum(jnp.max(sc, axis=-1, keepdims=True), jnp.max(sn, axis=-1, keepdims=True)), sink)
    pc = jnp.exp(sc - m)
    pn = jnp.exp(sn - m)
    rdenom = 1.0 / (jnp.sum(pc, axis=-1, keepdims=True) + jnp.sum(pn, axis=-1, keepdims=True) + jnp.exp(sink - m))
    pc = pc.astype(BF16)
    pn = pn.astype(BF16)
    att_rows = []
    for i in range(n_seq):
        vc_t = cv_ref[i]
        pieces = []
        for j in range(N_KV_HEADS):
            cj = slice(j * ATTN_HEAD_DIM, (j + 1) * ATTN_HEAD_DIM)
            gr = slice((i * N_KV_HEADS + j) * grp, (i * N_KV_HEADS + j + 1) * grp)
            vdup_t = jnp.concatenate([vc_t[cj], vc_t[cj]], axis=0).astype(BF16)
            o = (_dot_nt(pc[gr], vdup_t) + _dot(pn[gr], v_dup[j])) * rdenom[gr]
            for pr in range(Q_PER_KV // 2):
                even = o[(2 * pr) * dec_seq:(2 * pr + 1) * dec_seq]
                odd = o[(2 * pr + 1) * dec_seq:(2 * pr + 2) * dec_seq]
                pieces.append(jnp.where(lo, even, odd))
        att_rows.append(jnp.concatenate(pieces, axis=-1))
        new_shift = (keep - i * dec_seq) % WINDOW
        k_out[i] = jnp.where(old_lane, pltpu.roll(ck_ref[i], keep, 1), pltpu.roll(k_new_t, new_shift, 1))
        v_out[i] = jnp.where(old_lane, pltpu.roll(vc_t, keep, 1), pltpu.roll(v_new_t, new_shift, 1))
    mix_ref[:, SSD_D_INNER:] = jnp.concatenate(att_rows, axis=0).astype(BF16)


def _sample_mixer(z, xbc, q, k, v, dt, sconv, sssm, ck, cv, small, consts, n_seq_total, dec_seq):
    cw, cb, dtb, alog, dskip_x, gn, relb, sinks = small
    tcum, tseq, expand, bkt_c, bkt_n = consts
    rows = SAMPLE_SEQS * dec_seq
    tok = lambda w: pl.BlockSpec((rows, w), lambda i: (i, 0))
    per_s = lambda s: pl.BlockSpec((SAMPLE_SEQS,) + s, lambda i: (i,) + (0,) * len(s))
    out_shape = [
        jax.ShapeDtypeStruct((n_seq_total * dec_seq, D_MODEL), BF16),
        jax.ShapeDtypeStruct((CONV_K - 1, n_seq_total, CONV_DIM), F32),
        jax.ShapeDtypeStruct((n_seq_total, SSD_D_INNER, D_STATE), F32),
        jax.ShapeDtypeStruct((n_seq_total, WINDOW, KV_DIM), F32),
        jax.ShapeDtypeStruct((n_seq_total, WINDOW, KV_DIM), F32),
    ]
    conv_spec = pl.BlockSpec((CONV_K - 1, SAMPLE_SEQS, CONV_DIM), lambda i: (0, i, 0))
    return pl.pallas_call(
        functools.partial(_sample_mixer_body, dec_seq=dec_seq),
        grid=(n_seq_total // SAMPLE_SEQS,),
        in_specs=[tok(SSD_D_INNER), tok(CONV_DIM), tok(ATTN_DIM), tok(KV_DIM), tok(KV_DIM), tok(DT_PAD),
                  conv_spec, per_s((SSD_D_INNER, D_STATE)),
                  per_s((WINDOW, KV_DIM)), per_s((WINDOW, KV_DIM)),
                  _full_spec(cw.shape), _full_spec(cb.shape), _full_spec(dtb.shape), _full_spec(alog.shape),
                  _full_spec(dskip_x.shape), _full_spec(gn.shape),
                  _full_spec(tcum.shape), _full_spec(tseq.shape), _full_spec(expand.shape),
                  _full_spec(bkt_c.shape), _full_spec(bkt_n.shape), _smem_spec(), _smem_spec()],
        out_specs=[tok(D_MODEL), conv_spec, per_s((SSD_D_INNER, D_STATE)),
                   per_s((WINDOW, KV_DIM)), per_s((WINDOW, KV_DIM))],
        out_shape=out_shape,
        scratch_shapes=[
            pltpu.VMEM((SAMPLE_SEQS, 8 + dec_seq, CONV_DIM), F32),
            pltpu.VMEM((1, N_HEADS, dec_seq, WINDOW), F32),
            pltpu.VMEM((1, N_HEADS, rows, rows), F32),
        ],
        compiler_params=_params(("arbitrary",)),
        name="sample_mixer",
    )(z, xbc, q, k, v, dt, sconv, sssm, ck, cv, cw, cb, dtb, alog, dskip_x, gn,
      tcum, tseq, expand, bkt_c, bkt_n, relb, sinks)


def _post1_body(x_ref, mix_ref, wout_ref, gc_ref, wcq_ref, h_ref, qc_ref):
    h = x_ref[...] + _dot(mix_ref[...], wout_ref[...])
    h_ref[...] = h
    qc_ref[...] = _dot(_rms(h, gc_ref[...]).astype(BF16), wcq_ref[...]).astype(BF16)


def _post1(x2, mix, w_out, g_cross, w_cq, tm):
    t = x2.shape[0]
    row = lambda w: pl.BlockSpec((tm, w), lambda i: (i, 0))
    return pl.pallas_call(
        _post1_body,
        grid=(t // tm,),
        in_specs=[row(D_MODEL), row(D_MODEL), _full_spec(w_out.shape), _full_spec(g_cross.shape),
                  _full_spec(w_cq.shape)],
        out_specs=[row(D_MODEL), row(CA_DIM)],
        out_shape=[jax.ShapeDtypeStruct((t, D_MODEL), F32), jax.ShapeDtypeStruct((t, CA_DIM), BF16)],
        compiler_params=_params(("parallel",)),
        name="out_proj",
    )(x2, mix, w_out, g_cross, w_cq)


def _mem_kv_body(mem_ref, g_ref, wk_ref, wv_ref, k_ref, v_ref, kb_ref, vb_ref):
    mn = _rms(mem_ref[...], g_ref[...]).astype(BF16)
    k = _dot(mn, wk_ref[...])
    v = _dot(mn, wv_ref[...])
    rows = k.shape[0]
    for h in range(CA_HEADS):
        hs = slice(h * CA_HEAD_DIM, (h + 1) * CA_HEAD_DIM)
        k_ref[pl.ds(h, rows, stride=CA_HEADS), :] = k[:, hs]
        v_ref[pl.ds(h, rows, stride=CA_HEADS), :] = v[:, hs]
    kb_ref[...] = k.astype(BF16)
    vb_ref[...] = v.astype(BF16)


def _mem_kv(mem2, g_mem, w_ck, w_cv, tm):
    t = mem2.shape[0]
    row = lambda w: pl.BlockSpec((tm, w), lambda i: (i, 0))
    return pl.pallas_call(
        _mem_kv_body,
        grid=(t // tm,),
        in_specs=[row(D_MODEL), _full_spec(g_mem.shape), _full_spec(w_ck.shape), _full_spec(w_cv.shape)],
        out_specs=[pl.BlockSpec((tm * CA_HEADS, CA_HEAD_DIM), lambda i: (i, 0))] * 2 + [row(CA_DIM)] * 2,
        out_shape=([jax.ShapeDtypeStruct((t * CA_HEADS, CA_HEAD_DIM), F32)] * 2
                   + [jax.ShapeDtypeStruct((t, CA_DIM), BF16)] * 2),
        compiler_params=_params(("parallel",)),
        name="mem_kv",
    )(mem2, g_mem, w_ck, w_cv)


def _cross_sample_body(q_ref, k_ref, v_ref, o_ref, *, n_seq, dec_seq):
    q = q_ref[...].astype(F32)
    n_keys = N_MEM * CA_HEADS
    col_head = lax.broadcasted_iota(jnp.int32, (1, n_keys), 1) & (CA_HEADS - 1)
    grp = CA_HEADS * dec_seq
    row_head = (lax.broadcasted_iota(jnp.int32, (n_seq * grp, 1), 0) // dec_seq) & (CA_HEADS - 1)
    own = col_head == row_head
    parts = []
    for i in range(n_seq):
        qi = q[i * dec_seq:(i + 1) * dec_seq]
        qs = jnp.concatenate([qi[:, h * CA_HEAD_DIM:(h + 1) * CA_HEAD_DIM] for h in range(CA_HEADS)], axis=0)
        parts.append(_dot_nt(qs.astype(BF16), k_ref[i].astype(BF16)))
    s = jnp.where(own, jnp.concatenate(parts, axis=0) * (CA_HEAD_DIM ** -0.5), NEG)
    m = jnp.max(s, axis=-1, keepdims=True)
    p = jnp.exp(s - m)
    rdenom = 1.0 / jnp.sum(p, axis=-1, keepdims=True)
    p = p.astype(BF16)
    rows = []
    for i in range(n_seq):
        gr = slice(i * grp, (i + 1) * grp)
        o = _dot(p[gr], v_ref[i].astype(BF16)) * rdenom[gr]
        rows.append(jnp.concatenate([o[h * dec_seq:(h + 1) * dec_seq] for h in range(CA_HEADS)], axis=-1))
    o_ref[...] = jnp.concatenate(rows, axis=0).astype(BF16)


def _cross_sample(qc, ck, cv, n_seq_total, dec_seq, n_seq):
    rows = n_seq * dec_seq
    tok = pl.BlockSpec((rows, CA_DIM), lambda i: (i, 0))
    mem = pl.BlockSpec((n_seq, N_MEM * CA_HEADS, CA_HEAD_DIM), lambda i: (i, 0, 0))
    return pl.pallas_call(
        functools.partial(_cross_sample_body, n_seq=n_seq, dec_seq=dec_seq),
        grid=(n_seq_total // n_seq,),
        in_specs=[tok, mem, mem],
        out_specs=tok,
        out_shape=jax.ShapeDtypeStruct((n_seq_total * dec_seq, CA_DIM), BF16),
        compiler_params=_params(("parallel",)),
        name="cross_sample",
    )(qc, ck, cv)


def _post2_body(h_ref, o_ref, wco_ref, gf_ref, wg_ref, wu_ref, wd_ref, gfin_ref, y_ref):
    h = h_ref[...] + _dot(o_ref[...], wco_ref[...])
    hn = _rms(h, gf_ref[...]).astype(BF16)
    acc = h
    for lo, hi in FF_SPLITS:
        act = _silu(_dot(hn, wg_ref[:, lo:hi])) * _dot(hn, wu_ref[:, lo:hi])
        acc = acc + _dot(act.astype(BF16), wd_ref[lo:hi, :])
    y_ref[...] = _rms(acc, gfin_ref[...])


def _post2(h1, o, w_co, g_ffn, w_gate, w_up, w_down, g_final, tm):
    t = h1.shape[0]
    row = lambda w: pl.BlockSpec((tm, w), lambda i: (i, 0))
    return pl.pallas_call(
        _post2_body,
        grid=(t // tm,),
        in_specs=[row(D_MODEL), row(CA_DIM), _full_spec(w_co.shape), _full_spec(g_ffn.shape),
                  _full_spec(w_gate.shape), _full_spec(w_up.shape), _full_spec(w_down.shape),
                  _full_spec(g_final.shape)],
        out_specs=row(D_MODEL),
        out_shape=jax.ShapeDtypeStruct((t, D_MODEL), F32),
        compiler_params=_params(("parallel",)),
        name="ffn",
    )(h1, o, w_co, g_ffn, w_gate, w_up, w_down, g_final)


def _prompt_consts():
    qi = np.arange(WINDOW)[:, None]
    ji = np.arange(2 * WINDOW)[None, :]
    dist = qi + WINDOW - ji
    inband = (dist >= 0) & (dist < WINDOW)
    bucket = np.where(inband, _t5_bucket_np(dist), -1)
    first = np.where(ji >= WINDOW, bucket, -1)
    buckets = np.stack([first, bucket]).astype(np.int32)
    return (jnp.asarray(_tril_np(CHUNK), BF16), jnp.asarray(buckets))


def _sample_consts(dec_seq, cache_len):
    rows = SAMPLE_SEQS * dec_seq
    r = np.arange(rows)
    same = (r[:, None] // dec_seq) == (r[None, :] // dec_seq)
    tcum = (same & (r[None, :] <= r[:, None])).astype(np.float32)
    tseq = same.astype(np.float32)
    t = np.arange(dec_seq)[:, None]
    j = np.arange(cache_len)[None, :]
    dist_c = t + cache_len - j
    bkt_c = np.where((dist_c >= 0) & (dist_c < WINDOW), _t5_bucket_np(dist_c), -1).astype(np.int32)
    dist_n = (r[:, None] % dec_seq) - (r[None, :] % dec_seq)
    ok = same & (dist_n >= 0) & (dist_n < WINDOW)
    bkt_n = np.where(ok, _t5_bucket_np(dist_n), -1).astype(np.int32)
    return (jnp.asarray(tcum, BF16), jnp.asarray(tseq, BF16), jnp.asarray(_expand_np(), BF16),
            jnp.asarray(bkt_c), jnp.asarray(bkt_n))


def _pick_tile(t, pref):
    tm = min(t, pref)
    while t % tm:
        tm //= 2
    return tm


def kernel(x_prompt, x_sample, mem_prompt, state_conv, state_ssm, cache_swa_k, cache_swa_v, cache_mem_k, cache_mem_v, rel_bias, g_mix, w_in, conv_w, conv_b, dt_bias, a_log, d_skip, g_ssd_norm, sinks, w_out, g_cross, g_mem, w_cq, w_ck, w_cv, w_co, g_ffn, w_gate, w_up, w_down, g_final):
    assert g_mix.shape[0] == 1, "single-layer trunk"
    batch, seq, _ = x_prompt.shape
    n_dec, dec_seq, _ = x_sample.shape
    cache_len = cache_swa_k.shape[2]
    assert seq % CHUNK == 0 and cache_len == WINDOW and n_dec % SAMPLE_SEQS == 0 and dec_seq == 8

    wi = w_in[0]
    s1 = SSD_D_INNER + CONV_DIM
    s2 = s1 + SSD_HEADS
    w_in_r = jnp.concatenate(
        [wi[:, :s1], wi[:, s2:], jnp.pad(wi[:, s1:s2], ((0, 0), (0, DT_PAD - SSD_HEADS)))], axis=1).astype(BF16)
    row = lambda a: a.reshape(1, -1).astype(F32)
    pad_h = lambda a: jnp.pad(a.reshape(1, -1).astype(F32), ((0, 0), (0, DT_PAD - SSD_HEADS)))
    small = (conv_w[0].astype(F32), row(conv_b[0]), pad_h(dt_bias[0]), pad_h(a_log[0]),
             jnp.repeat(d_skip[0].astype(F32), SSD_HEAD_DIM).reshape(1, -1), row(g_ssd_norm[0]),
             rel_bias.astype(F32), sinks[0].astype(F32))
    bf = lambda w: w[0].astype(BF16)
    w_out_b, w_cq_b, w_ck_b, w_cv_b, w_co_b = bf(w_out), bf(w_cq), bf(w_ck), bf(w_cv), bf(w_co)
    w_gate_b, w_up_b, w_down_b = bf(w_gate), bf(w_up), bf(w_down)
    g_mix_r, g_cross_r, g_mem_r, g_ffn_r, g_fin_r = row(g_mix[0]), row(g_cross[0]), row(g_mem[0]), row(g_ffn[0]), row(g_final)

    n_chunks = 2 if seq % (2 * CHUNK) == 0 else 1
    mem2 = mem_prompt.reshape(batch * N_MEM, D_MODEL)
    mk, mv, mk_b, mv_b = _mem_kv(mem2, g_mem_r, w_ck_b, w_cv_b, _pick_tile(batch * N_MEM, 512))
    dense = (g_mix_r, w_in_r, w_out_b, g_cross_r, w_cq_b, w_co_b, g_ffn_r, w_gate_b, w_up_b, w_down_b, g_fin_r)
    y_prompt, p_conv, p_ssm, p_k, p_v = _prompt_layer(
        x_prompt, mk_b.reshape(batch, N_MEM, CA_DIM), mv_b.reshape(batch, N_MEM, CA_DIM),
        dense, small, _prompt_consts(), n_chunks)

    def channel_major(cache):
        return jnp.transpose(cache, (0, 2, 3, 1)).reshape(n_dec, KV_DIM, cache_len)

    def position_major(cache_t):
        return jnp.transpose(cache_t.reshape(n_dec, N_KV_HEADS, ATTN_HEAD_DIM, cache_len), (0, 3, 1, 2))

    ts = n_dec * dec_seq
    xs2 = x_sample.reshape(ts, D_MODEL)
    tm_s = _pick_tile(ts, 512)
    z, xbc, q, k, v, dt = _in_proj(xs2, g_mix_r, w_in_r, tm_s)
    mix_s, s_conv, s_ssm, s_k, s_v = _sample_mixer(
        z, xbc, q, k, v, dt, jnp.transpose(state_conv[0], (1, 0, 2)),
        state_ssm[0].reshape(n_dec, SSD_D_INNER, D_STATE),
        channel_major(cache_swa_k[0]), channel_major(cache_swa_v[0]),
        small, _sample_consts(dec_seq, cache_len), n_dec, dec_seq)
    h1s, qcs = _post1(xs2, mix_s, w_out_b, g_cross_r, w_cq_b, tm_s)
    os_ = _cross_sample(qcs, cache_mem_k[0].reshape(n_dec, N_MEM * CA_HEADS, CA_HEAD_DIM),
                        cache_mem_v[0].reshape(n_dec, N_MEM * CA_HEADS, CA_HEAD_DIM), n_dec, dec_seq, SAMPLE_SEQS)
    y_sample = _post2(h1s, os_, w_co_b, g_ffn_r, w_gate_b, w_up_b, w_down_b, g_fin_r, tm_s)

    return (y_prompt.reshape(batch, seq, D_MODEL), y_sample.reshape(n_dec, dec_seq, D_MODEL),
            p_conv[None], p_ssm.reshape(1, batch, SSD_HEADS, SSD_HEAD_DIM, D_STATE),
            p_k.reshape(1, batch, WINDOW, N_KV_HEADS, ATTN_HEAD_DIM),
            p_v.reshape(1, batch, WINDOW, N_KV_HEADS, ATTN_HEAD_DIM),
            mk.reshape(1, batch, N_MEM, CA_HEADS, CA_HEAD_DIM), mv.reshape(1, batch, N_MEM, CA_HEADS, CA_HEAD_DIM),
            jnp.transpose(s_conv, (1, 0, 2))[None], s_ssm.reshape(1, n_dec, SSD_HEADS, SSD_HEAD_DIM, D_STATE),
            position_major(s_k)[None], position_major(s_v)[None])
```

```python
import functools
import math

import numpy as np
import jax
import jax.numpy as jnp
from jax import lax
from jax.experimental import pallas as pl
from jax.experimental.pallas import tpu as pltpu

F32 = jnp.float32
BF16 = jnp.bfloat16

D_MODEL = 1024
SSD_D_INNER = 512
SSD_HEAD_DIM = 64
SSD_HEADS = 8
SSD_GROUPS = 2
GROUP_W = SSD_D_INNER // SSD_GROUPS
D_STATE = 128
CONV_K = 4
CONV_DIM = SSD_D_INNER + 2 * SSD_GROUPS * D_STATE
CHUNK = 128
ATTN_DIM = 512
ATTN_HEAD_DIM = 64
N_HEADS = 8
N_KV_HEADS = 2
Q_PER_KV = N_HEADS // N_KV_HEADS
KV_DIM = N_KV_HEADS * ATTN_HEAD_DIM
WINDOW = 128
N_BUCKETS = 32
MAX_EXACT = N_BUCKETS // 2
MAX_DISTANCE = 128
N_MEM = 256
CA_HEADS = 4
CA_HEAD_DIM = 128
CA_DIM = CA_HEADS * CA_HEAD_DIM
D_FF = 2816
EPS = 1e-6

LANES = 128
HALF = LANES // 2
DT_PAD = LANES
COL_Z = 0
COL_XBC = COL_Z + SSD_D_INNER
COL_Q = COL_XBC + CONV_DIM
COL_K = COL_Q + ATTN_DIM
COL_V = COL_K + KV_DIM
COL_DT = COL_V + KV_DIM
IN_COLS = COL_DT + DT_PAD
NEG = -1e30
SAMPLE_SEQS = 16
VMEM_LIMIT = 56 * 1024 * 1024
FF_SPLITS = ((0, 1024), (1024, 2048), (2048, D_FF))
SECOND_STREAM_LEAD = 0.5
FF_PIECES = tuple((lo, min(lo + 512, D_FF)) for lo in range(0, D_FF, 512))
PROJ_PIECES = ((COL_Z, COL_XBC), (COL_XBC, COL_XBC + 512), (COL_XBC + 512, COL_Q), (COL_Q, COL_K), (COL_K, IN_COLS))


def _rms(x, g):
    return x * lax.rsqrt(jnp.mean(x * x, axis=-1, keepdims=True) + EPS) * g


def _silu(x):
    return x * jax.nn.sigmoid(x)


def _softplus(x):
    return jnp.maximum(x, 0.0) + jnp.log1p(jnp.exp(-jnp.abs(x)))


def _dot(a, b):
    return jnp.dot(a, b, preferred_element_type=F32)


def _dot_nt(a, b):
    return lax.dot_general(a, b, (((1,), (1,)), ((), ())), preferred_element_type=F32)


def _split3(a):
    hi = a.astype(BF16)
    r = a - hi.astype(F32)
    mid = r.astype(BF16)
    lo = (r - mid.astype(F32)).astype(BF16)
    return hi, mid, lo


def _sel_left(t01, a):
    hi, mid, lo = _split3(a)
    return _dot(t01, hi) + _dot(t01, mid) + _dot(t01, lo)


def _sel_right(a, e01):
    hi, mid, lo = _split3(a)
    return _dot(hi, e01) + _dot(mid, e01) + _dot(lo, e01)


def _lane_lo():
    return lax.broadcasted_iota(jnp.int32, (1, LANES), 1) < HALF


def _t5_bucket_np(dist):
    n = np.maximum(dist, 0)
    ratio = np.log(np.maximum(n, 1).astype(np.float32) / np.float32(MAX_EXACT))
    large = MAX_EXACT + (ratio / np.float32(math.log(MAX_DISTANCE / MAX_EXACT))
                         * np.float32(N_BUCKETS - MAX_EXACT)).astype(np.int32)
    large = np.minimum(large, N_BUCKETS - 1)
    return np.where(n < MAX_EXACT, n, large).astype(np.int32)


def _tril_np(n):
    return np.tril(np.ones((n, n), np.float32))


def _expand_np():
    e = np.zeros((LANES, SSD_D_INNER), np.float32)
    for h in range(SSD_HEADS):
        e[h, h * SSD_HEAD_DIM:(h + 1) * SSD_HEAD_DIM] = 1.0
    return e


def _full_spec(shape):
    nd = len(shape)
    return pl.BlockSpec(shape, lambda *_: (0,) * nd, pipeline_mode=pl.Buffered(1))


def _smem_spec():
    return pl.BlockSpec(memory_space=pltpu.SMEM)


def _params(sem):
    return pltpu.CompilerParams(dimension_semantics=sem, vmem_limit_bytes=VMEM_LIMIT)


def _in_proj_body(x_ref, g_ref, w_ref, z_ref, xbc_ref, q_ref, k_ref, v_ref, dt_ref):
    xn = _rms(x_ref[...], g_ref[...]).astype(BF16)

    def seg(lo, hi):
        return _dot(xn, w_ref[:, lo:hi])

    z_ref[...] = seg(COL_Z, COL_XBC)
    xbc_ref[...] = seg(COL_XBC, COL_Q)
    q_ref[...] = (seg(COL_Q, COL_K) * (ATTN_HEAD_DIM ** -0.5)).astype(BF16)
    k_ref[...] = seg(COL_K, COL_V)
    v_ref[...] = seg(COL_V, COL_DT)
    dt_ref[...] = seg(COL_DT, IN_COLS)


def _in_proj(x2, g_mix, w_in_r, tm):
    t = x2.shape[0]
    row = lambda w: pl.BlockSpec((tm, w), lambda i: (i, 0))
    outs = [(SSD_D_INNER, F32), (CONV_DIM, F32), (ATTN_DIM, BF16), (KV_DIM, F32), (KV_DIM, F32), (DT_PAD, F32)]
    return pl.pallas_call(
        _in_proj_body,
        grid=(t // tm,),
        in_specs=[row(D_MODEL), _full_spec((1, D_MODEL)), _full_spec((D_MODEL, IN_COLS))],
        out_specs=[row(w) for w, _ in outs],
        out_shape=[jax.ShapeDtypeStruct((t, w), d) for w, d in outs],
        compiler_params=_params(("parallel",)),
        name="in_proj",
    )(x2, g_mix, w_in_r)


def _conv_taps(cw_ref, cb_ref, taps):
    acc = cb_ref[...] + taps[0] * cw_ref[0:1, :]
    for k in range(1, CONV_K):
        acc = acc + taps[k] * cw_ref[k:k + 1, :]
    return _silu(acc)


def _ssd_prepare(conv, dt_raw, dtb_ref, alog_ref, tcum, total_fn, extra_fn, expand):
    xs = conv[:, :SSD_D_INNER]
    bm = conv[:, SSD_D_INNER:SSD_D_INNER + SSD_GROUPS * D_STATE]
    cm = conv[:, SSD_D_INNER + SSD_GROUPS * D_STATE:]
    dt = _softplus(dt_raw + dtb_ref[...])
    a = dt * (-jnp.exp(alog_ref[...]))
    cs = _sel_left(tcum, a)
    total = total_fn(cs, a)
    pieces = [dt, jnp.exp(total - cs), jnp.exp(cs)] + extra_fn(total)
    rows = cs.shape[0]
    ex = _sel_right(jnp.concatenate(pieces, axis=0), expand)
    ex = [ex[i * rows:(i + 1) * rows] for i in range(len(pieces))]
    return xs, bm, cm, cs, cs.T, ex


def _ssd_diag(cs, cs_t, cb_g, xdt, mask, g):
    lo = _lane_lo()
    out = []
    for pr in range(2):
        h0 = g * 4 + 2 * pr
        xp = xdt[:, (h0 // 2) * LANES:(h0 // 2 + 1) * LANES]
        x_lo = jnp.where(lo, xp, 0.0).astype(BF16)
        x_hi = jnp.where(lo, 0.0, xp).astype(BF16)
        acc = None
        for h, xh in ((h0, x_lo), (h0 + 1, x_hi)):
            diff = cs[:, h:h + 1] - cs_t[h:h + 1, :]
            decay = jnp.exp(jnp.where(mask, diff, -jnp.inf))
            part = _dot((cb_g * decay).astype(BF16), xh)
            acc = part if acc is None else acc + part
        out.append(acc)
    return jnp.concatenate(out, axis=-1)


def _gated_norm(y, z, gn_ref):
    yf = y * _silu(z)
    parts = []
    for g in range(SSD_GROUPS):
        yg = yf[:, g * GROUP_W:(g + 1) * GROUP_W]
        parts.append(yg * lax.rsqrt(jnp.mean(yg * yg, axis=-1, keepdims=True) + EPS))
    return jnp.concatenate(parts, axis=-1) * gn_ref[...]


def _build_bias(bias_ref, bucket_of, relb_ref, n_tables):
    for i in range(n_tables):
        for h in range(N_HEADS):
            bias_ref[i, h] = jnp.full(bias_ref.shape[2:], NEG, F32)

    def body(t, carry):
        for i in range(n_tables):
            hit = bucket_of(i) == t
            for h in range(N_HEADS):
                bias_ref[i, h] = jnp.where(hit, relb_ref[t, h], bias_ref[i, h])
        return carry

    lax.fori_loop(0, N_BUCKETS, body, 0)


def _alternate(first, second):
    streams = [[0.0, 1.0, first], [0.0, SECOND_STREAM_LEAD, second]]
    while streams:
        entry = min(streams, key=lambda e: e[0])
        try:
            entry[0] += entry[1] * next(entry[2])
        except StopIteration:
            streams.remove(entry)


def _layer_step_body(x_ref, xprev_ref, mk_ref, mv_ref, gmix_ref, win_ref, wout_ref, gc_ref, wcq_ref,
                     wco_ref, gf_ref, wg_ref, wu_ref, wd_ref, gfin_ref,
                     cw_ref, cb_ref, dtb_ref, alog_ref, dskip_ref, gn_ref,
                     tril_ref, bucket_ref, relb_ref, sink_ref,
                     y_ref, conv_out, ssm_out, k_out, v_out,
                     xpad, state_t, kbuf, vbuf, bias, mix, proj, *, n_chunks, tiles_per_seq):
    step = pl.program_id(0)
    n_tiles = pl.num_programs(0) - 1
    c = lax.rem(jnp.minimum(step, n_tiles - 1), tiles_per_seq)
    last = tiles_per_seq - 1
    tq = n_chunks * CHUNK

    @pl.when(step == 0)
    def _():
        _build_bias(bias, lambda i: bucket_ref[i], relb_ref, 2)
        mix[...] = jnp.zeros_like(mix)

    @pl.when(c == 0)
    def _():
        xpad[0:8, :] = jnp.zeros((8, CONV_DIM), F32)
        state_t[...] = jnp.zeros_like(state_t)
        kbuf[:, 0:CHUNK, :] = jnp.zeros((2, CHUNK, KV_DIM), BF16)
        vbuf[:, 0:CHUNK, :] = jnp.zeros((2, CHUNK, KV_DIM), BF16)

    def back():
        h1 = xprev_ref[0] + _dot(mix[...], wout_ref[...])
        yield 4
        qc = _dot(_rms(h1, gc_ref[...]).astype(BF16), wcq_ref[...]).astype(BF16)
        yield 2
        heads = []
        for h in range(CA_HEADS):
            hs = slice(h * CA_HEAD_DIM, (h + 1) * CA_HEAD_DIM)
            sc = _dot_nt(qc[:, hs], mk_ref[0, :, hs]) * (CA_HEAD_DIM ** -0.5)
            m = jnp.max(sc, axis=-1, keepdims=True)
            p = jnp.exp(sc - m)
            heads.append(_dot(p.astype(BF16), mv_ref[0, :, hs]) / jnp.sum(p, axis=-1, keepdims=True))
            yield 3
        o = jnp.concatenate(heads, axis=-1).astype(BF16)
        h2 = h1 + _dot(o, wco_ref[...])
        hn = _rms(h2, gf_ref[...]).astype(BF16)
        yield 4
        acc = h2
        for f_lo, f_hi in FF_PIECES:
            width = (f_hi - f_lo) / 512
            gate = _dot(hn, wg_ref[:, f_lo:f_hi])
            yield 5 * width
            act = (_silu(gate) * _dot(hn, wu_ref[:, f_lo:f_hi])).astype(BF16)
            yield 6 * width
            acc = acc + _dot(act, wd_ref[f_lo:f_hi, :])
            yield 5 * width
        y_ref[0] = _rms(acc, gfin_ref[...])

    def front():
        x = x_ref[0]
        xn = _rms(x, gmix_ref[...]).astype(BF16)
        for p_lo, p_hi in PROJ_PIECES:
            if COL_XBC <= p_lo and p_hi <= COL_Q:
                xpad[8:8 + tq, p_lo - COL_XBC:p_hi - COL_XBC] = _dot(xn, win_ref[:, p_lo:p_hi])
            else:
                proj[:, p_lo:p_hi] = _dot(xn, win_ref[:, p_lo:p_hi])
            yield 4

        conv_halves = []
        half_w = CONV_DIM // 2
        for hf in range(2):
            cs_ = slice(hf * half_w, (hf + 1) * half_w)
            acc = cb_ref[:, cs_] + xpad[5:5 + tq, cs_] * cw_ref[0:1, cs_]
            for kk in range(1, CONV_K):
                acc = acc + xpad[5 + kk:5 + kk + tq, cs_] * cw_ref[kk:kk + 1, cs_]
            conv_halves.append(_silu(acc))
            yield 6
        xpad[5:8, :] = xpad[5 + tq:8 + tq, :]
        xs_all = conv_halves[0]
        bc_all = conv_halves[1]

        k_new = proj[:, COL_K:COL_V]
        v_new = proj[:, COL_V:COL_DT]
        kbuf[0, CHUNK:, :] = k_new.astype(BF16)
        kbuf[1, CHUNK:, :] = pltpu.roll(k_new, HALF, 1).astype(BF16)
        vbuf[0, CHUNK:, :] = v_new.astype(BF16)
        vbuf[1, CHUNK:, :] = pltpu.roll(v_new, HALF, 1).astype(BF16)
        q = (proj[:, COL_Q:COL_K] * (ATTN_HEAD_DIM ** -0.5)).astype(BF16)
        yield 2

        row = lax.broadcasted_iota(jnp.int32, (CHUNK, CHUNK), 0)
        col = lax.broadcasted_iota(jnp.int32, (CHUNK, CHUNK), 1)
        causal = col <= row
        lo = _lane_lo()
        for ci in range(n_chunks):
            rs = slice(ci * CHUNK, (ci + 1) * CHUNK)
            xs = xs_all[rs]
            bm = bc_all[rs, 0:SSD_GROUPS * D_STATE]
            cm = bc_all[rs, SSD_GROUPS * D_STATE:]
            dt = _softplus(proj[rs, COL_DT:IN_COLS] + dtb_ref[...])
            a = dt * (-jnp.exp(alog_ref[...]))
            cs = _sel_left(tril_ref[...], a)
            cs_t = cs.T
            dt_t = dt.T
            total = cs[CHUNK - 1:CHUNK, :]
            ecs = jnp.exp(cs)
            w_end = dt * jnp.exp(total - cs)
            bb = bm.astype(BF16)
            cb = cm.astype(BF16)
            yield 5
            y_parts = []
            for g in range(SSD_GROUPS):
                gs = slice(g * GROUP_W, (g + 1) * GROUP_W)
                ns = slice(g * D_STATE, (g + 1) * D_STATE)
                cb_g = _dot_nt(cb[:, ns], bb[:, ns])
                y_diag, xw, ecs_g = [], [], []
                for pr in range(2):
                    h0 = g * 4 + 2 * pr
                    xp = xs[:, (h0 // 2) * LANES:(h0 // 2 + 1) * LANES]
                    ecs_g.append(jnp.where(lo, ecs[:, h0:h0 + 1], ecs[:, h0 + 1:h0 + 2]))
                    xw.append((xp * jnp.where(lo, w_end[:, h0:h0 + 1], w_end[:, h0 + 1:h0 + 2])).astype(BF16))
                    x2 = jnp.concatenate([jnp.where(lo, xp, 0.0), jnp.where(lo, 0.0, xp)], axis=0).astype(BF16)
                    gmat = []
                    for h in (h0, h0 + 1):
                        diff = cs[:, h:h + 1] - cs_t[h:h + 1, :]
                        decay = jnp.exp(jnp.where(causal, diff, -jnp.inf))
                        gmat.append((cb_g * decay * dt_t[h:h + 1, :]).astype(BF16))
                    y_diag.append(_dot(jnp.concatenate(gmat, axis=1), x2))
                ecs_x = jnp.concatenate(ecs_g, axis=-1)
                st_old = state_t[:, gs]
                y_off = _dot(cb[:, ns], st_old.astype(BF16)) * ecs_x
                y_parts.append(jnp.concatenate(y_diag, axis=-1) + y_off)
                b_t = bm[:, ns].T.astype(BF16)
                state_t[:, gs] = ecs_x[CHUNK - 1:CHUNK, :] * st_old + _dot(b_t, jnp.concatenate(xw, axis=-1))
                yield 9
            y = jnp.concatenate(y_parts, axis=-1) + dskip_ref[...] * xs
            mix[rs, 0:SSD_D_INNER] = _gated_norm(y, proj[rs, COL_Z:COL_XBC], gn_ref).astype(BF16)
            yield 3

            keys = slice(ci * CHUNK, (ci + 2) * CHUNK)
            table = jnp.minimum(c, 1) if ci == 0 else 1
            probs, denoms = {}, {}
            for variant in range(2):
                hv = [h for h in range(N_HEADS) if (h // Q_PER_KV + h % 2) % 2 == variant]
                qms = []
                for h in hv:
                    q_pair = q[rs, (h // 2) * LANES:(h // 2 + 1) * LANES]
                    qms.append(jnp.where(lo, q_pair, 0.0) if h % 2 == 0 else jnp.where(lo, 0.0, q_pair))
                sc_all = _dot_nt(jnp.concatenate(qms, axis=0).astype(BF16), kbuf[variant, keys, :])
                for i, h in enumerate(hv):
                    sc = sc_all[i * CHUNK:(i + 1) * CHUNK] + bias[table, h]
                    sink = sink_ref[h]
                    m = jnp.maximum(jnp.max(sc, axis=-1, keepdims=True), sink)
                    p = jnp.exp(sc - m)
                    denoms[h] = jnp.sum(p, axis=-1, keepdims=True) + jnp.exp(sink - m)
                    probs[h] = p.astype(BF16)
                yield 10
            if ci == n_chunks - 1:
                yield 1000
            parts = {}
            for variant in range(2):
                vv = vbuf[variant, keys, :]
                for par in range(2):
                    hv = [h for h in range(N_HEADS) if (h // Q_PER_KV + h % 2) % 2 == variant and h % 2 == par]
                    vm = jnp.where(lo, vv, 0.0) if par == 0 else jnp.where(lo, 0.0, vv)
                    o_all = _dot(jnp.concatenate([probs[h] for h in hv], axis=0), vm.astype(BF16))
                    for i, h in enumerate(hv):
                        parts[h] = o_all[i * CHUNK:(i + 1) * CHUNK] / denoms[h]
            for pair in range(N_HEADS // 2):
                mix[rs, SSD_D_INNER + pair * LANES:SSD_D_INNER + (pair + 1) * LANES] = (
                    parts[2 * pair] + parts[2 * pair + 1]).astype(BF16)
            yield 4
        kbuf[:, 0:CHUNK, :] = kbuf[:, tq:tq + CHUNK, :]
        vbuf[:, 0:CHUNK, :] = vbuf[:, tq:tq + CHUNK, :]

    _alternate(back(), front())

    @pl.when((c == last) & (step < n_tiles))
    def _():
        conv_out[0] = xpad[5:8, :]
        ssm_out[0] = state_t[...].T
        k_out[0] = proj[tq - WINDOW:tq, COL_K:COL_V]
        v_out[0] = proj[tq - WINDOW:tq, COL_V:COL_DT]


def _prompt_layer(x3, mk_b, mv_b, dense, small, consts, n_chunks):
    cw, cb, dtb, alog, dskip_x, gn, relb, sinks = small
    tril, buckets = consts
    batch, seq, _ = x3.shape
    tq = n_chunks * CHUNK
    tps = seq // tq
    n_tiles = batch * tps
    front = lambda s: jnp.minimum(s, n_tiles - 1)
    back = lambda s: jnp.maximum(s - 1, 0)
    x_spec = pl.BlockSpec((1, tq, D_MODEL), lambda s: (front(s) // tps, front(s) % tps, 0))
    xprev_spec = pl.BlockSpec((1, tq, D_MODEL), lambda s: (back(s) // tps, back(s) % tps, 0))
    y_spec = pl.BlockSpec((1, tq, D_MODEL), lambda s: (back(s) // tps, back(s) % tps, 0))
    mem_spec = pl.BlockSpec((1, N_MEM, CA_DIM), lambda s: (back(s) // tps, 0, 0))
    per_b = lambda shp: pl.BlockSpec((1,) + shp, lambda s: (front(s) // tps,) + (0,) * len(shp))
    out_shape = [
        jax.ShapeDtypeStruct((batch, seq, D_MODEL), F32),
        jax.ShapeDtypeStruct((batch, CONV_K - 1, CONV_DIM), F32),
        jax.ShapeDtypeStruct((batch, SSD_D_INNER, D_STATE), F32),
        jax.ShapeDtypeStruct((batch, WINDOW, KV_DIM), F32),
        jax.ShapeDtypeStruct((batch, WINDOW, KV_DIM), F32),
    ]
    full = list(dense) + [cw, cb, dtb, alog, dskip_x, gn, tril, buckets]
    return pl.pallas_call(
        functools.partial(_layer_step_body, n_chunks=n_chunks, tiles_per_seq=tps),
        grid=(n_tiles + 1,),
        in_specs=([x_spec, xprev_spec, mem_spec, mem_spec] + [_full_spec(a.shape) for a in full]
                  + [_smem_spec(), _smem_spec()]),
        out_specs=[y_spec, per_b((CONV_K - 1, CONV_DIM)), per_b((SSD_D_INNER, D_STATE)),
                   per_b((WINDOW, KV_DIM)), per_b((WINDOW, KV_DIM))],
        out_shape=out_shape,
        scratch_shapes=[
            pltpu.VMEM((8 + tq, CONV_DIM), F32),
            pltpu.VMEM((D_STATE, SSD_D_INNER), F32),
            pltpu.VMEM((2, CHUNK + tq, KV_DIM), BF16),
            pltpu.VMEM((2, CHUNK + tq, KV_DIM), BF16),
            pltpu.VMEM((2, N_HEADS, WINDOW, 2 * WINDOW), F32),
            pltpu.VMEM((tq, D_MODEL), BF16),
            pltpu.VMEM((tq, IN_COLS), F32),
        ],
        compiler_params=_params(("arbitrary",)),
        name="prompt_layer",
    )(x3, x3, mk_b, mv_b, *full, relb, sinks)


def _sample_mixer_body(z_ref, xbc_ref, q_ref, k_ref, v_ref, dt_ref,
                       sconv_ref, sssm_ref, ck_ref, cv_ref,
                       cw_ref, cb_ref, dtb_ref, alog_ref, dskip_ref, gn_ref,
                       tcum_ref, tseq_ref, expand_ref, bkt_c_ref, bkt_n_ref, relb_ref, sink_ref,
                       mix_ref, conv_out, ssm_out, k_out, v_out,
                       xpad, bias_c, bias_n, *, dec_seq):
    step = pl.program_id(0)
    n_seq = SAMPLE_SEQS
    rows = n_seq * dec_seq

    @pl.when(step == 0)
    def _():
        _build_bias(bias_c, lambda i: bkt_c_ref[...], relb_ref, 1)
        _build_bias(bias_n, lambda i: bkt_n_ref[...], relb_ref, 1)

    xbc = xbc_ref[...]
    xpad[:, 8:8 + dec_seq, :] = xbc.reshape(n_seq, dec_seq, CONV_DIM)
    for r in range(CONV_K - 1):
        xpad[:, 5 + r, :] = sconv_ref[r]
    taps = [xpad[:, 5 + k:5 + k + dec_seq, :].reshape(rows, CONV_DIM) for k in range(CONV_K - 1)] + [xbc]
    conv = _conv_taps(cw_ref, cb_ref, taps)
    for r in range(CONV_K - 1):
        conv_out[r] = xpad[:, 5 + dec_seq + r, :]

    row = lax.broadcasted_iota(jnp.int32, (rows, rows), 0)
    col = lax.broadcasted_iota(jnp.int32, (rows, rows), 1)
    tseq = tseq_ref[...]
    same_seq = tseq > 0
    causal = same_seq & (col <= row)
    xs, bm, cm, cs, cs_t, (dt_x, dend_x, ecs_x, seqdec_x) = _ssd_prepare(
        conv, dt_ref[...], dtb_ref, alog_ref, tcum_ref[...],
        lambda cs_, a_: _sel_left(tseq, a_), lambda total: [jnp.exp(total)], expand_ref[...])
    xdt = xs * dt_x
    bb = bm.astype(BF16)
    cb = cm.astype(BF16)
    seq_of_row = lax.broadcasted_iota(jnp.int32, (rows, 1), 0) // dec_seq
    seq_of_lane = lax.broadcasted_iota(jnp.int32, (1, rows), 1) // dec_seq
    xde_t = (xdt * dend_x).T
    y_off = [None] * SSD_GROUPS
    for i in range(n_seq):
        st = sssm_ref[i]
        dec = jnp.concatenate(
            [jnp.broadcast_to(seqdec_x[i * dec_seq:i * dec_seq + 1, h * SSD_HEAD_DIM:h * SSD_HEAD_DIM + 1],
                              (SSD_HEAD_DIM, D_STATE)) for h in range(SSD_HEADS)], axis=0)
        new_parts = []
        for g in range(SSD_GROUPS):
            gs = slice(g * GROUP_W, (g + 1) * GROUP_W)
            ns = slice(g * D_STATE, (g + 1) * D_STATE)
            c_i = jnp.where(seq_of_row == i, cm[:, ns], 0.0).astype(BF16)
            part = _dot_nt(c_i, st[gs].astype(BF16))
            y_off[g] = part if y_off[g] is None else y_off[g] + part
            x_i = jnp.where(seq_of_lane == i, xde_t[gs], 0.0).astype(BF16)
            new_parts.append(dec[gs] * st[gs] + _dot(x_i, bb[:, ns]))
        ssm_out[i] = jnp.concatenate(new_parts, axis=0)
    y_parts = []
    for g in range(SSD_GROUPS):
        gs = slice(g * GROUP_W, (g + 1) * GROUP_W)
        ns = slice(g * D_STATE, (g + 1) * D_STATE)
        cb_g = _dot_nt(cb[:, ns], bb[:, ns])
        y_parts.append(_ssd_diag(cs, cs_t, cb_g, xdt, causal, g) + y_off[g] * ecs_x[:, gs])
    y = jnp.concatenate(y_parts, axis=-1) + dskip_ref[...] * xs
    mix_ref[:, 0:SSD_D_INNER] = _gated_norm(y, z_ref[...], gn_ref).astype(BF16)

    lo = _lane_lo()
    k_new = k_ref[...]
    v_new = v_ref[...]
    k_var = [k_new.astype(BF16), pltpu.roll(k_new, HALF, 1).astype(BF16)]
    v_new_r = pltpu.roll(v_new, HALF, 1)
    v_dup = [jnp.where(lo, v_new, v_new_r).astype(BF16), jnp.where(lo, v_new_r, v_new).astype(BF16)]
    qf = q_ref[...].astype(F32)
    q_masked = []
    s_new = []
    for h in range(N_HEADS):
        pair, par = h // 2, h % 2
        j = h // Q_PER_KV
        q_pair = qf[:, pair * LANES:(pair + 1) * LANES]
        qm = jnp.where(lo, q_pair, 0.0) if par == 0 else jnp.where(lo, 0.0, q_pair)
        q_masked.append(qm)
        s_new.append(_dot_nt(qm.astype(BF16), k_var[(j + par) % 2]) + bias_n[0, h])
    stack_rows = lax.broadcasted_iota(jnp.int32, (Q_PER_KV * dec_seq, 1), 0) // dec_seq
    k_new_t = k_new.T
    v_new_t = v_new.T
    keep = WINDOW - dec_seq
    old_lane = lax.broadcasted_iota(jnp.int32, (1, WINDOW), 1) < keep
    grp = Q_PER_KV * dec_seq
    sc_parts, sn_parts, sink_parts = [], [], []
    for i in range(n_seq):
        rs = slice(i * dec_seq, (i + 1) * dec_seq)
        kc_t = ck_ref[i]
        for j in range(N_KV_HEADS):
            heads = range(j * Q_PER_KV, (j + 1) * Q_PER_KV)
            cj = slice(j * ATTN_HEAD_DIM, (j + 1) * ATTN_HEAD_DIM)
            kdup_t = jnp.concatenate([kc_t[cj], kc_t[cj]], axis=0).astype(BF16)
            qs = jnp.concatenate([q_masked[h][rs] for h in heads], axis=0).astype(BF16)
            sc_parts.append(_dot(qs, kdup_t) + jnp.concatenate([bias_c[0, h] for h in heads], axis=0))
            sn_parts.append(jnp.concatenate([s_new[h][rs] for h in heads], axis=0))
            if i == 0:
                sink_j = jnp.zeros((grp, 1), F32)
                for hh, h in enumerate(heads):
                    sink_j = jnp.where(stack_rows == hh, sink_ref[h], sink_j)
                sink_parts.append(sink_j)
    sc = jnp.concatenate(sc_parts, axis=0)
    sn = jnp.concatenate(sn_parts, axis=0)
    sink = jnp.concatenate(sink_parts * n_seq, axis=0)
    m = jnp.maximum(jnp.maximum(jnp.max(sc, axis=-1, keepdims=True), jnp.max(sn, axis=-1, keepdims=True)), sink)
    pc = jnp.exp(sc - m)
    pn = jnp.exp(sn - m)
    rdenom = 1.0 / (jnp.sum(pc, axis=-1, keepdims=True) + jnp.sum(pn, axis=-1, keepdims=True) + jnp.exp(sink - m))
    pc = pc.astype(BF16)
    pn = pn.astype(BF16)
    att_rows = []
    for i in range(n_seq):
        vc_t = cv_ref[i]
        pieces = []
        for j in range(N_KV_HEADS):
            cj = slice(j * ATTN_HEAD_DIM, (j + 1) * ATTN_HEAD_DIM)
            gr = slice((i * N_KV_HEADS + j) * grp, (i * N_KV_HEADS + j + 1) * grp)
            vdup_t = jnp.concatenate([vc_t[cj], vc_t[cj]], axis=0).astype(BF16)
            o = (_dot_nt(pc[gr], vdup_t) + _dot(pn[gr], v_dup[j])) * rdenom[gr]
            for pr in range(Q_PER_KV // 2):
                even = o[(2 * pr) * dec_seq:(2 * pr + 1) * dec_seq]
                odd = o[(2 * pr + 1) * dec_seq:(2 * pr + 2) * dec_seq]
                pieces.append(jnp.where(lo, even, odd))
        att_rows.append(jnp.concatenate(pieces, axis=-1))
        new_shift = (keep - i * dec_seq) % WINDOW
        k_out[i] = jnp.where(old_lane, pltpu.roll(ck_ref[i], keep, 1), pltpu.roll(k_new_t, new_shift, 1))
        v_out[i] = jnp.where(old_lane, pltpu.roll(vc_t, keep, 1), pltpu.roll(v_new_t, new_shift, 1))
    mix_ref[:, SSD_D_INNER:] = jnp.concatenate(att_rows, axis=0).astype(BF16)


def _sample_mixer(z, xbc, q, k, v, dt, sconv, sssm, ck, cv, small, consts, n_seq_total, dec_seq):
    cw, cb, dtb, alog, dskip_x, gn, relb, sinks = small
    tcum, tseq, expand, bkt_c, bkt_n = consts
    rows = SAMPLE_SEQS * dec_seq
    tok = lambda w: pl.BlockSpec((rows, w), lambda i: (i, 0))
    per_s = lambda s: pl.BlockSpec((SAMPLE_SEQS,) + s, lambda i: (i,) + (0,) * len(s))
    out_shape = [
        jax.ShapeDtypeStruct((n_seq_total * dec_seq, D_MODEL), BF16),
        jax.ShapeDtypeStruct((CONV_K - 1, n_seq_total, CONV_DIM), F32),
        jax.ShapeDtypeStruct((n_seq_total, SSD_D_INNER, D_STATE), F32),
        jax.ShapeDtypeStruct((n_seq_total, WINDOW, KV_DIM), F32),
        jax.ShapeDtypeStruct((n_seq_total, WINDOW, KV_DIM), F32),
    ]
    conv_spec = pl.BlockSpec((CONV_K - 1, SAMPLE_SEQS, CONV_DIM), lambda i: (0, i, 0))
    return pl.pallas_call(
        functools.partial(_sample_mixer_body, dec_seq=dec_seq),
        grid=(n_seq_total // SAMPLE_SEQS,),
        in_specs=[tok(SSD_D_INNER), tok(CONV_DIM), tok(ATTN_DIM), tok(KV_DIM), tok(KV_DIM), tok(DT_PAD),
                  conv_spec, per_s((SSD_D_INNER, D_STATE)),
                  per_s((WINDOW, KV_DIM)), per_s((WINDOW, KV_DIM)),
                  _full_spec(cw.shape), _full_spec(cb.shape), _full_spec(dtb.shape), _full_spec(alog.shape),
                  _full_spec(dskip_x.shape), _full_spec(gn.shape),
                  _full_spec(tcum.shape), _full_spec(tseq.shape), _full_spec(expand.shape),
                  _full_spec(bkt_c.shape), _full_spec(bkt_n.shape), _smem_spec(), _smem_spec()],
        out_specs=[tok(D_MODEL), conv_spec, per_s((SSD_D_INNER, D_STATE)),
                   per_s((WINDOW, KV_DIM)), per_s((WINDOW, KV_DIM))],
        out_shape=out_shape,
        scratch_shapes=[
            pltpu.VMEM((SAMPLE_SEQS, 8 + dec_seq, CONV_DIM), F32),
            pltpu.VMEM((1, N_HEADS, dec_seq, WINDOW), F32),
            pltpu.VMEM((1, N_HEADS, rows, rows), F32),
        ],
        compiler_params=_params(("arbitrary",)),
        name="sample_mixer",
    )(z, xbc, q, k, v, dt, sconv, sssm, ck, cv, cw, cb, dtb, alog, dskip_x, gn,
      tcum, tseq, expand, bkt_c, bkt_n, relb, sinks)


def _post1_body(x_ref, mix_ref, wout_ref, gc_ref, wcq_ref, h_ref, qc_ref):
    h = x_ref[...] + _dot(mix_ref[...], wout_ref[...])
    h_ref[...] = h
    qc_ref[...] = _dot(_rms(h, gc_ref[...]).astype(BF16), wcq_ref[...]).astype(BF16)


def _post1(x2, mix, w_out, g_cross, w_cq, tm):
    t = x2.shape[0]
    row = lambda w: pl.BlockSpec((tm, w), lambda i: (i, 0))
    return pl.pallas_call(
        _post1_body,
        grid=(t // tm,),
        in_specs=[row(D_MODEL), row(D_MODEL), _full_spec(w_out.shape), _full_spec(g_cross.shape),
                  _full_spec(w_cq.shape)],
        out_specs=[row(D_MODEL), row(CA_DIM)],
        out_shape=[jax.ShapeDtypeStruct((t, D_MODEL), F32), jax.ShapeDtypeStruct((t, CA_DIM), BF16)],
        compiler_params=_params(("parallel",)),
        name="out_proj",
    )(x2, mix, w_out, g_cross, w_cq)


def _mem_kv_body(mem_ref, g_ref, wk_ref, wv_ref, k_ref, v_ref, kb_ref, vb_ref):
    mn = _rms(mem_ref[...], g_ref[...]).astype(BF16)
    k = _dot(mn, wk_ref[...])
    v = _dot(mn, wv_ref[...])
    rows = k.shape[0]
    for h in range(CA_HEADS):
        hs = slice(h * CA_HEAD_DIM, (h + 1) * CA_HEAD_DIM)
        k_ref[pl.ds(h, rows, stride=CA_HEADS), :] = k[:, hs]
        v_ref[pl.ds(h, rows, stride=CA_HEADS), :] = v[:, hs]
    kb_ref[...] = k.astype(BF16)
    vb_ref[...] = v.astype(BF16)


def _mem_kv(mem2, g_mem, w_ck, w_cv, tm):
    t = mem2.shape[0]
    row = lambda w: pl.BlockSpec((tm, w), lambda i: (i, 0))
    return pl.pallas_call(
        _mem_kv_body,
        grid=(t // tm,),
        in_specs=[row(D_MODEL), _full_spec(g_mem.shape), _full_spec(w_ck.shape), _full_spec(w_cv.shape)],
        out_specs=[pl.BlockSpec((tm * CA_HEADS, CA_HEAD_DIM), lambda i: (i, 0))] * 2 + [row(CA_DIM)] * 2,
        out_shape=([jax.ShapeDtypeStruct((t * CA_HEADS, CA_HEAD_DIM), F32)] * 2
                   + [jax.ShapeDtypeStruct((t, CA_DIM), BF16)] * 2),
        compiler_params=_params(("parallel",)),
        name="mem_kv",
    )(mem2, g_mem, w_ck, w_cv)


def _cross_sample_body(q_ref, k_ref, v_ref, o_ref, *, n_seq, dec_seq):
    q = q_ref[...].astype(F32)
    n_keys = N_MEM * CA_HEADS
    col_head = lax.broadcasted_iota(jnp.int32, (1, n_keys), 1) & (CA_HEADS - 1)
    grp = CA_HEADS * dec_seq
    row_head = (lax.broadcasted_iota(jnp.int32, (n_seq * grp, 1), 0) // dec_seq) & (CA_HEADS - 1)
    own = col_head == row_head
    parts = []
    for i in range(n_seq):
        qi = q[i * dec_seq:(i + 1) * dec_seq]
        qs = jnp.concatenate([qi[:, h * CA_HEAD_DIM:(h + 1) * CA_HEAD_DIM] for h in range(CA_HEADS)], axis=0)
        parts.append(_dot_nt(qs.astype(BF16), k_ref[i].astype(BF16)))
    s = jnp.where(own, jnp.concatenate(parts, axis=0) * (CA_HEAD_DIM ** -0.5), NEG)
    m = jnp.max(s, axis=-1, keepdims=True)
    p = jnp.exp(s - m)
    rdenom = 1.0 / jnp.sum(p, axis=-1, keepdims=True)
    p = p.astype(BF16)
    rows = []
    for i in range(n_seq):
        gr = slice(i * grp, (i + 1) * grp)
        o = _dot(p[gr], v_ref[i].astype(BF16)) * rdenom[gr]
        rows.append(jnp.concatenate([o[h * dec_seq:(h + 1) * dec_seq] for h in range(CA_HEADS)], axis=-1))
    o_ref[...] = jnp.concatenate(rows, axis=0).astype(BF16)


def _cross_sample(qc, ck, cv, n_seq_total, dec_seq, n_seq):
    rows = n_seq * dec_seq
    tok = pl.BlockSpec((rows, CA_DIM), lambda i: (i, 0))
    mem = pl.BlockSpec((n_seq, N_MEM * CA_HEADS, CA_HEAD_DIM), lambda i: (i, 0, 0))
    return pl.pallas_call(
        functools.partial(_cross_sample_body, n_seq=n_seq, dec_seq=dec_seq),
        grid=(n_seq_total // n_seq,),
        in_specs=[tok, mem, mem],
        out_specs=tok,
        out_shape=jax.ShapeDtypeStruct((n_seq_total * dec_seq, CA_DIM), BF16),
        compiler_params=_params(("parallel",)),
        name="cross_sample",
    )(qc, ck, cv)


def _post2_body(h_ref, o_ref, wco_ref, gf_ref, wg_ref, wu_ref, wd_ref, gfin_ref, y_ref):
    h = h_ref[...] + _dot(o_ref[...], wco_ref[...])
    hn = _rms(h, gf_ref[...]).astype(BF16)
    acc = h
    for lo, hi in FF_SPLITS:
        act = _silu(_dot(hn, wg_ref[:, lo:hi])) * _dot(hn, wu_ref[:, lo:hi])
        acc = acc + _dot(act.astype(BF16), wd_ref[lo:hi, :])
    y_ref[...] = _rms(acc, gfin_ref[...])


def _post2(h1, o, w_co, g_ffn, w_gate, w_up, w_down, g_final, tm):
    t = h1.shape[0]
    row = lambda w: pl.BlockSpec((tm, w), lambda i: (i, 0))
    return pl.pallas_call(
        _post2_body,
        grid=(t // tm,),
        in_specs=[row(D_MODEL), row(CA_DIM), _full_spec(w_co.shape), _full_spec(g_ffn.shape),
                  _full_spec(w_gate.shape), _full_spec(w_up.shape), _full_spec(w_down.shape),
                  _full_spec(g_final.shape)],
        out_specs=row(D_MODEL),
        out_shape=jax.ShapeDtypeStruct((t, D_MODEL), F32),
        compiler_params=_params(("parallel",)),
        name="ffn",
    )(h1, o, w_co, g_ffn, w_gate, w_up, w_down, g_final)


def _prompt_consts():
    qi = np.arange(WINDOW)[:, None]
    ji = np.arange(2 * WINDOW)[None, :]
    dist = qi + WINDOW - ji
    inband = (dist >= 0) & (dist < WINDOW)
    bucket = np.where(inband, _t5_bucket_np(dist), -1)
    first = np.where(ji >= WINDOW, bucket, -1)
    buckets = np.stack([first, bucket]).astype(np.int32)
    return (jnp.asarray(_tril_np(CHUNK), BF16), jnp.asarray(buckets))


def _sample_consts(dec_seq, cache_len):
    rows = SAMPLE_SEQS * dec_seq
    r = np.arange(rows)
    same = (r[:, None] // dec_seq) == (r[None, :] // dec_seq)
    tcum = (same & (r[None, :] <= r[:, None])).astype(np.float32)
    tseq = same.astype(np.float32)
    t = np.arange(dec_seq)[:, None]
    j = np.arange(cache_len)[None, :]
    dist_c = t + cache_len - j
    bkt_c = np.where((dist_c >= 0) & (dist_c < WINDOW), _t5_bucket_np(dist_c), -1).astype(np.int32)
    dist_n = (r[:, None] % dec_seq) - (r[None, :] % dec_seq)
    ok = same & (dist_n >= 0) & (dist_n < WINDOW)
    bkt_n = np.where(ok, _t5_bucket_np(dist_n), -1).astype(np.int32)
    return (jnp.asarray(tcum, BF16), jnp.asarray(tseq, BF16), jnp.asarray(_expand_np(), BF16),
            jnp.asarray(bkt_c), jnp.asarray(bkt_n))


def _pick_tile(t, pref):
    tm = min(t, pref)
    while t % tm:
        tm //= 2
    return tm


def kernel(x_prompt, x_sample, mem_prompt, state_conv, state_ssm, cache_swa_k, cache_swa_v, cache_mem_k, cache_mem_v, rel_bias, g_mix, w_in, conv_w, conv_b, dt_bias, a_log, d_skip, g_ssd_norm, sinks, w_out, g_cross, g_mem, w_cq, w_ck, w_cv, w_co, g_ffn, w_gate, w_up, w_down, g_final):
    assert g_mix.shape[0] == 1, "single-layer trunk"
    batch, seq, _ = x_prompt.shape
    n_dec, dec_seq, _ = x_sample.shape
    cache_len = cache_swa_k.shape[2]
    assert seq % CHUNK == 0 and cache_len == WINDOW and n_dec % SAMPLE_SEQS == 0 and dec_seq == 8

    wi = w_in[0]
    s1 = SSD_D_INNER + CONV_DIM
    s2 = s1 + SSD_HEADS
    w_in_r = jnp.concatenate(
        [wi[:, :s1], wi[:, s2:], jnp.pad(wi[:, s1:s2], ((0, 0), (0, DT_PAD - SSD_HEADS)))], axis=1).astype(BF16)
    row = lambda a: a.reshape(1, -1).astype(F32)
    pad_h = lambda a: jnp.pad(a.reshape(1, -1).astype(F32), ((0, 0), (0, DT_PAD - SSD_HEADS)))
    small = (conv_w[0].astype(F32), row(conv_b[0]), pad_h(dt_bias[0]), pad_h(a_log[0]),
             jnp.repeat(d_skip[0].astype(F32), SSD_HEAD_DIM).reshape(1, -1), row(g_ssd_norm[0]),
             rel_bias.astype(F32), sinks[0].astype(F32))
    bf = lambda w: w[0].astype(BF16)
    w_out_b, w_cq_b, w_ck_b, w_cv_b, w_co_b = bf(w_out), bf(w_cq), bf(w_ck), bf(w_cv), bf(w_co)
    w_gate_b, w_up_b, w_down_b = bf(w_gate), bf(w_up), bf(w_down)
    g_mix_r, g_cross_r, g_mem_r, g_ffn_r, g_fin_r = row(g_mix[0]), row(g_cross[0]), row(g_mem[0]), row(g_ffn[0]), row(g_final)

    n_chunks = 2 if seq % (2 * CHUNK) == 0 else 1
    mem2 = mem_prompt.reshape(batch * N_MEM, D_MODEL)
    mk, mv, mk_b, mv_b = _mem_kv(mem2, g_mem_r, w_ck_b, w_cv_b, _pick_tile(batch * N_MEM, 512))
    dense = (g_mix_r, w_in_r, w_out_b, g_cross_r, w_cq_b, w_co_b, g_ffn_r, w_gate_b, w_up_b, w_down_b, g_fin_r)
    y_prompt, p_conv, p_ssm, p_k, p_v = _prompt_layer(
        x_prompt, mk_b.reshape(batch, N_MEM, CA_DIM), mv_b.reshape(batch, N_MEM, CA_DIM),
        dense, small, _prompt_consts(), n_chunks)

    def channel_major(cache):
        return jnp.transpose(cache, (0, 2, 3, 1)).reshape(n_dec, KV_DIM, cache_len)

    def position_major(cache_t):
        return jnp.transpose(cache_t.reshape(n_dec, N_KV_HEADS, ATTN_HEAD_DIM, cache_len), (0, 3, 1, 2))

    ts = n_dec * dec_seq
    xs2 = x_sample.reshape(ts, D_MODEL)
    tm_s = _pick_tile(ts, 512)
    z, xbc, q, k, v, dt = _in_proj(xs2, g_mix_r, w_in_r, tm_s)
    mix_s, s_conv, s_ssm, s_k, s_v = _sample_mixer(
        z, xbc, q, k, v, dt, jnp.transpose(state_conv[0], (1, 0, 2)),
        state_ssm[0].reshape(n_dec, SSD_D_INNER, D_STATE),
        channel_major(cache_swa_k[0]), channel_major(cache_swa_v[0]),
        small, _sample_consts(dec_seq, cache_len), n_dec, dec_seq)
    h1s, qcs = _post1(xs2, mix_s, w_out_b, g_cross_r, w_cq_b, tm_s)
    os_ = _cross_sample(qcs, cache_mem_k[0].reshape(n_dec, N_MEM * CA_HEADS, CA_HEAD_DIM),
                        cache_mem_v[0].reshape(n_dec, N_MEM * CA_HEADS, CA_HEAD_DIM), n_dec, dec_seq, SAMPLE_SEQS)
    y_sample = _post2(h1s, os_, w_co_b, g_ffn_r, w_gate_b, w_up_b, w_down_b, g_fin_r, tm_s)

    return (y_prompt.reshape(batch, seq, D_MODEL), y_sample.reshape(n_dec, dec_seq, D_MODEL),
            p_conv[None], p_ssm.reshape(1, batch, SSD_HEADS, SSD_HEAD_DIM, D_STATE),
            p_k.reshape(1, batch, WINDOW, N_KV_HEADS, ATTN_HEAD_DIM),
            p_v.reshape(1, batch, WINDOW, N_KV_HEADS, ATTN_HEAD_DIM),
            mk.reshape(1, batch, N_MEM, CA_HEADS, CA_HEAD_DIM), mv.reshape(1, batch, N_MEM, CA_HEADS, CA_HEAD_DIM),
            jnp.transpose(s_conv, (1, 0, 2))[None], s_ssm.reshape(1, n_dec, SSD_HEADS, SSD_HEAD_DIM, D_STATE),
            position_major(s_k)[None], position_major(s_v)[None])
```

```python
import functools
import math

import numpy as np
import jax
import jax.numpy as jnp
from jax import lax
from jax.experimental import pallas as pl
from jax.experimental.pallas import tpu as pltpu

F32 = jnp.float32
BF16 = jnp.bfloat16

D_MODEL = 1024
SSD_D_INNER = 512
SSD_HEAD_DIM = 64
SSD_HEADS = 8
SSD_GROUPS = 2
GROUP_W = SSD_D_INNER // SSD_GROUPS
D_STATE = 128
CONV_K = 4
CONV_DIM = SSD_D_INNER + 2 * SSD_GROUPS * D_STATE
CHUNK = 128
ATTN_DIM = 512
ATTN_HEAD_DIM = 64
N_HEADS = 8
N_KV_HEADS = 2
Q_PER_KV = N_HEADS // N_KV_HEADS
KV_DIM = N_KV_HEADS * ATTN_HEAD_DIM
WINDOW = 128
N_BUCKETS = 32
MAX_EXACT = N_BUCKETS // 2
MAX_DISTANCE = 128
N_MEM = 256
CA_HEADS = 4
CA_HEAD_DIM = 128
CA_DIM = CA_HEADS * CA_HEAD_DIM
D_FF = 2816
EPS = 1e-6

LANES = 128
HALF = LANES // 2
DT_PAD = LANES
COL_Z = 0
COL_XBC = COL_Z + SSD_D_INNER
COL_Q = COL_XBC + CONV_DIM
COL_K = COL_Q + ATTN_DIM
COL_V = COL_K + KV_DIM
COL_DT = COL_V + KV_DIM
IN_COLS = COL_DT + DT_PAD
NEG = -1e30
SAMPLE_SEQS = 16
VMEM_LIMIT = 56 * 1024 * 1024
FF_SPLITS = ((0, 1024), (1024, 2048), (2048, D_FF))
SECOND_STREAM_LEAD = 0.5
FF_PIECES = tuple((lo, min(lo + 512, D_FF)) for lo in range(0, D_FF, 512))
PROJ_PIECES = ((COL_Z, COL_XBC), (COL_XBC, COL_XBC + 512), (COL_XBC + 512, COL_Q), (COL_Q, COL_K), (COL_K, IN_COLS))


def _rms(x, g):
    return x * lax.rsqrt(jnp.mean(x * x, axis=-1, keepdims=True) + EPS) * g


def _silu(x):
    return x * jax.nn.sigmoid(x)


def _softplus(x):
    return jnp.maximum(x, 0.0) + jnp.log1p(jnp.exp(-jnp.abs(x)))


def _dot(a, b):
    return jnp.dot(a, b, preferred_element_type=F32)


def _dot_nt(a, b):
    return lax.dot_general(a, b, (((1,), (1,)), ((), ())), preferred_element_type=F32)


def _split3(a):
    hi = a.astype(BF16)
    r = a - hi.astype(F32)
    mid = r.astype(BF16)
    lo = (r - mid.astype(F32)).astype(BF16)
    return hi, mid, lo


def _sel_left(t01, a):
    hi, mid, lo = _split3(a)
    return _dot(t01, hi) + _dot(t01, mid) + _dot(t01, lo)


def _sel_right(a, e01):
    hi, mid, lo = _split3(a)
    return _dot(hi, e01) + _dot(mid, e01) + _dot(lo, e01)


def _lane_lo():
    return lax.broadcasted_iota(jnp.int32, (1, LANES), 1) < HALF


def _t5_bucket_np(dist):
    n = np.maximum(dist, 0)
    ratio = np.log(np.maximum(n, 1).astype(np.float32) / np.float32(MAX_EXACT))
    large = MAX_EXACT + (ratio / np.float32(math.log(MAX_DISTANCE / MAX_EXACT))
                         * np.float32(N_BUCKETS - MAX_EXACT)).astype(np.int32)
    large = np.minimum(large, N_BUCKETS - 1)
    return np.where(n < MAX_EXACT, n, large).astype(np.int32)


def _tril_np(n):
    return np.tril(np.ones((n, n), np.float32))


def _expand_np():
    e = np.zeros((LANES, SSD_D_INNER), np.float32)
    for h in range(SSD_HEADS):
        e[h, h * SSD_HEAD_DIM:(h + 1) * SSD_HEAD_DIM] = 1.0
    return e


def _full_spec(shape):
    nd = len(shape)
    return pl.BlockSpec(shape, lambda *_: (0,) * nd, pipeline_mode=pl.Buffered(1))


def _smem_spec():
    return pl.BlockSpec(memory_space=pltpu.SMEM)


def _params(sem):
    return pltpu.CompilerParams(dimension_semantics=sem, vmem_limit_bytes=VMEM_LIMIT)


def _in_proj_body(x_ref, g_ref, w_ref, z_ref, xbc_ref, q_ref, k_ref, v_ref, dt_ref):
    xn = _rms(x_ref[...], g_ref[...]).astype(BF16)

    def seg(lo, hi):
        return _dot(xn, w_ref[:, lo:hi])

    z_ref[...] = seg(COL_Z, COL_XBC)
    xbc_ref[...] = seg(COL_XBC, COL_Q)
    q_ref[...] = (seg(COL_Q, COL_K) * (ATTN_HEAD_DIM ** -0.5)).astype(BF16)
    k_ref[...] = seg(COL_K, COL_V)
    v_ref[...] = seg(COL_V, COL_DT)
    dt_ref[...] = seg(COL_DT, IN_COLS)


def _in_proj(x2, g_mix, w_in_r, tm):
    t = x2.shape[0]
    row = lambda w: pl.BlockSpec((tm, w), lambda i: (i, 0))
    outs = [(SSD_D_INNER, F32), (CONV_DIM, F32), (ATTN_DIM, BF16), (KV_DIM, F32), (KV_DIM, F32), (DT_PAD, F32)]
    return pl.pallas_call(
        _in_proj_body,
        grid=(t // tm,),
        in_specs=[row(D_MODEL), _full_spec((1, D_MODEL)), _full_spec((D_MODEL, IN_COLS))],
        out_specs=[row(w) for w, _ in outs],
        out_shape=[jax.ShapeDtypeStruct((t, w), d) for w, d in outs],
        compiler_params=_params(("parallel",)),
        name="in_proj",
    )(x2, g_mix, w_in_r)


def _conv_taps(cw_ref, cb_ref, taps):
    acc = cb_ref[...] + taps[0] * cw_ref[0:1, :]
    for k in range(1, CONV_K):
        acc = acc + taps[k] * cw_ref[k:k + 1, :]
    return _silu(acc)


def _ssd_prepare(conv, dt_raw, dtb_ref, alog_ref, tcum, total_fn, extra_fn, expand):
    xs = conv[:, :SSD_D_INNER]
    bm = conv[:, SSD_D_INNER:SSD_D_INNER + SSD_GROUPS * D_STATE]
    cm = conv[:, SSD_D_INNER + SSD_GROUPS * D_STATE:]
    dt = _softplus(dt_raw + dtb_ref[...])
    a = dt * (-jnp.exp(alog_ref[...]))
    cs = _sel_left(tcum, a)
    total = total_fn(cs, a)
    pieces = [dt, jnp.exp(total - cs), jnp.exp(cs)] + extra_fn(total)
    rows = cs.shape[0]
    ex = _sel_right(jnp.concatenate(pieces, axis=0), expand)
    ex = [ex[i * rows:(i + 1) * rows] for i in range(len(pieces))]
    return xs, bm, cm, cs, cs.T, ex


def _ssd_diag(cs, cs_t, cb_g, xdt, mask, g):
    lo = _lane_lo()
    out = []
    for pr in range(2):
        h0 = g * 4 + 2 * pr
        xp = xdt[:, (h0 // 2) * LANES:(h0 // 2 + 1) * LANES]
        x_lo = jnp.where(lo, xp, 0.0).astype(BF16)
        x_hi = jnp.where(lo, 0.0, xp).astype(BF16)
        acc = None
        for h, xh in ((h0, x_lo), (h0 + 1, x_hi)):
            diff = cs[:, h:h + 1] - cs_t[h:h + 1, :]
            decay = jnp.exp(jnp.where(mask, diff, -jnp.inf))
            part = _dot((cb_g * decay).astype(BF16), xh)
            acc = part if acc is None else acc + part
        out.append(acc)
    return jnp.concatenate(out, axis=-1)


def _gated_norm(y, z, gn_ref):
    yf = y * _silu(z)
    parts = []
    for g in range(SSD_GROUPS):
        yg = yf[:, g * GROUP_W:(g + 1) * GROUP_W]
        parts.append(yg * lax.rsqrt(jnp.mean(yg * yg, axis=-1, keepdims=True) + EPS))
    return jnp.concatenate(parts, axis=-1) * gn_ref[...]


def _build_bias(bias_ref, bucket_of, relb_ref, n_tables):
    for i in range(n_tables):
        for h in range(N_HEADS):
            bias_ref[i, h] = jnp.full(bias_ref.shape[2:], NEG, F32)

    def body(t, carry):
        for i in range(n_tables):
            hit = bucket_of(i) == t
            for h in range(N_HEADS):
                bias_ref[i, h] = jnp.where(hit, relb_ref[t, h], bias_ref[i, h])
        return carry

    lax.fori_loop(0, N_BUCKETS, body, 0)


def _alternate(first, second):
    streams = [[0.0, 1.0, first], [0.0, SECOND_STREAM_LEAD, second]]
    while streams:
        entry = min(streams, key=lambda e: e[0])
        try:
            entry[0] += entry[1] * next(entry[2])
        except StopIteration:
            streams.remove(entry)


def _layer_step_body(x_ref, xprev_ref, mk_ref, mv_ref, gmix_ref, win_ref, wout_ref, gc_ref, wcq_ref,
                     wco_ref, gf_ref, wg_ref, wu_ref, wd_ref, gfin_ref,
                     cw_ref, cb_ref, dtb_ref, alog_ref, dskip_ref, gn_ref,
                     tril_ref, bucket_ref, relb_ref, sink_ref,
                     y_ref, conv_out, ssm_out, k_out, v_out,
                     xpad, state_t, kbuf, vbuf, bias, mix, proj, *, n_chunks, tiles_per_seq):
    step = pl.program_id(0)
    n_tiles = pl.num_programs(0) - 1
    c = lax.rem(jnp.minimum(step, n_tiles - 1), tiles_per_seq)
    last = tiles_per_seq - 1
    tq = n_chunks * CHUNK

    @pl.when(step == 0)
    def _():
        _build_bias(bias, lambda i: bucket_ref[i], relb_ref, 2)
        mix[...] = jnp.zeros_like(mix)

    @pl.when(c == 0)
    def _():
        xpad[0:8, :] = jnp.zeros((8, CONV_DIM), F32)
        state_t[...] = jnp.zeros_like(state_t)
        kbuf[:, 0:CHUNK, :] = jnp.zeros((2, CHUNK, KV_DIM), BF16)
        vbuf[:, 0:CHUNK, :] = jnp.zeros((2, CHUNK, KV_DIM), BF16)

    def back():
        h1 = xprev_ref[0] + _dot(mix[...], wout_ref[...])
        yield 4
        qc = _dot(_rms(h1, gc_ref[...]).astype(BF16), wcq_ref[...]).astype(BF16)
        yield 2
        heads = []
        for h in range(CA_HEADS):
            hs = slice(h * CA_HEAD_DIM, (h + 1) * CA_HEAD_DIM)
            sc = _dot_nt(qc[:, hs], mk_ref[0, :, hs]) * (CA_HEAD_DIM ** -0.5)
            m = jnp.max(sc, axis=-1, keepdims=True)
            p = jnp.exp(sc - m)
            heads.append(_dot(p.astype(BF16), mv_ref[0, :, hs]) / jnp.sum(p, axis=-1, keepdims=True))
            yield 3
        o = jnp.concatenate(heads, axis=-1).astype(BF16)
        h2 = h1 + _dot(o, wco_ref[...])
        hn = _rms(h2, gf_ref[...]).astype(BF16)
        yield 4
        acc = h2
        for f_lo, f_hi in FF_PIECES:
            width = (f_hi - f_lo) / 512
            halves = []
            for h_lo in range(f_lo, f_hi, 256):
                h_hi = min(h_lo + 256, f_hi)
                gate = _dot(hn, wg_ref[:, h_lo:h_hi])
                halves.append((_silu(gate) * _dot(hn, wu_ref[:, h_lo:h_hi])).astype(BF16))
                if h_lo == f_lo:
                    yield 5 * width
            yield 6 * width
            act = halves[0] if len(halves) == 1 else jnp.concatenate(halves, axis=-1)
            acc = acc + _dot(act, wd_ref[f_lo:f_hi, :])
            yield 5 * width
        y_ref[0] = _rms(acc, gfin_ref[...])

    def front():
        x = x_ref[0]
        xn = _rms(x, gmix_ref[...]).astype(BF16)
        for p_lo, p_hi in PROJ_PIECES:
            if COL_XBC <= p_lo and p_hi <= COL_Q:
                xpad[8:8 + tq, p_lo - COL_XBC:p_hi - COL_XBC] = _dot(xn, win_ref[:, p_lo:p_hi])
            else:
                proj[:, p_lo:p_hi] = _dot(xn, win_ref[:, p_lo:p_hi])
            yield 4

        conv_halves = []
        half_w = CONV_DIM // 2
        for hf in range(2):
            cs_ = slice(hf * half_w, (hf + 1) * half_w)
            acc = cb_ref[:, cs_] + xpad[5:5 + tq, cs_] * cw_ref[0:1, cs_]
            for kk in range(1, CONV_K):
                acc = acc + xpad[5 + kk:5 + kk + tq, cs_] * cw_ref[kk:kk + 1, cs_]
            conv_halves.append(_silu(acc))
            yield 6
        xpad[5:8, :] = xpad[5 + tq:8 + tq, :]
        xs_all = conv_halves[0]
        bc_all = conv_halves[1]

        k_new = proj[:, COL_K:COL_V]
        v_new = proj[:, COL_V:COL_DT]
        kbuf[0, CHUNK:, :] = k_new.astype(BF16)
        kbuf[1, CHUNK:, :] = pltpu.roll(k_new, HALF, 1).astype(BF16)
        vbuf[0, CHUNK:, :] = v_new.astype(BF16)
        vbuf[1, CHUNK:, :] = pltpu.roll(v_new, HALF, 1).astype(BF16)
        q = (proj[:, COL_Q:COL_K] * (ATTN_HEAD_DIM ** -0.5)).astype(BF16)
        yield 2

        row = lax.broadcasted_iota(jnp.int32, (CHUNK, CHUNK), 0)
        col = lax.broadcasted_iota(jnp.int32, (CHUNK, CHUNK), 1)
        causal = col <= row
        lo = _lane_lo()
        for ci in range(n_chunks):
            rs = slice(ci * CHUNK, (ci + 1) * CHUNK)
            xs = xs_all[rs]
            bm = bc_all[rs, 0:SSD_GROUPS * D_STATE]
            cm = bc_all[rs, SSD_GROUPS * D_STATE:]
            dt = _softplus(proj[rs, COL_DT:IN_COLS] + dtb_ref[...])
            a = dt * (-jnp.exp(alog_ref[...]))
            cs = _sel_left(tril_ref[...], a)
            cs_t = cs.T
            dt_t = dt.T
            total = cs[CHUNK - 1:CHUNK, :]
            ecs = jnp.exp(cs)
            w_end = dt * jnp.exp(total - cs)
            bb = bm.astype(BF16)
            cb = cm.astype(BF16)
            yield 5
            y_parts = []
            for g in range(SSD_GROUPS):
                gs = slice(g * GROUP_W, (g + 1) * GROUP_W)
                ns = slice(g * D_STATE, (g + 1) * D_STATE)
                cb_g = _dot_nt(cb[:, ns], bb[:, ns])
                y_diag, xw, ecs_g = [], [], []
                for pr in range(2):
                    h0 = g * 4 + 2 * pr
                    xp = xs[:, (h0 // 2) * LANES:(h0 // 2 + 1) * LANES]
                    ecs_g.append(jnp.where(lo, ecs[:, h0:h0 + 1], ecs[:, h0 + 1:h0 + 2]))
                    xw.append((xp * jnp.where(lo, w_end[:, h0:h0 + 1], w_end[:, h0 + 1:h0 + 2])).astype(BF16))
                    x2 = jnp.concatenate([jnp.where(lo, xp, 0.0), jnp.where(lo, 0.0, xp)], axis=0).astype(BF16)
                    gmat = []
                    for h in (h0, h0 + 1):
                        diff = cs[:, h:h + 1] - cs_t[h:h + 1, :]
                        decay = jnp.exp(jnp.where(causal, diff, -jnp.inf))
                        gmat.append((cb_g * decay * dt_t[h:h + 1, :]).astype(BF16))
                    y_diag.append(_dot(jnp.concatenate(gmat, axis=1), x2))
                ecs_x = jnp.concatenate(ecs_g, axis=-1)
                st_old = state_t[:, gs]
                y_off = _dot(cb[:, ns], st_old.astype(BF16)) * ecs_x
                y_parts.append(jnp.concatenate(y_diag, axis=-1) + y_off)
                b_t = bm[:, ns].T.astype(BF16)
                state_t[:, gs] = ecs_x[CHUNK - 1:CHUNK, :] * st_old + _dot(b_t, jnp.concatenate(xw, axis=-1))
                yield 9
            y = jnp.concatenate(y_parts, axis=-1) + dskip_ref[...] * xs
            mix[rs, 0:SSD_D_INNER] = _gated_norm(y, proj[rs, COL_Z:COL_XBC], gn_ref).astype(BF16)
            yield 3

            keys = slice(ci * CHUNK, (ci + 2) * CHUNK)
            table = jnp.minimum(c, 1) if ci == 0 else 1
            probs, denoms = {}, {}
            for variant in range(2):
                hv = [h for h in range(N_HEADS) if (h // Q_PER_KV + h % 2) % 2 == variant]
                qms = []
                for h in hv:
                    q_pair = q[rs, (h // 2) * LANES:(h // 2 + 1) * LANES]
                    qms.append(jnp.where(lo, q_pair, 0.0) if h % 2 == 0 else jnp.where(lo, 0.0, q_pair))
                sc_all = _dot_nt(jnp.concatenate(qms, axis=0).astype(BF16), kbuf[variant, keys, :])
                for i, h in enumerate(hv):
                    sc = sc_all[i * CHUNK:(i + 1) * CHUNK] + bias[table, h]
                    sink = sink_ref[h]
                    m = jnp.maximum(jnp.max(sc, axis=-1, keepdims=True), sink)
                    p = jnp.exp(sc - m)
                    denoms[h] = jnp.sum(p, axis=-1, keepdims=True) + jnp.exp(sink - m)
                    probs[h] = p.astype(BF16)
                yield 10
            if ci == n_chunks - 1:
                yield 1000
            parts = {}
            for variant in range(2):
                vv = vbuf[variant, keys, :]
                for par in range(2):
                    hv = [h for h in range(N_HEADS) if (h // Q_PER_KV + h % 2) % 2 == variant and h % 2 == par]
                    vm = jnp.where(lo, vv, 0.0) if par == 0 else jnp.where(lo, 0.0, vv)
                    o_all = _dot(jnp.concatenate([probs[h] for h in hv], axis=0), vm.astype(BF16))
                    for i, h in enumerate(hv):
                        parts[h] = o_all[i * CHUNK:(i + 1) * CHUNK] / denoms[h]
            for pair in range(N_HEADS // 2):
                mix[rs, SSD_D_INNER + pair * LANES:SSD_D_INNER + (pair + 1) * LANES] = (
                    parts[2 * pair] + parts[2 * pair + 1]).astype(BF16)
            yield 4
        kbuf[:, 0:CHUNK, :] = kbuf[:, tq:tq + CHUNK, :]
        vbuf[:, 0:CHUNK, :] = vbuf[:, tq:tq + CHUNK, :]

    _alternate(back(), front())

    @pl.when((c == last) & (step < n_tiles))
    def _():
        conv_out[0] = xpad[5:8, :]
        ssm_out[0] = state_t[...].T
        k_out[0] = proj[tq - WINDOW:tq, COL_K:COL_V]
        v_out[0] = proj[tq - WINDOW:tq, COL_V:COL_DT]


def _prompt_layer(x3, mk_b, mv_b, dense, small, consts, n_chunks):
    cw, cb, dtb, alog, dskip_x, gn, relb, sinks = small
    tril, buckets = consts
    batch, seq, _ = x3.shape
    tq = n_chunks * CHUNK
    tps = seq // tq
    n_tiles = batch * tps
    front = lambda s: jnp.minimum(s, n_tiles - 1)
    back = lambda s: jnp.maximum(s - 1, 0)
    x_spec = pl.BlockSpec((1, tq, D_MODEL), lambda s: (front(s) // tps, front(s) % tps, 0))
    xprev_spec = pl.BlockSpec((1, tq, D_MODEL), lambda s: (back(s) // tps, back(s) % tps, 0))
    y_spec = pl.BlockSpec((1, tq, D_MODEL), lambda s: (back(s) // tps, back(s) % tps, 0))
    mem_spec = pl.BlockSpec((1, N_MEM, CA_DIM), lambda s: (back(s) // tps, 0, 0))
    per_b = lambda shp: pl.BlockSpec((1,) + shp, lambda s: (front(s) // tps,) + (0,) * len(shp))
    out_shape = [
        jax.ShapeDtypeStruct((batch, seq, D_MODEL), F32),
        jax.ShapeDtypeStruct((batch, CONV_K - 1, CONV_DIM), F32),
        jax.ShapeDtypeStruct((batch, SSD_D_INNER, D_STATE), F32),
        jax.ShapeDtypeStruct((batch, WINDOW, KV_DIM), F32),
        jax.ShapeDtypeStruct((batch, WINDOW, KV_DIM), F32),
    ]
    full = list(dense) + [cw, cb, dtb, alog, dskip_x, gn, tril, buckets]
    return pl.pallas_call(
        functools.partial(_layer_step_body, n_chunks=n_chunks, tiles_per_seq=tps),
        grid=(n_tiles + 1,),
        in_specs=([x_spec, xprev_spec, mem_spec, mem_spec] + [_full_spec(a.shape) for a in full]
                  + [_smem_spec(), _smem_spec()]),
        out_specs=[y_spec, per_b((CONV_K - 1, CONV_DIM)), per_b((SSD_D_INNER, D_STATE)),
                   per_b((WINDOW, KV_DIM)), per_b((WINDOW, KV_DIM))],
        out_shape=out_shape,
        scratch_shapes=[
            pltpu.VMEM((8 + tq, CONV_DIM), F32),
            pltpu.VMEM((D_STATE, SSD_D_INNER), F32),
            pltpu.VMEM((2, CHUNK + tq, KV_DIM), BF16),
            pltpu.VMEM((2, CHUNK + tq, KV_DIM), BF16),
            pltpu.VMEM((2, N_HEADS, WINDOW, 2 * WINDOW), F32),
            pltpu.VMEM((tq, D_MODEL), BF16),
            pltpu.VMEM((tq, IN_COLS), F32),
        ],
        compiler_params=_params(("arbitrary",)),
        name="prompt_layer",
    )(x3, x3, mk_b, mv_b, *full, relb, sinks)


def _sample_mixer_body(z_ref, xbc_ref, q_ref, k_ref, v_ref, dt_ref,
                       sconv_ref, sssm_ref, ck_ref, cv_ref,
                       cw_ref, cb_ref, dtb_ref, alog_ref, dskip_ref, gn_ref,
                       tcum_ref, tseq_ref, expand_ref, bkt_c_ref, bkt_n_ref, relb_ref, sink_ref,
                       mix_ref, conv_out, ssm_out, k_out, v_out,
                       xpad, bias_c, bias_n, *, dec_seq):
    step = pl.program_id(0)
    n_seq = SAMPLE_SEQS
    rows = n_seq * dec_seq

    @pl.when(step == 0)
    def _():
        _build_bias(bias_c, lambda i: bkt_c_ref[...], relb_ref, 1)
        _build_bias(bias_n, lambda i: bkt_n_ref[...], relb_ref, 1)

    xbc = xbc_ref[...]
    xpad[:, 8:8 + dec_seq, :] = xbc.reshape(n_seq, dec_seq, CONV_DIM)
    for r in range(CONV_K - 1):
        xpad[:, 5 + r, :] = sconv_ref[r]
    taps = [xpad[:, 5 + k:5 + k + dec_seq, :].reshape(rows, CONV_DIM) for k in range(CONV_K - 1)] + [xbc]
    conv = _conv_taps(cw_ref, cb_ref, taps)
    for r in range(CONV_K - 1):
        conv_out[r] = xpad[:, 5 + dec_seq + r, :]

    row = lax.broadcasted_iota(jnp.int32, (rows, rows), 0)
    col = lax.broadcasted_iota(jnp.int32, (rows, rows), 1)
    tseq = tseq_ref[...]
    same_seq = tseq > 0
    causal = same_seq & (col <= row)
    xs, bm, cm, cs, cs_t, (dt_x, dend_x, ecs_x, seqdec_x) = _ssd_prepare(
        conv, dt_ref[...], dtb_ref, alog_ref, tcum_ref[...],
        lambda cs_, a_: _sel_left(tseq, a_), lambda total: [jnp.exp(total)], expand_ref[...])
    xdt = xs * dt_x
    bb = bm.astype(BF16)
    cb = cm.astype(BF16)
    seq_of_row = lax.broadcasted_iota(jnp.int32, (rows, 1), 0) // dec_seq
    seq_of_lane = lax.broadcasted_iota(jnp.int32, (1, rows), 1) // dec_seq
    xde_t = (xdt * dend_x).T
    y_off = [None] * SSD_GROUPS
    for i in range(n_seq):
        st = sssm_ref[i]
        dec = jnp.concatenate(
            [jnp.broadcast_to(seqdec_x[i * dec_seq:i * dec_seq + 1, h * SSD_HEAD_DIM:h * SSD_HEAD_DIM + 1],
                              (SSD_HEAD_DIM, D_STATE)) for h in range(SSD_HEADS)], axis=0)
        new_parts = []
        for g in range(SSD_GROUPS):
            gs = slice(g * GROUP_W, (g + 1) * GROUP_W)
            ns = slice(g * D_STATE, (g + 1) * D_STATE)
            c_i = jnp.where(seq_of_row == i, cm[:, ns], 0.0).astype(BF16)
            part = _dot_nt(c_i, st[gs].astype(BF16))
            y_off[g] = part if y_off[g] is None else y_off[g] + part
            x_i = jnp.where(seq_of_lane == i, xde_t[gs], 0.0).astype(BF16)
            new_parts.append(dec[gs] * st[gs] + _dot(x_i, bb[:, ns]))
        ssm_out[i] = jnp.concatenate(new_parts, axis=0)
    y_parts = []
    for g in range(SSD_GROUPS):
        gs = slice(g * GROUP_W, (g + 1) * GROUP_W)
        ns = slice(g * D_STATE, (g + 1) * D_STATE)
        cb_g = _dot_nt(cb[:, ns], bb[:, ns])
        y_parts.append(_ssd_diag(cs, cs_t, cb_g, xdt, causal, g) + y_off[g] * ecs_x[:, gs])
    y = jnp.concatenate(y_parts, axis=-1) + dskip_ref[...] * xs
    mix_ref[:, 0:SSD_D_INNER] = _gated_norm(y, z_ref[...], gn_ref).astype(BF16)

    lo = _lane_lo()
    k_new = k_ref[...]
    v_new = v_ref[...]
    k_var = [k_new.astype(BF16), pltpu.roll(k_new, HALF, 1).astype(BF16)]
    v_new_r = pltpu.roll(v_new, HALF, 1)
    v_dup = [jnp.where(lo, v_new, v_new_r).astype(BF16), jnp.where(lo, v_new_r, v_new).astype(BF16)]
    qf = q_ref[...].astype(F32)
    q_masked = []
    s_new = []
    for h in range(N_HEADS):
        pair, par = h // 2, h % 2
        j = h // Q_PER_KV
        q_pair = qf[:, pair * LANES:(pair + 1) * LANES]
        qm = jnp.where(lo, q_pair, 0.0) if par == 0 else jnp.where(lo, 0.0, q_pair)
        q_masked.append(qm)
        s_new.append(_dot_nt(qm.astype(BF16), k_var[(j + par) % 2]) + bias_n[0, h])
    stack_rows = lax.broadcasted_iota(jnp.int32, (Q_PER_KV * dec_seq, 1), 0) // dec_seq
    k_new_t = k_new.T
    v_new_t = v_new.T
    keep = WINDOW - dec_seq
    old_lane = lax.broadcasted_iota(jnp.int32, (1, WINDOW), 1) < keep
    grp = Q_PER_KV * dec_seq
    sc_parts, sn_parts, sink_parts = [], [], []
    for i in range(n_seq):
        rs = slice(i * dec_seq, (i + 1) * dec_seq)
        kc_t = ck_ref[i]
        for j in range(N_KV_HEADS):
            heads = range(j * Q_PER_KV, (j + 1) * Q_PER_KV)
            cj = slice(j * ATTN_HEAD_DIM, (j + 1) * ATTN_HEAD_DIM)
            kdup_t = jnp.concatenate([kc_t[cj], kc_t[cj]], axis=0).astype(BF16)
            qs = jnp.concatenate([q_masked[h][rs] for h in heads], axis=0).astype(BF16)
            sc_parts.append(_dot(qs, kdup_t) + jnp.concatenate([bias_c[0, h] for h in heads], axis=0))
            sn_parts.append(jnp.concatenate([s_new[h][rs] for h in heads], axis=0))
            if i == 0:
                sink_j = jnp.zeros((grp, 1), F32)
                for hh, h in enumerate(heads):
                    sink_j = jnp.where(stack_rows == hh, sink_ref[h], sink_j)
                sink_parts.append(sink_j)
    sc = jnp.concatenate(sc_parts, axis=0)
    sn = jnp.concatenate(sn_parts, axis=0)
    sink = jnp.concatenate(sink_parts * n_seq, axis=0)
    m = jnp.maximum(jnp.maximum(jnp.max(sc, axis=-1, keepdims=True), jnp.max(sn, axis=-1, keepdims=True)), sink)
    pc = jnp.exp(sc - m)
    pn = jnp.exp(sn - m)
    rdenom = 1.0 / (jnp.sum(pc, axis=-1, keepdims=True) + jnp.sum(pn, axis=-1, keepdims=True) + jnp.exp(sink - m))
    pc = pc.astype(BF16)
    pn = pn.astype(BF16)
    att_rows = []
    for i in range(n_seq):
        vc_t = cv_ref[i]
        pieces = []
        for j in range(N_KV_HEADS):
            cj = slice(j * ATTN_HEAD_DIM, (j + 1) * ATTN_HEAD_DIM)
            gr = slice((i * N_KV_HEADS + j) * grp, (i * N_KV_HEADS + j + 1) * grp)
            vdup_t = jnp.concatenate([vc_t[cj], vc_t[cj]], axis=0).astype(BF16)
            o = (_dot_nt(pc[gr], vdup_t) + _dot(pn[gr], v_dup[j])) * rdenom[gr]
            for pr in range(Q_PER_KV // 2):
                even = o[(2 * pr) * dec_seq:(2 * pr + 1) * dec_seq]
                odd = o[(2 * pr + 1) * dec_seq:(2 * pr + 2) * dec_seq]
                pieces.append(jnp.where(lo, even, odd))
        att_rows.append(jnp.concatenate(pieces, axis=-1))
        new_shift = (keep - i * dec_seq) % WINDOW
        k_out[i] = jnp.where(old_lane, pltpu.roll(ck_ref[i], keep, 1), pltpu.roll(k_new_t, new_shift, 1))
        v_out[i] = jnp.where(old_lane, pltpu.roll(vc_t, keep, 1), pltpu.roll(v_new_t, new_shift, 1))
    mix_ref[:, SSD_D_INNER:] = jnp.concatenate(att_rows, axis=0).astype(BF16)


def _sample_mixer(z, xbc, q, k, v, dt, sconv, sssm, ck, cv, small, consts, n_seq_total, dec_seq):
    cw, cb, dtb, alog, dskip_x, gn, relb, sinks = small
    tcum, tseq, expand, bkt_c, bkt_n = consts
    rows = SAMPLE_SEQS * dec_seq
    tok = lambda w: pl.BlockSpec((rows, w), lambda i: (i, 0))
    per_s = lambda s: pl.BlockSpec((SAMPLE_SEQS,) + s, lambda i: (i,) + (0,) * len(s))
    out_shape = [
        jax.ShapeDtypeStruct((n_seq_total * dec_seq, D_MODEL), BF16),
        jax.ShapeDtypeStruct((CONV_K - 1, n_seq_total, CONV_DIM), F32),
        jax.ShapeDtypeStruct((n_seq_total, SSD_D_INNER, D_STATE), F32),
        jax.ShapeDtypeStruct((n_seq_total, WINDOW, KV_DIM), F32),
        jax.ShapeDtypeStruct((n_seq_total, WINDOW, KV_DIM), F32),
    ]
    conv_spec = pl.BlockSpec((CONV_K - 1, SAMPLE_SEQS, CONV_DIM), lambda i: (0, i, 0))
    return pl.pallas_call(
        functools.partial(_sample_mixer_body, dec_seq=dec_seq),
        grid=(n_seq_total // SAMPLE_SEQS,),
        in_specs=[tok(SSD_D_INNER), tok(CONV_DIM), tok(ATTN_DIM), tok(KV_DIM), tok(KV_DIM), tok(DT_PAD),
                  conv_spec, per_s((SSD_D_INNER, D_STATE)),
                  per_s((WINDOW, KV_DIM)), per_s((WINDOW, KV_DIM)),
                  _full_spec(cw.shape), _full_spec(cb.shape), _full_spec(dtb.shape), _full_spec(alog.shape),
                  _full_spec(dskip_x.shape), _full_spec(gn.shape),
                  _full_spec(tcum.shape), _full_spec(tseq.shape), _full_spec(expand.shape),
                  _full_spec(bkt_c.shape), _full_spec(bkt_n.shape), _smem_spec(), _smem_spec()],
        out_specs=[tok(D_MODEL), conv_spec, per_s((SSD_D_INNER, D_STATE)),
                   per_s((WINDOW, KV_DIM)), per_s((WINDOW, KV_DIM))],
        out_shape=out_shape,
        scratch_shapes=[
            pltpu.VMEM((SAMPLE_SEQS, 8 + dec_seq, CONV_DIM), F32),
            pltpu.VMEM((1, N_HEADS, dec_seq, WINDOW), F32),
            pltpu.VMEM((1, N_HEADS, rows, rows), F32),
        ],
        compiler_params=_params(("arbitrary",)),
        name="sample_mixer",
    )(z, xbc, q, k, v, dt, sconv, sssm, ck, cv, cw, cb, dtb, alog, dskip_x, gn,
      tcum, tseq, expand, bkt_c, bkt_n, relb, sinks)


def _post1_body(x_ref, mix_ref, wout_ref, gc_ref, wcq_ref, h_ref, qc_ref):
    h = x_ref[...] + _dot(mix_ref[...], wout_ref[...])
    h_ref[...] = h
    qc_ref[...] = _dot(_rms(h, gc_ref[...]).astype(BF16), wcq_ref[...]).astype(BF16)


def _post1(x2, mix, w_out, g_cross, w_cq, tm):
    t = x2.shape[0]
    row = lambda w: pl.BlockSpec((tm, w), lambda i: (i, 0))
    return pl.pallas_call(
        _post1_body,
        grid=(t // tm,),
        in_specs=[row(D_MODEL), row(D_MODEL), _full_spec(w_out.shape), _full_spec(g_cross.shape),
                  _full_spec(w_cq.shape)],
        out_specs=[row(D_MODEL), row(CA_DIM)],
        out_shape=[jax.ShapeDtypeStruct((t, D_MODEL), F32), jax.ShapeDtypeStruct((t, CA_DIM), BF16)],
        compiler_params=_params(("parallel",)),
        name="out_proj",
    )(x2, mix, w_out, g_cross, w_cq)


def _mem_kv_body(mem_ref, g_ref, wk_ref, wv_ref, k_ref, v_ref, kb_ref, vb_ref):
    mn = _rms(mem_ref[...], g_ref[...]).astype(BF16)
    k = _dot(mn, wk_ref[...])
    v = _dot(mn, wv_ref[...])
    rows = k.shape[0]
    for h in range(CA_HEADS):
        hs = slice(h * CA_HEAD_DIM, (h + 1) * CA_HEAD_DIM)
        k_ref[pl.ds(h, rows, stride=CA_HEADS), :] = k[:, hs]
        v_ref[pl.ds(h, rows, stride=CA_HEADS), :] = v[:, hs]
    kb_ref[...] = k.astype(BF16)
    vb_ref[...] = v.astype(BF16)


def _mem_kv(mem2, g_mem, w_ck, w_cv, tm):
    t = mem2.shape[0]
    row = lambda w: pl.BlockSpec((tm, w), lambda i: (i, 0))
    return pl.pallas_call(
        _mem_kv_body,
        grid=(t // tm,),
        in_specs=[row(D_MODEL), _full_spec(g_mem.shape), _full_spec(w_ck.shape), _full_spec(w_cv.shape)],
        out_specs=[pl.BlockSpec((tm * CA_HEADS, CA_HEAD_DIM), lambda i: (i, 0))] * 2 + [row(CA_DIM)] * 2,
        out_shape=([jax.ShapeDtypeStruct((t * CA_HEADS, CA_HEAD_DIM), F32)] * 2
                   + [jax.ShapeDtypeStruct((t, CA_DIM), BF16)] * 2),
        compiler_params=_params(("parallel",)),
        name="mem_kv",
    )(mem2, g_mem, w_ck, w_cv)


def _cross_sample_body(q_ref, k_ref, v_ref, o_ref, *, n_seq, dec_seq):
    q = q_ref[...].astype(F32)
    n_keys = N_MEM * CA_HEADS
    col_head = lax.broadcasted_iota(jnp.int32, (1, n_keys), 1) & (CA_HEADS - 1)
    grp = CA_HEADS * dec_seq
    row_head = (lax.broadcasted_iota(jnp.int32, (n_seq * grp, 1), 0) // dec_seq) & (CA_HEADS - 1)
    own = col_head == row_head
    parts = []
    for i in range(n_seq):
        qi = q[i * dec_seq:(i + 1) * dec_seq]
        qs = jnp.concatenate([qi[:, h * CA_HEAD_DIM:(h + 1) * CA_HEAD_DIM] for h in range(CA_HEADS)], axis=0)
        parts.append(_dot_nt(qs.astype(BF16), k_ref[i].astype(BF16)))
    s = jnp.where(own, jnp.concatenate(parts, axis=0) * (CA_HEAD_DIM ** -0.5), NEG)
    m = jnp.max(s, axis=-1, keepdims=True)
    p = jnp.exp(s - m)
    rdenom = 1.0 / jnp.sum(p, axis=-1, keepdims=True)
    p = p.astype(BF16)
    rows = []
    for i in range(n_seq):
        gr = slice(i * grp, (i + 1) * grp)
        o = _dot(p[gr], v_ref[i].astype(BF16)) * rdenom[gr]
        rows.append(jnp.concatenate([o[h * dec_seq:(h + 1) * dec_seq] for h in range(CA_HEADS)], axis=-1))
    o_ref[...] = jnp.concatenate(rows, axis=0).astype(BF16)


def _cross_sample(qc, ck, cv, n_seq_total, dec_seq, n_seq):
    rows = n_seq * dec_seq
    tok = pl.BlockSpec((rows, CA_DIM), lambda i: (i, 0))
    mem = pl.BlockSpec((n_seq, N_MEM * CA_HEADS, CA_HEAD_DIM), lambda i: (i, 0, 0))
    return pl.pallas_call(
        functools.partial(_cross_sample_body, n_seq=n_seq, dec_seq=dec_seq),
        grid=(n_seq_total // n_seq,),
        in_specs=[tok, mem, mem],
        out_specs=tok,
        out_shape=jax.ShapeDtypeStruct((n_seq_total * dec_seq, CA_DIM), BF16),
        compiler_params=_params(("parallel",)),
        name="cross_sample",
    )(qc, ck, cv)


def _post2_body(h_ref, o_ref, wco_ref, gf_ref, wg_ref, wu_ref, wd_ref, gfin_ref, y_ref):
    h = h_ref[...] + _dot(o_ref[...], wco_ref[...])
    hn = _rms(h, gf_ref[...]).astype(BF16)
    acc = h
    for lo, hi in FF_SPLITS:
        act = _silu(_dot(hn, wg_ref[:, lo:hi])) * _dot(hn, wu_ref[:, lo:hi])
        acc = acc + _dot(act.astype(BF16), wd_ref[lo:hi, :])
    y_ref[...] = _rms(acc, gfin_ref[...])


def _post2(h1, o, w_co, g_ffn, w_gate, w_up, w_down, g_final, tm):
    t = h1.shape[0]
    row = lambda w: pl.BlockSpec((tm, w), lambda i: (i, 0))
    return pl.pallas_call(
        _post2_body,
        grid=(t // tm,),
        in_specs=[row(D_MODEL), row(CA_DIM), _full_spec(w_co.shape), _full_spec(g_ffn.shape),
                  _full_spec(w_gate.shape), _full_spec(w_up.shape), _full_spec(w_down.shape),
                  _full_spec(g_final.shape)],
        out_specs=row(D_MODEL),
        out_shape=jax.ShapeDtypeStruct((t, D_MODEL), F32),
        compiler_params=_params(("parallel",)),
        name="ffn",
    )(h1, o, w_co, g_ffn, w_gate, w_up, w_down, g_final)


def _prompt_consts():
    qi = np.arange(WINDOW)[:, None]
    ji = np.arange(2 * WINDOW)[None, :]
    dist = qi + WINDOW - ji
    inband = (dist >= 0) & (dist < WINDOW)
    bucket = np.where(inband, _t5_bucket_np(dist), -1)
    first = np.where(ji >= WINDOW, bucket, -1)
    buckets = np.stack([first, bucket]).astype(np.int32)
    return (jnp.asarray(_tril_np(CHUNK), BF16), jnp.asarray(buckets))


def _sample_consts(dec_seq, cache_len):
    rows = SAMPLE_SEQS * dec_seq
    r = np.arange(rows)
    same = (r[:, None] // dec_seq) == (r[None, :] // dec_seq)
    tcum = (same & (r[None, :] <= r[:, None])).astype(np.float32)
    tseq = same.astype(np.float32)
    t = np.arange(dec_seq)[:, None]
    j = np.arange(cache_len)[None, :]
    dist_c = t + cache_len - j
    bkt_c = np.where((dist_c >= 0) & (dist_c < WINDOW), _t5_bucket_np(dist_c), -1).astype(np.int32)
    dist_n = (r[:, None] % dec_seq) - (r[None, :] % dec_seq)
    ok = same & (dist_n >= 0) & (dist_n < WINDOW)
    bkt_n = np.where(ok, _t5_bucket_np(dist_n), -1).astype(np.int32)
    return (jnp.asarray(tcum, BF16), jnp.asarray(tseq, BF16), jnp.asarray(_expand_np(), BF16),
            jnp.asarray(bkt_c), jnp.asarray(bkt_n))


def _pick_tile(t, pref):
    tm = min(t, pref)
    while t % tm:
        tm //= 2
    return tm


def kernel(x_prompt, x_sample, mem_prompt, state_conv, state_ssm, cache_swa_k, cache_swa_v, cache_mem_k, cache_mem_v, rel_bias, g_mix, w_in, conv_w, conv_b, dt_bias, a_log, d_skip, g_ssd_norm, sinks, w_out, g_cross, g_mem, w_cq, w_ck, w_cv, w_co, g_ffn, w_gate, w_up, w_down, g_final):
    assert g_mix.shape[0] == 1, "single-layer trunk"
    batch, seq, _ = x_prompt.shape
    n_dec, dec_seq, _ = x_sample.shape
    cache_len = cache_swa_k.shape[2]
    assert seq % CHUNK == 0 and cache_len == WINDOW and n_dec % SAMPLE_SEQS == 0 and dec_seq == 8

    wi = w_in[0]
    s1 = SSD_D_INNER + CONV_DIM
    s2 = s1 + SSD_HEADS
    w_in_r = jnp.concatenate(
        [wi[:, :s1], wi[:, s2:], jnp.pad(wi[:, s1:s2], ((0, 0), (0, DT_PAD - SSD_HEADS)))], axis=1).astype(BF16)
    row = lambda a: a.reshape(1, -1).astype(F32)
    pad_h = lambda a: jnp.pad(a.reshape(1, -1).astype(F32), ((0, 0), (0, DT_PAD - SSD_HEADS)))
    small = (conv_w[0].astype(F32), row(conv_b[0]), pad_h(dt_bias[0]), pad_h(a_log[0]),
             jnp.repeat(d_skip[0].astype(F32), SSD_HEAD_DIM).reshape(1, -1), row(g_ssd_norm[0]),
             rel_bias.astype(F32), sinks[0].astype(F32))
    bf = lambda w: w[0].astype(BF16)
    w_out_b, w_cq_b, w_ck_b, w_cv_b, w_co_b = bf(w_out), bf(w_cq), bf(w_ck), bf(w_cv), bf(w_co)
    w_gate_b, w_up_b, w_down_b = bf(w_gate), bf(w_up), bf(w_down)
    g_mix_r, g_cross_r, g_mem_r, g_ffn_r, g_fin_r = row(g_mix[0]), row(g_cross[0]), row(g_mem[0]), row(g_ffn[0]), row(g_final)

    n_chunks = 2 if seq % (2 * CHUNK) == 0 else 1
    mem2 = mem_prompt.reshape(batch * N_MEM, D_MODEL)
    mk, mv, mk_b, mv_b = _mem_kv(mem2, g_mem_r, w_ck_b, w_cv_b, _pick_tile(batch * N_MEM, 512))
    dense = (g_mix_r, w_in_r, w_out_b, g_cross_r, w_cq_b, w_co_b, g_ffn_r, w_gate_b, w_up_b, w_down_b, g_fin_r)
    y_prompt, p_conv, p_ssm, p_k, p_v = _prompt_layer(
        x_prompt, mk_b.reshape(batch, N_MEM, CA_DIM), mv_b.reshape(batch, N_MEM, CA_DIM),
        dense, small, _prompt_consts(), n_chunks)

    def channel_major(cache):
        return jnp.transpose(cache, (0, 2, 3, 1)).reshape(n_dec, KV_DIM, cache_len)

    def position_major(cache_t):
        return jnp.transpose(cache_t.reshape(n_dec, N_KV_HEADS, ATTN_HEAD_DIM, cache_len), (0, 3, 1, 2))

    ts = n_dec * dec_seq
    xs2 = x_sample.reshape(ts, D_MODEL)
    tm_s = _pick_tile(ts, 512)
    z, xbc, q, k, v, dt = _in_proj(xs2, g_mix_r, w_in_r, tm_s)
    mix_s, s_conv, s_ssm, s_k, s_v = _sample_mixer(
        z, xbc, q, k, v, dt, jnp.transpose(state_conv[0], (1, 0, 2)),
        state_ssm[0].reshape(n_dec, SSD_D_INNER, D_STATE),
        channel_major(cache_swa_k[0]), channel_major(cache_swa_v[0]),
        small, _sample_consts(dec_seq, cache_len), n_dec, dec_seq)
    h1s, qcs = _post1(xs2, mix_s, w_out_b, g_cross_r, w_cq_b, tm_s)
    os_ = _cross_sample(qcs, cache_mem_k[0].reshape(n_dec, N_MEM * CA_HEADS, CA_HEAD_DIM),
                        cache_mem_v[0].reshape(n_dec, N_MEM * CA_HEADS, CA_HEAD_DIM), n_dec, dec_seq, SAMPLE_SEQS)
    y_sample = _post2(h1s, os_, w_co_b, g_ffn_r, w_gate_b, w_up_b, w_down_b, g_fin_r, tm_s)

    return (y_prompt.reshape(batch, seq, D_MODEL), y_sample.reshape(n_dec, dec_seq, D_MODEL),
            p_conv[None], p_ssm.reshape(1, batch, SSD_HEADS, SSD_HEAD_DIM, D_STATE),
            p_k.reshape(1, batch, WINDOW, N_KV_HEADS, ATTN_HEAD_DIM),
            p_v.reshape(1, batch, WINDOW, N_KV_HEADS, ATTN_HEAD_DIM),
            mk.reshape(1, batch, N_MEM, CA_HEADS, CA_HEAD_DIM), mv.reshape(1, batch, N_MEM, CA_HEADS, CA_HEAD_DIM),
            jnp.transpose(s_conv, (1, 0, 2))[None], s_ssm.reshape(1, n_dec, SSD_HEADS, SSD_HEAD_DIM, D_STATE),
            position_major(s_k)[None], position_major(s_v)[None])
```
